```python
import jax, jax.numpy as jnp
from jax import lax
import numpy as np

D_MODEL = 1024
BATCH = 1
SEQ = 16384
DEPTH = 1
DEC_BATCH = 4
DEC_SEQ = 4096
PAST_LEN = 128

N_HEADS = 8
Q_RANK = 384
KV_RANK = 256
D_NOPE = 64
D_ROPE = 32
D_V = 64
D_QK = D_NOPE + D_ROPE
ROPE_THETA = 10000.0
Q_BLOCK = 128
CONV_CH = 512
CONV_WIDTH = 31
CONV_PAD = (CONV_WIDTH - 1) // 2
N_EXPERTS = 32
TOP_K = 4
D_FF = 1024
SWIGLU_ALPHA = 1.702
SWIGLU_LIMIT = 7.0
MOE_BLOCK = 128
RMS_EPS = 1e-6
LN_EPS = 1e-5
IN_SPLITS = (Q_RANK, Q_RANK + KV_RANK, Q_RANK + KV_RANK + D_ROPE, Q_RANK + KV_RANK + D_ROPE + 2 * CONV_CH)
D_IN = IN_SPLITS[-1] + 2 * D_MODEL

kernel_name = 'hybrid_mla_conformer_moe_encoder'


def rmsnorm(x, g):
    xf = x.astype(jnp.float32)
    y = xf * lax.rsqrt(jnp.mean(xf * xf, axis=-1, keepdims=True) + RMS_EPS)
    return (y * g.astype(jnp.float32)).astype(x.dtype)


def layernorm(x, g, b):
    xf = x.astype(jnp.float32)
    mu = jnp.mean(xf, axis=-1, keepdims=True)
    xc = xf - mu
    y = xc * lax.rsqrt(jnp.mean(xc * xc, axis=-1, keepdims=True) + LN_EPS)
    return (y * g.astype(jnp.float32) + b.astype(jnp.float32)).astype(x.dtype)


def rotary_tables(seq, dtype):
    inv = 1.0 / (ROPE_THETA ** (jnp.arange(0, D_ROPE, 2, dtype=jnp.float32) / D_ROPE))
    ang = jnp.arange(seq, dtype=jnp.float32)[:, None] * inv[None, :]
    return jnp.cos(ang).astype(dtype), jnp.sin(ang).astype(dtype)


def apply_rope(x, cos, sin):
    x1, x2 = jnp.split(x, 2, axis=-1)
    return jnp.concatenate([x1 * cos - x2 * sin, x1 * sin + x2 * cos], axis=-1)


def mla_attention(q_nope, q_pe, k_nope, k_pe, v):
    B, S = q_nope.shape[0], q_nope.shape[1]
    nq = S // Q_BLOCK
    scale = D_QK ** -0.5

    def to_blocks(t):
        return jnp.moveaxis(t.reshape(B, nq, Q_BLOCK, *t.shape[2:]), 1, 0)

    def attend(blk):
        qn, qp = blk
        s = (jnp.einsum('bqhd,bkhd->bhqk', qn, k_nope, preferred_element_type=jnp.float32)
             + jnp.einsum('bqhr,bkr->bhqk', qp, k_pe, preferred_element_type=jnp.float32))
        p = jax.nn.softmax(s * scale, axis=-1)
        return jnp.einsum('bhqk,bkhd->bqhd', p.astype(v.dtype), v)

    out = lax.map(attend, (to_blocks(q_nope), to_blocks(q_pe)))
    return jnp.moveaxis(out, 0, 1).reshape(B, S, N_HEADS * D_V)


def conformer_conv(u_in, conv_w, conv_b, ln_g, ln_b, w_pb, b_pb):
    a, gate = jnp.split(u_in, 2, axis=-1)
    u = a * jax.nn.sigmoid(gate)
    u = lax.conv_general_dilated(u, conv_w[:, None, :], window_strides=(1,), padding=[(CONV_PAD, CONV_PAD)],
                                 dimension_numbers=('NWC', 'WIO', 'NWC'), feature_group_count=CONV_CH) + conv_b
    u = jax.nn.silu(layernorm(u, ln_g, ln_b))
    return u @ w_pb + b_pb


def moe_ffn(h, w_router, b_router, w_gu, b_gu, w_down, b_down):
    T = h.shape[0]
    TK = T * TOP_K
    P = TK + N_EXPERTS * MOE_BLOCK
    n_blocks = P // MOE_BLOCK
    logits = (h @ w_router).astype(jnp.float32) + b_router.astype(jnp.float32)
    top_v, top_e = lax.top_k(logits, TOP_K)
    gate_w = jax.nn.softmax(top_v, axis=-1)
    flat_e = top_e.reshape(-1).astype(jnp.int32)
    flat_w = gate_w.reshape(-1)
    flat_tok = jnp.arange(TK, dtype=jnp.int32) // TOP_K
    order = jnp.argsort(flat_e)
    sorted_e = flat_e[order]
    counts = jnp.bincount(flat_e, length=N_EXPERTS)
    padded = ((counts + MOE_BLOCK - 1) // MOE_BLOCK) * MOE_BLOCK
    start = jnp.cumsum(counts) - counts
    pad_end = jnp.cumsum(padded)
    pad_start = pad_end - padded
    dest = pad_start[sorted_e] + (jnp.arange(TK, dtype=jnp.int32) - start[sorted_e])
    slot_tok = jnp.full((P,), T, dtype=jnp.int32).at[dest].set(flat_tok[order])
    slot_w = jnp.zeros((P,), jnp.float32).at[dest].set(flat_w[order])
    block_e = jnp.clip(jnp.searchsorted(pad_end, jnp.arange(n_blocks, dtype=jnp.int32) * MOE_BLOCK, side='right'),
                       0, N_EXPERTS - 1).astype(jnp.int32)
    h_pad = jnp.concatenate([h, jnp.zeros((1, h.shape[1]), h.dtype)], axis=0)
    x_blocks = h_pad[slot_tok].reshape(n_blocks, MOE_BLOCK, h.shape[1])

    def run_block(args):
        xb, e = args
        gu = xb @ w_gu[e] + b_gu[e]
        g, u = jnp.split(gu, 2, axis=-1)
        g = jnp.minimum(g, SWIGLU_LIMIT)
        u = jnp.clip(u, -SWIGLU_LIMIT, SWIGLU_LIMIT)
        act = (u + 1.0) * (g * jax.nn.sigmoid(SWIGLU_ALPHA * g))
        return act @ w_down[e] + b_down[e]

    out = lax.map(run_block, (x_blocks, block_e)).reshape(P, h.shape[1])
    out = out * slot_w[:, None].astype(out.dtype)
    return jax.ops.segment_sum(out, slot_tok, num_segments=T + 1)[:T]


def encoder(x, params):
    (g_mix, w_in, b_gate, g_q, w_uq, g_kv, w_ukv, w_oa, conv_w, conv_b, ln_g, ln_b, w_pb, b_pb,
     w_out, g_ffn, w_router, b_router, w_gu, b_gu, w_down, b_down, g_final) = params
    B, S, D = x.shape
    cos, sin = rotary_tables(S, x.dtype)
    for l in range(DEPTH):
        h = rmsnorm(x, g_mix[l])
        proj = h @ w_in[l]
        c_q, c_kv, k_pe, conv_in, gate_logits = jnp.split(proj, IN_SPLITS, axis=-1)
        q = (rmsnorm(c_q, g_q[l]) @ w_uq[l]).reshape(B, S, N_HEADS, D_QK)
        q_nope, q_pe = q[..., :D_NOPE], q[..., D_NOPE:]
        kv = (rmsnorm(c_kv, g_kv[l]) @ w_ukv[l]).reshape(B, S, N_HEADS, D_NOPE + D_V)
        k_nope, v = kv[..., :D_NOPE], kv[..., D_NOPE:]
        q_pe = apply_rope(q_pe, cos[None, :, None, :], sin[None, :, None, :])
        k_pe = apply_rope(k_pe, cos[None], sin[None])
        branch_a = mla_attention(q_nope, q_pe, k_nope, k_pe, v) @ w_oa[l]
        branch_b = conformer_conv(conv_in, conv_w[l], conv_b[l], ln_g[l], ln_b[l], w_pb[l], b_pb[l])
        gates = jax.nn.sigmoid(gate_logits + b_gate[l])
        gate_a, gate_b = jnp.split(gates, 2, axis=-1)
        x = x + (gate_a * branch_a + gate_b * branch_b) @ w_out[l]
        h2 = rmsnorm(x, g_ffn[l]).reshape(B * S, D)
        x = x + moe_ffn(h2, w_router[l], b_router[l], w_gu[l], b_gu[l], w_down[l], b_down[l]).reshape(B, S, D)
    return rmsnorm(x, g_final)


def setup_inputs(seed: int = 0) -> dict:
    key = jax.random.key(seed)
    ks = jax.random.split(key, 26)
    L = DEPTH

    def nrm(k, shape, scale):
        return jax.random.normal(k, shape, jnp.float32) * scale

    def gain(k, shape):
        return 1.0 + 0.05 * jax.random.normal(k, shape, jnp.float32)

    return {
        'x_prompt': nrm(ks[0], (BATCH, SEQ, D_MODEL), 1.0),
        'x_sample': nrm(ks[1], (DEC_BATCH, DEC_SEQ, D_MODEL), 1.0),
        'g_mix': gain(ks[2], (L, D_MODEL)),
        'w_in': nrm(ks[3], (L, D_MODEL, D_IN), D_MODEL ** -0.5),
        'b_gate': nrm(ks[4], (L, 2 * D_MODEL), 0.1),
        'g_q': gain(ks[5], (L, Q_RANK)),
        'w_uq': nrm(ks[6], (L, Q_RANK, N_HEADS * D_QK), Q_RANK ** -0.5),
        'g_kv': gain(ks[7], (L, KV_RANK)),
        'w_ukv': nrm(ks[8], (L, KV_RANK, N_HEADS * (D_NOPE + D_V)), KV_RANK ** -0.5),
        'w_oa': nrm(ks[9], (L, N_HEADS * D_V, D_MODEL), (N_HEADS * D_V) ** -0.5),
        'conv_w': nrm(ks[10], (L, CONV_WIDTH, CONV_CH), CONV_WIDTH ** -0.5),
        'conv_b': nrm(ks[11], (L, CONV_CH), 0.02),
        'ln_g': gain(ks[12], (L, CONV_CH)),
        'ln_b': nrm(ks[13], (L, CONV_CH), 0.02),
        'w_pb': nrm(ks[14], (L, CONV_CH, D_MODEL), CONV_CH ** -0.5),
        'b_pb': nrm(ks[15], (L, D_MODEL), 0.02),
        'w_out': nrm(ks[16], (L, D_MODEL, D_MODEL), D_MODEL ** -0.5),
        'g_ffn': gain(ks[17], (L, D_MODEL)),
        'w_router': nrm(ks[18], (L, D_MODEL, N_EXPERTS), D_MODEL ** -0.5),
        'b_router': nrm(ks[19], (L, N_EXPERTS), 0.01),
        'w_gu': nrm(ks[20], (L, N_EXPERTS, D_MODEL, 2 * D_FF), D_MODEL ** -0.5),
        'b_gu': nrm(ks[21], (L, N_EXPERTS, 2 * D_FF), 0.02),
        'w_down': nrm(ks[22], (L, N_EXPERTS, D_FF, D_MODEL), D_FF ** -0.5),
        'b_down': nrm(ks[23], (L, N_EXPERTS, D_MODEL), 0.02),
        'g_final': gain(ks[24], (D_MODEL,)),
    }


def reference(x_prompt, x_sample, g_mix, w_in, b_gate, g_q, w_uq, g_kv, w_ukv, w_oa, conv_w, conv_b, ln_g, ln_b,
              w_pb, b_pb, w_out, g_ffn, w_router, b_router, w_gu, b_gu, w_down, b_down, g_final):
    params = (g_mix, w_in, b_gate, g_q, w_uq, g_kv, w_ukv, w_oa, conv_w, conv_b, ln_g, ln_b, w_pb, b_pb,
              w_out, g_ffn, w_router, b_router, w_gu, b_gu, w_down, b_down, g_final)
    y_prompt = encoder(x_prompt, params)
    y_sample = encoder(x_sample, params)
    return (y_prompt, y_sample)
```

```python
import functools
import math

import numpy as np
import jax
import jax.numpy as jnp
from jax import lax
from jax.experimental import pallas as pl
from jax.experimental.pallas import tpu as pltpu

F32 = jnp.float32
BF16 = jnp.bfloat16
I32 = jnp.int32

D_MODEL = 1024
N_HEADS = 8
Q_RANK = 384
KV_RANK = 256
D_NOPE = 64
D_ROPE = 32
D_V = 64
D_QK = D_NOPE + D_ROPE
ROPE_THETA = 10000.0
CONV_CH = 512
CONV_WIDTH = 31
CONV_PAD = (CONV_WIDTH - 1) // 2
N_EXPERTS = 32
TOP_K = 4
D_FF = 1024
SWIGLU_ALPHA = 1.702
SWIGLU_LIMIT = 7.0
RMS_EPS = 1e-6
LN_EPS = 1e-5

LANES = 128
HALO_ROWS = 16
NEG_BIG = -1e30

C_CQ = 0
C_CKV = Q_RANK
C_KPE = Q_RANK + KV_RANK
C_CONV = C_KPE + LANES
C_GATE = C_CONV + 2 * CONV_CH
D_IN_EXT = C_GATE + 2 * D_MODEL

VMEM_LIMIT = 56 * 1024 * 1024


def _rms(x, g):
    return x * lax.rsqrt(jnp.mean(x * x, axis=-1, keepdims=True) + RMS_EPS) * g


def _dot(a, b):
    return jnp.dot(a, b, preferred_element_type=F32)


def _dot_nt(a, b):
    return lax.dot_general(a, b, (((1,), (1,)), ((), ())), preferred_element_type=F32)


def _dot_tn(a, b):
    return lax.dot_general(a, b, (((0,), (0,)), ((), ())), preferred_element_type=F32)


def _const_spec(shape):
    return pl.BlockSpec(shape, lambda *_: (0,) * len(shape))


def _in_proj_kernel(x_ref, tabq_ref, tabk_ref, gmix_ref, win_ref, gq_ref, wqt_ref, gkv_ref, wk_ref, wvt_ref,
                    e2_ref, bgate_ref, qt_ref, k_ref, vt_ref, u_ref, gates_ref):
    h = _rms(x_ref[...], gmix_ref[...]).astype(BF16)
    proj = _dot(h, win_ref[...])

    cqn = _rms(proj[:, C_CQ:C_CQ + Q_RANK], gq_ref[...]).astype(BF16)
    q2 = _dot_nt(wqt_ref[...], cqn)
    cos_q = tabq_ref[0:LANES, :]
    sin_q = tabq_ref[LANES:2 * LANES, :]
    hw = N_HEADS * LANES
    for hh in range(N_HEADS):
        qa = q2[hh * LANES:(hh + 1) * LANES, :]
        qb = q2[hw + hh * LANES:hw + (hh + 1) * LANES, :]
        qt_ref[hh * LANES:(hh + 1) * LANES, :] = (qa * cos_q + qb * sin_q).astype(BF16)

    ckvn = _rms(proj[:, C_CKV:C_CKV + KV_RANK], gkv_ref[...]).astype(BF16)
    kpe = (proj[:, C_KPE:C_KPE + LANES] * tabk_ref[...]).astype(BF16)
    k_ref[...] = (_dot(ckvn, wk_ref[...]) + _dot(kpe, e2_ref[...])).astype(BF16)
    vt_ref[0] = _dot_nt(wvt_ref[...], ckvn).astype(BF16)

    a = proj[:, C_CONV:C_CONV + CONV_CH]
    g = proj[:, C_CONV + CONV_CH:C_CONV + 2 * CONV_CH]
    u_ref[...] = a * jax.nn.sigmoid(g)
    gates_ref[...] = jax.nn.sigmoid(proj[:, C_GATE:] + bgate_ref[...]).astype(BF16)


def _in_proj(x, tabq_t, tabk, g_mix, w_in_ext, g_q, wq_t, g_kv, wk_ext, wv_t, e2, b_gate, *, tm):
    t = x.shape[0]
    nt = t // tm
    row = lambda w: pl.BlockSpec((tm, w), lambda i: (i, 0))
    col = lambda h: pl.BlockSpec((h, tm), lambda i: (0, i))
    return pl.pallas_call(
        _in_proj_kernel,
        grid=(nt,),
        in_specs=[row(D_MODEL), col(2 * LANES), row(LANES), _const_spec((1, D_MODEL)),
                  _const_spec((D_MODEL, D_IN_EXT)), _const_spec((1, Q_RANK)),
                  _const_spec((2 * N_HEADS * LANES, Q_RANK)), _const_spec((1, KV_RANK)),
                  _const_spec((KV_RANK, N_HEADS * LANES)), _const_spec((N_HEADS * D_V, KV_RANK)),
                  _const_spec((LANES, N_HEADS * LANES)), _const_spec((1, 2 * D_MODEL))],
        out_specs=(col(N_HEADS * LANES), row(N_HEADS * LANES),
                   pl.BlockSpec((1, N_HEADS * D_V, tm), lambda i: (i, 0, 0)), row(CONV_CH), row(2 * D_MODEL)),
        out_shape=(jax.ShapeDtypeStruct((N_HEADS * LANES, t), BF16),
                   jax.ShapeDtypeStruct((t, N_HEADS * LANES), BF16),
                   jax.ShapeDtypeStruct((nt, N_HEADS * D_V, tm), BF16),
                   jax.ShapeDtypeStruct((t, CONV_CH), F32),
                   jax.ShapeDtypeStruct((t, 2 * D_MODEL), BF16)),
        compiler_params=pltpu.CompilerParams(dimension_semantics=("arbitrary",), vmem_limit_bytes=VMEM_LIMIT),
        name="in_proj",
    )(x, tabq_t, tabk, g_mix, w_in_ext, g_q, wq_t, g_kv, wk_ext, wv_t, e2, b_gate)


def _attn_kernel(qt_ref, k_ref, vt_ref, o_ref, *, n_chunks, tk):
    qt = qt_ref[...]
    tq = qt.shape[1]
    ones = jnp.ones((16, tk), BF16)

    def body(j, carry):
        m, acc = carry
        ks = pl.multiple_of(j * tk, tk)
        st = _dot(k_ref[pl.ds(ks, tk), :], qt)
        m_new = jnp.maximum(m, jnp.max(st, axis=0, keepdims=True))
        alpha = jnp.exp2(m - m_new)
        p = jnp.exp2(st - m_new).astype(BF16)
        v1 = jnp.concatenate([vt_ref[j], ones], axis=0)
        return m_new, alpha * acc + _dot(v1, p)

    m0 = jnp.full((1, tq), NEG_BIG, F32)
    acc0 = jnp.zeros((D_V + 16, tq), F32)
    _, acc = lax.fori_loop(0, n_chunks, body, (m0, acc0))
    o_ref[...] = (acc[:D_V] / acc[D_V:D_V + 1]).astype(o_ref.dtype)


def _attention(qt, k, vt3, *, seq_start, seq_len, n_seq, tq, tk):
    nq = seq_len // tq
    nkc = seq_len // tk
    q0 = seq_start // tq
    s0 = seq_start // seq_len
    return pl.pallas_call(
        functools.partial(_attn_kernel, n_chunks=nkc, tk=tk),
        grid=(n_seq, N_HEADS, nq),
        in_specs=[pl.BlockSpec((LANES, tq), lambda b, h, i: (h, q0 + b * nq + i)),
                  pl.BlockSpec((seq_len, LANES), lambda b, h, i: (s0 + b, h)),
                  pl.BlockSpec((nkc, D_V, tk), lambda b, h, i: (s0 + b, h, 0))],
        out_specs=pl.BlockSpec((D_V, tq), lambda b, h, i: (h, b * nq + i)),
        out_shape=jax.ShapeDtypeStruct((N_HEADS * D_V, n_seq * seq_len), BF16),
        compiler_params=pltpu.CompilerParams(dimension_semantics=("arbitrary", "arbitrary", "arbitrary"),
                                             vmem_limit_bytes=VMEM_LIMIT),
        name="attention",
    )(qt, k, vt3)


def _post_mix_kernel(x_ref, ot_ref, u_ref, up_ref, un_ref, gates_ref, woa_ref, cw_ref, cb_ref, lng_ref, lnb_ref,
                     wpb_ref, bpb_ref, wout_ref, gffn_ref, wr_ref, br_ref, ltri_ref,
                     x1_ref, route_ref, cnt_ref, ext_ref, cv_ref, carry_ref, *, tm, seq_bounds):
    i = pl.program_id(0)
    t0 = i * tm
    is_start = functools.reduce(jnp.logical_or, [t0 == s for s, _ in seq_bounds])
    is_end = functools.reduce(jnp.logical_or, [t0 + tm == e for _, e in seq_bounds])

    ext_ref[0:HALO_ROWS, :] = jnp.where(is_start, 0.0, up_ref[...])
    ext_ref[HALO_ROWS:HALO_ROWS + tm, :] = u_ref[...]
    ext_ref[HALO_ROWS + tm:2 * HALO_ROWS + tm, :] = jnp.where(is_end, 0.0, un_ref[...])
    rc = min(tm, 64)
    for c in range(tm // rc):
        acc = jnp.zeros((rc, CONV_CH), F32)
        for j in range(CONV_WIDTH):
            off = c * rc + HALO_ROWS - CONV_PAD + j
            acc = acc + cw_ref[j:j + 1, :] * ext_ref[off:off + rc, :]
        cv_ref[c * rc:(c + 1) * rc, :] = acc + cb_ref[...]

    cv = cv_ref[...]
    mu = jnp.mean(cv, axis=-1, keepdims=True)
    xc = cv - mu
    y = xc * lax.rsqrt(jnp.mean(xc * xc, axis=-1, keepdims=True) + LN_EPS) * lng_ref[...] + lnb_ref[...]
    y = y * jax.nn.sigmoid(y)
    branch_b = _dot(y.astype(BF16), wpb_ref[...]) + bpb_ref[...]
    branch_a = _dot_tn(ot_ref[...], woa_ref[...])
    gates = gates_ref[...]
    mix = gates[:, :D_MODEL].astype(F32) * branch_a + gates[:, D_MODEL:].astype(F32) * branch_b
    x1 = x_ref[...] + _dot(mix.astype(BF16), wout_ref[...])
    x1_ref[...] = x1

    h2 = _rms(x1, gffn_ref[...])
    logits = jnp.dot(h2, wr_ref[...], preferred_element_type=F32, precision=lax.Precision.HIGHEST) + br_ref[...]
    lane = lax.broadcasted_iota(I32, logits.shape, 1)
    lane_f = lane.astype(F32)
    vals, idxs, sels = [], [], []
    cur = logits
    for _ in range(TOP_K):
        mval = jnp.max(cur, axis=-1, keepdims=True)
        ik = jnp.min(jnp.where(cur == mval, lane_f, float(LANES)), axis=-1, keepdims=True)
        sel = lane_f == ik
        cur = jnp.where(sel, -jnp.inf, cur)
        vals.append(mval)
        idxs.append(ik)
        sels.append(sel)
    exps = [jnp.exp(v - vals[0]) for v in vals]
    denom = exps[0] + exps[1] + exps[2] + exps[3]

    onehot = sels[0].astype(F32) + sels[1].astype(F32) + sels[2].astype(F32) + sels[3].astype(F32)

    @pl.when(i == 0)
    def _():
        carry_ref[...] = jnp.zeros_like(carry_ref)

    base = carry_ref[...] + _dot(ltri_ref[...], onehot.astype(BF16))
    route = jnp.zeros(logits.shape, F32)
    for kk in range(TOP_K):
        rank = jnp.sum(jnp.where(sels[kk], base, 0.0), axis=-1, keepdims=True)
        route = jnp.where(lane == kk, idxs[kk], route)
        route = jnp.where(lane == TOP_K + kk, exps[kk] / denom, route)
        route = jnp.where(lane == 2 * TOP_K + kk, rank, route)
    route_ref[...] = route
    carry_ref[...] = carry_ref[...] + jnp.sum(onehot, axis=0, keepdims=True)
    cnt_ref[...] = carry_ref[...]


def _post_mix(x, ot, u, gates, w_oa, conv_w, conv_b, ln_g, ln_b, w_pb, b_pb, w_out, g_ffn, w_r, b_r, ltri, *,
              tm, seq_bounds):
    t = x.shape[0]
    nt = t // tm
    hb = tm // HALO_ROWS
    n_halo = t // HALO_ROWS
    row = lambda w: pl.BlockSpec((tm, w), lambda i: (i, 0))
    return pl.pallas_call(
        functools.partial(_post_mix_kernel, tm=tm, seq_bounds=seq_bounds),
        grid=(nt,),
        in_specs=[row(D_MODEL), pl.BlockSpec((N_HEADS * D_V, tm), lambda i: (0, i)), row(CONV_CH),
                  pl.BlockSpec((HALO_ROWS, CONV_CH), lambda i: (jnp.maximum(i * hb - 1, 0), 0)),
                  pl.BlockSpec((HALO_ROWS, CONV_CH), lambda i: (jnp.minimum((i + 1) * hb, n_halo - 1), 0)),
                  row(2 * D_MODEL), _const_spec((N_HEADS * D_V, D_MODEL)), _const_spec((CONV_WIDTH, CONV_CH)),
                  _const_spec((1, CONV_CH)), _const_spec((1, CONV_CH)), _const_spec((1, CONV_CH)),
                  _const_spec((CONV_CH, D_MODEL)), _const_spec((1, D_MODEL)), _const_spec((D_MODEL, D_MODEL)),
                  _const_spec((1, D_MODEL)), _const_spec((D_MODEL, LANES)), _const_spec((1, LANES)),
                  _const_spec((tm, tm))],
        out_specs=(row(D_MODEL), row(LANES), _const_spec((1, LANES))),
        out_shape=(jax.ShapeDtypeStruct((t, D_MODEL), F32), jax.ShapeDtypeStruct((t, LANES), F32),
                   jax.ShapeDtypeStruct((1, LANES), F32)),
        scratch_shapes=[pltpu.VMEM((tm + 2 * HALO_ROWS, CONV_CH), F32), pltpu.VMEM((tm, CONV_CH), F32),
                        pltpu.VMEM((1, LANES), F32)],
        compiler_params=pltpu.CompilerParams(dimension_semantics=("arbitrary",), vmem_limit_bytes=VMEM_LIMIT),
        name="post_mix",
    )(x, ot, u, u, u, gates, w_oa, conv_w, conv_b, ln_g, ln_b, w_pb, b_pb, w_out, g_ffn, w_r, b_r, ltri)


def _row_copy(src, s, dst, d, sem):
    return pltpu.make_async_copy(src.at[pl.ds(s, 1), :], dst.at[pl.ds(d, 1), :], sem)


def _dispatch_kernel(dest_hbm, x1_ref, gffn_ref, xs_hbm, idx_smem, hbuf, idx_sem, row_sem, *, td):
    i = pl.program_id(0)
    idx_cp = pltpu.make_async_copy(dest_hbm.at[i], idx_smem, idx_sem)
    idx_cp.start()
    hbuf[...] = _rms(x1_ref[...], gffn_ref[...])
    idx_cp.wait()

    def issue(t, c):
        for kk in range(TOP_K):
            _row_copy(hbuf, t, xs_hbm, idx_smem[t * TOP_K + kk], row_sem).start()
        return c

    lax.fori_loop(0, td, issue, 0)
    pltpu.make_async_copy(xs_hbm.at[pl.ds(0, td * TOP_K), :], xs_hbm.at[pl.ds(0, td * TOP_K), :], row_sem).wait()


def _dispatch(dest, x1, g_ffn, *, td):
    t = x1.shape[0]
    return pl.pallas_call(
        functools.partial(_dispatch_kernel, td=td),
        grid=(t // td,),
        in_specs=[pl.BlockSpec(memory_space=pl.ANY), pl.BlockSpec((td, D_MODEL), lambda i: (i, 0)),
                  _const_spec((1, D_MODEL))],
        out_specs=pl.BlockSpec(memory_space=pl.ANY),
        out_shape=jax.ShapeDtypeStruct((t * TOP_K, D_MODEL), F32),
        scratch_shapes=[pltpu.SMEM((td * TOP_K,), I32), pltpu.VMEM((td, D_MODEL), F32),
                        pltpu.SemaphoreType.DMA, pltpu.SemaphoreType.DMA],
        compiler_params=pltpu.CompilerParams(dimension_semantics=("arbitrary",), has_side_effects=True),
        name="dispatch",
    )(dest.reshape(t // td, td * TOP_K), x1, g_ffn)


def _moe_kernel(blk_ref, exp_ref, lo_ref, hi_ref, first_ref, xs_ref, wgu_ref, bgu_ref, wd_ref, bd_ref, y_ref):
    w = pl.program_id(0)
    lo = lo_ref[w]
    hi = hi_ref[w]

    @pl.when(hi > lo)
    def _():
        x = xs_ref[...].astype(BF16)
        gu = _dot(x, wgu_ref[0]) + bgu_ref[0]
        g = jnp.minimum(gu[:, :D_FF], SWIGLU_LIMIT)
        u = jnp.clip(gu[:, D_FF:], -SWIGLU_LIMIT, SWIGLU_LIMIT)
        act = (u + 1.0) * (g * jax.nn.sigmoid(SWIGLU_ALPHA * g))
        out = _dot(act.astype(BF16), wd_ref[0]) + bd_ref[0]
        row = lax.broadcasted_iota(I32, out.shape, 0)
        mine = (row >= lo) & (row < hi)

        @pl.when(first_ref[w] == 1)
        def _():
            y_ref[...] = jnp.where(mine, out, 0.0)

        @pl.when(first_ref[w] == 0)
        def _():
            y_ref[...] = jnp.where(mine, out, y_ref[...])


def _moe(meta, xs, w_gu, b_gu, w_down, b_down, *, bm):
    n_items = meta[0].shape[0]
    rows = xs.shape[0]
    return pl.pallas_call(
        _moe_kernel,
        grid_spec=pltpu.PrefetchScalarGridSpec(
            num_scalar_prefetch=5,
            grid=(n_items,),
            in_specs=[pl.BlockSpec((bm, D_MODEL), lambda w, blk, ex, lo, hi, fi: (blk[w], 0)),
                      pl.BlockSpec((1, D_MODEL, 2 * D_FF), lambda w, blk, ex, lo, hi, fi: (ex[w], 0, 0)),
                      pl.BlockSpec((1, 1, 2 * D_FF), lambda w, blk, ex, lo, hi, fi: (ex[w], 0, 0)),
                      pl.BlockSpec((1, D_FF, D_MODEL), lambda w, blk, ex, lo, hi, fi: (ex[w], 0, 0)),
                      pl.BlockSpec((1, 1, D_MODEL), lambda w, blk, ex, lo, hi, fi: (ex[w], 0, 0))],
            out_specs=pl.BlockSpec((bm, D_MODEL), lambda w, blk, ex, lo, hi, fi: (blk[w], 0)),
        ),
        out_shape=jax.ShapeDtypeStruct((rows, D_MODEL), F32),
        compiler_params=pltpu.CompilerParams(dimension_semantics=("arbitrary",), vmem_limit_bytes=VMEM_LIMIT),
        name="moe",
    )(*meta, xs, w_gu, b_gu, w_down, b_down)


def _moe_items(counts, *, bm, n_blocks):
    end = jnp.cumsum(counts)
    start = end - counts
    first_blk = start // bm
    last_blk = jnp.maximum(end - 1, 0) // bm
    n_e = jnp.where(counts > 0, last_blk - first_blk + 1, 0)
    item_end = jnp.cumsum(n_e)
    item_off = item_end - n_e
    total = item_end[-1]
    n_items = n_blocks + N_EXPERTS
    w = jnp.arange(n_items, dtype=I32)
    w_eff = jnp.minimum(w, total - 1)
    ex = jnp.clip(jnp.searchsorted(item_end, w_eff, side="right"), 0, N_EXPERTS - 1).astype(I32)
    blk = (first_blk[ex] + (w_eff - item_off[ex])).astype(I32)
    lo = jnp.clip(start[ex] - blk * bm, 0, bm)
    hi = jnp.clip(end[ex] - blk * bm, 0, bm)
    hi = jnp.where(w < total, hi, lo)
    first = jnp.concatenate([jnp.ones((1,), I32), (blk[1:] != blk[:-1]).astype(I32)])
    return blk, ex, lo.astype(I32), hi.astype(I32), first, start


def _combine_kernel(dest_hbm, x1_ref, route_ref, gfin_ref, y_hbm, out_ref, idx_smem, ybuf, idx_sem, row_sem, *, tc):
    i = pl.program_id(0)
    idx_cp = pltpu.make_async_copy(dest_hbm.at[i], idx_smem, idx_sem)
    idx_cp.start()
    idx_cp.wait()

    def issue(t, c):
        for kk in range(TOP_K):
            _row_copy(y_hbm, idx_smem[t * TOP_K + kk], ybuf.at[kk], t, row_sem).start()
        return c

    lax.fori_loop(0, tc, issue, 0)
    pltpu.make_async_copy(y_hbm.at[pl.ds(0, tc * TOP_K), :], y_hbm.at[pl.ds(0, tc * TOP_K), :], row_sem).wait()

    route = route_ref[...]
    acc = x1_ref[...]
    moe = route[:, TOP_K:TOP_K + 1] * ybuf[0]
    for kk in range(1, TOP_K):
        moe = moe + route[:, TOP_K + kk:TOP_K + kk + 1] * ybuf[kk]
    out_ref[...] = _rms(acc + moe, gfin_ref[...])


def _combine(dest, x1, route, g_final, y, *, tc):
    t = x1.shape[0]
    return pl.pallas_call(
        functools.partial(_combine_kernel, tc=tc),
        grid=(t // tc,),
        in_specs=[pl.BlockSpec(memory_space=pl.ANY), pl.BlockSpec((tc, D_MODEL), lambda i: (i, 0)),
                  pl.BlockSpec((tc, LANES), lambda i: (i, 0)), _const_spec((1, D_MODEL)),
                  pl.BlockSpec(memory_space=pl.ANY)],
        out_specs=pl.BlockSpec((tc, D_MODEL), lambda i: (i, 0)),
        out_shape=jax.ShapeDtypeStruct((t, D_MODEL), F32),
        scratch_shapes=[pltpu.SMEM((tc * TOP_K,), I32), pltpu.VMEM((TOP_K, tc, D_MODEL), F32),
                        pltpu.SemaphoreType.DMA, pltpu.SemaphoreType.DMA],
        compiler_params=pltpu.CompilerParams(dimension_semantics=("arbitrary",)),
        name="combine",
    )(dest.reshape(t // tc, tc * TOP_K), x1, route, g_final, y)


def _rot_cols(w):
    half = w.shape[-1] // 2
    return jnp.concatenate([-w[..., half:], w[..., :half]], axis=-1)


def _prep_weights(w_in, w_uq, w_ukv):
    d = w_in.shape[0]
    w_kpe = w_in[:, C_KPE:C_KPE + D_ROPE]
    w_in_ext = jnp.concatenate(
        [w_in[:, :C_KPE], w_kpe, _rot_cols(w_kpe), jnp.zeros((d, LANES - 2 * D_ROPE), F32),
         w_in[:, C_KPE + D_ROPE:]], axis=1).astype(BF16)

    wq = w_uq.reshape(Q_RANK, N_HEADS, D_QK)
    nope, pe = wq[..., :D_NOPE], wq[..., D_NOPE:]
    zpad = jnp.zeros((Q_RANK, N_HEADS, LANES - D_QK), F32)
    plain = jnp.concatenate([nope, pe, zpad], axis=-1).reshape(Q_RANK, N_HEADS * LANES)
    rot = jnp.concatenate([jnp.zeros_like(nope), _rot_cols(pe), zpad], axis=-1).reshape(Q_RANK, N_HEADS * LANES)
    wq_t = jnp.concatenate([plain, rot], axis=1).T.astype(BF16)

    wkv = w_ukv.reshape(KV_RANK, N_HEADS, D_NOPE + D_V)
    wk_ext = jnp.concatenate([wkv[..., :D_NOPE], jnp.zeros((KV_RANK, N_HEADS, LANES - D_NOPE), F32)],
                             axis=-1).reshape(KV_RANK, N_HEADS * LANES).astype(BF16)
    wv_t = wkv[..., D_NOPE:].reshape(KV_RANK, N_HEADS * D_V).T.astype(BF16)
    return w_in_ext, wq_t, wk_ext, wv_t


def _kpe_placement():
    e2 = np.zeros((LANES, N_HEADS * LANES), np.float32)
    for j in range(D_ROPE):
        for h in range(N_HEADS):
            e2[j, h * LANES + D_NOPE + j] = 1.0
            e2[D_ROPE + j, h * LANES + D_NOPE + j] = 1.0
    return jnp.asarray(e2, BF16)


def _rope_tables(pos):
    inv = 1.0 / (ROPE_THETA ** (jnp.arange(0, D_ROPE, 2, dtype=F32) / D_ROPE))
    ang = pos.astype(F32)[:, None] * inv[None, :]
    cos, sin = jnp.cos(ang), jnp.sin(ang)
    t = pos.shape[0]
    scale = (D_QK ** -0.5) * math.log2(math.e)
    cos_q = jnp.concatenate([jnp.ones((t, D_NOPE), F32), cos, cos, jnp.zeros((t, LANES - D_QK), F32)], axis=1)
    sin_q = jnp.concatenate([jnp.zeros((t, D_NOPE), F32), sin, sin, jnp.zeros((t, LANES - D_QK), F32)], axis=1)
    tabq_t = (jnp.concatenate([cos_q, sin_q], axis=1) * scale).T
    tabk = jnp.concatenate([cos, cos, sin, sin, jnp.zeros((t, LANES - 2 * D_ROPE), F32)], axis=1)
    return tabq_t, tabk


def _pick(n, pref):
    return pref if n % pref == 0 else n


def kernel(x_prompt, x_sample, g_mix, w_in, b_gate, g_q, w_uq, g_kv, w_ukv, w_oa, conv_w, conv_b, ln_g, ln_b, w_pb,
           b_pb, w_out, g_ffn, w_router, b_router, w_gu, b_gu, w_down, b_down, g_final):
    b1, s1, d = x_prompt.shape
    b2, s2, _ = x_sample.shape
    t1, t2 = b1 * s1, b2 * s2
    t = t1 + t2
    x = jnp.concatenate([x_prompt.reshape(t1, d), x_sample.reshape(t2, d)], axis=0)
    seq_bounds = tuple((b * s1, (b + 1) * s1) for b in range(b1)) + tuple(
        (t1 + b * s2, t1 + (b + 1) * s2) for b in range(b2))

    tm = _pick(math.gcd(s1, s2), 512)
    pos = jnp.concatenate([jnp.tile(jnp.arange(s1, dtype=I32), b1), jnp.tile(jnp.arange(s2, dtype=I32), b2)])
    tabq_t, tabk = _rope_tables(pos)
    w_in_ext, wq_t, wk_ext, wv_t = _prep_weights(w_in[0], w_uq[0], w_ukv[0])

    qt, k, vt3, u, gates = _in_proj(x, tabq_t, tabk, g_mix, w_in_ext, g_q, wq_t, g_kv, wk_ext, wv_t,
                                    _kpe_placement(), b_gate, tm=tm)

    ot = jnp.concatenate(
        [_attention(qt, k, vt3, seq_start=0, seq_len=s1, n_seq=b1, tq=_pick(s1, 512), tk=tm),
         _attention(qt, k, vt3, seq_start=t1, seq_len=s2, n_seq=b2, tq=_pick(s2, 512), tk=tm)], axis=1)

    w_r = jnp.concatenate([w_router[0], jnp.zeros((d, LANES - N_EXPERTS), F32)], axis=1)
    b_r = jnp.concatenate([b_router[0], jnp.full((LANES - N_EXPERTS,), NEG_BIG, F32)])[None, :]
    ltri = jnp.asarray(np.tril(np.ones((tm, tm), np.float32), -1), BF16)
    x1, route, cnt = _post_mix(x, ot, u, gates, w_oa[0].astype(BF16), conv_w[0], conv_b, ln_g, ln_b,
                               w_pb[0].astype(BF16), b_pb, w_out[0].astype(BF16), g_ffn, w_r, b_r, ltri,
                               tm=tm, seq_bounds=seq_bounds)

    bm = _pick(t * TOP_K, 256)
    counts = cnt[0, :N_EXPERTS].astype(I32)
    blk, ex, lo, hi, first, start = _moe_items(counts, bm=bm, n_blocks=t * TOP_K // bm)
    e_idx = route[:, :TOP_K].astype(I32)
    rank = route[:, 2 * TOP_K:3 * TOP_K].astype(I32)
    dest = jnp.sum(jnp.where(e_idx[..., None] == jnp.arange(N_EXPERTS, dtype=I32), start, 0), axis=-1) + rank

    td = _pick(t, 256)
    xs = _dispatch(dest, x1, g_ffn, td=td)
    y = _moe((blk, ex, lo, hi, first), xs, w_gu[0].astype(BF16), b_gu[0][:, None, :], w_down[0].astype(BF16),
             b_down[0][:, None, :], bm=bm)
    out = _combine(dest, x1, route, g_final[None, :], y, tc=td)
    return out[:t1].reshape(b1, s1, d), out[t1:].reshape(b2, s2, d)
```

```python
import functools
import math

import numpy as np
import jax
import jax.numpy as jnp
from jax import lax
from jax.experimental import pallas as pl
from jax.experimental.pallas import tpu as pltpu

F32 = jnp.float32
BF16 = jnp.bfloat16
I32 = jnp.int32

D_MODEL = 1024
N_HEADS = 8
Q_RANK = 384
KV_RANK = 256
D_NOPE = 64
D_ROPE = 32
D_V = 64
D_QK = D_NOPE + D_ROPE
ROPE_THETA = 10000.0
CONV_CH = 512
CONV_WIDTH = 31
CONV_PAD = (CONV_WIDTH - 1) // 2
N_EXPERTS = 32
TOP_K = 4
D_FF = 1024
SWIGLU_ALPHA = 1.702
SWIGLU_LIMIT = 7.0
RMS_EPS = 1e-6
LN_EPS = 1e-5

LANES = 128
HALO_ROWS = 16
NEG_BIG = -1e30

C_CQ = 0
C_CKV = Q_RANK
C_KPE = Q_RANK + KV_RANK
C_CONV = C_KPE + LANES
C_GATE = C_CONV + 2 * CONV_CH
D_IN_EXT = C_GATE + 2 * D_MODEL

VMEM_LIMIT = 56 * 1024 * 1024


def _rms(x, g):
    return x * lax.rsqrt(jnp.mean(x * x, axis=-1, keepdims=True) + RMS_EPS) * g


def _dot(a, b):
    return jnp.dot(a, b, preferred_element_type=F32)


def _dot_nt(a, b):
    return lax.dot_general(a, b, (((1,), (1,)), ((), ())), preferred_element_type=F32)


def _dot_tn(a, b):
    return lax.dot_general(a, b, (((0,), (0,)), ((), ())), preferred_element_type=F32)


def _const_spec(shape):
    return pl.BlockSpec(shape, lambda *_: (0,) * len(shape))


def _in_proj_kernel(x_ref, tabq_ref, tabk_ref, gmix_ref, win_ref, gq_ref, wqt_ref, gkv_ref, wk_ref, wvt_ref,
                    e2_ref, bgate_ref, qt_ref, k_ref, vt_ref, u_ref, gates_ref):
    h = _rms(x_ref[...], gmix_ref[...]).astype(BF16)
    proj = _dot(h, win_ref[...])

    cqn = _rms(proj[:, C_CQ:C_CQ + Q_RANK], gq_ref[...]).astype(BF16)
    q2 = _dot_nt(wqt_ref[...], cqn)
    cos_q = tabq_ref[0:LANES, :]
    sin_q = tabq_ref[LANES:2 * LANES, :]
    hw = N_HEADS * LANES
    for hh in range(N_HEADS):
        qa = q2[hh * LANES:(hh + 1) * LANES, :]
        qb = q2[hw + hh * LANES:hw + (hh + 1) * LANES, :]
        qt_ref[hh * LANES:(hh + 1) * LANES, :] = (qa * cos_q + qb * sin_q).astype(BF16)

    ckvn = _rms(proj[:, C_CKV:C_CKV + KV_RANK], gkv_ref[...]).astype(BF16)
    kpe = (proj[:, C_KPE:C_KPE + LANES] * tabk_ref[...]).astype(BF16)
    k_ref[...] = (_dot(ckvn, wk_ref[...]) + _dot(kpe, e2_ref[...])).astype(BF16)
    vt_ref[0] = _dot_nt(wvt_ref[...], ckvn).astype(BF16)

    a = proj[:, C_CONV:C_CONV + CONV_CH]
    g = proj[:, C_CONV + CONV_CH:C_CONV + 2 * CONV_CH]
    u_ref[...] = a * jax.nn.sigmoid(g)
    gates_ref[...] = jax.nn.sigmoid(proj[:, C_GATE:] + bgate_ref[...]).astype(BF16)


def _in_proj(x, tabq_t, tabk, g_mix, w_in_ext, g_q, wq_t, g_kv, wk_ext, wv_t, e2, b_gate, *, tm):
    t = x.shape[0]
    nt = t // tm
    row = lambda w: pl.BlockSpec((tm, w), lambda i: (i, 0))
    col = lambda h: pl.BlockSpec((h, tm), lambda i: (0, i))
    return pl.pallas_call(
        _in_proj_kernel,
        grid=(nt,),
        in_specs=[row(D_MODEL), col(2 * LANES), row(LANES), _const_spec((1, D_MODEL)),
                  _const_spec((D_MODEL, D_IN_EXT)), _const_spec((1, Q_RANK)),
                  _const_spec((2 * N_HEADS * LANES, Q_RANK)), _const_spec((1, KV_RANK)),
                  _const_spec((KV_RANK, N_HEADS * LANES)), _const_spec((N_HEADS * D_V, KV_RANK)),
                  _const_spec((LANES, N_HEADS * LANES)), _const_spec((1, 2 * D_MODEL))],
        out_specs=(col(N_HEADS * LANES), row(N_HEADS * LANES),
                   pl.BlockSpec((1, N_HEADS * D_V, tm), lambda i: (i, 0, 0)), row(CONV_CH), row(2 * D_MODEL)),
        out_shape=(jax.ShapeDtypeStruct((N_HEADS * LANES, t), BF16),
                   jax.ShapeDtypeStruct((t, N_HEADS * LANES), BF16),
                   jax.ShapeDtypeStruct((nt, N_HEADS * D_V, tm), BF16),
                   jax.ShapeDtypeStruct((t, CONV_CH), F32),
                   jax.ShapeDtypeStruct((t, 2 * D_MODEL), BF16)),
        compiler_params=pltpu.CompilerParams(dimension_semantics=("arbitrary",), vmem_limit_bytes=VMEM_LIMIT),
        name="in_proj",
    )(x, tabq_t, tabk, g_mix, w_in_ext, g_q, wq_t, g_kv, wk_ext, wv_t, e2, b_gate)


def _attn_kernel(qt_ref, k_ref, vt_ref, o_ref, s_ref, acc_ref, m_ref, *, n_chunks, tk):
    ones = jnp.ones((16, tk), BF16)

    def scores(j, slot):
        ks = pl.multiple_of(j * tk, tk)
        s_ref[slot] = _dot(k_ref[pl.ds(ks, tk), :], qt_ref[...])

    def softmax_pv(j, slot):
        st = s_ref[slot]
        m = m_ref[...]
        m_new = jnp.maximum(m, jnp.max(st, axis=0, keepdims=True))
        p = jnp.exp2(st - m_new).astype(BF16)
        v1 = jnp.concatenate([vt_ref[j], ones], axis=0)
        acc_ref[...] = jnp.exp2(m - m_new) * acc_ref[...] + _dot(v1, p)
        m_ref[...] = m_new

    m_ref[...] = jnp.full(m_ref.shape, NEG_BIG, F32)
    acc_ref[...] = jnp.zeros(acc_ref.shape, F32)
    scores(0, 0)
    n_pairs = (n_chunks - 1) // 2

    def pair(jj, c):
        j = 2 * jj
        scores(j + 1, 1)
        softmax_pv(j, 0)
        scores(j + 2, 0)
        softmax_pv(j + 1, 1)
        return c

    lax.fori_loop(0, n_pairs, pair, 0, unroll=3 if n_pairs % 3 == 0 else 1)
    j0 = 2 * n_pairs
    if n_chunks - j0 == 2:
        scores(j0 + 1, 1)
        softmax_pv(j0, 0)
        softmax_pv(j0 + 1, 1)
    else:
        softmax_pv(j0, 0)
    acc = acc_ref[...]
    o_ref[...] = (acc[:D_V] / acc[D_V:D_V + 1]).astype(o_ref.dtype)


def _attention(qt, k, vt3, *, seq_start, seq_len, n_seq, tq, tk):
    nq = seq_len // tq
    nkc = seq_len // tk
    q0 = seq_start // tq
    s0 = seq_start // seq_len
    return pl.pallas_call(
        functools.partial(_attn_kernel, n_chunks=nkc, tk=tk),
        grid=(n_seq, N_HEADS, nq),
        in_specs=[pl.BlockSpec((LANES, tq), lambda b, h, i: (h, q0 + b * nq + i)),
                  pl.BlockSpec((seq_len, LANES), lambda b, h, i: (s0 + b, h)),
                  pl.BlockSpec((nkc, D_V, tk), lambda b, h, i: (s0 + b, h, 0))],
        out_specs=pl.BlockSpec((D_V, tq), lambda b, h, i: (h, b * nq + i)),
        out_shape=jax.ShapeDtypeStruct((N_HEADS * D_V, n_seq * seq_len), BF16),
        scratch_shapes=[pltpu.VMEM((2, tk, tq), F32), pltpu.VMEM((D_V + 16, tq), F32), pltpu.VMEM((1, tq), F32)],
        compiler_params=pltpu.CompilerParams(dimension_semantics=("arbitrary", "arbitrary", "arbitrary"),
                                             vmem_limit_bytes=VMEM_LIMIT),
        name="attention",
    )(qt, k, vt3)


def _post_mix_kernel(x_ref, ot_ref, u_ref, up_ref, un_ref, gates_ref, woa_ref, cw_ref, cb_ref, lng_ref, lnb_ref,
                     wpb_ref, bpb_ref, wout_ref, gffn_ref, wr_ref, br_ref, ltri_ref,
                     x1_ref, route_ref, cnt_ref, ext_ref, cv_ref, ph_ref, carry_ref, *, tm, seq_bounds):
    i = pl.program_id(0)
    t0 = i * tm
    is_start = functools.reduce(jnp.logical_or, [t0 == s for s, _ in seq_bounds])
    is_end = functools.reduce(jnp.logical_or, [t0 + tm == e for _, e in seq_bounds])

    ext_ref[0:HALO_ROWS, :] = jnp.where(is_start, 0.0, up_ref[...])
    ext_ref[HALO_ROWS:HALO_ROWS + tm, :] = u_ref[...]
    ext_ref[HALO_ROWS + tm:2 * HALO_ROWS + tm, :] = jnp.where(is_end, 0.0, un_ref[...])
    rc = min(tm, 64)
    sub = 8
    base = HALO_ROWS - CONV_PAD
    for c in range(tm // rc):
        acc = None
        for b in range(sub):
            part = None
            for o in range(CONV_WIDTH):
                if (o + base) % sub != b:
                    continue
                a0 = c * rc + (o + base) - b
                term = cw_ref[o:o + 1, :] * ext_ref[a0:a0 + rc + sub, :]
                part = term if part is None else part + term
            if b == 0:
                shifted = part[:rc]
            else:
                ph_ref[b] = part
                shifted = ph_ref[b, b:b + rc, :]
            acc = shifted if acc is None else acc + shifted
        cv_ref[c * rc:(c + 1) * rc, :] = acc + cb_ref[...]

    cv = cv_ref[...]
    mu = jnp.mean(cv, axis=-1, keepdims=True)
    xc = cv - mu
    y = xc * lax.rsqrt(jnp.mean(xc * xc, axis=-1, keepdims=True) + LN_EPS) * lng_ref[...] + lnb_ref[...]
    y = y * jax.nn.sigmoid(y)
    branch_b = _dot(y.astype(BF16), wpb_ref[...]) + bpb_ref[...]
    branch_a = _dot_tn(ot_ref[...], woa_ref[...])
    gates = gates_ref[...]
    mix = gates[:, :D_MODEL].astype(F32) * branch_a + gates[:, D_MODEL:].astype(F32) * branch_b
    x1 = x_ref[...] + _dot(mix.astype(BF16), wout_ref[...])
    x1_ref[...] = x1

    h2 = _rms(x1, gffn_ref[...])
    h2_hi = h2.astype(BF16)
    h2_lo = (h2 - h2_hi.astype(F32)).astype(BF16)
    hi_both = _dot(h2_hi, wr_ref[...])
    logits = hi_both[:, :LANES] + (hi_both[:, LANES:] + _dot(h2_lo, wr_ref[:, :LANES])) + br_ref[...]
    lane = lax.broadcasted_iota(I32, logits.shape, 1)
    lane_f = lane.astype(F32)
    vals, idxs, sels = [], [], []
    cur = logits
    for _ in range(TOP_K):
        mval = jnp.max(cur, axis=-1, keepdims=True)
        ik = jnp.min(jnp.where(cur == mval, lane_f, float(LANES)), axis=-1, keepdims=True)
        sel = lane_f == ik
        cur = jnp.where(sel, -jnp.inf, cur)
        vals.append(mval)
        idxs.append(ik)
        sels.append(sel)
    exps = [jnp.exp(v - vals[0]) for v in vals]
    denom = exps[0] + exps[1] + exps[2] + exps[3]

    onehot = sels[0].astype(F32) + sels[1].astype(F32) + sels[2].astype(F32) + sels[3].astype(F32)

    @pl.when(i == 0)
    def _():
        carry_ref[...] = jnp.zeros_like(carry_ref)

    base = carry_ref[...] + _dot(ltri_ref[...], onehot.astype(BF16))
    route = jnp.zeros(logits.shape, F32)
    for kk in range(TOP_K):
        rank = jnp.sum(jnp.where(sels[kk], base, 0.0), axis=-1, keepdims=True)
        route = jnp.where(lane == kk, idxs[kk], route)
        route = jnp.where(lane == TOP_K + kk, exps[kk] / denom, route)
        route = jnp.where(lane == 2 * TOP_K + kk, rank, route)
    route_ref[...] = route
    carry_ref[...] = carry_ref[...] + jnp.sum(onehot, axis=0, keepdims=True)
    cnt_ref[...] = carry_ref[...]


def _post_mix(x, ot, u, gates, w_oa, conv_w, conv_b, ln_g, ln_b, w_pb, b_pb, w_out, g_ffn, w_r, b_r, ltri, *,
              tm, seq_bounds):
    t = x.shape[0]
    nt = t // tm
    hb = tm // HALO_ROWS
    n_halo = t // HALO_ROWS
    row = lambda w: pl.BlockSpec((tm, w), lambda i: (i, 0))
    return pl.pallas_call(
        functools.partial(_post_mix_kernel, tm=tm, seq_bounds=seq_bounds),
        grid=(nt,),
        in_specs=[row(D_MODEL), pl.BlockSpec((N_HEADS * D_V, tm), lambda i: (0, i)), row(CONV_CH),
                  pl.BlockSpec((HALO_ROWS, CONV_CH), lambda i: (jnp.maximum(i * hb - 1, 0), 0)),
                  pl.BlockSpec((HALO_ROWS, CONV_CH), lambda i: (jnp.minimum((i + 1) * hb, n_halo - 1), 0)),
                  row(2 * D_MODEL), _const_spec((N_HEADS * D_V, D_MODEL)), _const_spec((CONV_WIDTH, CONV_CH)),
                  _const_spec((1, CONV_CH)), _const_spec((1, CONV_CH)), _const_spec((1, CONV_CH)),
                  _const_spec((CONV_CH, D_MODEL)), _const_spec((1, D_MODEL)), _const_spec((D_MODEL, D_MODEL)),
                  _const_spec((1, D_MODEL)), _const_spec((D_MODEL, 2 * LANES)), _const_spec((1, LANES)),
                  _const_spec((tm, tm))],
        out_specs=(row(D_MODEL), row(LANES), _const_spec((1, LANES))),
        out_shape=(jax.ShapeDtypeStruct((t, D_MODEL), F32), jax.ShapeDtypeStruct((t, LANES), F32),
                   jax.ShapeDtypeStruct((1, LANES), F32)),
        scratch_shapes=[pltpu.VMEM((tm + 2 * HALO_ROWS, CONV_CH), F32), pltpu.VMEM((tm, CONV_CH), F32),
                        pltpu.VMEM((8, min(tm, 64) + 8, CONV_CH), F32),
                        pltpu.VMEM((1, LANES), F32)],
        compiler_params=pltpu.CompilerParams(dimension_semantics=("arbitrary",), vmem_limit_bytes=VMEM_LIMIT),
        name="post_mix",
    )(x, ot, u, u, u, gates, w_oa, conv_w, conv_b, ln_g, ln_b, w_pb, b_pb, w_out, g_ffn, w_r, b_r, ltri)


def _row_copy(src, s, dst, d, sem):
    return pltpu.make_async_copy(src.at[pl.ds(s, 1), :], dst.at[pl.ds(d, 1), :], sem)


def _dispatch_kernel(dest_hbm, x1_ref, gffn_ref, xs_hbm, idx_smem, hbuf, idx_sem, row_sem, *, td):
    i = pl.program_id(0)
    idx_cp = pltpu.make_async_copy(dest_hbm.at[i], idx_smem, idx_sem)
    idx_cp.start()
    hbuf[...] = _rms(x1_ref[...], gffn_ref[...])
    idx_cp.wait()

    def issue(t, c):
        for kk in range(TOP_K):
            _row_copy(hbuf, t, xs_hbm, idx_smem[t * TOP_K + kk], row_sem).start()
        return c

    lax.fori_loop(0, td, issue, 0)
    pltpu.make_async_copy(xs_hbm.at[pl.ds(0, td * TOP_K), :], xs_hbm.at[pl.ds(0, td * TOP_K), :], row_sem).wait()


def _dispatch(dest, x1, g_ffn, *, td):
    t = x1.shape[0]
    return pl.pallas_call(
        functools.partial(_dispatch_kernel, td=td),
        grid=(t // td,),
        in_specs=[pl.BlockSpec(memory_space=pl.ANY), pl.BlockSpec((td, D_MODEL), lambda i: (i, 0)),
                  _const_spec((1, D_MODEL))],
        out_specs=pl.BlockSpec(memory_space=pl.ANY),
        out_shape=jax.ShapeDtypeStruct((t * TOP_K, D_MODEL), F32),
        scratch_shapes=[pltpu.SMEM((td * TOP_K,), I32), pltpu.VMEM((td, D_MODEL), F32),
                        pltpu.SemaphoreType.DMA, pltpu.SemaphoreType.DMA],
        compiler_params=pltpu.CompilerParams(dimension_semantics=("arbitrary",), has_side_effects=True),
        name="dispatch",
    )(dest.reshape(t // td, td * TOP_K), x1, g_ffn)


def _moe_kernel(blk_ref, exp_ref, lo_ref, hi_ref, first_ref, xs_ref, wgu_ref, bgu_ref, wd_ref, bd_ref, y_ref):
    w = pl.program_id(0)
    lo = lo_ref[w]
    hi = hi_ref[w]

    @pl.when(hi > lo)
    def _():
        x = xs_ref[...].astype(BF16)
        gu = _dot(x, wgu_ref[0]) + bgu_ref[0]
        g = jnp.minimum(gu[:, :D_FF], SWIGLU_LIMIT)
        u = jnp.clip(gu[:, D_FF:], -SWIGLU_LIMIT, SWIGLU_LIMIT)
        act = (u + 1.0) * (g * jax.nn.sigmoid(SWIGLU_ALPHA * g))
        out = _dot(act.astype(BF16), wd_ref[0]) + bd_ref[0]
        row = lax.broadcasted_iota(I32, out.shape, 0)
        mine = (row >= lo) & (row < hi)

        @pl.when(first_ref[w] == 1)
        def _():
            y_ref[...] = jnp.where(mine, out, 0.0)

        @pl.when(first_ref[w] == 0)
        def _():
            y_ref[...] = jnp.where(mine, out, y_ref[...])


def _moe(meta, xs, w_gu, b_gu, w_down, b_down, *, bm):
    n_items = meta[0].shape[0]
    rows = xs.shape[0]
    return pl.pallas_call(
        _moe_kernel,
        grid_spec=pltpu.PrefetchScalarGridSpec(
            num_scalar_prefetch=5,
            grid=(n_items,),
            in_specs=[pl.BlockSpec((bm, D_MODEL), lambda w, blk, ex, lo, hi, fi: (blk[w], 0)),
                      pl.BlockSpec((1, D_MODEL, 2 * D_FF), lambda w, blk, ex, lo, hi, fi: (ex[w], 0, 0)),
                      pl.BlockSpec((1, 1, 2 * D_FF), lambda w, blk, ex, lo, hi, fi: (ex[w], 0, 0)),
                      pl.BlockSpec((1, D_FF, D_MODEL), lambda w, blk, ex, lo, hi, fi: (ex[w], 0, 0)),
                      pl.BlockSpec((1, 1, D_MODEL), lambda w, blk, ex, lo, hi, fi: (ex[w], 0, 0))],
            out_specs=pl.BlockSpec((bm, D_MODEL), lambda w, blk, ex, lo, hi, fi: (blk[w], 0)),
        ),
        out_shape=jax.ShapeDtypeStruct((rows, D_MODEL), F32),
        compiler_params=pltpu.CompilerParams(dimension_semantics=("arbitrary",), vmem_limit_bytes=VMEM_LIMIT),
        name="moe",
    )(*meta, xs, w_gu, b_gu, w_down, b_down)


def _moe_items(counts, *, bm, n_blocks):
    end = jnp.cumsum(counts)
    start = end - counts
    first_blk = start // bm
    last_blk = jnp.maximum(end - 1, 0) // bm
    n_e = jnp.where(counts > 0, last_blk - first_blk + 1, 0)
    item_end = jnp.cumsum(n_e)
    item_off = item_end - n_e
    total = item_end[-1]
    n_items = n_blocks + N_EXPERTS
    w = jnp.arange(n_items, dtype=I32)
    w_eff = jnp.minimum(w, total - 1)
    ex = jnp.minimum(jnp.sum((item_end[None, :] <= w_eff[:, None]).astype(I32), axis=1), N_EXPERTS - 1)
    onehot = ex[:, None] == jnp.arange(N_EXPERTS, dtype=I32)[None, :]
    pick = lambda v: jnp.sum(jnp.where(onehot, v[None, :], 0), axis=1)
    blk = (pick(first_blk) + (w_eff - pick(item_off))).astype(I32)
    lo = jnp.clip(pick(start) - blk * bm, 0, bm)
    hi = jnp.clip(pick(end) - blk * bm, 0, bm)
    hi = jnp.where(w < total, hi, lo)
    first = jnp.concatenate([jnp.ones((1,), I32), (blk[1:] != blk[:-1]).astype(I32)])
    return blk, ex, lo.astype(I32), hi.astype(I32), first, start


def _combine_kernel(dest_hbm, x1_ref, route_ref, gfin_ref, y_hbm, out_ref, idx_smem, ybuf, idx_sem, row_sem, *, tc):
    i = pl.program_id(0)
    idx_cp = pltpu.make_async_copy(dest_hbm.at[i], idx_smem, idx_sem)
    idx_cp.start()
    idx_cp.wait()

    def issue(t, c):
        for kk in range(TOP_K):
            _row_copy(y_hbm, idx_smem[t * TOP_K + kk], ybuf.at[kk], t, row_sem).start()
        return c

    lax.fori_loop(0, tc, issue, 0)
    pltpu.make_async_copy(y_hbm.at[pl.ds(0, tc * TOP_K), :], y_hbm.at[pl.ds(0, tc * TOP_K), :], row_sem).wait()

    route = route_ref[...]
    acc = x1_ref[...]
    moe = route[:, TOP_K:TOP_K + 1] * ybuf[0]
    for kk in range(1, TOP_K):
        moe = moe + route[:, TOP_K + kk:TOP_K + kk + 1] * ybuf[kk]
    out_ref[...] = _rms(acc + moe, gfin_ref[...])


def _combine(dest, x1, route, g_final, y, *, tc):
    t = x1.shape[0]
    return pl.pallas_call(
        functools.partial(_combine_kernel, tc=tc),
        grid=(t // tc,),
        in_specs=[pl.BlockSpec(memory_space=pl.ANY), pl.BlockSpec((tc, D_MODEL), lambda i: (i, 0)),
                  pl.BlockSpec((tc, LANES), lambda i: (i, 0)), _const_spec((1, D_MODEL)),
                  pl.BlockSpec(memory_space=pl.ANY)],
        out_specs=pl.BlockSpec((tc, D_MODEL), lambda i: (i, 0)),
        out_shape=jax.ShapeDtypeStruct((t, D_MODEL), F32),
        scratch_shapes=[pltpu.SMEM((tc * TOP_K,), I32), pltpu.VMEM((TOP_K, tc, D_MODEL), F32),
                        pltpu.SemaphoreType.DMA, pltpu.SemaphoreType.DMA],
        compiler_params=pltpu.CompilerParams(dimension_semantics=("arbitrary",)),
        name="combine",
    )(dest.reshape(t // tc, tc * TOP_K), x1, route, g_final, y)


def _rot_cols(w):
    half = w.shape[-1] // 2
    return jnp.concatenate([-w[..., half:], w[..., :half]], axis=-1)


def _prep_weights(w_in, w_uq, w_ukv):
    d = w_in.shape[0]
    w_kpe = w_in[:, C_KPE:C_KPE + D_ROPE]
    w_in_ext = jnp.concatenate(
        [w_in[:, :C_KPE], w_kpe, _rot_cols(w_kpe), jnp.zeros((d, LANES - 2 * D_ROPE), F32),
         w_in[:, C_KPE + D_ROPE:]], axis=1).astype(BF16)

    wq = w_uq.reshape(Q_RANK, N_HEADS, D_QK)
    nope, pe = wq[..., :D_NOPE], wq[..., D_NOPE:]
    zpad = jnp.zeros((Q_RANK, N_HEADS, LANES - D_QK), F32)
    plain = jnp.concatenate([nope, pe, zpad], axis=-1).reshape(Q_RANK, N_HEADS * LANES)
    rot = jnp.concatenate([jnp.zeros_like(nope), _rot_cols(pe), zpad], axis=-1).reshape(Q_RANK, N_HEADS * LANES)
    wq_t = jnp.concatenate([plain, rot], axis=1).T.astype(BF16)

    wkv = w_ukv.reshape(KV_RANK, N_HEADS, D_NOPE + D_V)
    wk_ext = jnp.concatenate([wkv[..., :D_NOPE], jnp.zeros((KV_RANK, N_HEADS, LANES - D_NOPE), F32)],
                             axis=-1).reshape(KV_RANK, N_HEADS * LANES).astype(BF16)
    wv_t = wkv[..., D_NOPE:].reshape(KV_RANK, N_HEADS * D_V).T.astype(BF16)
    return w_in_ext, wq_t, wk_ext, wv_t


def _kpe_placement():
    e2 = np.zeros((LANES, N_HEADS * LANES), np.float32)
    for j in range(D_ROPE):
        for h in range(N_HEADS):
            e2[j, h * LANES + D_NOPE + j] = 1.0
            e2[D_ROPE + j, h * LANES + D_NOPE + j] = 1.0
    return jnp.asarray(e2, BF16)


def _rope_tables(pos):
    inv = 1.0 / (ROPE_THETA ** (jnp.arange(0, D_ROPE, 2, dtype=F32) / D_ROPE))
    ang = pos.astype(F32)[:, None] * inv[None, :]
    cos, sin = jnp.cos(ang), jnp.sin(ang)
    t = pos.shape[0]
    scale = (D_QK ** -0.5) * math.log2(math.e)
    cos_q = jnp.concatenate([jnp.ones((t, D_NOPE), F32), cos, cos, jnp.zeros((t, LANES - D_QK), F32)], axis=1)
    sin_q = jnp.concatenate([jnp.zeros((t, D_NOPE), F32), sin, sin, jnp.zeros((t, LANES - D_QK), F32)], axis=1)
    tabq_t = (jnp.concatenate([cos_q, sin_q], axis=1) * scale).T
    tabk = jnp.concatenate([cos, cos, sin, sin, jnp.zeros((t, LANES - 2 * D_ROPE), F32)], axis=1)
    return tabq_t, tabk


def _pick(n, pref):
    return pref if n % pref == 0 else n


def kernel(x_prompt, x_sample, g_mix, w_in, b_gate, g_q, w_uq, g_kv, w_ukv, w_oa, conv_w, conv_b, ln_g, ln_b, w_pb,
           b_pb, w_out, g_ffn, w_router, b_router, w_gu, b_gu, w_down, b_down, g_final):
    b1, s1, d = x_prompt.shape
    b2, s2, _ = x_sample.shape
    t1, t2 = b1 * s1, b2 * s2
    t = t1 + t2
    x = jnp.concatenate([x_prompt.reshape(t1, d), x_sample.reshape(t2, d)], axis=0)
    seq_bounds = tuple((b * s1, (b + 1) * s1) for b in range(b1)) + tuple(
        (t1 + b * s2, t1 + (b + 1) * s2) for b in range(b2))

    tm = _pick(math.gcd(s1, s2), 512)
    pos = jnp.concatenate([jnp.tile(jnp.arange(s1, dtype=I32), b1), jnp.tile(jnp.arange(s2, dtype=I32), b2)])
    tabq_t, tabk = _rope_tables(pos)
    w_in_ext, wq_t, wk_ext, wv_t = _prep_weights(w_in[0], w_uq[0], w_ukv[0])

    qt, k, vt3, u, gates = _in_proj(x, tabq_t, tabk, g_mix, w_in_ext, g_q, wq_t, g_kv, wk_ext, wv_t,
                                    _kpe_placement(), b_gate, tm=tm)

    ot = jnp.concatenate(
        [_attention(qt, k, vt3, seq_start=0, seq_len=s1, n_seq=b1, tq=_pick(s1, 512), tk=tm),
         _attention(qt, k, vt3, seq_start=t1, seq_len=s2, n_seq=b2, tq=_pick(s2, 512), tk=tm)], axis=1)

    w_r = jnp.concatenate([w_router[0], jnp.zeros((d, LANES - N_EXPERTS), F32)], axis=1)
    w_r_hi = w_r.astype(BF16)
    w_r = jnp.concatenate([w_r_hi, (w_r - w_r_hi.astype(F32)).astype(BF16)], axis=1)
    b_r = jnp.concatenate([b_router[0], jnp.full((LANES - N_EXPERTS,), NEG_BIG, F32)])[None, :]
    ltri = jnp.asarray(np.tril(np.ones((tm, tm), np.float32), -1), BF16)
    x1, route, cnt = _post_mix(x, ot, u, gates, w_oa[0].astype(BF16), conv_w[0], conv_b, ln_g, ln_b,
                               w_pb[0].astype(BF16), b_pb, w_out[0].astype(BF16), g_ffn, w_r, b_r, ltri,
                               tm=tm, seq_bounds=seq_bounds)

    bm = _pick(t * TOP_K, 256)
    counts = cnt[0, :N_EXPERTS].astype(I32)
    blk, ex, lo, hi, first, start = _moe_items(counts, bm=bm, n_blocks=t * TOP_K // bm)
    e_idx = route[:, :TOP_K].astype(I32)
    rank = route[:, 2 * TOP_K:3 * TOP_K].astype(I32)
    dest = jnp.sum(jnp.where(e_idx[..., None] == jnp.arange(N_EXPERTS, dtype=I32), start, 0), axis=-1) + rank

    td = _pick(t, 256)
    xs = _dispatch(dest, x1, g_ffn, td=td)
    y = _moe((blk, ex, lo, hi, first), xs, w_gu[0].astype(BF16), b_gu[0][:, None, :], w_down[0].astype(BF16),
             b_down[0][:, None, :], bm=bm)
    out = _combine(dest, x1, route, g_final[None, :], y, tc=td)
    return out[:t1].reshape(b1, s1, d), out[t1:].reshape(b2, s2, d)
```

```python
import functools
import math

import numpy as np
import jax
import jax.numpy as jnp
from jax import lax
from jax.experimental import pallas as pl
from jax.experimental.pallas import tpu as pltpu

F32 = jnp.float32
BF16 = jnp.bfloat16
I32 = jnp.int32

D_MODEL = 1024
N_HEADS = 8
Q_RANK = 384
KV_RANK = 256
D_NOPE = 64
D_ROPE = 32
D_V = 64
D_QK = D_NOPE + D_ROPE
ROPE_THETA = 10000.0
CONV_CH = 512
CONV_WIDTH = 31
CONV_PAD = (CONV_WIDTH - 1) // 2
N_EXPERTS = 32
TOP_K = 4
D_FF = 1024
SWIGLU_ALPHA = 1.702
SWIGLU_LIMIT = 7.0
RMS_EPS = 1e-6
LN_EPS = 1e-5

LANES = 128
HALO_ROWS = 16
NEG_BIG = -1e30
RUN_PAD = 8
MOE_ROWS = 256

C_CQ = 0
C_CKV = Q_RANK
C_KPE = Q_RANK + KV_RANK
C_CONV = C_KPE + LANES
C_GATE = C_CONV + 2 * CONV_CH
D_IN_EXT = C_GATE + 2 * D_MODEL

VMEM_LIMIT = 56 * 1024 * 1024


def _rms(x, g):
    return x * lax.rsqrt(jnp.mean(x * x, axis=-1, keepdims=True) + RMS_EPS) * g


def _dot(a, b):
    return jnp.dot(a, b, preferred_element_type=F32)


def _dot_nt(a, b):
    return lax.dot_general(a, b, (((1,), (1,)), ((), ())), preferred_element_type=F32)


def _dot_tn(a, b):
    return lax.dot_general(a, b, (((0,), (0,)), ((), ())), preferred_element_type=F32)


def _const_spec(shape):
    return pl.BlockSpec(shape, lambda *_: (0,) * len(shape))


def _in_proj_kernel(x_ref, tabq_ref, tabk_ref, gmix_ref, win_ref, gq_ref, wqt_ref, gkv_ref, wk_ref, wvt_ref,
                    e2_ref, bgate_ref, qt_ref, k_ref, vt_ref, u_ref, gates_ref):
    h = _rms(x_ref[...], gmix_ref[...]).astype(BF16)
    proj = _dot(h, win_ref[...])

    cqn = _rms(proj[:, C_CQ:C_CQ + Q_RANK], gq_ref[...]).astype(BF16)
    q2 = _dot_nt(wqt_ref[...], cqn)
    cos_q = tabq_ref[0:LANES, :]
    sin_q = tabq_ref[LANES:2 * LANES, :]
    hw = N_HEADS * LANES
    for hh in range(N_HEADS):
        qa = q2[hh * LANES:(hh + 1) * LANES, :]
        qb = q2[hw + hh * LANES:hw + (hh + 1) * LANES, :]
        qt_ref[hh * LANES:(hh + 1) * LANES, :] = (qa * cos_q + qb * sin_q).astype(BF16)

    ckvn = _rms(proj[:, C_CKV:C_CKV + KV_RANK], gkv_ref[...]).astype(BF16)
    kpe = (proj[:, C_KPE:C_KPE + LANES] * tabk_ref[...]).astype(BF16)
    k_ref[...] = (_dot(ckvn, wk_ref[...]) + _dot(kpe, e2_ref[...])).astype(BF16)
    vt_ref[0] = _dot_nt(wvt_ref[...], ckvn).astype(BF16)

    a = proj[:, C_CONV:C_CONV + CONV_CH]
    g = proj[:, C_CONV + CONV_CH:C_CONV + 2 * CONV_CH]
    u_ref[...] = a * jax.nn.sigmoid(g)
    gates_ref[...] = jax.nn.sigmoid(proj[:, C_GATE:] + bgate_ref[...]).astype(BF16)


def _in_proj(x, tabq_t, tabk, g_mix, w_in_ext, g_q, wq_t, g_kv, wk_ext, wv_t, e2, b_gate, *, tm):
    t = x.shape[0]
    nt = t // tm
    row = lambda w: pl.BlockSpec((tm, w), lambda i: (i, 0))
    col = lambda h: pl.BlockSpec((h, tm), lambda i: (0, i))
    return pl.pallas_call(
        _in_proj_kernel,
        grid=(nt,),
        in_specs=[row(D_MODEL), col(2 * LANES), row(LANES), _const_spec((1, D_MODEL)),
                  _const_spec((D_MODEL, D_IN_EXT)), _const_spec((1, Q_RANK)),
                  _const_spec((2 * N_HEADS * LANES, Q_RANK)), _const_spec((1, KV_RANK)),
                  _const_spec((KV_RANK, N_HEADS * LANES)), _const_spec((N_HEADS * D_V, KV_RANK)),
                  _const_spec((LANES, N_HEADS * LANES)), _const_spec((1, 2 * D_MODEL))],
        out_specs=(col(N_HEADS * LANES), row(N_HEADS * LANES),
                   pl.BlockSpec((1, N_HEADS * D_V, tm), lambda i: (i, 0, 0)), row(CONV_CH), row(2 * D_MODEL)),
        out_shape=(jax.ShapeDtypeStruct((N_HEADS * LANES, t), BF16),
                   jax.ShapeDtypeStruct((t, N_HEADS * LANES), BF16),
                   jax.ShapeDtypeStruct((nt, N_HEADS * D_V, tm), BF16),
                   jax.ShapeDtypeStruct((t, CONV_CH), F32),
                   jax.ShapeDtypeStruct((t, 2 * D_MODEL), BF16)),
        compiler_params=pltpu.CompilerParams(dimension_semantics=("arbitrary",), vmem_limit_bytes=VMEM_LIMIT),
        name="in_proj",
    )(x, tabq_t, tabk, g_mix, w_in_ext, g_q, wq_t, g_kv, wk_ext, wv_t, e2, b_gate)


def _attn_kernel(qt_ref, k_ref, vt_ref, o_ref, s_ref, acc_ref, m_ref, *, n_chunks, tk):
    ones = jnp.ones((16, tk), BF16)

    def scores(j, slot):
        ks = pl.multiple_of(j * tk, tk)
        s_ref[slot] = _dot(k_ref[pl.ds(ks, tk), :], qt_ref[...])

    def softmax_pv(j, slot):
        st = s_ref[slot]
        m = m_ref[...]
        m_new = jnp.maximum(m, jnp.max(st, axis=0, keepdims=True))
        p = jnp.exp2(st - m_new).astype(BF16)
        v1 = jnp.concatenate([vt_ref[j], ones], axis=0)
        acc_ref[...] = jnp.exp2(m - m_new) * acc_ref[...] + _dot(v1, p)
        m_ref[...] = m_new

    m_ref[...] = jnp.full(m_ref.shape, NEG_BIG, F32)
    acc_ref[...] = jnp.zeros(acc_ref.shape, F32)
    scores(0, 0)
    n_pairs = (n_chunks - 1) // 2

    def pair(jj, c):
        j = 2 * jj
        scores(j + 1, 1)
        softmax_pv(j, 0)
        scores(j + 2, 0)
        softmax_pv(j + 1, 1)
        return c

    lax.fori_loop(0, n_pairs, pair, 0, unroll=3 if n_pairs % 3 == 0 else 1)
    j0 = 2 * n_pairs
    if n_chunks - j0 == 2:
        scores(j0 + 1, 1)
        softmax_pv(j0, 0)
        softmax_pv(j0 + 1, 1)
    else:
        softmax_pv(j0, 0)
    acc = acc_ref[...]
    o_ref[...] = (acc[:D_V] / acc[D_V:D_V + 1]).astype(o_ref.dtype)


def _attention(qt, k, vt3, *, seq_start, seq_len, n_seq, tq, tk):
    nq = seq_len // tq
    nkc = seq_len // tk
    q0 = seq_start // tq
    s0 = seq_start // seq_len
    return pl.pallas_call(
        functools.partial(_attn_kernel, n_chunks=nkc, tk=tk),
        grid=(n_seq, N_HEADS, nq),
        in_specs=[pl.BlockSpec((LANES, tq), lambda b, h, i: (h, q0 + b * nq + i)),
                  pl.BlockSpec((seq_len, LANES), lambda b, h, i: (s0 + b, h)),
                  pl.BlockSpec((nkc, D_V, tk), lambda b, h, i: (s0 + b, h, 0))],
        out_specs=pl.BlockSpec((D_V, tq), lambda b, h, i: (h, b * nq + i)),
        out_shape=jax.ShapeDtypeStruct((N_HEADS * D_V, n_seq * seq_len), BF16),
        scratch_shapes=[pltpu.VMEM((2, tk, tq), F32), pltpu.VMEM((D_V + 16, tq), F32), pltpu.VMEM((1, tq), F32)],
        compiler_params=pltpu.CompilerParams(dimension_semantics=("arbitrary", "arbitrary", "arbitrary"),
                                             vmem_limit_bytes=VMEM_LIMIT),
        name="attention",
    )(qt, k, vt3)


def _local_rows(tm):
    return TOP_K * tm + N_EXPERTS * RUN_PAD


def _sorted_rows(t, nt, bm):
    runs = t * TOP_K + nt * N_EXPERTS * (RUN_PAD - 1)
    return -(-runs // bm) * bm + N_EXPERTS * bm


def _rows_copy(src, s, dst, d, sem, rows=RUN_PAD):
    s = pl.multiple_of(s, RUN_PAD)
    d = pl.multiple_of(d, RUN_PAD)
    return pltpu.make_async_copy(src.at[pl.ds(s, rows), :], dst.at[pl.ds(d, rows), :], sem)


def _for_each_piece(n_ref, a_ref, b_ref, tile, fn):
    def per_expert(e, c):
        n = n_ref[tile * N_EXPERTS + e]
        a0 = a_ref[tile * N_EXPERTS + e]
        b0 = b_ref[tile * N_EXPERTS + e]

        def piece(q, c2):
            fn(a0 + q * RUN_PAD, b0 + q * RUN_PAD, e)
            return c2

        lax.fori_loop(0, n, piece, 0)
        return c + n

    return lax.fori_loop(0, N_EXPERTS, per_expert, 0)


def _tile_pieces(n_ref, tile):
    return lax.fori_loop(0, N_EXPERTS, lambda e, c: c + n_ref[tile * N_EXPERTS + e], 0)


def _post_mix_kernel(x_ref, ot_ref, u_ref, up_ref, un_ref, gates_ref, woa_ref, cw_ref, cb_ref, lng_ref, lnb_ref,
                     wpb_ref, bpb_ref, wout_ref, gffn_ref, wr_ref, br_ref, ltri_ref, utri_ref,
                     x1_ref, route_ref, meta_ref, ext_ref, cv_ref, ph_ref, carry_ref, *, tm, seq_bounds):
    i = pl.program_id(0)
    t0 = i * tm
    is_start = functools.reduce(jnp.logical_or, [t0 == s for s, _ in seq_bounds])
    is_end = functools.reduce(jnp.logical_or, [t0 + tm == e for _, e in seq_bounds])

    ext_ref[0:HALO_ROWS, :] = jnp.where(is_start, 0.0, up_ref[...])
    ext_ref[HALO_ROWS:HALO_ROWS + tm, :] = u_ref[...]
    ext_ref[HALO_ROWS + tm:2 * HALO_ROWS + tm, :] = jnp.where(is_end, 0.0, un_ref[...])
    rc = min(tm, 64)
    sub = 8
    base = HALO_ROWS - CONV_PAD
    for c in range(tm // rc):
        acc = None
        for b in range(sub):
            part = None
            for o in range(CONV_WIDTH):
                if (o + base) % sub != b:
                    continue
                a0 = c * rc + (o + base) - b
                term = cw_ref[o:o + 1, :] * ext_ref[a0:a0 + rc + sub, :]
                part = term if part is None else part + term
            if b == 0:
                shifted = part[:rc]
            else:
                ph_ref[b] = part
                shifted = ph_ref[b, b:b + rc, :]
            acc = shifted if acc is None else acc + shifted
        cv_ref[c * rc:(c + 1) * rc, :] = acc + cb_ref[...]

    cv = cv_ref[...]
    mu = jnp.mean(cv, axis=-1, keepdims=True)
    xc = cv - mu
    y = xc * lax.rsqrt(jnp.mean(xc * xc, axis=-1, keepdims=True) + LN_EPS) * lng_ref[...] + lnb_ref[...]
    y = y * jax.nn.sigmoid(y)
    branch_b = _dot(y.astype(BF16), wpb_ref[...]) + bpb_ref[...]
    branch_a = _dot_tn(ot_ref[...], woa_ref[...])
    gates = gates_ref[...]
    mix = gates[:, :D_MODEL].astype(F32) * branch_a + gates[:, D_MODEL:].astype(F32) * branch_b
    x1 = x_ref[...] + _dot(mix.astype(BF16), wout_ref[...])
    x1_ref[...] = x1

    h2 = _rms(x1, gffn_ref[...])
    h2_hi = h2.astype(BF16)
    h2_lo = (h2 - h2_hi.astype(F32)).astype(BF16)
    hi_both = _dot(h2_hi, wr_ref[...])
    logits = hi_both[:, :LANES] + (hi_both[:, LANES:] + _dot(h2_lo, wr_ref[:, :LANES])) + br_ref[...]
    lane = lax.broadcasted_iota(I32, logits.shape, 1)
    lane_f = lane.astype(F32)
    vals, idxs, sels = [], [], []
    cur = logits
    for _ in range(TOP_K):
        mval = jnp.max(cur, axis=-1, keepdims=True)
        ik = jnp.min(jnp.where(cur == mval, lane_f, float(LANES)), axis=-1, keepdims=True)
        sel = lane_f == ik
        cur = jnp.where(sel, -jnp.inf, cur)
        vals.append(mval)
        idxs.append(ik)
        sels.append(sel)
    exps = [jnp.exp(v - vals[0]) for v in vals]
    denom = exps[0] + exps[1] + exps[2] + exps[3]

    onehot = sels[0].astype(F32) + sels[1].astype(F32) + sels[2].astype(F32) + sels[3].astype(F32)

    @pl.when(i == 0)
    def _():
        carry_ref[...] = jnp.zeros_like(carry_ref)

    n_e = jnp.sum(onehot, axis=0, keepdims=True)
    n_pieces = jnp.floor((n_e + (RUN_PAD - 1)) * (1.0 / RUN_PAD))
    pieces_b = jnp.broadcast_to(n_pieces, (8, LANES)).astype(BF16)
    off = _dot(pieces_b, utri_ref[...])[0:1] * RUN_PAD
    total_pieces = jnp.sum(n_pieces, axis=-1, keepdims=True)
    carry = carry_ref[...]
    prefix = _dot(ltri_ref[...], onehot.astype(BF16))
    local = off + prefix
    route = jnp.zeros(logits.shape, F32)
    for kk in range(TOP_K):
        pos = jnp.sum(jnp.where(sels[kk], local, 0.0), axis=-1, keepdims=True)
        route = jnp.where(lane == kk, idxs[kk], route)
        route = jnp.where(lane == TOP_K + kk, exps[kk] / denom, route)
        route = jnp.where(lane == 2 * TOP_K + kk, pos, route)
    route_ref[...] = route
    carry_ref[...] = carry + n_pieces * RUN_PAD

    row8 = lax.broadcasted_iota(I32, (8, LANES), 0)
    metav = jnp.where(row8 == 0, jnp.broadcast_to(carry, (8, LANES)), 0.0)
    metav = jnp.where(row8 == 1, jnp.broadcast_to(n_pieces, (8, LANES)), metav)
    metav = jnp.where(row8 == 2, jnp.broadcast_to(off, (8, LANES)), metav)
    metav = jnp.where(row8 == 3, jnp.broadcast_to(total_pieces, (8, LANES)), metav)
    meta_ref[0] = metav.astype(I32)


def _post_mix(x, ot, u, gates, w_oa, conv_w, conv_b, ln_g, ln_b, w_pb, b_pb, w_out, g_ffn, w_r, b_r, ltri, utri, *,
              tm, seq_bounds):
    t = x.shape[0]
    nt = t // tm
    hb = tm // HALO_ROWS
    n_halo = t // HALO_ROWS
    row = lambda w: pl.BlockSpec((tm, w), lambda i: (i, 0))
    return pl.pallas_call(
        functools.partial(_post_mix_kernel, tm=tm, seq_bounds=seq_bounds),
        grid=(nt,),
        in_specs=[row(D_MODEL), pl.BlockSpec((N_HEADS * D_V, tm), lambda i: (0, i)), row(CONV_CH),
                  pl.BlockSpec((HALO_ROWS, CONV_CH), lambda i: (jnp.maximum(i * hb - 1, 0), 0)),
                  pl.BlockSpec((HALO_ROWS, CONV_CH), lambda i: (jnp.minimum((i + 1) * hb, n_halo - 1), 0)),
                  row(2 * D_MODEL), _const_spec((N_HEADS * D_V, D_MODEL)), _const_spec((CONV_WIDTH, CONV_CH)),
                  _const_spec((1, CONV_CH)), _const_spec((1, CONV_CH)), _const_spec((1, CONV_CH)),
                  _const_spec((CONV_CH, D_MODEL)), _const_spec((1, D_MODEL)), _const_spec((D_MODEL, D_MODEL)),
                  _const_spec((1, D_MODEL)), _const_spec((D_MODEL, 2 * LANES)), _const_spec((1, LANES)),
                  _const_spec((tm, tm)), _const_spec((LANES, LANES))],
        out_specs=(row(D_MODEL), row(LANES), pl.BlockSpec((1, 8, LANES), lambda i: (i, 0, 0))),
        out_shape=(jax.ShapeDtypeStruct((t, D_MODEL), F32), jax.ShapeDtypeStruct((t, LANES), F32),
                   jax.ShapeDtypeStruct((nt, 8, LANES), I32)),
        scratch_shapes=[pltpu.VMEM((tm + 2 * HALO_ROWS, CONV_CH), F32), pltpu.VMEM((tm, CONV_CH), F32),
                        pltpu.VMEM((8, min(tm, 64) + 8, CONV_CH), F32), pltpu.VMEM((1, LANES), F32)],
        compiler_params=pltpu.CompilerParams(dimension_semantics=("arbitrary",), vmem_limit_bytes=VMEM_LIMIT),
        name="post_mix",
    )(x, ot, u, u, u, gates, w_oa, conv_w, conv_b, ln_g, ln_b, w_pb, b_pb, w_out, g_ffn, w_r, b_r, ltri, utri)


def _dispatch_kernel(npc_s, off_s, dst_s, start_s, x1_ref, route_ref, gffn_ref, xs_hbm, xloc_ref, zero_ref, row_sem,
                     *, tm, nt, bm, n_rows):
    i = pl.program_id(0)
    slot = lax.rem(i, 2)
    n_loc = _local_rows(tm)

    h2 = _rms(x1_ref[...], gffn_ref[...]).astype(BF16)
    pos_t = jnp.transpose(route_ref[...])[2 * TOP_K:3 * TOP_K, :].astype(I32)
    prow = lax.broadcasted_iota(I32, (n_loc, tm), 0)
    perm = jnp.zeros((n_loc, tm), F32)
    for kk in range(TOP_K):
        perm = jnp.where(prow == pos_t[kk:kk + 1, :], 1.0, perm)
    xloc_ref[slot] = _dot(perm.astype(BF16), h2)

    def wait_pieces(n):
        def one(q, c):
            _rows_copy(xloc_ref.at[0], 0, xs_hbm, 0, row_sem).wait()
            return c
        lax.fori_loop(0, n, one, 0)

    @pl.when(i > 0)
    def _():
        wait_pieces(_tile_pieces(npc_s, i - 1))

    _for_each_piece(npc_s, off_s, dst_s, i,
                    lambda s, d, e: _rows_copy(xloc_ref.at[slot], s, xs_hbm, d, row_sem).start())

    @pl.when(i == nt - 1)
    def _():
        wait_pieces(_tile_pieces(npc_s, i))
        zero_ref[...] = jnp.zeros(zero_ref.shape, F32)

        def tail(e, c):
            end = dst_s[i * N_EXPERTS + e] + npc_s[i * N_EXPERTS + e] * RUN_PAD
            n = lax.shift_right_logical(start_s[e + 1] - end, int(math.log2(RUN_PAD)))

            def piece(q, c2):
                _rows_copy(zero_ref, 0, xs_hbm, end + q * RUN_PAD, row_sem).start()
                return c2

            lax.fori_loop(0, n, piece, 0)
            return c + n

        wait_pieces(lax.fori_loop(0, N_EXPERTS, tail, 0))
        used = start_s[N_EXPERTS]
        n_free = (n_rows - used) // bm

        def fill(b, c):
            _rows_copy(zero_ref, 0, xs_hbm, used + b * bm, row_sem, rows=bm).start()
            return c

        def fill_wait(b, c):
            _rows_copy(zero_ref, 0, xs_hbm, 0, row_sem, rows=bm).wait()
            return c

        lax.fori_loop(0, n_free, fill, 0)
        lax.fori_loop(0, n_free, fill_wait, 0)


def _dispatch(n_pieces, off, dst, start, x1, route, g_ffn, *, tm, bm, n_rows):
    t = x1.shape[0]
    nt = t // tm
    return pl.pallas_call(
        functools.partial(_dispatch_kernel, tm=tm, nt=nt, bm=bm, n_rows=n_rows),
        grid_spec=pltpu.PrefetchScalarGridSpec(
            num_scalar_prefetch=4,
            grid=(nt,),
            in_specs=[pl.BlockSpec((tm, D_MODEL), lambda i, *_: (i, 0)), pl.BlockSpec((tm, LANES), lambda i, *_: (i, 0)),
                      pl.BlockSpec((1, D_MODEL), lambda i, *_: (0, 0))],
            out_specs=pl.BlockSpec(memory_space=pl.ANY),
            scratch_shapes=[pltpu.VMEM((2, _local_rows(tm), D_MODEL), F32), pltpu.VMEM((bm, D_MODEL), F32),
                            pltpu.SemaphoreType.DMA],
        ),
        out_shape=jax.ShapeDtypeStruct((n_rows, D_MODEL), F32),
        compiler_params=pltpu.CompilerParams(dimension_semantics=("arbitrary",), vmem_limit_bytes=VMEM_LIMIT,
                                             has_side_effects=True),
        name="dispatch",
    )(n_pieces, off, dst, start, x1, route, g_ffn)


def _moe_kernel(exp_ref, n_ref, xs_ref, wgu_ref, bgu_ref, wd_ref, bd_ref, y_ref):
    used = pl.program_id(0) < n_ref[0]

    @pl.when(used)
    def _():
        x = xs_ref[...].astype(BF16)
        gu = _dot(x, wgu_ref[0]) + bgu_ref[0]
        g = jnp.minimum(gu[:, :D_FF], SWIGLU_LIMIT)
        u = jnp.clip(gu[:, D_FF:], -SWIGLU_LIMIT, SWIGLU_LIMIT)
        act = (u + 1.0) * (g * jax.nn.sigmoid(SWIGLU_ALPHA * g))
        y_ref[...] = _dot(act.astype(BF16), wd_ref[0]) + bd_ref[0]

    @pl.when(jnp.logical_not(used))
    def _():
        y_ref[...] = jnp.zeros(y_ref.shape, F32)


def _moe(ex, n_used, xs, w_gu, b_gu, w_down, b_down, *, bm):
    return pl.pallas_call(
        _moe_kernel,
        grid_spec=pltpu.PrefetchScalarGridSpec(
            num_scalar_prefetch=2,
            grid=(xs.shape[0] // bm,),
            in_specs=[pl.BlockSpec((bm, D_MODEL), lambda w, ex, n: (w, 0)),
                      pl.BlockSpec((1, D_MODEL, 2 * D_FF), lambda w, ex, n: (ex[w], 0, 0)),
                      pl.BlockSpec((1, 1, 2 * D_FF), lambda w, ex, n: (ex[w], 0, 0)),
                      pl.BlockSpec((1, D_FF, D_MODEL), lambda w, ex, n: (ex[w], 0, 0)),
                      pl.BlockSpec((1, 1, D_MODEL), lambda w, ex, n: (ex[w], 0, 0))],
            out_specs=pl.BlockSpec((bm, D_MODEL), lambda w, ex, n: (w, 0)),
        ),
        out_shape=jax.ShapeDtypeStruct(xs.shape, F32),
        compiler_params=pltpu.CompilerParams(dimension_semantics=("arbitrary",), vmem_limit_bytes=VMEM_LIMIT),
        name="moe",
    )(ex, n_used, xs, w_gu, b_gu, w_down, b_down)


def _expert_layout(totals, *, bm, n_blocks):
    n_blk = (totals + bm - 1) // bm
    blk_end = jnp.cumsum(n_blk)
    start = jnp.concatenate([jnp.zeros((1,), I32), blk_end * bm]).astype(I32)
    w = jnp.arange(n_blocks, dtype=I32)
    ex = jnp.minimum(jnp.sum((blk_end[None, :] <= w[:, None]).astype(I32), axis=1), N_EXPERTS - 1)
    return start, ex.astype(I32), blk_end[-1].astype(I32)[None]


def _combine_kernel(npc_s, src_s, off_s, x1_ref, route_ref, gfin_ref, y_hbm, out_ref, yloc_ref, sems, *, tm, nt):
    i = pl.program_id(0)
    slot = lax.rem(i, 2)
    n_loc = _local_rows(tm)

    def gather(tile, sl):
        _for_each_piece(npc_s, src_s, off_s, tile,
                        lambda s, d, e: _rows_copy(y_hbm, s, yloc_ref.at[sl], d, sems.at[sl]).start())

    @pl.when(i == 0)
    def _():
        yloc_ref[...] = jnp.zeros(yloc_ref.shape, F32)
        gather(0, 0)

    @pl.when(i + 1 < nt)
    def _():
        gather(i + 1, 1 - slot)

    def wait_one(q, c):
        _rows_copy(y_hbm, 0, yloc_ref.at[slot], 0, sems.at[slot]).wait()
        return c

    lax.fori_loop(0, _tile_pieces(npc_s, i), wait_one, 0)

    route = route_ref[...]
    lanes = lax.broadcasted_iota(I32, (tm, n_loc), 1)
    pw = jnp.zeros((tm, n_loc), F32)
    for kk in range(TOP_K):
        pos = route[:, 2 * TOP_K + kk:2 * TOP_K + kk + 1].astype(I32)
        pw = jnp.where(lanes == pos, route[:, TOP_K + kk:TOP_K + kk + 1], pw)
    pw_hi = pw.astype(BF16)
    pw_lo = (pw - pw_hi.astype(F32)).astype(BF16)
    yb = yloc_ref[slot].astype(BF16)
    moe = _dot(pw_hi, yb) + _dot(pw_lo, yb)
    out_ref[...] = _rms(x1_ref[...] + moe, gfin_ref[...])


def _combine(n_pieces, src, off, x1, route, g_final, y, *, tm):
    t = x1.shape[0]
    nt = t // tm
    return pl.pallas_call(
        functools.partial(_combine_kernel, tm=tm, nt=nt),
        grid_spec=pltpu.PrefetchScalarGridSpec(
            num_scalar_prefetch=3,
            grid=(nt,),
            in_specs=[pl.BlockSpec((tm, D_MODEL), lambda i, *_: (i, 0)), pl.BlockSpec((tm, LANES), lambda i, *_: (i, 0)),
                      pl.BlockSpec((1, D_MODEL), lambda i, *_: (0, 0)), pl.BlockSpec(memory_space=pl.ANY)],
            out_specs=pl.BlockSpec((tm, D_MODEL), lambda i, *_: (i, 0)),
            scratch_shapes=[pltpu.VMEM((2, _local_rows(tm), D_MODEL), F32), pltpu.SemaphoreType.DMA((2,))],
        ),
        out_shape=jax.ShapeDtypeStruct((t, D_MODEL), F32),
        compiler_params=pltpu.CompilerParams(dimension_semantics=("arbitrary",), vmem_limit_bytes=VMEM_LIMIT),
        name="combine",
    )(n_pieces, src, off, x1, route, g_final, y)


def _rot_cols(w):
    half = w.shape[-1] // 2
    return jnp.concatenate([-w[..., half:], w[..., :half]], axis=-1)


def _prep_weights(w_in, w_uq, w_ukv):
    d = w_in.shape[0]
    w_kpe = w_in[:, C_KPE:C_KPE + D_ROPE]
    w_in_ext = jnp.concatenate(
        [w_in[:, :C_KPE], w_kpe, _rot_cols(w_kpe), jnp.zeros((d, LANES - 2 * D_ROPE), F32),
         w_in[:, C_KPE + D_ROPE:]], axis=1).astype(BF16)

    wq = w_uq.reshape(Q_RANK, N_HEADS, D_QK)
    nope, pe = wq[..., :D_NOPE], wq[..., D_NOPE:]
    zpad = jnp.zeros((Q_RANK, N_HEADS, LANES - D_QK), F32)
    plain = jnp.concatenate([nope, pe, zpad], axis=-1).reshape(Q_RANK, N_HEADS * LANES)
    rot = jnp.concatenate([jnp.zeros_like(nope), _rot_cols(pe), zpad], axis=-1).reshape(Q_RANK, N_HEADS * LANES)
    wq_t = jnp.concatenate([plain, rot], axis=1).T.astype(BF16)

    wkv = w_ukv.reshape(KV_RANK, N_HEADS, D_NOPE + D_V)
    wk_ext = jnp.concatenate([wkv[..., :D_NOPE], jnp.zeros((KV_RANK, N_HEADS, LANES - D_NOPE), F32)],
                             axis=-1).reshape(KV_RANK, N_HEADS * LANES).astype(BF16)
    wv_t = wkv[..., D_NOPE:].reshape(KV_RANK, N_HEADS * D_V).T.astype(BF16)
    return w_in_ext, wq_t, wk_ext, wv_t


def _kpe_placement():
    e2 = np.zeros((LANES, N_HEADS * LANES), np.float32)
    for j in range(D_ROPE):
        for h in range(N_HEADS):
            e2[j, h * LANES + D_NOPE + j] = 1.0
            e2[D_ROPE + j, h * LANES + D_NOPE + j] = 1.0
    return jnp.asarray(e2, BF16)


def _rope_tables(pos):
    inv = 1.0 / (ROPE_THETA ** (jnp.arange(0, D_ROPE, 2, dtype=F32) / D_ROPE))
    ang = pos.astype(F32)[:, None] * inv[None, :]
    cos, sin = jnp.cos(ang), jnp.sin(ang)
    t = pos.shape[0]
    scale = (D_QK ** -0.5) * math.log2(math.e)
    cos_q = jnp.concatenate([jnp.ones((t, D_NOPE), F32), cos, cos, jnp.zeros((t, LANES - D_QK), F32)], axis=1)
    sin_q = jnp.concatenate([jnp.zeros((t, D_NOPE), F32), sin, sin, jnp.zeros((t, LANES - D_QK), F32)], axis=1)
    tabq_t = (jnp.concatenate([cos_q, sin_q], axis=1) * scale).T
    tabk = jnp.concatenate([cos, cos, sin, sin, jnp.zeros((t, LANES - 2 * D_ROPE), F32)], axis=1)
    return tabq_t, tabk


def _pick(n, pref):
    return pref if n % pref == 0 else n


def kernel(x_prompt, x_sample, g_mix, w_in, b_gate, g_q, w_uq, g_kv, w_ukv, w_oa, conv_w, conv_b, ln_g, ln_b, w_pb,
           b_pb, w_out, g_ffn, w_router, b_router, w_gu, b_gu, w_down, b_down, g_final):
    b1, s1, d = x_prompt.shape
    b2, s2, _ = x_sample.shape
    t1, t2 = b1 * s1, b2 * s2
    t = t1 + t2
    x = jnp.concatenate([x_prompt.reshape(t1, d), x_sample.reshape(t2, d)], axis=0)
    seq_bounds = tuple((b * s1, (b + 1) * s1) for b in range(b1)) + tuple(
        (t1 + b * s2, t1 + (b + 1) * s2) for b in range(b2))

    tm = _pick(math.gcd(s1, s2), 512)
    nt = t // tm
    pos = jnp.concatenate([jnp.tile(jnp.arange(s1, dtype=I32), b1), jnp.tile(jnp.arange(s2, dtype=I32), b2)])
    tabq_t, tabk = _rope_tables(pos)
    w_in_ext, wq_t, wk_ext, wv_t = _prep_weights(w_in[0], w_uq[0], w_ukv[0])

    qt, k, vt3, u, gates = _in_proj(x, tabq_t, tabk, g_mix, w_in_ext, g_q, wq_t, g_kv, wk_ext, wv_t,
                                    _kpe_placement(), b_gate, tm=tm)

    ot = jnp.concatenate(
        [_attention(qt, k, vt3, seq_start=0, seq_len=s1, n_seq=b1, tq=_pick(s1, 512), tk=tm),
         _attention(qt, k, vt3, seq_start=t1, seq_len=s2, n_seq=b2, tq=_pick(s2, 512), tk=tm)], axis=1)

    w_r = jnp.concatenate([w_router[0], jnp.zeros((d, LANES - N_EXPERTS), F32)], axis=1)
    w_r_hi = w_r.astype(BF16)
    w_r = jnp.concatenate([w_r_hi, (w_r - w_r_hi.astype(F32)).astype(BF16)], axis=1)
    b_r = jnp.concatenate([b_router[0], jnp.full((LANES - N_EXPERTS,), NEG_BIG, F32)])[None, :]
    ltri = jnp.asarray(np.tril(np.ones((tm, tm), np.float32), -1), BF16)
    utri = jnp.asarray(np.triu(np.ones((LANES, LANES), np.float32), 1), BF16)
    x1, route, meta = _post_mix(x, ot, u, gates, w_oa[0].astype(BF16), conv_w[0], conv_b, ln_g, ln_b,
                                w_pb[0].astype(BF16), b_pb, w_out[0].astype(BF16), g_ffn, w_r, b_r, ltri, utri,
                                tm=tm, seq_bounds=seq_bounds)

    bm = MOE_ROWS
    n_rows = _sorted_rows(t, nt, bm)
    carry, n_pieces, off = (meta[:, r, :N_EXPERTS] for r in range(3))
    start, ex, n_used = _expert_layout(carry[-1] + n_pieces[-1] * RUN_PAD, bm=bm, n_blocks=n_rows // bm)
    run_row = (carry + start[None, :N_EXPERTS]).reshape(-1)
    n_pieces, off = n_pieces.reshape(-1), off.reshape(-1)

    xs = _dispatch(n_pieces, off, run_row, start, x1, route, g_ffn, tm=tm, bm=bm, n_rows=n_rows)
    y = _moe(ex, n_used, xs, w_gu[0].astype(BF16), b_gu[0][:, None, :], w_down[0].astype(BF16),
             b_down[0][:, None, :], bm=bm)
    out = _combine(n_pieces, run_row, off, x1, route, g_final[None, :], y, tm=tm)
    return out[:t1].reshape(b1, s1, d), out[t1:].reshape(b2, s2, d)
```

```python
import functools
import math

import numpy as np
import jax
import jax.numpy as jnp
from jax import lax
from jax.experimental import pallas as pl
from jax.experimental.pallas import tpu as pltpu

F32 = jnp.float32
BF16 = jnp.bfloat16
I32 = jnp.int32

D_MODEL = 1024
N_HEADS = 8
Q_RANK = 384
KV_RANK = 256
D_NOPE = 64
D_ROPE = 32
D_V = 64
D_QK = D_NOPE + D_ROPE
ROPE_THETA = 10000.0
CONV_CH = 512
CONV_WIDTH = 31
CONV_PAD = (CONV_WIDTH - 1) // 2
N_EXPERTS = 32
TOP_K = 4
D_FF = 1024
SWIGLU_ALPHA = 1.702
SWIGLU_LIMIT = 7.0
RMS_EPS = 1e-6
LN_EPS = 1e-5

LANES = 128
HALO_ROWS = 16
NEG_BIG = -1e30
RUN_PAD = 8
BIG_PIECE = 32
MOE_ROWS = 256

C_CQ = 0
C_CKV = Q_RANK
C_KPE = Q_RANK + KV_RANK
C_CONV = C_KPE + LANES
C_GATE = C_CONV + 2 * CONV_CH
D_IN_EXT = C_GATE + 2 * D_MODEL

VMEM_LIMIT = 56 * 1024 * 1024


def _rms(x, g):
    return x * lax.rsqrt(jnp.mean(x * x, axis=-1, keepdims=True) + RMS_EPS) * g


def _dot(a, b):
    return jnp.dot(a, b, preferred_element_type=F32)


def _dot_nt(a, b):
    return lax.dot_general(a, b, (((1,), (1,)), ((), ())), preferred_element_type=F32)


def _dot_tn(a, b):
    return lax.dot_general(a, b, (((0,), (0,)), ((), ())), preferred_element_type=F32)


def _const_spec(shape):
    return pl.BlockSpec(shape, lambda *_: (0,) * len(shape))


def _tile_select(i, n_first, a_ref, b_ref):
    return jnp.where(i < n_first, a_ref[...], b_ref[...])


def _in_proj_kernel(xa_ref, xb_ref, inv_ref, gmix_ref, win_ref, gq_ref, wqt_ref, gkv_ref, wk_ref, wvt_ref,
                    e2_ref, bgate_ref, qt_ref, k_ref, vt_ref, u_ref, gates_ref, *, tm, n_first, seq_starts):
    i = pl.program_id(0)
    h = _rms(_tile_select(i, n_first, xa_ref, xb_ref), gmix_ref[...]).astype(BF16)
    proj = _dot(h, win_ref[...])

    t0 = i * tm
    s0 = 0
    for s in seq_starts:
        s0 = jnp.where(t0 >= s, s, s0)
    half = D_ROPE // 2
    pos = (t0 - s0 + lax.broadcasted_iota(I32, (half, tm), 1)).astype(F32)
    ang = pos * inv_ref[...]
    cos, sin = jnp.cos(ang), jnp.sin(ang)
    scale = (D_QK ** -0.5) * math.log2(math.e)
    cos_q, sin_q = cos * scale, sin * scale

    cqn = _rms(proj[:, C_CQ:C_CQ + Q_RANK], gq_ref[...]).astype(BF16)
    q2 = _dot_nt(wqt_ref[...], cqn)
    hw = N_HEADS * LANES
    for hh in range(N_HEADS):
        qa = q2[hh * LANES:(hh + 1) * LANES, :]
        qb = q2[hw + hh * D_ROPE:hw + (hh + 1) * D_ROPE, :]
        r0 = hh * LANES
        qt_ref[r0:r0 + D_NOPE, :] = (qa[:D_NOPE] * scale).astype(BF16)
        for part in range(2):
            lo = D_NOPE + part * half
            qt_ref[r0 + lo:r0 + lo + half, :] = (
                qa[lo:lo + half] * cos_q + qb[part * half:(part + 1) * half] * sin_q).astype(BF16)
        qt_ref[r0 + D_QK:r0 + LANES, :] = jnp.zeros((LANES - D_QK, tm), BF16)

    ckvn = _rms(proj[:, C_CKV:C_CKV + KV_RANK], gkv_ref[...]).astype(BF16)
    tabk = jnp.transpose(jnp.concatenate([cos, cos, sin, sin, jnp.zeros((LANES - 2 * D_ROPE, tm), F32)], axis=0))
    kpe = (proj[:, C_KPE:C_KPE + LANES] * tabk).astype(BF16)
    k_ref[...] = (_dot(ckvn, wk_ref[...]) + _dot(kpe, e2_ref[...])).astype(BF16)
    vt_ref[0] = _dot_nt(wvt_ref[...], ckvn).astype(BF16)

    a = proj[:, C_CONV:C_CONV + CONV_CH]
    g = proj[:, C_CONV + CONV_CH:C_CONV + 2 * CONV_CH]
    u_ref[...] = a * jax.nn.sigmoid(g)
    gates_ref[...] = jax.nn.sigmoid(proj[:, C_GATE:] + bgate_ref[...]).astype(BF16)


def _split_specs(block, n_first, n_second, lane_major=False):
    first = lambda i: jnp.minimum(i, n_first - 1)
    second = lambda i: jnp.clip(i - n_first, 0, n_second - 1)
    if lane_major:
        return [pl.BlockSpec(block, lambda i: (0, first(i))), pl.BlockSpec(block, lambda i: (0, second(i)))]
    return [pl.BlockSpec(block, lambda i: (first(i), 0)), pl.BlockSpec(block, lambda i: (second(i), 0))]


def _in_proj(xa, xb, inv_b, g_mix, w_in_ext, g_q, wq_t, g_kv, wk_ext, wv_t, e2, b_gate, *, tm, seq_starts):
    na, nb = xa.shape[0] // tm, xb.shape[0] // tm
    nt = na + nb
    t = nt * tm
    row = lambda w: pl.BlockSpec((tm, w), lambda i: (i, 0))
    col = lambda h: pl.BlockSpec((h, tm), lambda i: (0, i))
    return pl.pallas_call(
        functools.partial(_in_proj_kernel, tm=tm, n_first=na, seq_starts=seq_starts),
        grid=(nt,),
        in_specs=_split_specs((tm, D_MODEL), na, nb) + [
                  _const_spec((D_ROPE // 2, tm)), _const_spec((1, D_MODEL)),
                  _const_spec((D_MODEL, D_IN_EXT)), _const_spec((1, Q_RANK)),
                  _const_spec((N_HEADS * (LANES + D_ROPE), Q_RANK)), _const_spec((1, KV_RANK)),
                  _const_spec((KV_RANK, N_HEADS * LANES)), _const_spec((N_HEADS * D_V, KV_RANK)),
                  _const_spec((LANES, N_HEADS * LANES)), _const_spec((1, 2 * D_MODEL))],
        out_specs=(col(N_HEADS * LANES), row(N_HEADS * LANES),
                   pl.BlockSpec((1, N_HEADS * D_V, tm), lambda i: (i, 0, 0)), row(CONV_CH), row(2 * D_MODEL)),
        out_shape=(jax.ShapeDtypeStruct((N_HEADS * LANES, t), BF16),
                   jax.ShapeDtypeStruct((t, N_HEADS * LANES), BF16),
                   jax.ShapeDtypeStruct((nt, N_HEADS * D_V, tm), BF16),
                   jax.ShapeDtypeStruct((t, CONV_CH), F32),
                   jax.ShapeDtypeStruct((t, 2 * D_MODEL), BF16)),
        compiler_params=pltpu.CompilerParams(dimension_semantics=("arbitrary",), vmem_limit_bytes=VMEM_LIMIT),
        name="in_proj",
    )(xa, xb, inv_b, g_mix, w_in_ext, g_q, wq_t, g_kv, wk_ext, wv_t, e2, b_gate)


def _attn_kernel(qt_ref, k_ref, vt_ref, o_ref, s_ref, acc_ref, m_ref, *, n_chunks, tk):
    ones = jnp.ones((16, tk), BF16)

    def scores(j, slot):
        ks = pl.multiple_of(j * tk, tk)
        s_ref[slot] = _dot(k_ref[pl.ds(ks, tk), :], qt_ref[...])

    def softmax_pv(j, slot):
        st = s_ref[slot]
        m = m_ref[...]
        m_new = jnp.maximum(m, jnp.max(st, axis=0, keepdims=True))
        p = jnp.exp2(st - m_new).astype(BF16)
        v1 = jnp.concatenate([vt_ref[j], ones], axis=0)
        acc_ref[...] = jnp.exp2(m - m_new) * acc_ref[...] + _dot(v1, p)
        m_ref[...] = m_new

    m_ref[...] = jnp.full(m_ref.shape, NEG_BIG, F32)
    acc_ref[...] = jnp.zeros(acc_ref.shape, F32)
    scores(0, 0)
    n_pairs = (n_chunks - 1) // 2

    def pair(jj, c):
        j = 2 * jj
        scores(j + 1, 1)
        softmax_pv(j, 0)
        scores(j + 2, 0)
        softmax_pv(j + 1, 1)
        return c

    lax.fori_loop(0, n_pairs, pair, 0, unroll=3 if n_pairs % 3 == 0 else 1)
    j0 = 2 * n_pairs
    if n_chunks - j0 == 2:
        scores(j0 + 1, 1)
        softmax_pv(j0, 0)
        softmax_pv(j0 + 1, 1)
    else:
        softmax_pv(j0, 0)
    acc = acc_ref[...]
    o_ref[...] = (acc[:D_V] / acc[D_V:D_V + 1]).astype(o_ref.dtype)


def _attention(qt, k, vt3, *, seq_start, seq_len, n_seq, tq, tk):
    nq = seq_len // tq
    nkc = seq_len // tk
    q0 = seq_start // tq
    s0 = seq_start // seq_len
    return pl.pallas_call(
        functools.partial(_attn_kernel, n_chunks=nkc, tk=tk),
        grid=(n_seq, N_HEADS, nq),
        in_specs=[pl.BlockSpec((LANES, tq), lambda b, h, i: (h, q0 + b * nq + i)),
                  pl.BlockSpec((seq_len, LANES), lambda b, h, i: (s0 + b, h)),
                  pl.BlockSpec((nkc, D_V, tk), lambda b, h, i: (s0 + b, h, 0))],
        out_specs=pl.BlockSpec((D_V, tq), lambda b, h, i: (h, b * nq + i)),
        out_shape=jax.ShapeDtypeStruct((N_HEADS * D_V, n_seq * seq_len), BF16),
        scratch_shapes=[pltpu.VMEM((2, tk, tq), F32), pltpu.VMEM((D_V + 16, tq), F32), pltpu.VMEM((1, tq), F32)],
        compiler_params=pltpu.CompilerParams(dimension_semantics=("arbitrary", "arbitrary", "arbitrary"),
                                             vmem_limit_bytes=VMEM_LIMIT),
        name="attention",
    )(qt, k, vt3)


def _local_rows(tm):
    return TOP_K * tm + N_EXPERTS * RUN_PAD


def _sorted_rows(t, nt, bm):
    runs = t * TOP_K + nt * N_EXPERTS * (RUN_PAD - 1)
    return -(-runs // bm) * bm + N_EXPERTS * bm


def _rows_copy(src, s, dst, d, sem, rows=RUN_PAD):
    s = pl.multiple_of(s, RUN_PAD)
    d = pl.multiple_of(d, RUN_PAD)
    return pltpu.make_async_copy(src.at[pl.ds(s, rows), :], dst.at[pl.ds(d, rows), :], sem)


def _for_each_piece(n_ref, a_ref, b_ref, tile, fn):
    def per_expert(e, c):
        n = n_ref[tile * N_EXPERTS + e]
        a0 = a_ref[tile * N_EXPERTS + e]
        b0 = b_ref[tile * N_EXPERTS + e]
        n_big = lax.div(n, BIG_PIECE // RUN_PAD)
        done = n_big * BIG_PIECE

        def big(q, c2):
            fn(a0 + q * BIG_PIECE, b0 + q * BIG_PIECE, BIG_PIECE)
            return c2

        def small(q, c2):
            fn(a0 + done + q * RUN_PAD, b0 + done + q * RUN_PAD, RUN_PAD)
            return c2

        lax.fori_loop(0, n_big, big, 0)
        lax.fori_loop(0, n - n_big * (BIG_PIECE // RUN_PAD), small, 0)
        return c

    lax.fori_loop(0, N_EXPERTS, per_expert, 0)


def _tile_pieces(n_ref, tile):
    def per_expert(e, c):
        n = n_ref[tile * N_EXPERTS + e]
        n_big = lax.div(n, BIG_PIECE // RUN_PAD)
        return c[0] + n_big, c[1] + n - n_big * (BIG_PIECE // RUN_PAD)

    return lax.fori_loop(0, N_EXPERTS, per_expert, (0, 0))


def _wait_rows(src, dst, sem, count, rows):
    def one(q, c):
        _rows_copy(src, 0, dst, 0, sem, rows=rows).wait()
        return c

    lax.fori_loop(0, count, one, 0)


def _wait_tile(src, dst, sem, n_ref, tile):
    n_big, n_small = _tile_pieces(n_ref, tile)
    _wait_rows(src, dst, sem, n_big, BIG_PIECE)
    _wait_rows(src, dst, sem, n_small, RUN_PAD)


def _post_mix_kernel(xa_ref, xb_ref, ota_ref, otb_ref, u_ref, up_ref, un_ref, gates_ref, woa_ref, cw_ref, cb_ref,
                     lng_ref, lnb_ref, wpb_ref, bpb_ref, wout_ref, gffn_ref, wr_ref, br_ref, ltri_ref, utri_ref,
                     x1_ref, route_ref, meta_ref, ext_ref, cv_ref, ph_ref, carry_ref, *, tm, n_first, seq_bounds):
    i = pl.program_id(0)
    t0 = i * tm
    is_start = functools.reduce(jnp.logical_or, [t0 == s for s, _ in seq_bounds])
    is_end = functools.reduce(jnp.logical_or, [t0 + tm == e for _, e in seq_bounds])

    ext_ref[0:HALO_ROWS, :] = jnp.where(is_start, 0.0, up_ref[...])
    ext_ref[HALO_ROWS:HALO_ROWS + tm, :] = u_ref[...]
    ext_ref[HALO_ROWS + tm:2 * HALO_ROWS + tm, :] = jnp.where(is_end, 0.0, un_ref[...])
    rc = min(tm, 64)
    sub = 8
    base = HALO_ROWS - CONV_PAD
    for c in range(tm // rc):
        acc = None
        for b in range(sub):
            part = None
            for o in range(CONV_WIDTH):
                if (o + base) % sub != b:
                    continue
                a0 = c * rc + (o + base) - b
                term = cw_ref[o:o + 1, :] * ext_ref[a0:a0 + rc + sub, :]
                part = term if part is None else part + term
            if b == 0:
                shifted = part[:rc]
            else:
                ph_ref[b] = part
                shifted = ph_ref[b, b:b + rc, :]
            acc = shifted if acc is None else acc + shifted
        cv_ref[c * rc:(c + 1) * rc, :] = acc + cb_ref[...]

    cv = cv_ref[...]
    mu = jnp.mean(cv, axis=-1, keepdims=True)
    xc = cv - mu
    y = xc * lax.rsqrt(jnp.mean(xc * xc, axis=-1, keepdims=True) + LN_EPS) * lng_ref[...] + lnb_ref[...]
    y = y * jax.nn.sigmoid(y)
    branch_b = _dot(y.astype(BF16), wpb_ref[...]) + bpb_ref[...]
    branch_a = _dot_tn(_tile_select(i, n_first, ota_ref, otb_ref), woa_ref[...])
    gates = gates_ref[...]
    mix = gates[:, :D_MODEL].astype(F32) * branch_a + gates[:, D_MODEL:].astype(F32) * branch_b
    x1 = _tile_select(i, n_first, xa_ref, xb_ref) + _dot(mix.astype(BF16), wout_ref[...])
    x1_ref[...] = x1

    h2 = _rms(x1, gffn_ref[...])
    h2_hi = h2.astype(BF16)
    h2_lo = (h2 - h2_hi.astype(F32)).astype(BF16)
    hi_both = _dot(h2_hi, wr_ref[...])
    logits = hi_both[:, :LANES] + (hi_both[:, LANES:] + _dot(h2_lo, wr_ref[:, :LANES])) + br_ref[...]
    lane = lax.broadcasted_iota(I32, logits.shape, 1)
    lane_f = lane.astype(F32)
    vals, idxs, sels = [], [], []
    cur = logits
    for _ in range(TOP_K):
        mval = jnp.max(cur, axis=-1, keepdims=True)
        ik = jnp.min(jnp.where(cur == mval, lane_f, float(LANES)), axis=-1, keepdims=True)
        sel = lane_f == ik
        cur = jnp.where(sel, -jnp.inf, cur)
        vals.append(mval)
        idxs.append(ik)
        sels.append(sel)
    exps = [jnp.exp(v - vals[0]) for v in vals]
    denom = exps[0] + exps[1] + exps[2] + exps[3]

    onehot = sels[0].astype(F32) + sels[1].astype(F32) + sels[2].astype(F32) + sels[3].astype(F32)

    @pl.when(i == 0)
    def _():
        carry_ref[...] = jnp.zeros_like(carry_ref)

    n_e = jnp.sum(onehot, axis=0, keepdims=True)
    n_pieces = jnp.floor((n_e + (RUN_PAD - 1)) * (1.0 / RUN_PAD))
    pieces_b = jnp.broadcast_to(n_pieces, (8, LANES)).astype(BF16)
    off = _dot(pieces_b, utri_ref[...])[0:1] * RUN_PAD
    total_pieces = jnp.sum(n_pieces, axis=-1, keepdims=True)
    carry = carry_ref[...]
    prefix = _dot(ltri_ref[...], onehot.astype(BF16))
    local = off + prefix
    route = jnp.zeros(logits.shape, F32)
    for kk in range(TOP_K):
        pos = jnp.sum(jnp.where(sels[kk], local, 0.0), axis=-1, keepdims=True)
        route = jnp.where(lane == kk, idxs[kk], route)
        route = jnp.where(lane == TOP_K + kk, exps[kk] / denom, route)
        route = jnp.where(lane == 2 * TOP_K + kk, pos, route)
    route_ref[...] = route
    carry_ref[...] = carry + n_pieces * RUN_PAD

    row8 = lax.broadcasted_iota(I32, (8, LANES), 0)
    metav = jnp.where(row8 == 0, jnp.broadcast_to(carry, (8, LANES)), 0.0)
    metav = jnp.where(row8 == 1, jnp.broadcast_to(n_pieces, (8, LANES)), metav)
    metav = jnp.where(row8 == 2, jnp.broadcast_to(off, (8, LANES)), metav)
    metav = jnp.where(row8 == 3, jnp.broadcast_to(total_pieces, (8, LANES)), metav)
    meta_ref[0] = metav.astype(I32)


def _post_mix(xa, xb, ota, otb, u, gates, w_oa, conv_w, conv_b, ln_g, ln_b, w_pb, b_pb, w_out, g_ffn, w_r, b_r, ltri,
              utri, *, tm, seq_bounds):
    na, nb = xa.shape[0] // tm, xb.shape[0] // tm
    nt = na + nb
    t = nt * tm
    hb = tm // HALO_ROWS
    n_halo = t // HALO_ROWS
    row = lambda w: pl.BlockSpec((tm, w), lambda i: (i, 0))
    return pl.pallas_call(
        functools.partial(_post_mix_kernel, tm=tm, n_first=na, seq_bounds=seq_bounds),
        grid=(nt,),
        in_specs=_split_specs((tm, D_MODEL), na, nb) + _split_specs((N_HEADS * D_V, tm), na, nb, lane_major=True) + [
                  row(CONV_CH),
                  pl.BlockSpec((HALO_ROWS, CONV_CH), lambda i: (jnp.maximum(i * hb - 1, 0), 0)),
                  pl.BlockSpec((HALO_ROWS, CONV_CH), lambda i: (jnp.minimum((i + 1) * hb, n_halo - 1), 0)),
                  row(2 * D_MODEL), _const_spec((N_HEADS * D_V, D_MODEL)), _const_spec((CONV_WIDTH, CONV_CH)),
                  _const_spec((1, CONV_CH)), _const_spec((1, CONV_CH)), _const_spec((1, CONV_CH)),
                  _const_spec((CONV_CH, D_MODEL)), _const_spec((1, D_MODEL)), _const_spec((D_MODEL, D_MODEL)),
                  _const_spec((1, D_MODEL)), _const_spec((D_MODEL, 2 * LANES)), _const_spec((1, LANES)),
                  _const_spec((tm, tm)), _const_spec((LANES, LANES))],
        out_specs=(row(D_MODEL), row(LANES), pl.BlockSpec((1, 8, LANES), lambda i: (i, 0, 0))),
        out_shape=(jax.ShapeDtypeStruct((t, D_MODEL), F32), jax.ShapeDtypeStruct((t, LANES), F32),
                   jax.ShapeDtypeStruct((nt, 8, LANES), I32)),
        scratch_shapes=[pltpu.VMEM((tm + 2 * HALO_ROWS, CONV_CH), F32), pltpu.VMEM((tm, CONV_CH), F32),
                        pltpu.VMEM((8, min(tm, 64) + 8, CONV_CH), F32), pltpu.VMEM((1, LANES), F32)],
        compiler_params=pltpu.CompilerParams(dimension_semantics=("arbitrary",), vmem_limit_bytes=VMEM_LIMIT),
        name="post_mix",
    )(xa, xb, ota, otb, u, u, u, gates, w_oa, conv_w, conv_b, ln_g, ln_b, w_pb, b_pb, w_out, g_ffn, w_r, b_r, ltri, utri)


def _dispatch_kernel(npc_s, off_s, dst_s, start_s, x1_ref, route_ref, gffn_ref, xs_hbm, xloc_ref, zero_ref, row_sem,
                     *, tm, nt, bm, n_rows):
    i = pl.program_id(0)
    slot = lax.rem(i, 2)
    n_loc = _local_rows(tm)

    h2 = _rms(x1_ref[...], gffn_ref[...]).astype(BF16)
    pos_t = jnp.transpose(route_ref[...])[2 * TOP_K:3 * TOP_K, :].astype(I32)
    prow = lax.broadcasted_iota(I32, (n_loc, tm), 0)
    perm = jnp.zeros((n_loc, tm), F32)
    for kk in range(TOP_K):
        perm = jnp.where(prow == pos_t[kk:kk + 1, :], 1.0, perm)
    xloc_ref[slot] = _dot(perm.astype(BF16), h2)

    @pl.when(i > 0)
    def _():
        _wait_tile(xloc_ref.at[0], xs_hbm, row_sem, npc_s, i - 1)

    _for_each_piece(npc_s, off_s, dst_s, i,
                    lambda s, d, rows: _rows_copy(xloc_ref.at[slot], s, xs_hbm, d, row_sem, rows=rows).start())

    @pl.when(i == nt - 1)
    def _():
        _wait_tile(xloc_ref.at[0], xs_hbm, row_sem, npc_s, i)
        zero_ref[...] = jnp.zeros(zero_ref.shape, F32)

        def tail(e, c):
            end = dst_s[i * N_EXPERTS + e] + npc_s[i * N_EXPERTS + e] * RUN_PAD
            n = lax.shift_right_logical(start_s[e + 1] - end, int(math.log2(RUN_PAD)))

            def piece(q, c2):
                _rows_copy(zero_ref, 0, xs_hbm, end + q * RUN_PAD, row_sem).start()
                return c2

            lax.fori_loop(0, n, piece, 0)
            return c + n

        _wait_rows(zero_ref, xs_hbm, row_sem, lax.fori_loop(0, N_EXPERTS, tail, 0), RUN_PAD)
        used = start_s[N_EXPERTS]
        n_free = (n_rows - used) // bm

        def fill(b, c):
            _rows_copy(zero_ref, 0, xs_hbm, used + b * bm, row_sem, rows=bm).start()
            return c

        lax.fori_loop(0, n_free, fill, 0)
        _wait_rows(zero_ref, xs_hbm, row_sem, n_free, bm)


def _dispatch(n_pieces, off, dst, start, x1, route, g_ffn, *, tm, bm, n_rows):
    t = x1.shape[0]
    nt = t // tm
    return pl.pallas_call(
        functools.partial(_dispatch_kernel, tm=tm, nt=nt, bm=bm, n_rows=n_rows),
        grid_spec=pltpu.PrefetchScalarGridSpec(
            num_scalar_prefetch=4,
            grid=(nt,),
            in_specs=[pl.BlockSpec((tm, D_MODEL), lambda i, *_: (i, 0)), pl.BlockSpec((tm, LANES), lambda i, *_: (i, 0)),
                      pl.BlockSpec((1, D_MODEL), lambda i, *_: (0, 0))],
            out_specs=pl.BlockSpec(memory_space=pl.ANY),
            scratch_shapes=[pltpu.VMEM((2, _local_rows(tm), D_MODEL), F32), pltpu.VMEM((bm, D_MODEL), F32),
                            pltpu.SemaphoreType.DMA],
        ),
        out_shape=jax.ShapeDtypeStruct((n_rows, D_MODEL), F32),
        compiler_params=pltpu.CompilerParams(dimension_semantics=("arbitrary",), vmem_limit_bytes=VMEM_LIMIT,
                                             has_side_effects=True),
        name="dispatch",
    )(n_pieces, off, dst, start, x1, route, g_ffn)


def _moe_kernel(exp_ref, n_ref, xs_ref, wgu_ref, bgu_ref, wd_ref, bd_ref, y_ref, wgu_bf, wd_bf):
    w = pl.program_id(0)
    used = w < n_ref[0]

    @pl.when(jnp.logical_or(w == 0, exp_ref[w] != exp_ref[jnp.maximum(w - 1, 0)]))
    def _():
        wgu_bf[...] = wgu_ref[0].astype(BF16)
        wd_bf[...] = wd_ref[0].astype(BF16)

    @pl.when(used)
    def _():
        x = xs_ref[...].astype(BF16)
        gu = _dot(x, wgu_bf[...]) + bgu_ref[0]
        g = jnp.minimum(gu[:, :D_FF], SWIGLU_LIMIT)
        u = jnp.clip(gu[:, D_FF:], -SWIGLU_LIMIT, SWIGLU_LIMIT)
        act = (u + 1.0) * (g * jax.nn.sigmoid(SWIGLU_ALPHA * g))
        y_ref[...] = _dot(act.astype(BF16), wd_bf[...]) + bd_ref[0]

    @pl.when(jnp.logical_not(used))
    def _():
        y_ref[...] = jnp.zeros(y_ref.shape, F32)


def _moe(ex, n_used, xs, w_gu, b_gu, w_down, b_down, *, bm):
    return pl.pallas_call(
        _moe_kernel,
        grid_spec=pltpu.PrefetchScalarGridSpec(
            num_scalar_prefetch=2,
            grid=(xs.shape[0] // bm,),
            in_specs=[pl.BlockSpec((bm, D_MODEL), lambda w, ex, n: (w, 0)),
                      pl.BlockSpec((1, D_MODEL, 2 * D_FF), lambda w, ex, n: (ex[w], 0, 0)),
                      pl.BlockSpec((1, 1, 2 * D_FF), lambda w, ex, n: (ex[w], 0, 0)),
                      pl.BlockSpec((1, D_FF, D_MODEL), lambda w, ex, n: (ex[w], 0, 0)),
                      pl.BlockSpec((1, 1, D_MODEL), lambda w, ex, n: (ex[w], 0, 0))],
            out_specs=pl.BlockSpec((bm, D_MODEL), lambda w, ex, n: (w, 0)),
            scratch_shapes=[pltpu.VMEM((D_MODEL, 2 * D_FF), BF16), pltpu.VMEM((D_FF, D_MODEL), BF16)],
        ),
        out_shape=jax.ShapeDtypeStruct(xs.shape, F32),
        compiler_params=pltpu.CompilerParams(dimension_semantics=("arbitrary",), vmem_limit_bytes=VMEM_LIMIT),
        name="moe",
    )(ex, n_used, xs, w_gu, b_gu, w_down, b_down)


def _expert_layout(totals, *, bm, n_blocks):
    n_blk = (totals + bm - 1) // bm
    blk_end = jnp.cumsum(n_blk)
    start = jnp.concatenate([jnp.zeros((1,), I32), blk_end * bm]).astype(I32)
    w = jnp.arange(n_blocks, dtype=I32)
    ex = jnp.minimum(jnp.sum((blk_end[None, :] <= w[:, None]).astype(I32), axis=1), N_EXPERTS - 1)
    return start, ex.astype(I32), blk_end[-1].astype(I32)[None]


def _combine_kernel(npc_s, src_s, off_s, x1_ref, route_ref, gfin_ref, y_hbm, out_ref, yloc_ref, sems, *, tm, nt):
    i = pl.program_id(0)
    slot = lax.rem(i, 2)
    n_loc = _local_rows(tm)

    def gather(tile, sl):
        _for_each_piece(npc_s, src_s, off_s, tile,
                        lambda s, d, rows: _rows_copy(y_hbm, s, yloc_ref.at[sl], d, sems.at[sl], rows=rows).start())

    @pl.when(i == 0)
    def _():
        yloc_ref[...] = jnp.zeros(yloc_ref.shape, F32)
        gather(0, 0)

    @pl.when(i + 1 < nt)
    def _():
        gather(i + 1, 1 - slot)

    _wait_tile(y_hbm, yloc_ref.at[slot], sems.at[slot], npc_s, i)

    route = route_ref[...]
    lanes = lax.broadcasted_iota(I32, (tm, n_loc), 1)
    pw = jnp.zeros((tm, n_loc), F32)
    for kk in range(TOP_K):
        pos = route[:, 2 * TOP_K + kk:2 * TOP_K + kk + 1].astype(I32)
        pw = jnp.where(lanes == pos, route[:, TOP_K + kk:TOP_K + kk + 1], pw)
    pw_hi = pw.astype(BF16)
    pw_lo = (pw - pw_hi.astype(F32)).astype(BF16)
    yb = yloc_ref[slot].astype(BF16)
    moe = _dot(pw_hi, yb) + _dot(pw_lo, yb)
    out_ref[...] = _rms(x1_ref[...] + moe, gfin_ref[...])


def _combine(n_pieces, src, off, x1, route, g_final, y, *, tm):
    t = x1.shape[0]
    nt = t // tm
    return pl.pallas_call(
        functools.partial(_combine_kernel, tm=tm, nt=nt),
        grid_spec=pltpu.PrefetchScalarGridSpec(
            num_scalar_prefetch=3,
            grid=(nt,),
            in_specs=[pl.BlockSpec((tm, D_MODEL), lambda i, *_: (i, 0)), pl.BlockSpec((tm, LANES), lambda i, *_: (i, 0)),
                      pl.BlockSpec((1, D_MODEL), lambda i, *_: (0, 0)), pl.BlockSpec(memory_space=pl.ANY)],
            out_specs=pl.BlockSpec((tm, D_MODEL), lambda i, *_: (i, 0)),
            scratch_shapes=[pltpu.VMEM((2, _local_rows(tm), D_MODEL), F32), pltpu.SemaphoreType.DMA((2,))],
        ),
        out_shape=jax.ShapeDtypeStruct((t, D_MODEL), F32),
        compiler_params=pltpu.CompilerParams(dimension_semantics=("arbitrary",), vmem_limit_bytes=VMEM_LIMIT),
        name="combine",
    )(n_pieces, src, off, x1, route, g_final, y)


def _rot_cols(w):
    half = w.shape[-1] // 2
    return jnp.concatenate([-w[..., half:], w[..., :half]], axis=-1)


def _prep_weights(w_in, w_uq, w_ukv):
    d = w_in.shape[0]
    w_kpe = w_in[:, C_KPE:C_KPE + D_ROPE]
    w_in_ext = jnp.concatenate(
        [w_in[:, :C_KPE], w_kpe, _rot_cols(w_kpe), jnp.zeros((d, LANES - 2 * D_ROPE), F32),
         w_in[:, C_KPE + D_ROPE:]], axis=1).astype(BF16)

    wq = w_uq.reshape(Q_RANK, N_HEADS, D_QK)
    nope, pe = wq[..., :D_NOPE], wq[..., D_NOPE:]
    zpad = jnp.zeros((Q_RANK, N_HEADS, LANES - D_QK), F32)
    plain = jnp.concatenate([nope, pe, zpad], axis=-1).reshape(Q_RANK, N_HEADS * LANES)
    rot = _rot_cols(pe).reshape(Q_RANK, N_HEADS * D_ROPE)
    wq_t = jnp.concatenate([plain, rot], axis=1).T.astype(BF16)

    wkv = w_ukv.reshape(KV_RANK, N_HEADS, D_NOPE + D_V)
    wk_ext = jnp.concatenate([wkv[..., :D_NOPE], jnp.zeros((KV_RANK, N_HEADS, LANES - D_NOPE), F32)],
                             axis=-1).reshape(KV_RANK, N_HEADS * LANES).astype(BF16)
    wv_t = wkv[..., D_NOPE:].reshape(KV_RANK, N_HEADS * D_V).T.astype(BF16)
    return w_in_ext, wq_t, wk_ext, wv_t


def _kpe_placement():
    e2 = np.zeros((LANES, N_HEADS * LANES), np.float32)
    for j in range(D_ROPE):
        for h in range(N_HEADS):
            e2[j, h * LANES + D_NOPE + j] = 1.0
            e2[D_ROPE + j, h * LANES + D_NOPE + j] = 1.0
    return jnp.asarray(e2, BF16)


def _inv_freq(tm):
    inv = 1.0 / (ROPE_THETA ** (jnp.arange(0, D_ROPE, 2, dtype=F32) / D_ROPE))
    return jnp.broadcast_to(inv[:, None], (D_ROPE // 2, tm))


def _pick(n, pref):
    return pref if n % pref == 0 else n


def kernel(x_prompt, x_sample, g_mix, w_in, b_gate, g_q, w_uq, g_kv, w_ukv, w_oa, conv_w, conv_b, ln_g, ln_b, w_pb,
           b_pb, w_out, g_ffn, w_router, b_router, w_gu, b_gu, w_down, b_down, g_final):
    b1, s1, d = x_prompt.shape
    b2, s2, _ = x_sample.shape
    t1, t2 = b1 * s1, b2 * s2
    t = t1 + t2
    xa, xb = x_prompt.reshape(t1, d), x_sample.reshape(t2, d)
    seq_bounds = tuple((b * s1, (b + 1) * s1) for b in range(b1)) + tuple(
        (t1 + b * s2, t1 + (b + 1) * s2) for b in range(b2))

    tm = _pick(math.gcd(s1, s2), 512)
    nt = t // tm
    w_in_ext, wq_t, wk_ext, wv_t = _prep_weights(w_in[0], w_uq[0], w_ukv[0])

    qt, k, vt3, u, gates = _in_proj(xa, xb, _inv_freq(tm), g_mix, w_in_ext, g_q, wq_t, g_kv, wk_ext, wv_t,
                                    _kpe_placement(), b_gate, tm=tm, seq_starts=tuple(s for s, _ in seq_bounds))

    ota = _attention(qt, k, vt3, seq_start=0, seq_len=s1, n_seq=b1, tq=_pick(s1, 512), tk=tm)
    otb = _attention(qt, k, vt3, seq_start=t1, seq_len=s2, n_seq=b2, tq=_pick(s2, 512), tk=tm)

    w_r = jnp.concatenate([w_router[0], jnp.zeros((d, LANES - N_EXPERTS), F32)], axis=1)
    w_r_hi = w_r.astype(BF16)
    w_r = jnp.concatenate([w_r_hi, (w_r - w_r_hi.astype(F32)).astype(BF16)], axis=1)
    b_r = jnp.concatenate([b_router[0], jnp.full((LANES - N_EXPERTS,), NEG_BIG, F32)])[None, :]
    ltri = jnp.asarray(np.tril(np.ones((tm, tm), np.float32), -1), BF16)
    utri = jnp.asarray(np.triu(np.ones((LANES, LANES), np.float32), 1), BF16)
    x1, route, meta = _post_mix(xa, xb, ota, otb, u, gates, w_oa[0].astype(BF16), conv_w[0], conv_b, ln_g, ln_b,
                                w_pb[0].astype(BF16), b_pb, w_out[0].astype(BF16), g_ffn, w_r, b_r, ltri, utri,
                                tm=tm, seq_bounds=seq_bounds)

    bm = MOE_ROWS
    n_rows = _sorted_rows(t, nt, bm)
    carry, n_pieces, off = (meta[:, r, :N_EXPERTS] for r in range(3))
    start, ex, n_used = _expert_layout(carry[-1] + n_pieces[-1] * RUN_PAD, bm=bm, n_blocks=n_rows // bm)
    run_row = (carry + start[None, :N_EXPERTS]).reshape(-1)
    n_pieces, off = n_pieces.reshape(-1), off.reshape(-1)

    xs = _dispatch(n_pieces, off, run_row, start, x1, route, g_ffn, tm=tm, bm=bm, n_rows=n_rows)
    y = _moe(ex, n_used, xs, w_gu[0], b_gu[0][:, None, :], w_down[0], b_down[0][:, None, :], bm=bm)
    out = _combine(n_pieces, run_row, off, x1, route, g_final[None, :], y, tm=tm)
    return out[:t1].reshape(b1, s1, d), out[t1:].reshape(b2, s2, d)
```

```python
import functools
import math

import numpy as np
import jax
import jax.numpy as jnp
from jax import lax
from jax.experimental import pallas as pl
from jax.experimental.pallas import tpu as pltpu

F32 = jnp.float32
BF16 = jnp.bfloat16
I32 = jnp.int32
U32 = jnp.uint32

D_MODEL = 1024
N_HEADS = 8
Q_RANK = 384
KV_RANK = 256
D_NOPE = 64
D_ROPE = 32
D_V = 64
D_QK = D_NOPE + D_ROPE
ROPE_THETA = 10000.0
CONV_CH = 512
CONV_WIDTH = 31
CONV_PAD = (CONV_WIDTH - 1) // 2
N_EXPERTS = 32
TOP_K = 4
D_FF = 1024
SWIGLU_ALPHA = 1.702
SWIGLU_LIMIT = 7.0
RMS_EPS = 1e-6
LN_EPS = 1e-5

LANES = 128
HALO_ROWS = 16
NEG_BIG = -1e30
RUN_PAD = 8
BIG_PIECE = 32
MOE_ROWS = 256

C_CQ = 0
C_CKV = Q_RANK
C_KPE = Q_RANK + KV_RANK
C_CONV = C_KPE + LANES
C_GATE = C_CONV + 2 * CONV_CH
D_IN_EXT = C_GATE + 2 * D_MODEL

VMEM_LIMIT = 56 * 1024 * 1024


def _rms(x, g):
    return x * lax.rsqrt(jnp.mean(x * x, axis=-1, keepdims=True) + RMS_EPS) * g


def _dot(a, b):
    return jnp.dot(a, b, preferred_element_type=F32)


def _dot_nt(a, b):
    return lax.dot_general(a, b, (((1,), (1,)), ((), ())), preferred_element_type=F32)


def _dot_tn(a, b):
    return lax.dot_general(a, b, (((0,), (0,)), ((), ())), preferred_element_type=F32)


def _const_spec(shape):
    return pl.BlockSpec(shape, lambda *_: (0,) * len(shape))


PACKED = D_MODEL // 2
HIGH_HALF = 0xFFFF0000


def _pack_rows(v):
    bits = lambda x: pltpu.bitcast(x.astype(BF16).astype(F32), U32)
    return bits(v[:, :PACKED]) | lax.shift_right_logical(bits(v[:, PACKED:]), jnp.uint32(16))


def _unpack_rows(w):
    hi = pltpu.bitcast(w & jnp.uint32(HIGH_HALF), F32).astype(BF16)
    lo = pltpu.bitcast(lax.shift_left(w, jnp.uint32(16)), F32).astype(BF16)
    return jnp.concatenate([hi, lo], axis=1)


def _tile_select(i, n_first, a_ref, b_ref):
    return jnp.where(i < n_first, a_ref[...], b_ref[...])


def _in_proj_kernel(xa_ref, xb_ref, inv_ref, gmix_ref, win_ref, gq_ref, wqt_ref, gkv_ref, wk_ref, wvt_ref,
                    e2_ref, bgate_ref, qt_ref, k_ref, vt_ref, u_ref, gates_ref, *, tm, n_first, seq_starts):
    i = pl.program_id(0)
    h = _rms(_tile_select(i, n_first, xa_ref, xb_ref), gmix_ref[...]).astype(BF16)
    proj = _dot(h, win_ref[...])

    t0 = i * tm
    s0 = 0
    for s in seq_starts:
        s0 = jnp.where(t0 >= s, s, s0)
    half = D_ROPE // 2
    pos = (t0 - s0 + lax.broadcasted_iota(I32, (half, tm), 1)).astype(F32)
    ang = pos * inv_ref[...]
    cos, sin = jnp.cos(ang), jnp.sin(ang)
    scale = (D_QK ** -0.5) * math.log2(math.e)
    cos_q, sin_q = cos * scale, sin * scale

    cqn = _rms(proj[:, C_CQ:C_CQ + Q_RANK], gq_ref[...]).astype(BF16)
    q2 = _dot_nt(wqt_ref[...], cqn)
    hw = N_HEADS * LANES
    for hh in range(N_HEADS):
        qa = q2[hh * LANES:(hh + 1) * LANES, :]
        qb = q2[hw + hh * D_ROPE:hw + (hh + 1) * D_ROPE, :]
        r0 = hh * LANES
        qt_ref[r0:r0 + D_NOPE, :] = (qa[:D_NOPE] * scale).astype(BF16)
        for part in range(2):
            lo = D_NOPE + part * half
            qt_ref[r0 + lo:r0 + lo + half, :] = (
                qa[lo:lo + half] * cos_q + qb[part * half:(part + 1) * half] * sin_q).astype(BF16)
        qt_ref[r0 + D_QK:r0 + LANES, :] = jnp.zeros((LANES - D_QK, tm), BF16)

    ckvn = _rms(proj[:, C_CKV:C_CKV + KV_RANK], gkv_ref[...]).astype(BF16)
    tabk = jnp.transpose(jnp.concatenate([cos, cos, sin, sin, jnp.zeros((LANES - 2 * D_ROPE, tm), F32)], axis=0))
    kpe = (proj[:, C_KPE:C_KPE + LANES] * tabk).astype(BF16)
    k_ref[...] = (_dot(ckvn, wk_ref[...]) + _dot(kpe, e2_ref[...])).astype(BF16)
    vt_ref[0] = _dot_nt(wvt_ref[...], ckvn).astype(BF16)

    a = proj[:, C_CONV:C_CONV + CONV_CH]
    g = proj[:, C_CONV + CONV_CH:C_CONV + 2 * CONV_CH]
    u_ref[...] = a * jax.nn.sigmoid(g)
    gates_ref[...] = jax.nn.sigmoid(proj[:, C_GATE:] + bgate_ref[...]).astype(BF16)


def _split_specs(block, n_first, n_second, lane_major=False):
    first = lambda i: jnp.minimum(i, n_first - 1)
    second = lambda i: jnp.clip(i - n_first, 0, n_second - 1)
    if lane_major:
        return [pl.BlockSpec(block, lambda i: (0, first(i))), pl.BlockSpec(block, lambda i: (0, second(i)))]
    return [pl.BlockSpec(block, lambda i: (first(i), 0)), pl.BlockSpec(block, lambda i: (second(i), 0))]


def _in_proj(xa, xb, inv_b, g_mix, w_in_ext, g_q, wq_t, g_kv, wk_ext, wv_t, e2, b_gate, *, tm, seq_starts):
    na, nb = xa.shape[0] // tm, xb.shape[0] // tm
    nt = na + nb
    t = nt * tm
    row = lambda w: pl.BlockSpec((tm, w), lambda i: (i, 0))
    col = lambda h: pl.BlockSpec((h, tm), lambda i: (0, i))
    return pl.pallas_call(
        functools.partial(_in_proj_kernel, tm=tm, n_first=na, seq_starts=seq_starts),
        grid=(nt,),
        in_specs=_split_specs((tm, D_MODEL), na, nb) + [
                  _const_spec((D_ROPE // 2, tm)), _const_spec((1, D_MODEL)),
                  _const_spec((D_MODEL, D_IN_EXT)), _const_spec((1, Q_RANK)),
                  _const_spec((N_HEADS * (LANES + D_ROPE), Q_RANK)), _const_spec((1, KV_RANK)),
                  _const_spec((KV_RANK, N_HEADS * LANES)), _const_spec((N_HEADS * D_V, KV_RANK)),
                  _const_spec((LANES, N_HEADS * LANES)), _const_spec((1, 2 * D_MODEL))],
        out_specs=(col(N_HEADS * LANES), row(N_HEADS * LANES),
                   pl.BlockSpec((1, N_HEADS * D_V, tm), lambda i: (i, 0, 0)), row(CONV_CH), row(2 * D_MODEL)),
        out_shape=(jax.ShapeDtypeStruct((N_HEADS * LANES, t), BF16),
                   jax.ShapeDtypeStruct((t, N_HEADS * LANES), BF16),
                   jax.ShapeDtypeStruct((nt, N_HEADS * D_V, tm), BF16),
                   jax.ShapeDtypeStruct((t, CONV_CH), F32),
                   jax.ShapeDtypeStruct((t, 2 * D_MODEL), BF16)),
        compiler_params=pltpu.CompilerParams(dimension_semantics=("arbitrary",), vmem_limit_bytes=VMEM_LIMIT),
        name="in_proj",
    )(xa, xb, inv_b, g_mix, w_in_ext, g_q, wq_t, g_kv, wk_ext, wv_t, e2, b_gate)


def _attn_kernel(qt_ref, k_ref, vt_ref, o_ref, s_ref, acc_ref, m_ref, *, n_chunks, tk):
    ones = jnp.ones((16, tk), BF16)

    def scores(j, slot):
        ks = pl.multiple_of(j * tk, tk)
        s_ref[slot] = _dot(k_ref[pl.ds(ks, tk), :], qt_ref[...])

    def softmax_pv(j, slot):
        st = s_ref[slot]
        m = m_ref[...]
        m_new = jnp.maximum(m, jnp.max(st, axis=0, keepdims=True))
        p = jnp.exp2(st - m_new).astype(BF16)
        v1 = jnp.concatenate([vt_ref[j], ones], axis=0)
        acc_ref[...] = jnp.exp2(m - m_new) * acc_ref[...] + _dot(v1, p)
        m_ref[...] = m_new

    m_ref[...] = jnp.full(m_ref.shape, NEG_BIG, F32)
    acc_ref[...] = jnp.zeros(acc_ref.shape, F32)
    scores(0, 0)
    n_pairs = (n_chunks - 1) // 2

    def pair(jj, c):
        j = 2 * jj
        scores(j + 1, 1)
        softmax_pv(j, 0)
        scores(j + 2, 0)
        softmax_pv(j + 1, 1)
        return c

    lax.fori_loop(0, n_pairs, pair, 0, unroll=5 if n_pairs % 5 == 0 else (3 if n_pairs % 3 == 0 else 1))
    j0 = 2 * n_pairs
    if n_chunks - j0 == 2:
        scores(j0 + 1, 1)
        softmax_pv(j0, 0)
        softmax_pv(j0 + 1, 1)
    else:
        softmax_pv(j0, 0)
    acc = acc_ref[...]
    o_ref[...] = (acc[:D_V] / acc[D_V:D_V + 1]).astype(o_ref.dtype)


def _attention(qt, k, vt3, *, seq_start, seq_len, n_seq, tq, tk):
    nq = seq_len // tq
    nkc = seq_len // tk
    q0 = seq_start // tq
    s0 = seq_start // seq_len
    return pl.pallas_call(
        functools.partial(_attn_kernel, n_chunks=nkc, tk=tk),
        grid=(n_seq, N_HEADS, nq),
        in_specs=[pl.BlockSpec((LANES, tq), lambda b, h, i: (h, q0 + b * nq + i)),
                  pl.BlockSpec((seq_len, LANES), lambda b, h, i: (s0 + b, h)),
                  pl.BlockSpec((nkc, D_V, tk), lambda b, h, i: (s0 + b, h, 0))],
        out_specs=pl.BlockSpec((D_V, tq), lambda b, h, i: (h, b * nq + i)),
        out_shape=jax.ShapeDtypeStruct((N_HEADS * D_V, n_seq * seq_len), BF16),
        scratch_shapes=[pltpu.VMEM((2, tk, tq), F32), pltpu.VMEM((D_V + 16, tq), F32), pltpu.VMEM((1, tq), F32)],
        compiler_params=pltpu.CompilerParams(dimension_semantics=("arbitrary", "arbitrary", "arbitrary"),
                                             vmem_limit_bytes=VMEM_LIMIT),
        name="attention",
    )(qt, k, vt3)


def _local_rows(tm):
    return TOP_K * tm + N_EXPERTS * RUN_PAD


def _sorted_rows(t, nt, bm):
    runs = t * TOP_K + nt * N_EXPERTS * (RUN_PAD - 1)
    return -(-runs // bm) * bm + N_EXPERTS * bm


def _rows_copy(src, s, dst, d, sem, rows=RUN_PAD):
    s = pl.multiple_of(s, RUN_PAD)
    d = pl.multiple_of(d, RUN_PAD)
    return pltpu.make_async_copy(src.at[pl.ds(s, rows), :], dst.at[pl.ds(d, rows), :], sem)


def _for_each_piece(n_ref, a_ref, b_ref, tile, fn):
    def per_expert(e, c):
        n = n_ref[tile * N_EXPERTS + e]
        a0 = a_ref[tile * N_EXPERTS + e]
        b0 = b_ref[tile * N_EXPERTS + e]
        n_big = lax.div(n, BIG_PIECE // RUN_PAD)
        done = n_big * BIG_PIECE

        def big(q, c2):
            fn(a0 + q * BIG_PIECE, b0 + q * BIG_PIECE, BIG_PIECE)
            return c2

        def small(q, c2):
            fn(a0 + done + q * RUN_PAD, b0 + done + q * RUN_PAD, RUN_PAD)
            return c2

        lax.fori_loop(0, n_big, big, 0)
        lax.fori_loop(0, n - n_big * (BIG_PIECE // RUN_PAD), small, 0)
        return c

    lax.fori_loop(0, N_EXPERTS, per_expert, 0)


def _tile_pieces(n_ref, tile):
    def per_expert(e, c):
        n = n_ref[tile * N_EXPERTS + e]
        n_big = lax.div(n, BIG_PIECE // RUN_PAD)
        return c[0] + n_big, c[1] + n - n_big * (BIG_PIECE // RUN_PAD)

    return lax.fori_loop(0, N_EXPERTS, per_expert, (0, 0))


def _wait_rows(src, dst, sem, count, rows):
    def one(q, c):
        _rows_copy(src, 0, dst, 0, sem, rows=rows).wait()
        return c

    lax.fori_loop(0, count, one, 0)


def _wait_tile(src, dst, sem, n_ref, tile):
    n_big, n_small = _tile_pieces(n_ref, tile)
    _wait_rows(src, dst, sem, n_big, BIG_PIECE)
    _wait_rows(src, dst, sem, n_small, RUN_PAD)


def _post_mix_kernel(xa_ref, xb_ref, ota_ref, otb_ref, u_ref, up_ref, un_ref, gates_ref, woa_ref, cw_ref, cb_ref,
                     lng_ref, lnb_ref, wpb_ref, bpb_ref, wout_ref, gffn_ref, wr_ref, br_ref, ltri_ref, utri_ref,
                     x1_ref, route_ref, meta_ref, ext_ref, cv_ref, ph_ref, carry_ref, *, tm, n_first, seq_bounds):
    i = pl.program_id(0)
    t0 = i * tm
    is_start = functools.reduce(jnp.logical_or, [t0 == s for s, _ in seq_bounds])
    is_end = functools.reduce(jnp.logical_or, [t0 + tm == e for _, e in seq_bounds])

    ext_ref[0:HALO_ROWS, :] = jnp.where(is_start, 0.0, up_ref[...])
    ext_ref[HALO_ROWS:HALO_ROWS + tm, :] = u_ref[...]
    ext_ref[HALO_ROWS + tm:2 * HALO_ROWS + tm, :] = jnp.where(is_end, 0.0, un_ref[...])
    rc = min(tm, 64)
    sub = 8
    base = HALO_ROWS - CONV_PAD
    for c in range(tm // rc):
        acc = None
        for b in range(sub):
            part = None
            for o in range(CONV_WIDTH):
                if (o + base) % sub != b:
                    continue
                a0 = c * rc + (o + base) - b
                term = cw_ref[o:o + 1, :] * ext_ref[a0:a0 + rc + sub, :]
                part = term if part is None else part + term
            if b == 0:
                shifted = part[:rc]
            else:
                ph_ref[b] = part
                shifted = ph_ref[b, b:b + rc, :]
            acc = shifted if acc is None else acc + shifted
        cv_ref[c * rc:(c + 1) * rc, :] = acc + cb_ref[...]

    cv = cv_ref[...]
    mu = jnp.mean(cv, axis=-1, keepdims=True)
    xc = cv - mu
    y = xc * lax.rsqrt(jnp.mean(xc * xc, axis=-1, keepdims=True) + LN_EPS) * lng_ref[...] + lnb_ref[...]
    y = y * jax.nn.sigmoid(y)
    branch_b = _dot(y.astype(BF16), wpb_ref[...]) + bpb_ref[...]
    branch_a = _dot_tn(_tile_select(i, n_first, ota_ref, otb_ref), woa_ref[...])
    gates = gates_ref[...]
    mix = gates[:, :D_MODEL].astype(F32) * branch_a + gates[:, D_MODEL:].astype(F32) * branch_b
    x1 = _tile_select(i, n_first, xa_ref, xb_ref) + _dot(mix.astype(BF16), wout_ref[...])
    x1_ref[...] = x1

    h2 = _rms(x1, gffn_ref[...])
    h2_hi = h2.astype(BF16)
    h2_lo = (h2 - h2_hi.astype(F32)).astype(BF16)
    hi_both = _dot(h2_hi, wr_ref[...])
    logits = hi_both[:, :LANES] + (hi_both[:, LANES:] + _dot(h2_lo, wr_ref[:, :LANES])) + br_ref[...]
    lane = lax.broadcasted_iota(I32, logits.shape, 1)
    lane_f = lane.astype(F32)
    vals, idxs, sels = [], [], []
    cur = logits
    for _ in range(TOP_K):
        mval = jnp.max(cur, axis=-1, keepdims=True)
        ik = jnp.min(jnp.where(cur == mval, lane_f, float(LANES)), axis=-1, keepdims=True)
        sel = lane_f == ik
        cur = jnp.where(sel, -jnp.inf, cur)
        vals.append(mval)
        idxs.append(ik)
        sels.append(sel)
    exps = [jnp.exp(v - vals[0]) for v in vals]
    denom = exps[0] + exps[1] + exps[2] + exps[3]

    onehot = sels[0].astype(F32) + sels[1].astype(F32) + sels[2].astype(F32) + sels[3].astype(F32)

    @pl.when(i == 0)
    def _():
        carry_ref[...] = jnp.zeros_like(carry_ref)

    n_e = jnp.sum(onehot, axis=0, keepdims=True)
    n_pieces = jnp.floor((n_e + (RUN_PAD - 1)) * (1.0 / RUN_PAD))
    pieces_b = jnp.broadcast_to(n_pieces, (8, LANES)).astype(BF16)
    off = _dot(pieces_b, utri_ref[...])[0:1] * RUN_PAD
    total_pieces = jnp.sum(n_pieces, axis=-1, keepdims=True)
    carry = carry_ref[...]
    prefix = _dot(ltri_ref[...], onehot.astype(BF16))
    local = off + prefix
    route = jnp.zeros(logits.shape, F32)
    for kk in range(TOP_K):
        pos = jnp.sum(jnp.where(sels[kk], local, 0.0), axis=-1, keepdims=True)
        route = jnp.where(lane == kk, idxs[kk], route)
        route = jnp.where(lane == TOP_K + kk, exps[kk] / denom, route)
        route = jnp.where(lane == 2 * TOP_K + kk, pos, route)
    route_ref[...] = route
    carry_ref[...] = carry + n_pieces * RUN_PAD

    row8 = lax.broadcasted_iota(I32, (8, LANES), 0)
    metav = jnp.where(row8 == 0, jnp.broadcast_to(carry, (8, LANES)), 0.0)
    metav = jnp.where(row8 == 1, jnp.broadcast_to(n_pieces, (8, LANES)), metav)
    metav = jnp.where(row8 == 2, jnp.broadcast_to(off, (8, LANES)), metav)
    metav = jnp.where(row8 == 3, jnp.broadcast_to(total_pieces, (8, LANES)), metav)
    meta_ref[0] = metav.astype(I32)


def _post_mix(xa, xb, ota, otb, u, gates, w_oa, conv_w, conv_b, ln_g, ln_b, w_pb, b_pb, w_out, g_ffn, w_r, b_r, ltri,
              utri, *, tm, seq_bounds):
    na, nb = xa.shape[0] // tm, xb.shape[0] // tm
    nt = na + nb
    t = nt * tm
    hb = tm // HALO_ROWS
    n_halo = t // HALO_ROWS
    row = lambda w: pl.BlockSpec((tm, w), lambda i: (i, 0))
    return pl.pallas_call(
        functools.partial(_post_mix_kernel, tm=tm, n_first=na, seq_bounds=seq_bounds),
        grid=(nt,),
        in_specs=_split_specs((tm, D_MODEL), na, nb) + _split_specs((N_HEADS * D_V, tm), na, nb, lane_major=True) + [
                  row(CONV_CH),
                  pl.BlockSpec((HALO_ROWS, CONV_CH), lambda i: (jnp.maximum(i * hb - 1, 0), 0)),
                  pl.BlockSpec((HALO_ROWS, CONV_CH), lambda i: (jnp.minimum((i + 1) * hb, n_halo - 1), 0)),
                  row(2 * D_MODEL), _const_spec((N_HEADS * D_V, D_MODEL)), _const_spec((CONV_WIDTH, CONV_CH)),
                  _const_spec((1, CONV_CH)), _const_spec((1, CONV_CH)), _const_spec((1, CONV_CH)),
                  _const_spec((CONV_CH, D_MODEL)), _const_spec((1, D_MODEL)), _const_spec((D_MODEL, D_MODEL)),
                  _const_spec((1, D_MODEL)), _const_spec((D_MODEL, 2 * LANES)), _const_spec((1, LANES)),
                  _const_spec((tm, tm)), _const_spec((LANES, LANES))],
        out_specs=(row(D_MODEL), row(LANES), pl.BlockSpec((1, 8, LANES), lambda i: (i, 0, 0))),
        out_shape=(jax.ShapeDtypeStruct((t, D_MODEL), F32), jax.ShapeDtypeStruct((t, LANES), F32),
                   jax.ShapeDtypeStruct((nt, 8, LANES), I32)),
        scratch_shapes=[pltpu.VMEM((tm + 2 * HALO_ROWS, CONV_CH), F32), pltpu.VMEM((tm, CONV_CH), F32),
                        pltpu.VMEM((8, min(tm, 64) + 8, CONV_CH), F32), pltpu.VMEM((1, LANES), F32)],
        compiler_params=pltpu.CompilerParams(dimension_semantics=("arbitrary",), vmem_limit_bytes=VMEM_LIMIT),
        name="post_mix",
    )(xa, xb, ota, otb, u, u, u, gates, w_oa, conv_w, conv_b, ln_g, ln_b, w_pb, b_pb, w_out, g_ffn, w_r, b_r, ltri, utri)


def _dispatch_kernel(npc_s, off_s, dst_s, start_s, x1_ref, route_ref, gffn_ref, xs_hbm, xloc_ref, zero_ref, row_sem,
                     *, tm, nt, bm, n_rows):
    i = pl.program_id(0)
    slot = lax.rem(i, 2)
    n_loc = _local_rows(tm)

    h2 = _rms(x1_ref[...], gffn_ref[...]).astype(BF16)
    pos_t = jnp.transpose(route_ref[...])[2 * TOP_K:3 * TOP_K, :].astype(I32)
    prow = lax.broadcasted_iota(I32, (n_loc, tm), 0)
    perm = jnp.zeros((n_loc, tm), F32)
    for kk in range(TOP_K):
        perm = jnp.where(prow == pos_t[kk:kk + 1, :], 1.0, perm)
    xloc_ref[slot] = _pack_rows(_dot(perm.astype(BF16), h2))

    @pl.when(i > 0)
    def _():
        _wait_tile(xloc_ref.at[0], xs_hbm, row_sem, npc_s, i - 1)

    _for_each_piece(npc_s, off_s, dst_s, i,
                    lambda s, d, rows: _rows_copy(xloc_ref.at[slot], s, xs_hbm, d, row_sem, rows=rows).start())

    @pl.when(i == nt - 1)
    def _():
        _wait_tile(xloc_ref.at[0], xs_hbm, row_sem, npc_s, i)
        zero_ref[...] = jnp.zeros(zero_ref.shape, U32)

        def tail(e, c):
            end = dst_s[i * N_EXPERTS + e] + npc_s[i * N_EXPERTS + e] * RUN_PAD
            n = lax.shift_right_logical(start_s[e + 1] - end, int(math.log2(RUN_PAD)))

            def piece(q, c2):
                _rows_copy(zero_ref, 0, xs_hbm, end + q * RUN_PAD, row_sem).start()
                return c2

            lax.fori_loop(0, n, piece, 0)
            return c + n

        _wait_rows(zero_ref, xs_hbm, row_sem, lax.fori_loop(0, N_EXPERTS, tail, 0), RUN_PAD)
        used = start_s[N_EXPERTS]
        n_free = (n_rows - used) // bm

        def fill(b, c):
            _rows_copy(zero_ref, 0, xs_hbm, used + b * bm, row_sem, rows=bm).start()
            return c

        lax.fori_loop(0, n_free, fill, 0)
        _wait_rows(zero_ref, xs_hbm, row_sem, n_free, bm)


def _dispatch(n_pieces, off, dst, start, x1, route, g_ffn, *, tm, bm, n_rows):
    t = x1.shape[0]
    nt = t // tm
    return pl.pallas_call(
        functools.partial(_dispatch_kernel, tm=tm, nt=nt, bm=bm, n_rows=n_rows),
        grid_spec=pltpu.PrefetchScalarGridSpec(
            num_scalar_prefetch=4,
            grid=(nt,),
            in_specs=[pl.BlockSpec((tm, D_MODEL), lambda i, *_: (i, 0)), pl.BlockSpec((tm, LANES), lambda i, *_: (i, 0)),
                      pl.BlockSpec((1, D_MODEL), lambda i, *_: (0, 0))],
            out_specs=pl.BlockSpec(memory_space=pl.ANY),
            scratch_shapes=[pltpu.VMEM((2, _local_rows(tm), PACKED), U32), pltpu.VMEM((bm, PACKED), U32),
                            pltpu.SemaphoreType.DMA],
        ),
        out_shape=jax.ShapeDtypeStruct((n_rows, PACKED), U32),
        compiler_params=pltpu.CompilerParams(dimension_semantics=("arbitrary",), vmem_limit_bytes=VMEM_LIMIT,
                                             has_side_effects=True),
        name="dispatch",
    )(n_pieces, off, dst, start, x1, route, g_ffn)


def _moe_kernel(exp_ref, n_ref, xs_ref, wgu_ref, bgu_ref, wd_ref, bd_ref, y_ref, wgu_bf, wd_bf):
    w = pl.program_id(0)
    used = w < n_ref[0]

    @pl.when(jnp.logical_or(w == 0, exp_ref[w] != exp_ref[jnp.maximum(w - 1, 0)]))
    def _():
        wgu_bf[...] = wgu_ref[0].astype(BF16)
        wd_bf[...] = wd_ref[0].astype(BF16)

    @pl.when(used)
    def _():
        x = _unpack_rows(xs_ref[...])
        gu = _dot(x, wgu_bf[...]) + bgu_ref[0]
        g = jnp.minimum(gu[:, :D_FF], SWIGLU_LIMIT)
        u = jnp.clip(gu[:, D_FF:], -SWIGLU_LIMIT, SWIGLU_LIMIT)
        act = (u + 1.0) * (g * jax.nn.sigmoid(SWIGLU_ALPHA * g))
        y_ref[...] = _pack_rows(_dot(act.astype(BF16), wd_bf[...]) + bd_ref[0])

    @pl.when(jnp.logical_not(used))
    def _():
        y_ref[...] = jnp.zeros(y_ref.shape, U32)


def _moe(ex, n_used, xs, w_gu, b_gu, w_down, b_down, *, bm):
    return pl.pallas_call(
        _moe_kernel,
        grid_spec=pltpu.PrefetchScalarGridSpec(
            num_scalar_prefetch=2,
            grid=(xs.shape[0] // bm,),
            in_specs=[pl.BlockSpec((bm, PACKED), lambda w, ex, n: (w, 0)),
                      pl.BlockSpec((1, D_MODEL, 2 * D_FF), lambda w, ex, n: (ex[w], 0, 0)),
                      pl.BlockSpec((1, 1, 2 * D_FF), lambda w, ex, n: (ex[w], 0, 0)),
                      pl.BlockSpec((1, D_FF, D_MODEL), lambda w, ex, n: (ex[w], 0, 0)),
                      pl.BlockSpec((1, 1, D_MODEL), lambda w, ex, n: (ex[w], 0, 0))],
            out_specs=pl.BlockSpec((bm, PACKED), lambda w, ex, n: (w, 0)),
            scratch_shapes=[pltpu.VMEM((D_MODEL, 2 * D_FF), BF16), pltpu.VMEM((D_FF, D_MODEL), BF16)],
        ),
        out_shape=jax.ShapeDtypeStruct(xs.shape, U32),
        compiler_params=pltpu.CompilerParams(dimension_semantics=("arbitrary",), vmem_limit_bytes=VMEM_LIMIT),
        name="moe",
    )(ex, n_used, xs, w_gu, b_gu, w_down, b_down)


def _expert_layout(totals, *, bm, n_blocks):
    n_blk = (totals + bm - 1) // bm
    blk_end = jnp.cumsum(n_blk)
    start = jnp.concatenate([jnp.zeros((1,), I32), blk_end * bm]).astype(I32)
    w = jnp.arange(n_blocks, dtype=I32)
    ex = jnp.minimum(jnp.sum((blk_end[None, :] <= w[:, None]).astype(I32), axis=1), N_EXPERTS - 1)
    return start, ex.astype(I32), blk_end[-1].astype(I32)[None]


def _combine_kernel(npc_s, src_s, off_s, x1_ref, route_ref, gfin_ref, y_hbm, out_ref, yloc_ref, sems, *, tm, nt):
    i = pl.program_id(0)
    slot = lax.rem(i, 2)
    n_loc = _local_rows(tm)

    def gather(tile, sl):
        _for_each_piece(npc_s, src_s, off_s, tile,
                        lambda s, d, rows: _rows_copy(y_hbm, s, yloc_ref.at[sl], d, sems.at[sl], rows=rows).start())

    @pl.when(i == 0)
    def _():
        yloc_ref[...] = jnp.zeros(yloc_ref.shape, U32)
        gather(0, 0)

    @pl.when(i + 1 < nt)
    def _():
        gather(i + 1, 1 - slot)

    _wait_tile(y_hbm, yloc_ref.at[slot], sems.at[slot], npc_s, i)

    route = route_ref[...]
    lanes = lax.broadcasted_iota(I32, (tm, n_loc), 1)
    pw = jnp.zeros((tm, n_loc), F32)
    for kk in range(TOP_K):
        pos = route[:, 2 * TOP_K + kk:2 * TOP_K + kk + 1].astype(I32)
        pw = jnp.where(lanes == pos, route[:, TOP_K + kk:TOP_K + kk + 1], pw)
    pw_hi = pw.astype(BF16)
    pw_lo = (pw - pw_hi.astype(F32)).astype(BF16)
    yb = _unpack_rows(yloc_ref[slot])
    moe = _dot(pw_hi, yb) + _dot(pw_lo, yb)
    out_ref[...] = _rms(x1_ref[...] + moe, gfin_ref[...])


def _combine(n_pieces, src, off, x1, route, g_final, y, *, tm):
    t = x1.shape[0]
    nt = t // tm
    return pl.pallas_call(
        functools.partial(_combine_kernel, tm=tm, nt=nt),
        grid_spec=pltpu.PrefetchScalarGridSpec(
            num_scalar_prefetch=3,
            grid=(nt,),
            in_specs=[pl.BlockSpec((tm, D_MODEL), lambda i, *_: (i, 0)), pl.BlockSpec((tm, LANES), lambda i, *_: (i, 0)),
                      pl.BlockSpec((1, D_MODEL), lambda i, *_: (0, 0)), pl.BlockSpec(memory_space=pl.ANY)],
            out_specs=pl.BlockSpec((tm, D_MODEL), lambda i, *_: (i, 0)),
            scratch_shapes=[pltpu.VMEM((2, _local_rows(tm), PACKED), U32), pltpu.SemaphoreType.DMA((2,))],
        ),
        out_shape=jax.ShapeDtypeStruct((t, D_MODEL), F32),
        compiler_params=pltpu.CompilerParams(dimension_semantics=("arbitrary",), vmem_limit_bytes=VMEM_LIMIT),
        name="combine",
    )(n_pieces, src, off, x1, route, g_final, y)


def _rot_cols(w):
    half = w.shape[-1] // 2
    return jnp.concatenate([-w[..., half:], w[..., :half]], axis=-1)


def _prep_weights(w_in, w_uq, w_ukv):
    d = w_in.shape[0]
    w_kpe = w_in[:, C_KPE:C_KPE + D_ROPE]
    w_in_ext = jnp.concatenate(
        [w_in[:, :C_KPE], w_kpe, _rot_cols(w_kpe), jnp.zeros((d, LANES - 2 * D_ROPE), F32),
         w_in[:, C_KPE + D_ROPE:]], axis=1).astype(BF16)

    wq = w_uq.reshape(Q_RANK, N_HEADS, D_QK)
    nope, pe = wq[..., :D_NOPE], wq[..., D_NOPE:]
    zpad = jnp.zeros((Q_RANK, N_HEADS, LANES - D_QK), F32)
    plain = jnp.concatenate([nope, pe, zpad], axis=-1).reshape(Q_RANK, N_HEADS * LANES)
    rot = _rot_cols(pe).reshape(Q_RANK, N_HEADS * D_ROPE)
    wq_t = jnp.concatenate([plain, rot], axis=1).T.astype(BF16)

    wkv = w_ukv.reshape(KV_RANK, N_HEADS, D_NOPE + D_V)
    wk_ext = jnp.concatenate([wkv[..., :D_NOPE], jnp.zeros((KV_RANK, N_HEADS, LANES - D_NOPE), F32)],
                             axis=-1).reshape(KV_RANK, N_HEADS * LANES).astype(BF16)
    wv_t = wkv[..., D_NOPE:].reshape(KV_RANK, N_HEADS * D_V).T.astype(BF16)
    return w_in_ext, wq_t, wk_ext, wv_t


def _kpe_placement():
    e2 = np.zeros((LANES, N_HEADS * LANES), np.float32)
    for j in range(D_ROPE):
        for h in range(N_HEADS):
            e2[j, h * LANES + D_NOPE + j] = 1.0
            e2[D_ROPE + j, h * LANES + D_NOPE + j] = 1.0
    return jnp.asarray(e2, BF16)


def _inv_freq(tm):
    inv = 1.0 / (ROPE_THETA ** (jnp.arange(0, D_ROPE, 2, dtype=F32) / D_ROPE))
    return jnp.broadcast_to(inv[:, None], (D_ROPE // 2, tm))


def _pick(n, pref):
    return pref if n % pref == 0 else n


def kernel(x_prompt, x_sample, g_mix, w_in, b_gate, g_q, w_uq, g_kv, w_ukv, w_oa, conv_w, conv_b, ln_g, ln_b, w_pb,
           b_pb, w_out, g_ffn, w_router, b_router, w_gu, b_gu, w_down, b_down, g_final):
    b1, s1, d = x_prompt.shape
    b2, s2, _ = x_sample.shape
    t1, t2 = b1 * s1, b2 * s2
    t = t1 + t2
    xa, xb = x_prompt.reshape(t1, d), x_sample.reshape(t2, d)
    seq_bounds = tuple((b * s1, (b + 1) * s1) for b in range(b1)) + tuple(
        (t1 + b * s2, t1 + (b + 1) * s2) for b in range(b2))

    tm = _pick(math.gcd(s1, s2), 512)
    nt = t // tm
    w_in_ext, wq_t, wk_ext, wv_t = _prep_weights(w_in[0], w_uq[0], w_ukv[0])

    qt, k, vt3, u, gates = _in_proj(xa, xb, _inv_freq(tm), g_mix, w_in_ext, g_q, wq_t, g_kv, wk_ext, wv_t,
                                    _kpe_placement(), b_gate, tm=tm, seq_starts=tuple(s for s, _ in seq_bounds))

    ota = _attention(qt, k, vt3, seq_start=0, seq_len=s1, n_seq=b1, tq=_pick(s1, 512), tk=tm)
    otb = _attention(qt, k, vt3, seq_start=t1, seq_len=s2, n_seq=b2, tq=_pick(s2, 512), tk=tm)

    w_r = jnp.concatenate([w_router[0], jnp.zeros((d, LANES - N_EXPERTS), F32)], axis=1)
    w_r_hi = w_r.astype(BF16)
    w_r = jnp.concatenate([w_r_hi, (w_r - w_r_hi.astype(F32)).astype(BF16)], axis=1)
    b_r = jnp.concatenate([b_router[0], jnp.full((LANES - N_EXPERTS,), NEG_BIG, F32)])[None, :]
    ltri = jnp.asarray(np.tril(np.ones((tm, tm), np.float32), -1), BF16)
    utri = jnp.asarray(np.triu(np.ones((LANES, LANES), np.float32), 1), BF16)
    x1, route, meta = _post_mix(xa, xb, ota, otb, u, gates, w_oa[0].astype(BF16), conv_w[0], conv_b, ln_g, ln_b,
                                w_pb[0].astype(BF16), b_pb, w_out[0].astype(BF16), g_ffn, w_r, b_r, ltri, utri,
                                tm=tm, seq_bounds=seq_bounds)

    bm = MOE_ROWS
    n_rows = _sorted_rows(t, nt, bm)
    carry, n_pieces, off = (meta[:, r, :N_EXPERTS] for r in range(3))
    start, ex, n_used = _expert_layout(carry[-1] + n_pieces[-1] * RUN_PAD, bm=bm, n_blocks=n_rows // bm)
    run_row = (carry + start[None, :N_EXPERTS]).reshape(-1)
    n_pieces, off = n_pieces.reshape(-1), off.reshape(-1)

    xs = _dispatch(n_pieces, off, run_row, start, x1, route, g_ffn, tm=tm, bm=bm, n_rows=n_rows)
    y = _moe(ex, n_used, xs, w_gu[0], b_gu[0][:, None, :], w_down[0], b_down[0][:, None, :], bm=bm)
    out = _combine(n_pieces, run_row, off, x1, route, g_final[None, :], y, tm=tm)
    return out[:t1].reshape(b1, s1, d), out[t1:].reshape(b2, s2, d)
```

```python
import functools
import math

import numpy as np
import jax
import jax.numpy as jnp
from jax import lax
from jax.experimental import pallas as pl
from jax.experimental.pallas import tpu as pltpu

F32 = jnp.float32
BF16 = jnp.bfloat16
I32 = jnp.int32
U32 = jnp.uint32

D_MODEL = 1024
N_HEADS = 8
Q_RANK = 384
KV_RANK = 256
D_NOPE = 64
D_ROPE = 32
D_V = 64
D_QK = D_NOPE + D_ROPE
ROPE_THETA = 10000.0
CONV_CH = 512
CONV_WIDTH = 31
CONV_PAD = (CONV_WIDTH - 1) // 2
N_EXPERTS = 32
TOP_K = 4
D_FF = 1024
SWIGLU_ALPHA = 1.702
SWIGLU_LIMIT = 7.0
RMS_EPS = 1e-6
LN_EPS = 1e-5

LANES = 128
HALO_ROWS = 16
NEG_BIG = -1e30
RUN_PAD = 8
BIG_PIECE = 32
MOE_ROWS = 512

C_CQ = 0
C_CKV = Q_RANK
C_KPE = Q_RANK + KV_RANK
C_CONV = C_KPE + LANES
C_GATE = C_CONV + 2 * CONV_CH
D_IN_EXT = C_GATE + 2 * D_MODEL

VMEM_LIMIT = 56 * 1024 * 1024


def _rms(x, g):
    return x * lax.rsqrt(jnp.mean(x * x, axis=-1, keepdims=True) + RMS_EPS) * g


def _dot(a, b):
    return jnp.dot(a, b, preferred_element_type=F32)


def _dot_nt(a, b):
    return lax.dot_general(a, b, (((1,), (1,)), ((), ())), preferred_element_type=F32)


def _dot_tn(a, b):
    return lax.dot_general(a, b, (((0,), (0,)), ((), ())), preferred_element_type=F32)


def _const_spec(shape):
    return pl.BlockSpec(shape, lambda *_: (0,) * len(shape))


PACKED = D_MODEL // 2
HIGH_HALF = 0xFFFF0000


def _pack_rows(v):
    bits = lambda x: pltpu.bitcast(x.astype(BF16).astype(F32), U32)
    return bits(v[:, :PACKED]) | lax.shift_right_logical(bits(v[:, PACKED:]), jnp.uint32(16))


def _unpack_rows(w):
    hi = pltpu.bitcast(w & jnp.uint32(HIGH_HALF), F32).astype(BF16)
    lo = pltpu.bitcast(lax.shift_left(w, jnp.uint32(16)), F32).astype(BF16)
    return jnp.concatenate([hi, lo], axis=1)


def _tile_select(i, n_first, a_ref, b_ref):
    return jnp.where(i < n_first, a_ref[...], b_ref[...])


def _in_proj_kernel(xa_ref, xb_ref, inv_ref, gmix_ref, win_ref, gq_ref, wqt_ref, gkv_ref, wk_ref, wvt_ref,
                    e2_ref, bgate_ref, qt_ref, k_ref, vt_ref, u_ref, gates_ref, *, tm, n_first, seq_starts):
    i = pl.program_id(0)
    h = _rms(_tile_select(i, n_first, xa_ref, xb_ref), gmix_ref[...]).astype(BF16)
    proj = _dot(h, win_ref[...])

    t0 = i * tm
    s0 = 0
    for s in seq_starts:
        s0 = jnp.where(t0 >= s, s, s0)
    half = D_ROPE // 2
    pos = (t0 - s0 + lax.broadcasted_iota(I32, (half, tm), 1)).astype(F32)
    ang = pos * inv_ref[...]
    cos, sin = jnp.cos(ang), jnp.sin(ang)
    scale = (D_QK ** -0.5) * math.log2(math.e)
    cos_q, sin_q = cos * scale, sin * scale

    cqn = _rms(proj[:, C_CQ:C_CQ + Q_RANK], gq_ref[...]).astype(BF16)
    q2 = _dot_nt(wqt_ref[...], cqn)
    hw = N_HEADS * LANES
    for hh in range(N_HEADS):
        qa = q2[hh * LANES:(hh + 1) * LANES, :]
        qb = q2[hw + hh * D_ROPE:hw + (hh + 1) * D_ROPE, :]
        r0 = hh * LANES
        qt_ref[0, r0:r0 + D_NOPE, :] = (qa[:D_NOPE] * scale).astype(BF16)
        for part in range(2):
            lo = D_NOPE + part * half
            qt_ref[0, r0 + lo:r0 + lo + half, :] = (
                qa[lo:lo + half] * cos_q + qb[part * half:(part + 1) * half] * sin_q).astype(BF16)
        qt_ref[0, r0 + D_QK:r0 + LANES, :] = jnp.zeros((LANES - D_QK, tm), BF16)

    ckvn = _rms(proj[:, C_CKV:C_CKV + KV_RANK], gkv_ref[...]).astype(BF16)
    tabk = jnp.transpose(jnp.concatenate([cos, cos, sin, sin, jnp.zeros((LANES - 2 * D_ROPE, tm), F32)], axis=0))
    kpe = (proj[:, C_KPE:C_KPE + LANES] * tabk).astype(BF16)
    k_ref[...] = (_dot(ckvn, wk_ref[...]) + _dot(kpe, e2_ref[...])).astype(BF16)
    vt_ref[0] = _dot_nt(wvt_ref[...], ckvn).astype(BF16)

    a = proj[:, C_CONV:C_CONV + CONV_CH]
    g = proj[:, C_CONV + CONV_CH:C_CONV + 2 * CONV_CH]
    u_ref[...] = a * jax.nn.sigmoid(g)
    gates_ref[...] = jax.nn.sigmoid(proj[:, C_GATE:] + bgate_ref[...]).astype(BF16)


def _split_specs(block, n_first, n_second):
    first = lambda i: jnp.minimum(i, n_first - 1)
    second = lambda i: jnp.clip(i - n_first, 0, n_second - 1)
    rest = (0,) * (len(block) - 1)
    return [pl.BlockSpec(block, lambda i: (first(i),) + rest), pl.BlockSpec(block, lambda i: (second(i),) + rest)]


def _in_proj(xa, xb, inv_b, g_mix, w_in_ext, g_q, wq_t, g_kv, wk_ext, wv_t, e2, b_gate, *, tm, seq_starts):
    na, nb = xa.shape[0] // tm, xb.shape[0] // tm
    nt = na + nb
    t = nt * tm
    row = lambda w: pl.BlockSpec((tm, w), lambda i: (i, 0))
    col = lambda h: pl.BlockSpec((h, tm), lambda i: (0, i))
    return pl.pallas_call(
        functools.partial(_in_proj_kernel, tm=tm, n_first=na, seq_starts=seq_starts),
        grid=(nt,),
        in_specs=_split_specs((tm, D_MODEL), na, nb) + [
                  _const_spec((D_ROPE // 2, tm)), _const_spec((1, D_MODEL)),
                  _const_spec((D_MODEL, D_IN_EXT)), _const_spec((1, Q_RANK)),
                  _const_spec((N_HEADS * (LANES + D_ROPE), Q_RANK)), _const_spec((1, KV_RANK)),
                  _const_spec((KV_RANK, N_HEADS * LANES)), _const_spec((N_HEADS * D_V, KV_RANK)),
                  _const_spec((LANES, N_HEADS * LANES)), _const_spec((1, 2 * D_MODEL))],
        out_specs=(pl.BlockSpec((1, N_HEADS * LANES, tm), lambda i: (i, 0, 0)), row(N_HEADS * LANES),
                   pl.BlockSpec((1, N_HEADS * D_V, tm), lambda i: (i, 0, 0)), row(CONV_CH), row(2 * D_MODEL)),
        out_shape=(jax.ShapeDtypeStruct((nt, N_HEADS * LANES, tm), BF16),
                   jax.ShapeDtypeStruct((t, N_HEADS * LANES), BF16),
                   jax.ShapeDtypeStruct((nt, N_HEADS * D_V, tm), BF16),
                   jax.ShapeDtypeStruct((t, CONV_CH), F32),
                   jax.ShapeDtypeStruct((t, 2 * D_MODEL), BF16)),
        compiler_params=pltpu.CompilerParams(dimension_semantics=("arbitrary",), vmem_limit_bytes=VMEM_LIMIT),
        name="in_proj",
    )(xa, xb, inv_b, g_mix, w_in_ext, g_q, wq_t, g_kv, wk_ext, wv_t, e2, b_gate)


def _attn_kernel(qt_ref, k_ref, vt_ref, o_ref, s_ref, acc_ref, m_ref, *, nq, n_chunks, tk, unroll):
    half = n_chunks // 2
    ones = jnp.ones((16, tk), BF16)

    def scores(a, j, slot):
        ks = pl.multiple_of(j * tk, tk)
        s_ref[slot] = _dot(k_ref[pl.ds(ks, tk), :], qt_ref[a])

    def softmax_pv(j, slot):
        st = s_ref[slot]
        m = jnp.where(j == 0, NEG_BIG, m_ref[...])
        m_new = jnp.maximum(m, jnp.max(st, axis=0, keepdims=True))
        p = jnp.exp2(st - m_new).astype(BF16)
        v1 = jnp.concatenate([vt_ref[j], ones], axis=0)
        acc = jnp.exp2(m - m_new) * acc_ref[...] + _dot(v1, p)
        acc_ref[...] = acc
        m_ref[...] = m_new
        return acc

    m_ref[...] = jnp.full(m_ref.shape, NEG_BIG, F32)
    acc_ref[...] = jnp.zeros(acc_ref.shape, F32)
    scores(0, 0, 0)

    def pair(pp, c):
        a = lax.div(pp, half)
        j = 2 * lax.rem(pp, half)
        scores(a, j + 1, 1)
        softmax_pv(j, 0)
        nxt = pp + 1
        scores(jnp.minimum(lax.div(nxt, half), nq - 1), 2 * lax.rem(nxt, half), 0)
        acc = softmax_pv(j + 1, 1)
        o_ref[a] = (acc[:D_V] / acc[D_V:D_V + 1]).astype(o_ref.dtype)
        return c

    lax.fori_loop(0, nq * half, pair, 0, unroll=unroll)


def _attention(qt3, k, vt3, *, seq_start, seq_len, n_seq, tq, tk):
    nq = seq_len // tq
    nkc = seq_len // tk
    assert nkc % 2 == 0, "key chunks are processed in pairs"
    s0 = seq_start // seq_len
    n_pairs = nq * (nkc // 2)
    unroll = next(u for u in (8, 4, 2, 1) if n_pairs % u == 0)
    return pl.pallas_call(
        functools.partial(_attn_kernel, nq=nq, n_chunks=nkc, tk=tk, unroll=unroll),
        grid=(n_seq, N_HEADS),
        in_specs=[pl.BlockSpec((nq, LANES, tq), lambda b, h: (s0 + b, h, 0)),
                  pl.BlockSpec((seq_len, LANES), lambda b, h: (s0 + b, h)),
                  pl.BlockSpec((nkc, D_V, tk), lambda b, h: (s0 + b, h, 0))],
        out_specs=pl.BlockSpec((nq, D_V, tq), lambda b, h: (b, h, 0)),
        out_shape=jax.ShapeDtypeStruct((n_seq * nq, N_HEADS * D_V, tq), BF16),
        scratch_shapes=[pltpu.VMEM((2, tk, tq), F32), pltpu.VMEM((D_V + 16, tq), F32), pltpu.VMEM((1, tq), F32)],
        compiler_params=pltpu.CompilerParams(dimension_semantics=("arbitrary", "arbitrary"),
                                             vmem_limit_bytes=VMEM_LIMIT),
        name="attention",
    )(qt3, k, vt3)


def _local_rows(tm):
    return TOP_K * tm + N_EXPERTS * RUN_PAD


def _sorted_rows(t, nt, bm):
    runs = t * TOP_K + nt * N_EXPERTS * (RUN_PAD - 1)
    return -(-runs // bm) * bm + N_EXPERTS * bm


def _rows_copy(src, s, dst, d, sem, rows=RUN_PAD):
    s = pl.multiple_of(s, RUN_PAD)
    d = pl.multiple_of(d, RUN_PAD)
    return pltpu.make_async_copy(src.at[pl.ds(s, rows), :], dst.at[pl.ds(d, rows), :], sem)


def _for_each_piece(n_ref, a_ref, b_ref, tile, fn):
    def per_expert(e, c):
        n = n_ref[tile * N_EXPERTS + e]
        a0 = a_ref[tile * N_EXPERTS + e]
        b0 = b_ref[tile * N_EXPERTS + e]
        n_big = lax.div(n, BIG_PIECE // RUN_PAD)
        done = n_big * BIG_PIECE

        def big(q, c2):
            fn(a0 + q * BIG_PIECE, b0 + q * BIG_PIECE, BIG_PIECE)
            return c2

        def small(q, c2):
            fn(a0 + done + q * RUN_PAD, b0 + done + q * RUN_PAD, RUN_PAD)
            return c2

        lax.fori_loop(0, n_big, big, 0)
        lax.fori_loop(0, n - n_big * (BIG_PIECE // RUN_PAD), small, 0)
        return c

    lax.fori_loop(0, N_EXPERTS, per_expert, 0)


def _tile_pieces(n_ref, tile):
    def per_expert(e, c):
        n = n_ref[tile * N_EXPERTS + e]
        n_big = lax.div(n, BIG_PIECE // RUN_PAD)
        return c[0] + n_big, c[1] + n - n_big * (BIG_PIECE // RUN_PAD)

    return lax.fori_loop(0, N_EXPERTS, per_expert, (0, 0))


def _wait_rows(src, dst, sem, count, rows):
    def one(q, c):
        _rows_copy(src, 0, dst, 0, sem, rows=rows).wait()
        return c

    lax.fori_loop(0, count, one, 0)


def _wait_tile(src, dst, sem, n_ref, tile):
    n_big, n_small = _tile_pieces(n_ref, tile)
    _wait_rows(src, dst, sem, n_big, BIG_PIECE)
    _wait_rows(src, dst, sem, n_small, RUN_PAD)


def _post_mix_kernel(xa_ref, xb_ref, ota_ref, otb_ref, u_ref, up_ref, un_ref, gates_ref, woa_ref, cw_ref, cb_ref,
                     lng_ref, lnb_ref, wpb_ref, bpb_ref, wout_ref, gffn_ref, wr_ref, br_ref, ltri_ref, utri_ref,
                     x1_ref, route_ref, meta_ref, ext_ref, cv_ref, ph_ref, carry_ref, *, tm, n_first, seq_bounds):
    i = pl.program_id(0)
    t0 = i * tm
    is_start = functools.reduce(jnp.logical_or, [t0 == s for s, _ in seq_bounds])
    is_end = functools.reduce(jnp.logical_or, [t0 + tm == e for _, e in seq_bounds])

    ext_ref[0:HALO_ROWS, :] = jnp.where(is_start, 0.0, up_ref[...])
    ext_ref[HALO_ROWS:HALO_ROWS + tm, :] = u_ref[...]
    ext_ref[HALO_ROWS + tm:2 * HALO_ROWS + tm, :] = jnp.where(is_end, 0.0, un_ref[...])
    rc = min(tm, 64)
    sub = 8
    base = HALO_ROWS - CONV_PAD
    for c in range(tm // rc):
        acc = None
        for b in range(sub):
            part = None
            for o in range(CONV_WIDTH):
                if (o + base) % sub != b:
                    continue
                a0 = c * rc + (o + base) - b
                term = cw_ref[o:o + 1, :] * ext_ref[a0:a0 + rc + sub, :]
                part = term if part is None else part + term
            if b == 0:
                shifted = part[:rc]
            else:
                ph_ref[b] = part
                shifted = ph_ref[b, b:b + rc, :]
            acc = shifted if acc is None else acc + shifted
        cv_ref[c * rc:(c + 1) * rc, :] = acc + cb_ref[...]

    cv = cv_ref[...]
    mu = jnp.mean(cv, axis=-1, keepdims=True)
    xc = cv - mu
    y = xc * lax.rsqrt(jnp.mean(xc * xc, axis=-1, keepdims=True) + LN_EPS) * lng_ref[...] + lnb_ref[...]
    y = y * jax.nn.sigmoid(y)
    branch_b = _dot(y.astype(BF16), wpb_ref[...]) + bpb_ref[...]
    branch_a = _dot_tn(_tile_select(i, n_first, ota_ref, otb_ref)[0], woa_ref[...])
    gates = gates_ref[...]
    mix = gates[:, :D_MODEL].astype(F32) * branch_a + gates[:, D_MODEL:].astype(F32) * branch_b
    x1 = _tile_select(i, n_first, xa_ref, xb_ref) + _dot(mix.astype(BF16), wout_ref[...])
    x1_ref[...] = x1

    h2 = _rms(x1, gffn_ref[...])
    h2_hi = h2.astype(BF16)
    h2_lo = (h2 - h2_hi.astype(F32)).astype(BF16)
    hi_both = _dot(h2_hi, wr_ref[...])
    logits = hi_both[:, :LANES] + (hi_both[:, LANES:] + _dot(h2_lo, wr_ref[:, :LANES])) + br_ref[...]
    lane = lax.broadcasted_iota(I32, logits.shape, 1)
    lane_f = lane.astype(F32)
    vals, idxs, sels = [], [], []
    cur = logits
    for _ in range(TOP_K):
        mval = jnp.max(cur, axis=-1, keepdims=True)
        ik = jnp.min(jnp.where(cur == mval, lane_f, float(LANES)), axis=-1, keepdims=True)
        sel = lane_f == ik
        cur = jnp.where(sel, -jnp.inf, cur)
        vals.append(mval)
        idxs.append(ik)
        sels.append(sel)
    exps = [jnp.exp(v - vals[0]) for v in vals]
    denom = exps[0] + exps[1] + exps[2] + exps[3]

    onehot = sels[0].astype(F32) + sels[1].astype(F32) + sels[2].astype(F32) + sels[3].astype(F32)

    @pl.when(i == 0)
    def _():
        carry_ref[...] = jnp.zeros_like(carry_ref)

    n_e = jnp.sum(onehot, axis=0, keepdims=True)
    n_pieces = jnp.floor((n_e + (RUN_PAD - 1)) * (1.0 / RUN_PAD))
    pieces_b = jnp.broadcast_to(n_pieces, (8, LANES)).astype(BF16)
    off = _dot(pieces_b, utri_ref[...])[0:1] * RUN_PAD
    total_pieces = jnp.sum(n_pieces, axis=-1, keepdims=True)
    carry = carry_ref[...]
    prefix = _dot(ltri_ref[...], onehot.astype(BF16))
    local = off + prefix
    route = jnp.zeros(logits.shape, F32)
    for kk in range(TOP_K):
        pos = jnp.sum(jnp.where(sels[kk], local, 0.0), axis=-1, keepdims=True)
        route = jnp.where(lane == kk, idxs[kk], route)
        route = jnp.where(lane == TOP_K + kk, exps[kk] / denom, route)
        route = jnp.where(lane == 2 * TOP_K + kk, pos, route)
    route_ref[...] = route
    carry_ref[...] = carry + n_pieces * RUN_PAD

    row8 = lax.broadcasted_iota(I32, (8, LANES), 0)
    metav = jnp.where(row8 == 0, jnp.broadcast_to(carry, (8, LANES)), 0.0)
    metav = jnp.where(row8 == 1, jnp.broadcast_to(n_pieces, (8, LANES)), metav)
    metav = jnp.where(row8 == 2, jnp.broadcast_to(off, (8, LANES)), metav)
    metav = jnp.where(row8 == 3, jnp.broadcast_to(total_pieces, (8, LANES)), metav)
    meta_ref[0] = metav.astype(I32)


def _post_mix(xa, xb, ota, otb, u, gates, w_oa, conv_w, conv_b, ln_g, ln_b, w_pb, b_pb, w_out, g_ffn, w_r, b_r, ltri,
              utri, *, tm, seq_bounds):
    na, nb = xa.shape[0] // tm, xb.shape[0] // tm
    nt = na + nb
    t = nt * tm
    hb = tm // HALO_ROWS
    n_halo = t // HALO_ROWS
    row = lambda w: pl.BlockSpec((tm, w), lambda i: (i, 0))
    return pl.pallas_call(
        functools.partial(_post_mix_kernel, tm=tm, n_first=na, seq_bounds=seq_bounds),
        grid=(nt,),
        in_specs=_split_specs((tm, D_MODEL), na, nb) + _split_specs((1, N_HEADS * D_V, tm), na, nb) + [
                  row(CONV_CH),
                  pl.BlockSpec((HALO_ROWS, CONV_CH), lambda i: (jnp.maximum(i * hb - 1, 0), 0)),
                  pl.BlockSpec((HALO_ROWS, CONV_CH), lambda i: (jnp.minimum((i + 1) * hb, n_halo - 1), 0)),
                  row(2 * D_MODEL), _const_spec((N_HEADS * D_V, D_MODEL)), _const_spec((CONV_WIDTH, CONV_CH)),
                  _const_spec((1, CONV_CH)), _const_spec((1, CONV_CH)), _const_spec((1, CONV_CH)),
                  _const_spec((CONV_CH, D_MODEL)), _const_spec((1, D_MODEL)), _const_spec((D_MODEL, D_MODEL)),
                  _const_spec((1, D_MODEL)), _const_spec((D_MODEL, 2 * LANES)), _const_spec((1, LANES)),
                  _const_spec((tm, tm)), _const_spec((LANES, LANES))],
        out_specs=(row(D_MODEL), row(LANES), pl.BlockSpec((1, 8, LANES), lambda i: (i, 0, 0))),
        out_shape=(jax.ShapeDtypeStruct((t, D_MODEL), F32), jax.ShapeDtypeStruct((t, LANES), F32),
                   jax.ShapeDtypeStruct((nt, 8, LANES), I32)),
        scratch_shapes=[pltpu.VMEM((tm + 2 * HALO_ROWS, CONV_CH), F32), pltpu.VMEM((tm, CONV_CH), F32),
                        pltpu.VMEM((8, min(tm, 64) + 8, CONV_CH), F32), pltpu.VMEM((1, LANES), F32)],
        compiler_params=pltpu.CompilerParams(dimension_semantics=("arbitrary",), vmem_limit_bytes=VMEM_LIMIT),
        name="post_mix",
    )(xa, xb, ota, otb, u, u, u, gates, w_oa, conv_w, conv_b, ln_g, ln_b, w_pb, b_pb, w_out, g_ffn, w_r, b_r, ltri, utri)


def _dispatch_kernel(npc_s, off_s, dst_s, start_s, x1_ref, route_ref, gffn_ref, xs_hbm, xloc_ref, zero_ref, row_sem,
                     *, tm, nt, bm, n_rows):
    i = pl.program_id(0)
    slot = lax.rem(i, 2)
    n_loc = _local_rows(tm)

    h2 = _rms(x1_ref[...], gffn_ref[...]).astype(BF16)
    pos_t = jnp.transpose(route_ref[...])[2 * TOP_K:3 * TOP_K, :].astype(I32)
    prow = lax.broadcasted_iota(I32, (n_loc, tm), 0)
    perm = jnp.zeros((n_loc, tm), F32)
    for kk in range(TOP_K):
        perm = jnp.where(prow == pos_t[kk:kk + 1, :], 1.0, perm)
    xloc_ref[slot] = _pack_rows(_dot(perm.astype(BF16), h2))

    @pl.when(i > 0)
    def _():
        _wait_tile(xloc_ref.at[0], xs_hbm, row_sem, npc_s, i - 1)

    _for_each_piece(npc_s, off_s, dst_s, i,
                    lambda s, d, rows: _rows_copy(xloc_ref.at[slot], s, xs_hbm, d, row_sem, rows=rows).start())

    @pl.when(i == nt - 1)
    def _():
        _wait_tile(xloc_ref.at[0], xs_hbm, row_sem, npc_s, i)
        zero_ref[...] = jnp.zeros(zero_ref.shape, U32)

        def tail(e, c):
            end = dst_s[i * N_EXPERTS + e] + npc_s[i * N_EXPERTS + e] * RUN_PAD
            n = lax.shift_right_logical(start_s[e + 1] - end, int(math.log2(RUN_PAD)))

            def piece(q, c2):
                _rows_copy(zero_ref, 0, xs_hbm, end + q * RUN_PAD, row_sem).start()
                return c2

            lax.fori_loop(0, n, piece, 0)
            return c + n

        _wait_rows(zero_ref, xs_hbm, row_sem, lax.fori_loop(0, N_EXPERTS, tail, 0), RUN_PAD)
        used = start_s[N_EXPERTS]
        n_free = (n_rows - used) // bm

        def fill(b, c):
            _rows_copy(zero_ref, 0, xs_hbm, used + b * bm, row_sem, rows=bm).start()
            return c

        lax.fori_loop(0, n_free, fill, 0)
        _wait_rows(zero_ref, xs_hbm, row_sem, n_free, bm)


def _dispatch(n_pieces, off, dst, start, x1, route, g_ffn, *, tm, bm, n_rows):
    t = x1.shape[0]
    nt = t // tm
    return pl.pallas_call(
        functools.partial(_dispatch_kernel, tm=tm, nt=nt, bm=bm, n_rows=n_rows),
        grid_spec=pltpu.PrefetchScalarGridSpec(
            num_scalar_prefetch=4,
            grid=(nt,),
            in_specs=[pl.BlockSpec((tm, D_MODEL), lambda i, *_: (i, 0)), pl.BlockSpec((tm, LANES), lambda i, *_: (i, 0)),
                      pl.BlockSpec((1, D_MODEL), lambda i, *_: (0, 0))],
            out_specs=pl.BlockSpec(memory_space=pl.ANY),
            scratch_shapes=[pltpu.VMEM((2, _local_rows(tm), PACKED), U32), pltpu.VMEM((bm, PACKED), U32),
                            pltpu.SemaphoreType.DMA],
        ),
        out_shape=jax.ShapeDtypeStruct((n_rows, PACKED), U32),
        compiler_params=pltpu.CompilerParams(dimension_semantics=("arbitrary",), vmem_limit_bytes=VMEM_LIMIT,
                                             has_side_effects=True),
        name="dispatch",
    )(n_pieces, off, dst, start, x1, route, g_ffn)


def _moe_kernel(exp_ref, n_ref, xs_ref, wgu_ref, bgu_ref, wd_ref, bd_ref, y_ref, wgu_bf, wd_bf):
    w = pl.program_id(0)
    used = w < n_ref[0]

    @pl.when(jnp.logical_or(w == 0, exp_ref[w] != exp_ref[jnp.maximum(w - 1, 0)]))
    def _():
        wgu_bf[...] = wgu_ref[0].astype(BF16)
        wd_bf[...] = wd_ref[0].astype(BF16)

    @pl.when(used)
    def _():
        x = _unpack_rows(xs_ref[...])
        gu = _dot(x, wgu_bf[...]) + bgu_ref[0]
        g = jnp.minimum(gu[:, :D_FF], SWIGLU_LIMIT)
        u = jnp.clip(gu[:, D_FF:], -SWIGLU_LIMIT, SWIGLU_LIMIT)
        act = (u + 1.0) * (g * jax.nn.sigmoid(SWIGLU_ALPHA * g))
        y_ref[...] = _pack_rows(_dot(act.astype(BF16), wd_bf[...]) + bd_ref[0])

    @pl.when(jnp.logical_not(used))
    def _():
        y_ref[...] = jnp.zeros(y_ref.shape, U32)


def _moe(ex, n_used, xs, w_gu, b_gu, w_down, b_down, *, bm):
    return pl.pallas_call(
        _moe_kernel,
        grid_spec=pltpu.PrefetchScalarGridSpec(
            num_scalar_prefetch=2,
            grid=(xs.shape[0] // bm,),
            in_specs=[pl.BlockSpec((bm, PACKED), lambda w, ex, n: (w, 0)),
                      pl.BlockSpec((1, D_MODEL, 2 * D_FF), lambda w, ex, n: (ex[w], 0, 0)),
                      pl.BlockSpec((1, 1, 2 * D_FF), lambda w, ex, n: (ex[w], 0, 0)),
                      pl.BlockSpec((1, D_FF, D_MODEL), lambda w, ex, n: (ex[w], 0, 0)),
                      pl.BlockSpec((1, 1, D_MODEL), lambda w, ex, n: (ex[w], 0, 0))],
            out_specs=pl.BlockSpec((bm, PACKED), lambda w, ex, n: (w, 0)),
            scratch_shapes=[pltpu.VMEM((D_MODEL, 2 * D_FF), BF16), pltpu.VMEM((D_FF, D_MODEL), BF16)],
        ),
        out_shape=jax.ShapeDtypeStruct(xs.shape, U32),
        compiler_params=pltpu.CompilerParams(dimension_semantics=("arbitrary",), vmem_limit_bytes=VMEM_LIMIT),
        name="moe",
    )(ex, n_used, xs, w_gu, b_gu, w_down, b_down)


def _expert_layout(totals, *, bm, n_blocks):
    n_blk = (totals + bm - 1) // bm
    blk_end = jnp.cumsum(n_blk)
    start = jnp.concatenate([jnp.zeros((1,), I32), blk_end * bm]).astype(I32)
    w = jnp.arange(n_blocks, dtype=I32)
    ex = jnp.minimum(jnp.sum((blk_end[None, :] <= w[:, None]).astype(I32), axis=1), N_EXPERTS - 1)
    return start, ex.astype(I32), blk_end[-1].astype(I32)[None]


def _combine_kernel(npc_s, src_s, off_s, x1_ref, route_ref, gfin_ref, y_hbm, out_ref, yloc_ref, sems, *, tm, nt):
    i = pl.program_id(0)
    slot = lax.rem(i, 2)
    n_loc = _local_rows(tm)

    def gather(tile, sl):
        _for_each_piece(npc_s, src_s, off_s, tile,
                        lambda s, d, rows: _rows_copy(y_hbm, s, yloc_ref.at[sl], d, sems.at[sl], rows=rows).start())

    @pl.when(i == 0)
    def _():
        yloc_ref[...] = jnp.zeros(yloc_ref.shape, U32)
        gather(0, 0)

    @pl.when(i + 1 < nt)
    def _():
        gather(i + 1, 1 - slot)

    _wait_tile(y_hbm, yloc_ref.at[slot], sems.at[slot], npc_s, i)

    route = route_ref[...]
    lanes = lax.broadcasted_iota(I32, (tm, n_loc), 1)
    pw = jnp.zeros((tm, n_loc), F32)
    for kk in range(TOP_K):
        pos = route[:, 2 * TOP_K + kk:2 * TOP_K + kk + 1].astype(I32)
        pw = jnp.where(lanes == pos, route[:, TOP_K + kk:TOP_K + kk + 1], pw)
    pw_hi = pw.astype(BF16)
    pw_lo = (pw - pw_hi.astype(F32)).astype(BF16)
    yb = _unpack_rows(yloc_ref[slot])
    moe = _dot(pw_hi, yb) + _dot(pw_lo, yb)
    out_ref[...] = _rms(x1_ref[...] + moe, gfin_ref[...])


def _combine(n_pieces, src, off, x1, route, g_final, y, *, tm):
    t = x1.shape[0]
    nt = t // tm
    return pl.pallas_call(
        functools.partial(_combine_kernel, tm=tm, nt=nt),
        grid_spec=pltpu.PrefetchScalarGridSpec(
            num_scalar_prefetch=3,
            grid=(nt,),
            in_specs=[pl.BlockSpec((tm, D_MODEL), lambda i, *_: (i, 0)), pl.BlockSpec((tm, LANES), lambda i, *_: (i, 0)),
                      pl.BlockSpec((1, D_MODEL), lambda i, *_: (0, 0)), pl.BlockSpec(memory_space=pl.ANY)],
            out_specs=pl.BlockSpec((tm, D_MODEL), lambda i, *_: (i, 0)),
            scratch_shapes=[pltpu.VMEM((2, _local_rows(tm), PACKED), U32), pltpu.SemaphoreType.DMA((2,))],
        ),
        out_shape=jax.ShapeDtypeStruct((t, D_MODEL), F32),
        compiler_params=pltpu.CompilerParams(dimension_semantics=("arbitrary",), vmem_limit_bytes=VMEM_LIMIT),
        name="combine",
    )(n_pieces, src, off, x1, route, g_final, y)


def _rot_cols(w):
    half = w.shape[-1] // 2
    return jnp.concatenate([-w[..., half:], w[..., :half]], axis=-1)


def _prep_weights(w_in, w_uq, w_ukv):
    d = w_in.shape[0]
    w_kpe = w_in[:, C_KPE:C_KPE + D_ROPE]
    w_in_ext = jnp.concatenate(
        [w_in[:, :C_KPE], w_kpe, _rot_cols(w_kpe), jnp.zeros((d, LANES - 2 * D_ROPE), F32),
         w_in[:, C_KPE + D_ROPE:]], axis=1).astype(BF16)

    wq = w_uq.reshape(Q_RANK, N_HEADS, D_QK)
    nope, pe = wq[..., :D_NOPE], wq[..., D_NOPE:]
    zpad = jnp.zeros((Q_RANK, N_HEADS, LANES - D_QK), F32)
    plain = jnp.concatenate([nope, pe, zpad], axis=-1).reshape(Q_RANK, N_HEADS * LANES)
    rot = _rot_cols(pe).reshape(Q_RANK, N_HEADS * D_ROPE)
    wq_t = jnp.concatenate([plain, rot], axis=1).T.astype(BF16)

    wkv = w_ukv.reshape(KV_RANK, N_HEADS, D_NOPE + D_V)
    wk_ext = jnp.concatenate([wkv[..., :D_NOPE], jnp.zeros((KV_RANK, N_HEADS, LANES - D_NOPE), F32)],
                             axis=-1).reshape(KV_RANK, N_HEADS * LANES).astype(BF16)
    wv_t = wkv[..., D_NOPE:].reshape(KV_RANK, N_HEADS * D_V).T.astype(BF16)
    return w_in_ext, wq_t, wk_ext, wv_t


def _kpe_placement():
    e2 = np.zeros((LANES, N_HEADS * LANES), np.float32)
    for j in range(D_ROPE):
        for h in range(N_HEADS):
            e2[j, h * LANES + D_NOPE + j] = 1.0
            e2[D_ROPE + j, h * LANES + D_NOPE + j] = 1.0
    return jnp.asarray(e2, BF16)


def _inv_freq(tm):
    inv = 1.0 / (ROPE_THETA ** (jnp.arange(0, D_ROPE, 2, dtype=F32) / D_ROPE))
    return jnp.broadcast_to(inv[:, None], (D_ROPE // 2, tm))


def _pick(n, pref):
    return pref if n % pref == 0 else n


def kernel(x_prompt, x_sample, g_mix, w_in, b_gate, g_q, w_uq, g_kv, w_ukv, w_oa, conv_w, conv_b, ln_g, ln_b, w_pb,
           b_pb, w_out, g_ffn, w_router, b_router, w_gu, b_gu, w_down, b_down, g_final):
    b1, s1, d = x_prompt.shape
    b2, s2, _ = x_sample.shape
    t1, t2 = b1 * s1, b2 * s2
    t = t1 + t2
    xa, xb = x_prompt.reshape(t1, d), x_sample.reshape(t2, d)
    seq_bounds = tuple((b * s1, (b + 1) * s1) for b in range(b1)) + tuple(
        (t1 + b * s2, t1 + (b + 1) * s2) for b in range(b2))

    tm = _pick(math.gcd(s1, s2), 512)
    nt = t // tm
    w_in_ext, wq_t, wk_ext, wv_t = _prep_weights(w_in[0], w_uq[0], w_ukv[0])

    qt, k, vt3, u, gates = _in_proj(xa, xb, _inv_freq(tm), g_mix, w_in_ext, g_q, wq_t, g_kv, wk_ext, wv_t,
                                    _kpe_placement(), b_gate, tm=tm, seq_starts=tuple(s for s, _ in seq_bounds))

    ota = _attention(qt, k, vt3, seq_start=0, seq_len=s1, n_seq=b1, tq=tm, tk=tm)
    otb = _attention(qt, k, vt3, seq_start=t1, seq_len=s2, n_seq=b2, tq=tm, tk=tm)

    w_r = jnp.concatenate([w_router[0], jnp.zeros((d, LANES - N_EXPERTS), F32)], axis=1)
    w_r_hi = w_r.astype(BF16)
    w_r = jnp.concatenate([w_r_hi, (w_r - w_r_hi.astype(F32)).astype(BF16)], axis=1)
    b_r = jnp.concatenate([b_router[0], jnp.full((LANES - N_EXPERTS,), NEG_BIG, F32)])[None, :]
    ltri = jnp.asarray(np.tril(np.ones((tm, tm), np.float32), -1), BF16)
    utri = jnp.asarray(np.triu(np.ones((LANES, LANES), np.float32), 1), BF16)
    x1, route, meta = _post_mix(xa, xb, ota, otb, u, gates, w_oa[0].astype(BF16), conv_w[0], conv_b, ln_g, ln_b,
                                w_pb[0].astype(BF16), b_pb, w_out[0].astype(BF16), g_ffn, w_r, b_r, ltri, utri,
                                tm=tm, seq_bounds=seq_bounds)

    bm = MOE_ROWS
    n_rows = _sorted_rows(t, nt, bm)
    carry, n_pieces, off = (meta[:, r, :N_EXPERTS] for r in range(3))
    start, ex, n_used = _expert_layout(carry[-1] + n_pieces[-1] * RUN_PAD, bm=bm, n_blocks=n_rows // bm)
    run_row = (carry + start[None, :N_EXPERTS]).reshape(-1)
    n_pieces, off = n_pieces.reshape(-1), off.reshape(-1)

    xs = _dispatch(n_pieces, off, run_row, start, x1, route, g_ffn, tm=tm, bm=bm, n_rows=n_rows)
    y = _moe(ex, n_used, xs, w_gu[0], b_gu[0][:, None, :], w_down[0], b_down[0][:, None, :], bm=bm)
    out = _combine(n_pieces, run_row, off, x1, route, g_final[None, :], y, tm=tm)
    return out[:t1].reshape(b1, s1, d), out[t1:].reshape(b2, s2, d)
```

```python
import functools
import math

import numpy as np
import jax
import jax.numpy as jnp
from jax import lax
from jax.experimental import pallas as pl
from jax.experimental.pallas import tpu as pltpu

F32 = jnp.float32
BF16 = jnp.bfloat16
I32 = jnp.int32
U32 = jnp.uint32

D_MODEL = 1024
N_HEADS = 8
Q_RANK = 384
KV_RANK = 256
D_NOPE = 64
D_ROPE = 32
D_V = 64
D_QK = D_NOPE + D_ROPE
ROPE_THETA = 10000.0
CONV_CH = 512
CONV_WIDTH = 31
CONV_PAD = (CONV_WIDTH - 1) // 2
N_EXPERTS = 32
TOP_K = 4
D_FF = 1024
SWIGLU_ALPHA = 1.702
SWIGLU_LIMIT = 7.0
RMS_EPS = 1e-6
LN_EPS = 1e-5

LANES = 128
HALO_ROWS = 16
NEG_BIG = -1e30
RUN_PAD = 8
BIG_PIECE = 32
N_DISPATCH_BUFS = 3
MOE_ROWS = 512

C_CQ = 0
C_CKV = Q_RANK
C_KPE = Q_RANK + KV_RANK
C_CONV = C_KPE + LANES
C_GATE = C_CONV + 2 * CONV_CH
D_IN_EXT = C_GATE + 2 * D_MODEL

VMEM_LIMIT = 56 * 1024 * 1024


def _rms(x, g):
    return x * lax.rsqrt(jnp.mean(x * x, axis=-1, keepdims=True) + RMS_EPS) * g


def _dot(a, b):
    return jnp.dot(a, b, preferred_element_type=F32)


def _dot_nt(a, b):
    return lax.dot_general(a, b, (((1,), (1,)), ((), ())), preferred_element_type=F32)


def _dot_tn(a, b):
    return lax.dot_general(a, b, (((0,), (0,)), ((), ())), preferred_element_type=F32)


def _const_spec(shape):
    return pl.BlockSpec(shape, lambda *_: (0,) * len(shape))


PACKED = D_MODEL // 2
HIGH_HALF = 0xFFFF0000


def _pack_rows(v):
    bits = lambda x: pltpu.bitcast(x.astype(BF16).astype(F32), U32)
    return bits(v[:, :PACKED]) | lax.shift_right_logical(bits(v[:, PACKED:]), jnp.uint32(16))


def _unpack_rows(w):
    hi = pltpu.bitcast(w & jnp.uint32(HIGH_HALF), F32).astype(BF16)
    lo = pltpu.bitcast(lax.shift_left(w, jnp.uint32(16)), F32).astype(BF16)
    return jnp.concatenate([hi, lo], axis=1)


def _tile_select(i, n_first, a_ref, b_ref):
    return jnp.where(i < n_first, a_ref[...], b_ref[...])


def _in_proj_kernel(xa_ref, xb_ref, inv_ref, gmix_ref, win_ref, gq_ref, wqt_ref, gkv_ref, wk_ref, wvt_ref,
                    e2_ref, bgate_ref, qt_ref, k_ref, vt_ref, u_ref, gates_ref, *, tm, n_first, seq_starts):
    i = pl.program_id(0)
    h = _rms(_tile_select(i, n_first, xa_ref, xb_ref), gmix_ref[...]).astype(BF16)
    proj = _dot(h, win_ref[...])

    t0 = i * tm
    s0 = 0
    for s in seq_starts:
        s0 = jnp.where(t0 >= s, s, s0)
    half = D_ROPE // 2
    pos = (t0 - s0 + lax.broadcasted_iota(I32, (half, tm), 1)).astype(F32)
    ang = pos * inv_ref[...]
    cos, sin = jnp.cos(ang), jnp.sin(ang)
    scale = (D_QK ** -0.5) * math.log2(math.e)
    cos_q, sin_q = cos * scale, sin * scale

    cqn = _rms(proj[:, C_CQ:C_CQ + Q_RANK], gq_ref[...]).astype(BF16)
    q2 = _dot_nt(wqt_ref[...], cqn)
    hw = N_HEADS * LANES
    for hh in range(N_HEADS):
        qa = q2[hh * LANES:(hh + 1) * LANES, :]
        qb = q2[hw + hh * D_ROPE:hw + (hh + 1) * D_ROPE, :]
        r0 = hh * LANES
        qt_ref[0, r0:r0 + D_NOPE, :] = (qa[:D_NOPE] * scale).astype(BF16)
        for part in range(2):
            lo = D_NOPE + part * half
            qt_ref[0, r0 + lo:r0 + lo + half, :] = (
                qa[lo:lo + half] * cos_q + qb[part * half:(part + 1) * half] * sin_q).astype(BF16)
        qt_ref[0, r0 + D_QK:r0 + LANES, :] = jnp.zeros((LANES - D_QK, tm), BF16)

    ckvn = _rms(proj[:, C_CKV:C_CKV + KV_RANK], gkv_ref[...]).astype(BF16)
    tabk = jnp.transpose(jnp.concatenate([cos, cos, sin, sin, jnp.zeros((LANES - 2 * D_ROPE, tm), F32)], axis=0))
    kpe = (proj[:, C_KPE:C_KPE + LANES] * tabk).astype(BF16)
    k_ref[...] = (_dot(ckvn, wk_ref[...]) + _dot(kpe, e2_ref[...])).astype(BF16)
    vt_ref[0] = _dot_nt(wvt_ref[...], ckvn).astype(BF16)

    a = proj[:, C_CONV:C_CONV + CONV_CH]
    g = proj[:, C_CONV + CONV_CH:C_CONV + 2 * CONV_CH]
    u_ref[...] = a * jax.nn.sigmoid(g)
    gates_ref[...] = jax.nn.sigmoid(proj[:, C_GATE:] + bgate_ref[...]).astype(BF16)


def _split_specs(block, n_first, n_second):
    first = lambda i: jnp.minimum(i, n_first - 1)
    second = lambda i: jnp.clip(i - n_first, 0, n_second - 1)
    rest = (0,) * (len(block) - 1)
    return [pl.BlockSpec(block, lambda i: (first(i),) + rest), pl.BlockSpec(block, lambda i: (second(i),) + rest)]


def _in_proj(xa, xb, inv_b, g_mix, w_in_ext, g_q, wq_t, g_kv, wk_ext, wv_t, e2, b_gate, *, tm, seq_starts):
    na, nb = xa.shape[0] // tm, xb.shape[0] // tm
    nt = na + nb
    t = nt * tm
    row = lambda w: pl.BlockSpec((tm, w), lambda i: (i, 0))
    col = lambda h: pl.BlockSpec((h, tm), lambda i: (0, i))
    return pl.pallas_call(
        functools.partial(_in_proj_kernel, tm=tm, n_first=na, seq_starts=seq_starts),
        grid=(nt,),
        in_specs=_split_specs((tm, D_MODEL), na, nb) + [
                  _const_spec((D_ROPE // 2, tm)), _const_spec((1, D_MODEL)),
                  _const_spec((D_MODEL, D_IN_EXT)), _const_spec((1, Q_RANK)),
                  _const_spec((N_HEADS * (LANES + D_ROPE), Q_RANK)), _const_spec((1, KV_RANK)),
                  _const_spec((KV_RANK, N_HEADS * LANES)), _const_spec((N_HEADS * D_V, KV_RANK)),
                  _const_spec((LANES, N_HEADS * LANES)), _const_spec((1, 2 * D_MODEL))],
        out_specs=(pl.BlockSpec((1, N_HEADS * LANES, tm), lambda i: (i, 0, 0)), row(N_HEADS * LANES),
                   pl.BlockSpec((1, N_HEADS * D_V, tm), lambda i: (i, 0, 0)), row(CONV_CH), row(2 * D_MODEL)),
        out_shape=(jax.ShapeDtypeStruct((nt, N_HEADS * LANES, tm), BF16),
                   jax.ShapeDtypeStruct((t, N_HEADS * LANES), BF16),
                   jax.ShapeDtypeStruct((nt, N_HEADS * D_V, tm), BF16),
                   jax.ShapeDtypeStruct((t, CONV_CH), F32),
                   jax.ShapeDtypeStruct((t, 2 * D_MODEL), BF16)),
        compiler_params=pltpu.CompilerParams(dimension_semantics=("arbitrary",), vmem_limit_bytes=VMEM_LIMIT),
        name="in_proj",
    )(xa, xb, inv_b, g_mix, w_in_ext, g_q, wq_t, g_kv, wk_ext, wv_t, e2, b_gate)


def _attn_kernel(qt_ref, k_ref, vt_ref, o_ref, s_ref, acc_ref, m_ref, *, nq, n_chunks, tk, unroll):
    half = n_chunks // 2
    ones = jnp.ones((16, tk), BF16)

    def scores(a, j, slot):
        ks = pl.multiple_of(j * tk, tk)
        s_ref[slot] = _dot(k_ref[pl.ds(ks, tk), :], qt_ref[a])

    def softmax_pv(j, slot):
        st = s_ref[slot]
        m = jnp.where(j == 0, NEG_BIG, m_ref[...])
        m_new = jnp.maximum(m, jnp.max(st, axis=0, keepdims=True))
        p = jnp.exp2(st - m_new).astype(BF16)
        v1 = jnp.concatenate([vt_ref[j], ones], axis=0)
        acc = jnp.exp2(m - m_new) * acc_ref[...] + _dot(v1, p)
        acc_ref[...] = acc
        m_ref[...] = m_new
        return acc

    m_ref[...] = jnp.full(m_ref.shape, NEG_BIG, F32)
    acc_ref[...] = jnp.zeros(acc_ref.shape, F32)
    scores(0, 0, 0)

    def pair(pp, may_end_tile):
        a = lax.div(pp, half)
        j = 2 * lax.rem(pp, half)
        scores(a, j + 1, 1)
        softmax_pv(j, 0)
        nxt = pp + 1
        scores(jnp.minimum(lax.div(nxt, half), nq - 1), 2 * lax.rem(nxt, half), 0)
        acc = softmax_pv(j + 1, 1)
        if may_end_tile:
            o_ref[a] = (acc[:D_V] / acc[D_V:D_V + 1]).astype(o_ref.dtype)

    period = math.gcd(half, unroll)

    def trip(blk, c):
        for u in range(unroll):
            pair(blk * unroll + u, (u + 1) % period == 0)
        return c

    lax.fori_loop(0, nq * half // unroll, trip, 0)


def _attention(qt3, k, vt3, *, seq_start, seq_len, n_seq, tq, tk):
    nq = seq_len // tq
    nkc = seq_len // tk
    assert nkc % 2 == 0, "key chunks are processed in pairs"
    s0 = seq_start // seq_len
    n_pairs = nq * (nkc // 2)
    unroll = next(u for u in (8, 4, 2, 1) if n_pairs % u == 0)
    return pl.pallas_call(
        functools.partial(_attn_kernel, nq=nq, n_chunks=nkc, tk=tk, unroll=unroll),
        grid=(n_seq, N_HEADS),
        in_specs=[pl.BlockSpec((nq, LANES, tq), lambda b, h: (s0 + b, h, 0)),
                  pl.BlockSpec((seq_len, LANES), lambda b, h: (s0 + b, h)),
                  pl.BlockSpec((nkc, D_V, tk), lambda b, h: (s0 + b, h, 0))],
        out_specs=pl.BlockSpec((nq, D_V, tq), lambda b, h: (b, h, 0)),
        out_shape=jax.ShapeDtypeStruct((n_seq * nq, N_HEADS * D_V, tq), BF16),
        scratch_shapes=[pltpu.VMEM((2, tk, tq), F32), pltpu.VMEM((D_V + 16, tq), F32), pltpu.VMEM((1, tq), F32)],
        compiler_params=pltpu.CompilerParams(dimension_semantics=("arbitrary", "arbitrary"),
                                             vmem_limit_bytes=VMEM_LIMIT),
        name="attention",
    )(qt3, k, vt3)


def _local_rows(tm):
    return TOP_K * tm + N_EXPERTS * RUN_PAD


def _sorted_rows(t, nt, bm):
    runs = t * TOP_K + nt * N_EXPERTS * (RUN_PAD - 1)
    return -(-runs // bm) * bm + N_EXPERTS * bm


def _rows_copy(src, s, dst, d, sem, rows=RUN_PAD):
    s = pl.multiple_of(s, RUN_PAD)
    d = pl.multiple_of(d, RUN_PAD)
    return pltpu.make_async_copy(src.at[pl.ds(s, rows), :], dst.at[pl.ds(d, rows), :], sem)


def _for_each_piece(n_ref, a_ref, b_ref, tile, fn):
    def per_expert(e, c):
        n = n_ref[tile * N_EXPERTS + e]
        a0 = a_ref[tile * N_EXPERTS + e]
        b0 = b_ref[tile * N_EXPERTS + e]
        n_big = lax.div(n, BIG_PIECE // RUN_PAD)
        done = n_big * BIG_PIECE

        def big(q, c2):
            fn(a0 + q * BIG_PIECE, b0 + q * BIG_PIECE, BIG_PIECE)
            return c2

        def small(q, c2):
            fn(a0 + done + q * RUN_PAD, b0 + done + q * RUN_PAD, RUN_PAD)
            return c2

        lax.fori_loop(0, n_big, big, 0)
        lax.fori_loop(0, n - n_big * (BIG_PIECE // RUN_PAD), small, 0)
        return c

    lax.fori_loop(0, N_EXPERTS, per_expert, 0)


def _tile_pieces(n_ref, tile):
    def per_expert(e, c):
        n = n_ref[tile * N_EXPERTS + e]
        n_big = lax.div(n, BIG_PIECE // RUN_PAD)
        return c[0] + n_big, c[1] + n - n_big * (BIG_PIECE // RUN_PAD)

    return lax.fori_loop(0, N_EXPERTS, per_expert, (0, 0))


def _wait_rows(src, dst, sem, count, rows):
    def one(q, c):
        _rows_copy(src, 0, dst, 0, sem, rows=rows).wait()
        return c

    lax.fori_loop(0, count, one, 0)


def _wait_tile(src, dst, sem, n_ref, tile):
    n_big, n_small = _tile_pieces(n_ref, tile)
    _wait_rows(src, dst, sem, n_big, BIG_PIECE)
    _wait_rows(src, dst, sem, n_small, RUN_PAD)


def _post_mix_kernel(xa_ref, xb_ref, ota_ref, otb_ref, u_ref, up_ref, un_ref, gates_ref, woa_ref, cw_ref, cb_ref,
                     lng_ref, lnb_ref, wpb_ref, bpb_ref, wout_ref, gffn_ref, wr_ref, br_ref, ltri_ref, utri_ref,
                     x1_ref, route_ref, meta_ref, ext_ref, cv_ref, ph_ref, carry_ref, *, tm, n_first, seq_bounds):
    i = pl.program_id(0)
    t0 = i * tm
    is_start = functools.reduce(jnp.logical_or, [t0 == s for s, _ in seq_bounds])
    is_end = functools.reduce(jnp.logical_or, [t0 + tm == e for _, e in seq_bounds])

    ext_ref[0:HALO_ROWS, :] = jnp.where(is_start, 0.0, up_ref[...])
    ext_ref[HALO_ROWS:HALO_ROWS + tm, :] = u_ref[...]
    ext_ref[HALO_ROWS + tm:2 * HALO_ROWS + tm, :] = jnp.where(is_end, 0.0, un_ref[...])
    rc = min(tm, 64)
    sub = 8
    base = HALO_ROWS - CONV_PAD
    for c in range(tm // rc):
        acc = None
        for b in range(sub):
            part = None
            for o in range(CONV_WIDTH):
                if (o + base) % sub != b:
                    continue
                a0 = c * rc + (o + base) - b
                term = cw_ref[o:o + 1, :] * ext_ref[a0:a0 + rc + sub, :]
                part = term if part is None else part + term
            if b == 0:
                shifted = part[:rc]
            else:
                ph_ref[b] = part
                shifted = ph_ref[b, b:b + rc, :]
            acc = shifted if acc is None else acc + shifted
        cv_ref[c * rc:(c + 1) * rc, :] = acc + cb_ref[...]

    cv = cv_ref[...]
    mu = jnp.mean(cv, axis=-1, keepdims=True)
    xc = cv - mu
    y = xc * lax.rsqrt(jnp.mean(xc * xc, axis=-1, keepdims=True) + LN_EPS) * lng_ref[...] + lnb_ref[...]
    y = y * jax.nn.sigmoid(y)
    branch_b = _dot(y.astype(BF16), wpb_ref[...]) + bpb_ref[...]
    branch_a = _dot_tn(_tile_select(i, n_first, ota_ref, otb_ref)[0], woa_ref[...])
    gates = gates_ref[...]
    mix = gates[:, :D_MODEL].astype(F32) * branch_a + gates[:, D_MODEL:].astype(F32) * branch_b
    x1 = _tile_select(i, n_first, xa_ref, xb_ref) + _dot(mix.astype(BF16), wout_ref[...])
    x1_ref[...] = x1

    h2 = _rms(x1, gffn_ref[...])
    h2_hi = h2.astype(BF16)
    h2_lo = (h2 - h2_hi.astype(F32)).astype(BF16)
    hi_both = _dot(h2_hi, wr_ref[...])
    logits = hi_both[:, :LANES] + (hi_both[:, LANES:] + _dot(h2_lo, wr_ref[:, :LANES])) + br_ref[...]
    lane = lax.broadcasted_iota(I32, logits.shape, 1)
    lane_f = lane.astype(F32)
    vals, idxs, sels = [], [], []
    cur = logits
    for _ in range(TOP_K):
        mval = jnp.max(cur, axis=-1, keepdims=True)
        ik = jnp.min(jnp.where(cur == mval, lane_f, float(LANES)), axis=-1, keepdims=True)
        sel = lane_f == ik
        cur = jnp.where(sel, -jnp.inf, cur)
        vals.append(mval)
        idxs.append(ik)
        sels.append(sel)
    exps = [jnp.exp(v - vals[0]) for v in vals]
    denom = exps[0] + exps[1] + exps[2] + exps[3]

    onehot = sels[0].astype(F32) + sels[1].astype(F32) + sels[2].astype(F32) + sels[3].astype(F32)

    @pl.when(i == 0)
    def _():
        carry_ref[...] = jnp.zeros_like(carry_ref)

    n_e = jnp.sum(onehot, axis=0, keepdims=True)
    n_pieces = jnp.floor((n_e + (RUN_PAD - 1)) * (1.0 / RUN_PAD))
    pieces_b = jnp.broadcast_to(n_pieces, (8, LANES)).astype(BF16)
    off = _dot(pieces_b, utri_ref[...])[0:1] * RUN_PAD
    total_pieces = jnp.sum(n_pieces, axis=-1, keepdims=True)
    carry = carry_ref[...]
    prefix = _dot(ltri_ref[...], onehot.astype(BF16))
    local = off + prefix
    route = jnp.zeros(logits.shape, F32)
    for kk in range(TOP_K):
        pos = jnp.sum(jnp.where(sels[kk], local, 0.0), axis=-1, keepdims=True)
        route = jnp.where(lane == kk, idxs[kk], route)
        route = jnp.where(lane == TOP_K + kk, exps[kk] / denom, route)
        route = jnp.where(lane == 2 * TOP_K + kk, pos, route)
    route_ref[...] = route
    carry_ref[...] = carry + n_pieces * RUN_PAD

    row8 = lax.broadcasted_iota(I32, (8, LANES), 0)
    metav = jnp.where(row8 == 0, jnp.broadcast_to(carry, (8, LANES)), 0.0)
    metav = jnp.where(row8 == 1, jnp.broadcast_to(n_pieces, (8, LANES)), metav)
    metav = jnp.where(row8 == 2, jnp.broadcast_to(off, (8, LANES)), metav)
    metav = jnp.where(row8 == 3, jnp.broadcast_to(total_pieces, (8, LANES)), metav)
    meta_ref[0] = metav.astype(I32)


def _post_mix(xa, xb, ota, otb, u, gates, w_oa, conv_w, conv_b, ln_g, ln_b, w_pb, b_pb, w_out, g_ffn, w_r, b_r, ltri,
              utri, *, tm, seq_bounds):
    na, nb = xa.shape[0] // tm, xb.shape[0] // tm
    nt = na + nb
    t = nt * tm
    hb = tm // HALO_ROWS
    n_halo = t // HALO_ROWS
    row = lambda w: pl.BlockSpec((tm, w), lambda i: (i, 0))
    return pl.pallas_call(
        functools.partial(_post_mix_kernel, tm=tm, n_first=na, seq_bounds=seq_bounds),
        grid=(nt,),
        in_specs=_split_specs((tm, D_MODEL), na, nb) + _split_specs((1, N_HEADS * D_V, tm), na, nb) + [
                  row(CONV_CH),
                  pl.BlockSpec((HALO_ROWS, CONV_CH), lambda i: (jnp.maximum(i * hb - 1, 0), 0)),
                  pl.BlockSpec((HALO_ROWS, CONV_CH), lambda i: (jnp.minimum((i + 1) * hb, n_halo - 1), 0)),
                  row(2 * D_MODEL), _const_spec((N_HEADS * D_V, D_MODEL)), _const_spec((CONV_WIDTH, CONV_CH)),
                  _const_spec((1, CONV_CH)), _const_spec((1, CONV_CH)), _const_spec((1, CONV_CH)),
                  _const_spec((CONV_CH, D_MODEL)), _const_spec((1, D_MODEL)), _const_spec((D_MODEL, D_MODEL)),
                  _const_spec((1, D_MODEL)), _const_spec((D_MODEL, 2 * LANES)), _const_spec((1, LANES)),
                  _const_spec((tm, tm)), _const_spec((LANES, LANES))],
        out_specs=(row(D_MODEL), row(LANES), pl.BlockSpec((1, 8, LANES), lambda i: (i, 0, 0))),
        out_shape=(jax.ShapeDtypeStruct((t, D_MODEL), F32), jax.ShapeDtypeStruct((t, LANES), F32),
                   jax.ShapeDtypeStruct((nt, 8, LANES), I32)),
        scratch_shapes=[pltpu.VMEM((tm + 2 * HALO_ROWS, CONV_CH), F32), pltpu.VMEM((tm, CONV_CH), F32),
                        pltpu.VMEM((8, min(tm, 64) + 8, CONV_CH), F32), pltpu.VMEM((1, LANES), F32)],
        compiler_params=pltpu.CompilerParams(dimension_semantics=("arbitrary",), vmem_limit_bytes=VMEM_LIMIT),
        name="post_mix",
    )(xa, xb, ota, otb, u, u, u, gates, w_oa, conv_w, conv_b, ln_g, ln_b, w_pb, b_pb, w_out, g_ffn, w_r, b_r, ltri, utri)


def _dispatch_kernel(npc_s, off_s, dst_s, start_s, x1_ref, route_ref, gffn_ref, xs_hbm, xloc_ref, zero_ref, row_sems,
                     *, tm, nt, bm, n_rows):
    i = pl.program_id(0)
    slot = lax.rem(i, N_DISPATCH_BUFS)
    n_loc = _local_rows(tm)

    h2 = _rms(x1_ref[...], gffn_ref[...]).astype(BF16)
    pos_t = jnp.transpose(route_ref[...])[2 * TOP_K:3 * TOP_K, :].astype(I32)
    prow = lax.broadcasted_iota(I32, (n_loc, tm), 0)
    perm = jnp.zeros((n_loc, tm), F32)
    for kk in range(TOP_K):
        perm = jnp.where(prow == pos_t[kk:kk + 1, :], 1.0, perm)
    xloc_ref[slot] = _pack_rows(_dot(perm.astype(BF16), h2))

    def wait_tile(tile):
        _wait_tile(xloc_ref.at[0], xs_hbm, row_sems.at[lax.rem(tile, N_DISPATCH_BUFS)], npc_s, tile)

    @pl.when(i >= N_DISPATCH_BUFS - 1)
    def _():
        wait_tile(i - (N_DISPATCH_BUFS - 1))

    _for_each_piece(npc_s, off_s, dst_s, i,
                    lambda s, d, rows: _rows_copy(xloc_ref.at[slot], s, xs_hbm, d, row_sems.at[slot], rows=rows).start())

    @pl.when(i == nt - 1)
    def _():
        for back in range(N_DISPATCH_BUFS - 2, -1, -1):
            @pl.when(i - back >= 0)
            def _():
                wait_tile(i - back)
        row_sem = row_sems.at[0]
        zero_ref[...] = jnp.zeros(zero_ref.shape, U32)

        def tail(e, c):
            end = dst_s[i * N_EXPERTS + e] + npc_s[i * N_EXPERTS + e] * RUN_PAD
            n = lax.shift_right_logical(start_s[e + 1] - end, int(math.log2(RUN_PAD)))

            def piece(q, c2):
                _rows_copy(zero_ref, 0, xs_hbm, end + q * RUN_PAD, row_sem).start()
                return c2

            lax.fori_loop(0, n, piece, 0)
            return c + n

        _wait_rows(zero_ref, xs_hbm, row_sem, lax.fori_loop(0, N_EXPERTS, tail, 0), RUN_PAD)
        used = start_s[N_EXPERTS]
        n_free = (n_rows - used) // bm

        def fill(b, c):
            _rows_copy(zero_ref, 0, xs_hbm, used + b * bm, row_sem, rows=bm).start()
            return c

        lax.fori_loop(0, n_free, fill, 0)
        _wait_rows(zero_ref, xs_hbm, row_sem, n_free, bm)


def _dispatch(n_pieces, off, dst, start, x1, route, g_ffn, *, tm, bm, n_rows):
    t = x1.shape[0]
    nt = t // tm
    return pl.pallas_call(
        functools.partial(_dispatch_kernel, tm=tm, nt=nt, bm=bm, n_rows=n_rows),
        grid_spec=pltpu.PrefetchScalarGridSpec(
            num_scalar_prefetch=4,
            grid=(nt,),
            in_specs=[pl.BlockSpec((tm, D_MODEL), lambda i, *_: (i, 0)), pl.BlockSpec((tm, LANES), lambda i, *_: (i, 0)),
                      pl.BlockSpec((1, D_MODEL), lambda i, *_: (0, 0))],
            out_specs=pl.BlockSpec(memory_space=pl.ANY),
            scratch_shapes=[pltpu.VMEM((N_DISPATCH_BUFS, _local_rows(tm), PACKED), U32),
                            pltpu.VMEM((bm, PACKED), U32), pltpu.SemaphoreType.DMA((N_DISPATCH_BUFS,))],
        ),
        out_shape=jax.ShapeDtypeStruct((n_rows, PACKED), U32),
        compiler_params=pltpu.CompilerParams(dimension_semantics=("arbitrary",), vmem_limit_bytes=VMEM_LIMIT,
                                             has_side_effects=True),
        name="dispatch",
    )(n_pieces, off, dst, start, x1, route, g_ffn)


def _moe_kernel(exp_ref, n_ref, xs_ref, wgu_ref, bgu_ref, wd_ref, bd_ref, y_ref, wgu_bf, wd_bf):
    w = pl.program_id(0)
    used = w < n_ref[0]

    @pl.when(jnp.logical_or(w == 0, exp_ref[w] != exp_ref[jnp.maximum(w - 1, 0)]))
    def _():
        wgu_bf[...] = wgu_ref[0].astype(BF16)
        wd_bf[...] = wd_ref[0].astype(BF16)

    @pl.when(used)
    def _():
        x = _unpack_rows(xs_ref[...])
        gu = _dot(x, wgu_bf[...]) + bgu_ref[0]
        g = jnp.minimum(gu[:, :D_FF], SWIGLU_LIMIT)
        u = jnp.clip(gu[:, D_FF:], -SWIGLU_LIMIT, SWIGLU_LIMIT)
        act = (u + 1.0) * (g * jax.nn.sigmoid(SWIGLU_ALPHA * g))
        y_ref[...] = _pack_rows(_dot(act.astype(BF16), wd_bf[...]) + bd_ref[0])

    @pl.when(jnp.logical_not(used))
    def _():
        y_ref[...] = jnp.zeros(y_ref.shape, U32)


def _moe(ex, n_used, xs, w_gu, b_gu, w_down, b_down, *, bm):
    return pl.pallas_call(
        _moe_kernel,
        grid_spec=pltpu.PrefetchScalarGridSpec(
            num_scalar_prefetch=2,
            grid=(xs.shape[0] // bm,),
            in_specs=[pl.BlockSpec((bm, PACKED), lambda w, ex, n: (w, 0)),
                      pl.BlockSpec((1, D_MODEL, 2 * D_FF), lambda w, ex, n: (ex[w], 0, 0)),
                      pl.BlockSpec((1, 1, 2 * D_FF), lambda w, ex, n: (ex[w], 0, 0)),
                      pl.BlockSpec((1, D_FF, D_MODEL), lambda w, ex, n: (ex[w], 0, 0)),
                      pl.BlockSpec((1, 1, D_MODEL), lambda w, ex, n: (ex[w], 0, 0))],
            out_specs=pl.BlockSpec((bm, PACKED), lambda w, ex, n: (w, 0)),
            scratch_shapes=[pltpu.VMEM((D_MODEL, 2 * D_FF), BF16), pltpu.VMEM((D_FF, D_MODEL), BF16)],
        ),
        out_shape=jax.ShapeDtypeStruct(xs.shape, U32),
        compiler_params=pltpu.CompilerParams(dimension_semantics=("arbitrary",), vmem_limit_bytes=VMEM_LIMIT),
        name="moe",
    )(ex, n_used, xs, w_gu, b_gu, w_down, b_down)


def _expert_layout(totals, *, bm, n_blocks):
    n_blk = (totals + bm - 1) // bm
    blk_end = jnp.cumsum(n_blk)
    start = jnp.concatenate([jnp.zeros((1,), I32), blk_end * bm]).astype(I32)
    w = jnp.arange(n_blocks, dtype=I32)
    ex = jnp.minimum(jnp.sum((blk_end[None, :] <= w[:, None]).astype(I32), axis=1), N_EXPERTS - 1)
    return start, ex.astype(I32), blk_end[-1].astype(I32)[None]


def _combine_kernel(npc_s, src_s, off_s, x1_ref, route_ref, gfin_ref, y_hbm, out_ref, yloc_ref, sems, *, tm, nt):
    i = pl.program_id(0)
    slot = lax.rem(i, 2)
    n_loc = _local_rows(tm)

    def gather(tile, sl):
        _for_each_piece(npc_s, src_s, off_s, tile,
                        lambda s, d, rows: _rows_copy(y_hbm, s, yloc_ref.at[sl], d, sems.at[sl], rows=rows).start())

    @pl.when(i == 0)
    def _():
        yloc_ref[...] = jnp.zeros(yloc_ref.shape, U32)
        gather(0, 0)

    @pl.when(i + 1 < nt)
    def _():
        gather(i + 1, 1 - slot)

    _wait_tile(y_hbm, yloc_ref.at[slot], sems.at[slot], npc_s, i)

    route = route_ref[...]
    lanes = lax.broadcasted_iota(I32, (tm, n_loc), 1)
    pw = jnp.zeros((tm, n_loc), F32)
    for kk in range(TOP_K):
        pos = route[:, 2 * TOP_K + kk:2 * TOP_K + kk + 1].astype(I32)
        pw = jnp.where(lanes == pos, route[:, TOP_K + kk:TOP_K + kk + 1], pw)
    moe = _dot(pw.astype(BF16), _unpack_rows(yloc_ref[slot]))
    out_ref[...] = _rms(x1_ref[...] + moe, gfin_ref[...])


def _combine(n_pieces, src, off, x1, route, g_final, y, *, tm):
    t = x1.shape[0]
    nt = t // tm
    return pl.pallas_call(
        functools.partial(_combine_kernel, tm=tm, nt=nt),
        grid_spec=pltpu.PrefetchScalarGridSpec(
            num_scalar_prefetch=3,
            grid=(nt,),
            in_specs=[pl.BlockSpec((tm, D_MODEL), lambda i, *_: (i, 0)), pl.BlockSpec((tm, LANES), lambda i, *_: (i, 0)),
                      pl.BlockSpec((1, D_MODEL), lambda i, *_: (0, 0)), pl.BlockSpec(memory_space=pl.ANY)],
            out_specs=pl.BlockSpec((tm, D_MODEL), lambda i, *_: (i, 0)),
            scratch_shapes=[pltpu.VMEM((2, _local_rows(tm), PACKED), U32), pltpu.SemaphoreType.DMA((2,))],
        ),
        out_shape=jax.ShapeDtypeStruct((t, D_MODEL), F32),
        compiler_params=pltpu.CompilerParams(dimension_semantics=("arbitrary",), vmem_limit_bytes=VMEM_LIMIT),
        name="combine",
    )(n_pieces, src, off, x1, route, g_final, y)


def _rot_cols(w):
    half = w.shape[-1] // 2
    return jnp.concatenate([-w[..., half:], w[..., :half]], axis=-1)


def _prep_weights(w_in, w_uq, w_ukv):
    d = w_in.shape[0]
    w_kpe = w_in[:, C_KPE:C_KPE + D_ROPE]
    w_in_ext = jnp.concatenate(
        [w_in[:, :C_KPE], w_kpe, _rot_cols(w_kpe), jnp.zeros((d, LANES - 2 * D_ROPE), F32),
         w_in[:, C_KPE + D_ROPE:]], axis=1).astype(BF16)

    wq = w_uq.reshape(Q_RANK, N_HEADS, D_QK)
    nope, pe = wq[..., :D_NOPE], wq[..., D_NOPE:]
    zpad = jnp.zeros((Q_RANK, N_HEADS, LANES - D_QK), F32)
    plain = jnp.concatenate([nope, pe, zpad], axis=-1).reshape(Q_RANK, N_HEADS * LANES)
    rot = _rot_cols(pe).reshape(Q_RANK, N_HEADS * D_ROPE)
    wq_t = jnp.concatenate([plain, rot], axis=1).T.astype(BF16)

    wkv = w_ukv.reshape(KV_RANK, N_HEADS, D_NOPE + D_V)
    wk_ext = jnp.concatenate([wkv[..., :D_NOPE], jnp.zeros((KV_RANK, N_HEADS, LANES - D_NOPE), F32)],
                             axis=-1).reshape(KV_RANK, N_HEADS * LANES).astype(BF16)
    wv_t = wkv[..., D_NOPE:].reshape(KV_RANK, N_HEADS * D_V).T.astype(BF16)
    return w_in_ext, wq_t, wk_ext, wv_t


def _kpe_placement():
    e2 = np.zeros((LANES, N_HEADS * LANES), np.float32)
    for j in range(D_ROPE):
        for h in range(N_HEADS):
            e2[j, h * LANES + D_NOPE + j] = 1.0
            e2[D_ROPE + j, h * LANES + D_NOPE + j] = 1.0
    return jnp.asarray(e2, BF16)


def _inv_freq(tm):
    inv = 1.0 / (ROPE_THETA ** (jnp.arange(0, D_ROPE, 2, dtype=F32) / D_ROPE))
    return jnp.broadcast_to(inv[:, None], (D_ROPE // 2, tm))


def _pick(n, pref):
    return pref if n % pref == 0 else n


def kernel(x_prompt, x_sample, g_mix, w_in, b_gate, g_q, w_uq, g_kv, w_ukv, w_oa, conv_w, conv_b, ln_g, ln_b, w_pb,
           b_pb, w_out, g_ffn, w_router, b_router, w_gu, b_gu, w_down, b_down, g_final):
    b1, s1, d = x_prompt.shape
    b2, s2, _ = x_sample.shape
    t1, t2 = b1 * s1, b2 * s2
    t = t1 + t2
    xa, xb = x_prompt.reshape(t1, d), x_sample.reshape(t2, d)
    seq_bounds = tuple((b * s1, (b + 1) * s1) for b in range(b1)) + tuple(
        (t1 + b * s2, t1 + (b + 1) * s2) for b in range(b2))

    tm = _pick(math.gcd(s1, s2), 512)
    nt = t // tm
    w_in_ext, wq_t, wk_ext, wv_t = _prep_weights(w_in[0], w_uq[0], w_ukv[0])

    qt, k, vt3, u, gates = _in_proj(xa, xb, _inv_freq(tm), g_mix, w_in_ext, g_q, wq_t, g_kv, wk_ext, wv_t,
                                    _kpe_placement(), b_gate, tm=tm, seq_starts=tuple(s for s, _ in seq_bounds))

    ota = _attention(qt, k, vt3, seq_start=0, seq_len=s1, n_seq=b1, tq=tm, tk=tm)
    otb = _attention(qt, k, vt3, seq_start=t1, seq_len=s2, n_seq=b2, tq=tm, tk=tm)

    w_r = jnp.concatenate([w_router[0], jnp.zeros((d, LANES - N_EXPERTS), F32)], axis=1)
    w_r_hi = w_r.astype(BF16)
    w_r = jnp.concatenate([w_r_hi, (w_r - w_r_hi.astype(F32)).astype(BF16)], axis=1)
    b_r = jnp.concatenate([b_router[0], jnp.full((LANES - N_EXPERTS,), NEG_BIG, F32)])[None, :]
    ltri = jnp.asarray(np.tril(np.ones((tm, tm), np.float32), -1), BF16)
    utri = jnp.asarray(np.triu(np.ones((LANES, LANES), np.float32), 1), BF16)
    x1, route, meta = _post_mix(xa, xb, ota, otb, u, gates, w_oa[0].astype(BF16), conv_w[0], conv_b, ln_g, ln_b,
                                w_pb[0].astype(BF16), b_pb, w_out[0].astype(BF16), g_ffn, w_r, b_r, ltri, utri,
                                tm=tm, seq_bounds=seq_bounds)

    bm = MOE_ROWS
    n_rows = _sorted_rows(t, nt, bm)
    carry, n_pieces, off = (meta[:, r, :N_EXPERTS] for r in range(3))
    start, ex, n_used = _expert_layout(carry[-1] + n_pieces[-1] * RUN_PAD, bm=bm, n_blocks=n_rows // bm)
    run_row = (carry + start[None, :N_EXPERTS]).reshape(-1)
    n_pieces, off = n_pieces.reshape(-1), off.reshape(-1)

    xs = _dispatch(n_pieces, off, run_row, start, x1, route, g_ffn, tm=tm, bm=bm, n_rows=n_rows)
    y = _moe(ex, n_used, xs, w_gu[0], b_gu[0][:, None, :], w_down[0], b_down[0][:, None, :], bm=bm)
    out = _combine(n_pieces, run_row, off, x1, route, g_final[None, :], y, tm=tm)
    return out[:t1].reshape(b1, s1, d), out[t1:].reshape(b2, s2, d)
```

```python
import functools
import math

import numpy as np
import jax
import jax.numpy as jnp
from jax import lax
from jax.experimental import pallas as pl
from jax.experimental.pallas import tpu as pltpu

F32 = jnp.float32
BF16 = jnp.bfloat16
I32 = jnp.int32
U32 = jnp.uint32

D_MODEL = 1024
N_HEADS = 8
Q_RANK = 384
KV_RANK = 256
D_NOPE = 64
D_ROPE = 32
D_V = 64
D_QK = D_NOPE + D_ROPE
ROPE_THETA = 10000.0
CONV_CH = 512
CONV_WIDTH = 31
CONV_PAD = (CONV_WIDTH - 1) // 2
N_EXPERTS = 32
TOP_K = 4
D_FF = 1024
SWIGLU_ALPHA = 1.702
SWIGLU_LIMIT = 7.0
RMS_EPS = 1e-6
LN_EPS = 1e-5

LANES = 128
HALO_ROWS = 16
NEG_BIG = -1e30
RUN_PAD = 8
BIG_PIECE = 32
N_DISPATCH_BUFS = 3
MOE_ROWS = 512

C_CQ = 0
C_CKV = Q_RANK
C_KPE = Q_RANK + KV_RANK
C_CONV = C_KPE + LANES
C_GATE = C_CONV + 2 * CONV_CH
D_IN_EXT = C_GATE + 2 * D_MODEL

VMEM_LIMIT = 56 * 1024 * 1024


def _rms(x, g):
    return x * lax.rsqrt(jnp.mean(x * x, axis=-1, keepdims=True) + RMS_EPS) * g


def _dot(a, b):
    return jnp.dot(a, b, preferred_element_type=F32)


def _dot_nt(a, b):
    return lax.dot_general(a, b, (((1,), (1,)), ((), ())), preferred_element_type=F32)


def _dot_tn(a, b):
    return lax.dot_general(a, b, (((0,), (0,)), ((), ())), preferred_element_type=F32)


def _const_spec(shape):
    return pl.BlockSpec(shape, lambda *_: (0,) * len(shape))


PACKED = D_MODEL // 2
HIGH_HALF = 0xFFFF0000


def _pack_rows(v):
    bits = lambda x: pltpu.bitcast(x.astype(BF16).astype(F32), U32)
    return bits(v[:, :PACKED]) | lax.shift_right_logical(bits(v[:, PACKED:]), jnp.uint32(16))


def _unpack_rows(w):
    hi = pltpu.bitcast(w & jnp.uint32(HIGH_HALF), F32).astype(BF16)
    lo = pltpu.bitcast(lax.shift_left(w, jnp.uint32(16)), F32).astype(BF16)
    return jnp.concatenate([hi, lo], axis=1)


def _tile_select(i, n_first, a_ref, b_ref):
    return jnp.where(i < n_first, a_ref[...], b_ref[...])


def _in_proj_kernel(xa_ref, xb_ref, inv_ref, gmix_ref, win_ref, gq_ref, wqt_ref, gkv_ref, wk_ref, wvt_ref,
                    e2_ref, bgate_ref, qt_ref, k_ref, vt_ref, u_ref, gates_ref, *, tm, n_first, seq_starts):
    i = pl.program_id(0)
    h = _rms(_tile_select(i, n_first, xa_ref, xb_ref), gmix_ref[...]).astype(BF16)
    proj = _dot(h, win_ref[...])

    t0 = i * tm
    s0 = 0
    for s in seq_starts:
        s0 = jnp.where(t0 >= s, s, s0)
    half = D_ROPE // 2
    pos = (t0 - s0 + lax.broadcasted_iota(I32, (half, tm), 1)).astype(F32)
    ang = pos * inv_ref[...]
    cos, sin = jnp.cos(ang), jnp.sin(ang)
    scale = (D_QK ** -0.5) * math.log2(math.e)
    cos_q, sin_q = cos * scale, sin * scale

    cqn = _rms(proj[:, C_CQ:C_CQ + Q_RANK], gq_ref[...]).astype(BF16)
    q2 = _dot_nt(wqt_ref[...], cqn)
    hw = N_HEADS * LANES
    for hh in range(N_HEADS):
        qa = q2[hh * LANES:(hh + 1) * LANES, :]
        qb = q2[hw + hh * D_ROPE:hw + (hh + 1) * D_ROPE, :]
        r0 = hh * LANES
        qt_ref[0, r0:r0 + D_NOPE, :] = (qa[:D_NOPE] * scale).astype(BF16)
        for part in range(2):
            lo = D_NOPE + part * half
            qt_ref[0, r0 + lo:r0 + lo + half, :] = (
                qa[lo:lo + half] * cos_q + qb[part * half:(part + 1) * half] * sin_q).astype(BF16)
        qt_ref[0, r0 + D_QK:r0 + LANES, :] = jnp.zeros((LANES - D_QK, tm), BF16)

    ckvn = _rms(proj[:, C_CKV:C_CKV + KV_RANK], gkv_ref[...]).astype(BF16)
    tabk = jnp.transpose(jnp.concatenate([cos, cos, sin, sin, jnp.zeros((LANES - 2 * D_ROPE, tm), F32)], axis=0))
    kpe = (proj[:, C_KPE:C_KPE + LANES] * tabk).astype(BF16)
    k_ref[...] = (_dot(ckvn, wk_ref[...]) + _dot(kpe, e2_ref[...])).astype(BF16)
    vt_ref[0] = _dot_nt(wvt_ref[...], ckvn).astype(BF16)

    a = proj[:, C_CONV:C_CONV + CONV_CH]
    g = proj[:, C_CONV + CONV_CH:C_CONV + 2 * CONV_CH]
    u_ref[...] = a * jax.nn.sigmoid(g)
    gates_ref[...] = jax.nn.sigmoid(proj[:, C_GATE:] + bgate_ref[...]).astype(BF16)


def _split_specs(block, n_first, n_second):
    first = lambda i: jnp.minimum(i, n_first - 1)
    second = lambda i: jnp.clip(i - n_first, 0, n_second - 1)
    rest = (0,) * (len(block) - 1)
    return [pl.BlockSpec(block, lambda i: (first(i),) + rest), pl.BlockSpec(block, lambda i: (second(i),) + rest)]


def _in_proj(xa, xb, inv_b, g_mix, w_in_ext, g_q, wq_t, g_kv, wk_ext, wv_t, e2, b_gate, *, tm, seq_starts):
    na, nb = xa.shape[0] // tm, xb.shape[0] // tm
    nt = na + nb
    t = nt * tm
    row = lambda w: pl.BlockSpec((tm, w), lambda i: (i, 0))
    col = lambda h: pl.BlockSpec((h, tm), lambda i: (0, i))
    return pl.pallas_call(
        functools.partial(_in_proj_kernel, tm=tm, n_first=na, seq_starts=seq_starts),
        grid=(nt,),
        in_specs=_split_specs((tm, D_MODEL), na, nb) + [
                  _const_spec((D_ROPE // 2, tm)), _const_spec((1, D_MODEL)),
                  _const_spec((D_MODEL, D_IN_EXT)), _const_spec((1, Q_RANK)),
                  _const_spec((N_HEADS * (LANES + D_ROPE), Q_RANK)), _const_spec((1, KV_RANK)),
                  _const_spec((KV_RANK, N_HEADS * LANES)), _const_spec((N_HEADS * D_V, KV_RANK)),
                  _const_spec((LANES, N_HEADS * LANES)), _const_spec((1, 2 * D_MODEL))],
        out_specs=(pl.BlockSpec((1, N_HEADS * LANES, tm), lambda i: (i, 0, 0)), row(N_HEADS * LANES),
                   pl.BlockSpec((1, N_HEADS * D_V, tm), lambda i: (i, 0, 0)), row(CONV_CH), row(2 * D_MODEL)),
        out_shape=(jax.ShapeDtypeStruct((nt, N_HEADS * LANES, tm), BF16),
                   jax.ShapeDtypeStruct((t, N_HEADS * LANES), BF16),
                   jax.ShapeDtypeStruct((nt, N_HEADS * D_V, tm), BF16),
                   jax.ShapeDtypeStruct((t, CONV_CH), F32),
                   jax.ShapeDtypeStruct((t, 2 * D_MODEL), BF16)),
        compiler_params=pltpu.CompilerParams(dimension_semantics=("arbitrary",), vmem_limit_bytes=VMEM_LIMIT),
        name="in_proj",
    )(xa, xb, inv_b, g_mix, w_in_ext, g_q, wq_t, g_kv, wk_ext, wv_t, e2, b_gate)


def _attn_kernel(qt_ref, k_ref, vt_ref, o_ref, s_ref, smax_ref, acc_ref, m_ref, *, nq, n_chunks, tk, unroll):
    half = n_chunks // 2
    ones = jnp.ones((16, tk), BF16)

    def scores(a, j, slot):
        ks = pl.multiple_of(j * tk, tk)
        st = _dot(k_ref[pl.ds(ks, tk), :], qt_ref[a])
        s_ref[slot] = st
        smax_ref[slot] = jnp.max(st, axis=0, keepdims=True)

    def softmax_pv(j, slot):
        st = s_ref[slot]
        m = jnp.where(j == 0, NEG_BIG, m_ref[...])
        m_new = jnp.maximum(m, smax_ref[slot])
        p = jnp.exp2(st - m_new).astype(BF16)
        v1 = jnp.concatenate([vt_ref[j], ones], axis=0)
        acc = jnp.exp2(m - m_new) * acc_ref[...] + _dot(v1, p)
        acc_ref[...] = acc
        m_ref[...] = m_new
        return acc

    m_ref[...] = jnp.full(m_ref.shape, NEG_BIG, F32)
    acc_ref[...] = jnp.zeros(acc_ref.shape, F32)
    scores(0, 0, 0)

    def pair(pp, may_end_tile):
        a = lax.div(pp, half)
        j = 2 * lax.rem(pp, half)
        scores(a, j + 1, 1)
        softmax_pv(j, 0)
        nxt = pp + 1
        scores(jnp.minimum(lax.div(nxt, half), nq - 1), 2 * lax.rem(nxt, half), 0)
        acc = softmax_pv(j + 1, 1)
        if may_end_tile:
            o_ref[a] = (acc[:D_V] / acc[D_V:D_V + 1]).astype(o_ref.dtype)

    period = math.gcd(half, unroll)

    def trip(blk, c):
        for u in range(unroll):
            pair(blk * unroll + u, (u + 1) % period == 0)
        return c

    lax.fori_loop(0, nq * half // unroll, trip, 0)


def _attention(qt3, k, vt3, *, seq_start, seq_len, n_seq, tq, tk):
    nq = seq_len // tq
    nkc = seq_len // tk
    assert nkc % 2 == 0, "key chunks are processed in pairs"
    s0 = seq_start // seq_len
    n_pairs = nq * (nkc // 2)
    unroll = next(u for u in (8, 4, 2, 1) if n_pairs % u == 0)
    return pl.pallas_call(
        functools.partial(_attn_kernel, nq=nq, n_chunks=nkc, tk=tk, unroll=unroll),
        grid=(n_seq, N_HEADS),
        in_specs=[pl.BlockSpec((nq, LANES, tq), lambda b, h: (s0 + b, h, 0)),
                  pl.BlockSpec((seq_len, LANES), lambda b, h: (s0 + b, h)),
                  pl.BlockSpec((nkc, D_V, tk), lambda b, h: (s0 + b, h, 0))],
        out_specs=pl.BlockSpec((nq, D_V, tq), lambda b, h: (b, h, 0)),
        out_shape=jax.ShapeDtypeStruct((n_seq * nq, N_HEADS * D_V, tq), BF16),
        scratch_shapes=[pltpu.VMEM((2, tk, tq), F32), pltpu.VMEM((2, 1, tq), F32), pltpu.VMEM((D_V + 16, tq), F32),
                        pltpu.VMEM((1, tq), F32)],
        compiler_params=pltpu.CompilerParams(dimension_semantics=("arbitrary", "arbitrary"),
                                             vmem_limit_bytes=VMEM_LIMIT),
        name="attention",
    )(qt3, k, vt3)


def _local_rows(tm):
    return TOP_K * tm + N_EXPERTS * RUN_PAD


def _sorted_rows(t, nt, bm):
    runs = t * TOP_K + nt * N_EXPERTS * (RUN_PAD - 1)
    return -(-runs // bm) * bm + N_EXPERTS * bm


def _rows_copy(src, s, dst, d, sem, rows=RUN_PAD):
    s = pl.multiple_of(s, RUN_PAD)
    d = pl.multiple_of(d, RUN_PAD)
    return pltpu.make_async_copy(src.at[pl.ds(s, rows), :], dst.at[pl.ds(d, rows), :], sem)


def _for_each_piece(n_ref, a_ref, b_ref, tile, fn):
    def per_expert(e, c):
        n = n_ref[tile * N_EXPERTS + e]
        a0 = a_ref[tile * N_EXPERTS + e]
        b0 = b_ref[tile * N_EXPERTS + e]
        n_big = lax.div(n, BIG_PIECE // RUN_PAD)
        done = n_big * BIG_PIECE

        def big(q, c2):
            fn(a0 + q * BIG_PIECE, b0 + q * BIG_PIECE, BIG_PIECE)
            return c2

        def small(q, c2):
            fn(a0 + done + q * RUN_PAD, b0 + done + q * RUN_PAD, RUN_PAD)
            return c2

        lax.fori_loop(0, n_big, big, 0)
        lax.fori_loop(0, n - n_big * (BIG_PIECE // RUN_PAD), small, 0)
        return c

    lax.fori_loop(0, N_EXPERTS, per_expert, 0)


def _tile_pieces(n_ref, tile):
    def per_expert(e, c):
        n = n_ref[tile * N_EXPERTS + e]
        n_big = lax.div(n, BIG_PIECE // RUN_PAD)
        return c[0] + n_big, c[1] + n - n_big * (BIG_PIECE // RUN_PAD)

    return lax.fori_loop(0, N_EXPERTS, per_expert, (0, 0))


def _wait_rows(src, dst, sem, count, rows):
    def one(q, c):
        _rows_copy(src, 0, dst, 0, sem, rows=rows).wait()
        return c

    lax.fori_loop(0, count, one, 0)


def _wait_tile(src, dst, sem, n_ref, tile):
    n_big, n_small = _tile_pieces(n_ref, tile)
    _wait_rows(src, dst, sem, n_big, BIG_PIECE)
    _wait_rows(src, dst, sem, n_small, RUN_PAD)


def _post_mix_kernel(xa_ref, xb_ref, ota_ref, otb_ref, u_ref, up_ref, un_ref, gates_ref, woa_ref, cw_ref, cb_ref,
                     lng_ref, lnb_ref, wpb_ref, bpb_ref, wout_ref, gffn_ref, wr_ref, br_ref, ltri_ref, utri_ref,
                     x1_ref, route_ref, meta_ref, ext_ref, cv_ref, ph_ref, carry_ref, *, tm, n_first, seq_bounds):
    i = pl.program_id(0)
    t0 = i * tm
    is_start = functools.reduce(jnp.logical_or, [t0 == s for s, _ in seq_bounds])
    is_end = functools.reduce(jnp.logical_or, [t0 + tm == e for _, e in seq_bounds])

    ext_ref[0:HALO_ROWS, :] = jnp.where(is_start, 0.0, up_ref[...])
    ext_ref[HALO_ROWS:HALO_ROWS + tm, :] = u_ref[...]
    ext_ref[HALO_ROWS + tm:2 * HALO_ROWS + tm, :] = jnp.where(is_end, 0.0, un_ref[...])
    rc = min(tm, 64)
    sub = 8
    base = HALO_ROWS - CONV_PAD
    for c in range(tm // rc):
        acc = None
        for b in range(sub):
            part = None
            for o in range(CONV_WIDTH):
                if (o + base) % sub != b:
                    continue
                a0 = c * rc + (o + base) - b
                term = cw_ref[o:o + 1, :] * ext_ref[a0:a0 + rc + sub, :]
                part = term if part is None else part + term
            if b == 0:
                shifted = part[:rc]
            else:
                ph_ref[b] = part
                shifted = ph_ref[b, b:b + rc, :]
            acc = shifted if acc is None else acc + shifted
        cv_ref[c * rc:(c + 1) * rc, :] = acc + cb_ref[...]

    cv = cv_ref[...]
    mu = jnp.mean(cv, axis=-1, keepdims=True)
    xc = cv - mu
    y = xc * lax.rsqrt(jnp.mean(xc * xc, axis=-1, keepdims=True) + LN_EPS) * lng_ref[...] + lnb_ref[...]
    y = y * jax.nn.sigmoid(y)
    branch_b = _dot(y.astype(BF16), wpb_ref[...]) + bpb_ref[...]
    branch_a = _dot_tn(_tile_select(i, n_first, ota_ref, otb_ref)[0], woa_ref[...])
    gates = gates_ref[...]
    mix = gates[:, :D_MODEL].astype(F32) * branch_a + gates[:, D_MODEL:].astype(F32) * branch_b
    x1 = _tile_select(i, n_first, xa_ref, xb_ref) + _dot(mix.astype(BF16), wout_ref[...])
    x1_ref[...] = x1

    h2 = _rms(x1, gffn_ref[...])
    h2_hi = h2.astype(BF16)
    h2_lo = (h2 - h2_hi.astype(F32)).astype(BF16)
    hi_both = _dot(h2_hi, wr_ref[...])
    logits = hi_both[:, :LANES] + (hi_both[:, LANES:] + _dot(h2_lo, wr_ref[:, :LANES])) + br_ref[...]
    lane = lax.broadcasted_iota(I32, logits.shape, 1)
    lane_f = lane.astype(F32)
    vals, idxs, sels = [], [], []
    cur = logits
    for _ in range(TOP_K):
        mval = jnp.max(cur, axis=-1, keepdims=True)
        ik = jnp.min(jnp.where(cur == mval, lane_f, float(LANES)), axis=-1, keepdims=True)
        sel = lane_f == ik
        cur = jnp.where(sel, -jnp.inf, cur)
        vals.append(mval)
        idxs.append(ik)
        sels.append(sel)
    exps = [jnp.exp(v - vals[0]) for v in vals]
    denom = exps[0] + exps[1] + exps[2] + exps[3]

    onehot = sels[0].astype(F32) + sels[1].astype(F32) + sels[2].astype(F32) + sels[3].astype(F32)

    @pl.when(i == 0)
    def _():
        carry_ref[...] = jnp.zeros_like(carry_ref)

    n_e = jnp.sum(onehot, axis=0, keepdims=True)
    n_pieces = jnp.floor((n_e + (RUN_PAD - 1)) * (1.0 / RUN_PAD))
    pieces_b = jnp.broadcast_to(n_pieces, (8, LANES)).astype(BF16)
    off = _dot(pieces_b, utri_ref[...])[0:1] * RUN_PAD
    total_pieces = jnp.sum(n_pieces, axis=-1, keepdims=True)
    carry = carry_ref[...]
    prefix = _dot(ltri_ref[...], onehot.astype(BF16))
    local = off + prefix
    route = jnp.zeros(logits.shape, F32)
    for kk in range(TOP_K):
        pos = jnp.sum(jnp.where(sels[kk], local, 0.0), axis=-1, keepdims=True)
        route = jnp.where(lane == kk, idxs[kk], route)
        route = jnp.where(lane == TOP_K + kk, exps[kk] / denom, route)
        route = jnp.where(lane == 2 * TOP_K + kk, pos, route)
    route_ref[...] = route
    carry_ref[...] = carry + n_pieces * RUN_PAD

    row8 = lax.broadcasted_iota(I32, (8, LANES), 0)
    metav = jnp.where(row8 == 0, jnp.broadcast_to(carry, (8, LANES)), 0.0)
    metav = jnp.where(row8 == 1, jnp.broadcast_to(n_pieces, (8, LANES)), metav)
    metav = jnp.where(row8 == 2, jnp.broadcast_to(off, (8, LANES)), metav)
    metav = jnp.where(row8 == 3, jnp.broadcast_to(total_pieces, (8, LANES)), metav)
    meta_ref[0] = metav.astype(I32)


def _post_mix(xa, xb, ota, otb, u, gates, w_oa, conv_w, conv_b, ln_g, ln_b, w_pb, b_pb, w_out, g_ffn, w_r, b_r, ltri,
              utri, *, tm, seq_bounds):
    na, nb = xa.shape[0] // tm, xb.shape[0] // tm
    nt = na + nb
    t = nt * tm
    hb = tm // HALO_ROWS
    n_halo = t // HALO_ROWS
    row = lambda w: pl.BlockSpec((tm, w), lambda i: (i, 0))
    return pl.pallas_call(
        functools.partial(_post_mix_kernel, tm=tm, n_first=na, seq_bounds=seq_bounds),
        grid=(nt,),
        in_specs=_split_specs((tm, D_MODEL), na, nb) + _split_specs((1, N_HEADS * D_V, tm), na, nb) + [
                  row(CONV_CH),
                  pl.BlockSpec((HALO_ROWS, CONV_CH), lambda i: (jnp.maximum(i * hb - 1, 0), 0)),
                  pl.BlockSpec((HALO_ROWS, CONV_CH), lambda i: (jnp.minimum((i + 1) * hb, n_halo - 1), 0)),
                  row(2 * D_MODEL), _const_spec((N_HEADS * D_V, D_MODEL)), _const_spec((CONV_WIDTH, CONV_CH)),
                  _const_spec((1, CONV_CH)), _const_spec((1, CONV_CH)), _const_spec((1, CONV_CH)),
                  _const_spec((CONV_CH, D_MODEL)), _const_spec((1, D_MODEL)), _const_spec((D_MODEL, D_MODEL)),
                  _const_spec((1, D_MODEL)), _const_spec((D_MODEL, 2 * LANES)), _const_spec((1, LANES)),
                  _const_spec((tm, tm)), _const_spec((LANES, LANES))],
        out_specs=(row(D_MODEL), row(LANES), pl.BlockSpec((1, 8, LANES), lambda i: (i, 0, 0))),
        out_shape=(jax.ShapeDtypeStruct((t, D_MODEL), F32), jax.ShapeDtypeStruct((t, LANES), F32),
                   jax.ShapeDtypeStruct((nt, 8, LANES), I32)),
        scratch_shapes=[pltpu.VMEM((tm + 2 * HALO_ROWS, CONV_CH), F32), pltpu.VMEM((tm, CONV_CH), F32),
                        pltpu.VMEM((8, min(tm, 64) + 8, CONV_CH), F32), pltpu.VMEM((1, LANES), F32)],
        compiler_params=pltpu.CompilerParams(dimension_semantics=("arbitrary",), vmem_limit_bytes=VMEM_LIMIT),
        name="post_mix",
    )(xa, xb, ota, otb, u, u, u, gates, w_oa, conv_w, conv_b, ln_g, ln_b, w_pb, b_pb, w_out, g_ffn, w_r, b_r, ltri, utri)


def _dispatch_kernel(npc_s, off_s, dst_s, start_s, x1_ref, route_ref, gffn_ref, xs_hbm, xloc_ref, zero_ref, row_sems,
                     *, tm, nt, bm, n_rows):
    i = pl.program_id(0)
    slot = lax.rem(i, N_DISPATCH_BUFS)
    n_loc = _local_rows(tm)

    h2 = _rms(x1_ref[...], gffn_ref[...]).astype(BF16)
    pos_t = jnp.transpose(route_ref[...])[2 * TOP_K:3 * TOP_K, :].astype(I32)
    prow = lax.broadcasted_iota(I32, (n_loc, tm), 0)
    perm = jnp.zeros((n_loc, tm), F32)
    for kk in range(TOP_K):
        perm = jnp.where(prow == pos_t[kk:kk + 1, :], 1.0, perm)
    xloc_ref[slot] = _pack_rows(_dot(perm.astype(BF16), h2))

    def wait_tile(tile):
        _wait_tile(xloc_ref.at[0], xs_hbm, row_sems.at[lax.rem(tile, N_DISPATCH_BUFS)], npc_s, tile)

    @pl.when(i >= N_DISPATCH_BUFS - 1)
    def _():
        wait_tile(i - (N_DISPATCH_BUFS - 1))

    _for_each_piece(npc_s, off_s, dst_s, i,
                    lambda s, d, rows: _rows_copy(xloc_ref.at[slot], s, xs_hbm, d, row_sems.at[slot], rows=rows).start())

    @pl.when(i == nt - 1)
    def _():
        for back in range(N_DISPATCH_BUFS - 2, -1, -1):
            @pl.when(i - back >= 0)
            def _():
                wait_tile(i - back)
        row_sem = row_sems.at[0]
        zero_ref[...] = jnp.zeros(zero_ref.shape, U32)

        def tail(e, c):
            end = dst_s[i * N_EXPERTS + e] + npc_s[i * N_EXPERTS + e] * RUN_PAD
            n = lax.shift_right_logical(start_s[e + 1] - end, int(math.log2(RUN_PAD)))

            def piece(q, c2):
                _rows_copy(zero_ref, 0, xs_hbm, end + q * RUN_PAD, row_sem).start()
                return c2

            lax.fori_loop(0, n, piece, 0)
            return c + n

        _wait_rows(zero_ref, xs_hbm, row_sem, lax.fori_loop(0, N_EXPERTS, tail, 0), RUN_PAD)
        used = start_s[N_EXPERTS]
        n_free = (n_rows - used) // bm

        def fill(b, c):
            _rows_copy(zero_ref, 0, xs_hbm, used + b * bm, row_sem, rows=bm).start()
            return c

        lax.fori_loop(0, n_free, fill, 0)
        _wait_rows(zero_ref, xs_hbm, row_sem, n_free, bm)


def _dispatch(n_pieces, off, dst, start, x1, route, g_ffn, *, tm, bm, n_rows):
    t = x1.shape[0]
    nt = t // tm
    return pl.pallas_call(
        functools.partial(_dispatch_kernel, tm=tm, nt=nt, bm=bm, n_rows=n_rows),
        grid_spec=pltpu.PrefetchScalarGridSpec(
            num_scalar_prefetch=4,
            grid=(nt,),
            in_specs=[pl.BlockSpec((tm, D_MODEL), lambda i, *_: (i, 0)), pl.BlockSpec((tm, LANES), lambda i, *_: (i, 0)),
                      pl.BlockSpec((1, D_MODEL), lambda i, *_: (0, 0))],
            out_specs=pl.BlockSpec(memory_space=pl.ANY),
            scratch_shapes=[pltpu.VMEM((N_DISPATCH_BUFS, _local_rows(tm), PACKED), U32),
                            pltpu.VMEM((bm, PACKED), U32), pltpu.SemaphoreType.DMA((N_DISPATCH_BUFS,))],
        ),
        out_shape=jax.ShapeDtypeStruct((n_rows, PACKED), U32),
        compiler_params=pltpu.CompilerParams(dimension_semantics=("arbitrary",), vmem_limit_bytes=VMEM_LIMIT,
                                             has_side_effects=True),
        name="dispatch",
    )(n_pieces, off, dst, start, x1, route, g_ffn)


def _moe_kernel(exp_ref, n_ref, xs_ref, wgu_ref, bgu_ref, wd_ref, bd_ref, y_ref, wgu_bf, wd_bf):
    w = pl.program_id(0)
    used = w < n_ref[0]

    @pl.when(jnp.logical_or(w == 0, exp_ref[w] != exp_ref[jnp.maximum(w - 1, 0)]))
    def _():
        wgu_bf[...] = wgu_ref[0].astype(BF16)
        wd_bf[...] = wd_ref[0].astype(BF16)

    @pl.when(used)
    def _():
        x = _unpack_rows(xs_ref[...])
        gu = _dot(x, wgu_bf[...]) + bgu_ref[0]
        g = jnp.minimum(gu[:, :D_FF], SWIGLU_LIMIT)
        u = jnp.clip(gu[:, D_FF:], -SWIGLU_LIMIT, SWIGLU_LIMIT)
        act = (u + 1.0) * (g * jax.nn.sigmoid(SWIGLU_ALPHA * g))
        y_ref[...] = _pack_rows(_dot(act.astype(BF16), wd_bf[...]) + bd_ref[0])

    @pl.when(jnp.logical_not(used))
    def _():
        y_ref[...] = jnp.zeros(y_ref.shape, U32)


def _moe(ex, n_used, xs, w_gu, b_gu, w_down, b_down, *, bm):
    return pl.pallas_call(
        _moe_kernel,
        grid_spec=pltpu.PrefetchScalarGridSpec(
            num_scalar_prefetch=2,
            grid=(xs.shape[0] // bm,),
            in_specs=[pl.BlockSpec((bm, PACKED), lambda w, ex, n: (w, 0)),
                      pl.BlockSpec((1, D_MODEL, 2 * D_FF), lambda w, ex, n: (ex[w], 0, 0)),
                      pl.BlockSpec((1, 1, 2 * D_FF), lambda w, ex, n: (ex[w], 0, 0)),
                      pl.BlockSpec((1, D_FF, D_MODEL), lambda w, ex, n: (ex[w], 0, 0)),
                      pl.BlockSpec((1, 1, D_MODEL), lambda w, ex, n: (ex[w], 0, 0))],
            out_specs=pl.BlockSpec((bm, PACKED), lambda w, ex, n: (w, 0)),
            scratch_shapes=[pltpu.VMEM((D_MODEL, 2 * D_FF), BF16), pltpu.VMEM((D_FF, D_MODEL), BF16)],
        ),
        out_shape=jax.ShapeDtypeStruct(xs.shape, U32),
        compiler_params=pltpu.CompilerParams(dimension_semantics=("arbitrary",), vmem_limit_bytes=VMEM_LIMIT),
        name="moe",
    )(ex, n_used, xs, w_gu, b_gu, w_down, b_down)


def _expert_layout(totals, *, bm, n_blocks):
    n_blk = (totals + bm - 1) // bm
    blk_end = jnp.cumsum(n_blk)
    start = jnp.concatenate([jnp.zeros((1,), I32), blk_end * bm]).astype(I32)
    w = jnp.arange(n_blocks, dtype=I32)
    ex = jnp.minimum(jnp.sum((blk_end[None, :] <= w[:, None]).astype(I32), axis=1), N_EXPERTS - 1)
    return start, ex.astype(I32), blk_end[-1].astype(I32)[None]


def _combine_kernel(npc_s, src_s, off_s, x1_ref, route_ref, gfin_ref, y_hbm, outa_ref, outb_ref, yloc_ref, sems,
                    *, tm, nt, n_first):
    i = pl.program_id(0)
    slot = lax.rem(i, 2)
    n_loc = _local_rows(tm)

    def gather(tile, sl):
        _for_each_piece(npc_s, src_s, off_s, tile,
                        lambda s, d, rows: _rows_copy(y_hbm, s, yloc_ref.at[sl], d, sems.at[sl], rows=rows).start())

    @pl.when(i == 0)
    def _():
        yloc_ref[...] = jnp.zeros(yloc_ref.shape, U32)
        gather(0, 0)

    @pl.when(i + 1 < nt)
    def _():
        gather(i + 1, 1 - slot)

    _wait_tile(y_hbm, yloc_ref.at[slot], sems.at[slot], npc_s, i)

    route = route_ref[...]
    lanes = lax.broadcasted_iota(I32, (tm, n_loc), 1)
    pw = jnp.zeros((tm, n_loc), F32)
    for kk in range(TOP_K):
        pos = route[:, 2 * TOP_K + kk:2 * TOP_K + kk + 1].astype(I32)
        pw = jnp.where(lanes == pos, route[:, TOP_K + kk:TOP_K + kk + 1], pw)
    moe = _dot(pw.astype(BF16), _unpack_rows(yloc_ref[slot]))
    res = _rms(x1_ref[...] + moe, gfin_ref[...])

    @pl.when(i < n_first)
    def _():
        outa_ref[...] = res

    @pl.when(i >= n_first)
    def _():
        outb_ref[...] = res


def _combine(n_pieces, src, off, x1, route, g_final, y, *, tm, n_first):
    t = x1.shape[0]
    nt = t // tm
    n_second = nt - n_first
    return pl.pallas_call(
        functools.partial(_combine_kernel, tm=tm, nt=nt, n_first=n_first),
        grid_spec=pltpu.PrefetchScalarGridSpec(
            num_scalar_prefetch=3,
            grid=(nt,),
            in_specs=[pl.BlockSpec((tm, D_MODEL), lambda i, *_: (i, 0)), pl.BlockSpec((tm, LANES), lambda i, *_: (i, 0)),
                      pl.BlockSpec((1, D_MODEL), lambda i, *_: (0, 0)), pl.BlockSpec(memory_space=pl.ANY)],
            out_specs=(pl.BlockSpec((tm, D_MODEL), lambda i, *_: (jnp.minimum(i, n_first - 1), 0)),
                       pl.BlockSpec((tm, D_MODEL), lambda i, *_: (jnp.clip(i - n_first, 0, n_second - 1), 0))),
            scratch_shapes=[pltpu.VMEM((2, _local_rows(tm), PACKED), U32), pltpu.SemaphoreType.DMA((2,))],
        ),
        out_shape=(jax.ShapeDtypeStruct((n_first * tm, D_MODEL), F32),
                   jax.ShapeDtypeStruct((n_second * tm, D_MODEL), F32)),
        compiler_params=pltpu.CompilerParams(dimension_semantics=("arbitrary",), vmem_limit_bytes=VMEM_LIMIT),
        name="combine",
    )(n_pieces, src, off, x1, route, g_final, y)


def _rot_cols(w):
    half = w.shape[-1] // 2
    return jnp.concatenate([-w[..., half:], w[..., :half]], axis=-1)


def _prep_weights(w_in, w_uq, w_ukv):
    d = w_in.shape[0]
    w_kpe = w_in[:, C_KPE:C_KPE + D_ROPE]
    w_in_ext = jnp.concatenate(
        [w_in[:, :C_KPE], w_kpe, _rot_cols(w_kpe), jnp.zeros((d, LANES - 2 * D_ROPE), F32),
         w_in[:, C_KPE + D_ROPE:]], axis=1).astype(BF16)

    wq = w_uq.reshape(Q_RANK, N_HEADS, D_QK)
    nope, pe = wq[..., :D_NOPE], wq[..., D_NOPE:]
    zpad = jnp.zeros((Q_RANK, N_HEADS, LANES - D_QK), F32)
    plain = jnp.concatenate([nope, pe, zpad], axis=-1).reshape(Q_RANK, N_HEADS * LANES)
    rot = _rot_cols(pe).reshape(Q_RANK, N_HEADS * D_ROPE)
    wq_t = jnp.concatenate([plain, rot], axis=1).T.astype(BF16)

    wkv = w_ukv.reshape(KV_RANK, N_HEADS, D_NOPE + D_V)
    wk_ext = jnp.concatenate([wkv[..., :D_NOPE], jnp.zeros((KV_RANK, N_HEADS, LANES - D_NOPE), F32)],
                             axis=-1).reshape(KV_RANK, N_HEADS * LANES).astype(BF16)
    wv_t = wkv[..., D_NOPE:].reshape(KV_RANK, N_HEADS * D_V).T.astype(BF16)
    return w_in_ext, wq_t, wk_ext, wv_t


def _kpe_placement():
    e2 = np.zeros((LANES, N_HEADS * LANES), np.float32)
    for j in range(D_ROPE):
        for h in range(N_HEADS):
            e2[j, h * LANES + D_NOPE + j] = 1.0
            e2[D_ROPE + j, h * LANES + D_NOPE + j] = 1.0
    return jnp.asarray(e2, BF16)


def _inv_freq(tm):
    inv = 1.0 / (ROPE_THETA ** (jnp.arange(0, D_ROPE, 2, dtype=F32) / D_ROPE))
    return jnp.broadcast_to(inv[:, None], (D_ROPE // 2, tm))


def _pick(n, pref):
    return pref if n % pref == 0 else n


def kernel(x_prompt, x_sample, g_mix, w_in, b_gate, g_q, w_uq, g_kv, w_ukv, w_oa, conv_w, conv_b, ln_g, ln_b, w_pb,
           b_pb, w_out, g_ffn, w_router, b_router, w_gu, b_gu, w_down, b_down, g_final):
    b1, s1, d = x_prompt.shape
    b2, s2, _ = x_sample.shape
    t1, t2 = b1 * s1, b2 * s2
    t = t1 + t2
    xa, xb = x_prompt.reshape(t1, d), x_sample.reshape(t2, d)
    seq_bounds = tuple((b * s1, (b + 1) * s1) for b in range(b1)) + tuple(
        (t1 + b * s2, t1 + (b + 1) * s2) for b in range(b2))

    tm = _pick(math.gcd(s1, s2), 512)
    nt = t // tm
    w_in_ext, wq_t, wk_ext, wv_t = _prep_weights(w_in[0], w_uq[0], w_ukv[0])

    qt, k, vt3, u, gates = _in_proj(xa, xb, _inv_freq(tm), g_mix, w_in_ext, g_q, wq_t, g_kv, wk_ext, wv_t,
                                    _kpe_placement(), b_gate, tm=tm, seq_starts=tuple(s for s, _ in seq_bounds))

    ota = _attention(qt, k, vt3, seq_start=0, seq_len=s1, n_seq=b1, tq=tm, tk=tm)
    otb = _attention(qt, k, vt3, seq_start=t1, seq_len=s2, n_seq=b2, tq=tm, tk=tm)

    w_r = jnp.concatenate([w_router[0], jnp.zeros((d, LANES - N_EXPERTS), F32)], axis=1)
    w_r_hi = w_r.astype(BF16)
    w_r = jnp.concatenate([w_r_hi, (w_r - w_r_hi.astype(F32)).astype(BF16)], axis=1)
    b_r = jnp.concatenate([b_router[0], jnp.full((LANES - N_EXPERTS,), NEG_BIG, F32)])[None, :]
    ltri = jnp.asarray(np.tril(np.ones((tm, tm), np.float32), -1), BF16)
    utri = jnp.asarray(np.triu(np.ones((LANES, LANES), np.float32), 1), BF16)
    x1, route, meta = _post_mix(xa, xb, ota, otb, u, gates, w_oa[0].astype(BF16), conv_w[0], conv_b, ln_g, ln_b,
                                w_pb[0].astype(BF16), b_pb, w_out[0].astype(BF16), g_ffn, w_r, b_r, ltri, utri,
                                tm=tm, seq_bounds=seq_bounds)

    bm = MOE_ROWS
    n_rows = _sorted_rows(t, nt, bm)
    carry, n_pieces, off = (meta[:, r, :N_EXPERTS] for r in range(3))
    start, ex, n_used = _expert_layout(carry[-1] + n_pieces[-1] * RUN_PAD, bm=bm, n_blocks=n_rows // bm)
    run_row = (carry + start[None, :N_EXPERTS]).reshape(-1)
    n_pieces, off = n_pieces.reshape(-1), off.reshape(-1)

    xs = _dispatch(n_pieces, off, run_row, start, x1, route, g_ffn, tm=tm, bm=bm, n_rows=n_rows)
    y = _moe(ex, n_used, xs, w_gu[0], b_gu[0][:, None, :], w_down[0], b_down[0][:, None, :], bm=bm)
    out_a, out_b = _combine(n_pieces, run_row, off, x1, route, g_final[None, :], y, tm=tm, n_first=t1 // tm)
    return out_a.reshape(b1, s1, d), out_b.reshape(b2, s2, d)
```

```python
import functools
import math

import numpy as np
import jax
import jax.numpy as jnp
from jax import lax
from jax.experimental import pallas as pl
from jax.experimental.pallas import tpu as pltpu

F32 = jnp.float32
BF16 = jnp.bfloat16
I32 = jnp.int32
U32 = jnp.uint32

D_MODEL = 1024
N_HEADS = 8
Q_RANK = 384
KV_RANK = 256
D_NOPE = 64
D_ROPE = 32
D_V = 64
D_QK = D_NOPE + D_ROPE
ROPE_THETA = 10000.0
CONV_CH = 512
CONV_WIDTH = 31
CONV_PAD = (CONV_WIDTH - 1) // 2
N_EXPERTS = 32
TOP_K = 4
D_FF = 1024
SWIGLU_ALPHA = 1.702
SWIGLU_LIMIT = 7.0
RMS_EPS = 1e-6
LN_EPS = 1e-5

LANES = 128
HALO_ROWS = 16
CONV_ROWS = 512
NEG_BIG = -1e30
RUN_PAD = 8
BIG_PIECE = 32
N_DISPATCH_BUFS = 3
MOE_ROWS = 512

C_CQ = 0
C_CKV = Q_RANK
C_KPE = Q_RANK + KV_RANK
C_CONV = C_KPE + LANES
C_GATE = C_CONV + 2 * CONV_CH
D_IN_EXT = C_GATE + 2 * D_MODEL

VMEM_LIMIT = 56 * 1024 * 1024


def _rms(x, g):
    return x * lax.rsqrt(jnp.mean(x * x, axis=-1, keepdims=True) + RMS_EPS) * g


def _dot(a, b):
    return jnp.dot(a, b, preferred_element_type=F32)


def _dot_nt(a, b):
    return lax.dot_general(a, b, (((1,), (1,)), ((), ())), preferred_element_type=F32)


def _dot_tn(a, b):
    return lax.dot_general(a, b, (((0,), (0,)), ((), ())), preferred_element_type=F32)


def _const_spec(shape):
    return pl.BlockSpec(shape, lambda *_: (0,) * len(shape))


PACKED = D_MODEL // 2
HIGH_HALF = 0xFFFF0000


def _pack_rows(v):
    bits = lambda x: pltpu.bitcast(x.astype(BF16).astype(F32), U32)
    return bits(v[:, :PACKED]) | lax.shift_right_logical(bits(v[:, PACKED:]), jnp.uint32(16))


def _unpack_rows(w):
    hi = pltpu.bitcast(w & jnp.uint32(HIGH_HALF), F32).astype(BF16)
    lo = pltpu.bitcast(lax.shift_left(w, jnp.uint32(16)), F32).astype(BF16)
    return jnp.concatenate([hi, lo], axis=1)


def _tile_select(i, n_first, a_ref, b_ref):
    return jnp.where(i < n_first, a_ref[...], b_ref[...])


def _in_proj_kernel(xa_ref, xb_ref, inv_ref, gmix_ref, win_ref, gq_ref, wqt_ref, gkv_ref, wk_ref, wvt_ref,
                    e2_ref, bgate_ref, qt_ref, k_ref, vt_ref, u_ref, gates_ref, *, tm, n_first, seq_starts):
    i = pl.program_id(0)
    h = _rms(_tile_select(i, n_first, xa_ref, xb_ref), gmix_ref[...]).astype(BF16)
    proj = _dot(h, win_ref[...])

    t0 = i * tm
    s0 = 0
    for s in seq_starts:
        s0 = jnp.where(t0 >= s, s, s0)
    half = D_ROPE // 2
    pos = (t0 - s0 + lax.broadcasted_iota(I32, (half, tm), 1)).astype(F32)
    ang = pos * inv_ref[...]
    cos, sin = jnp.cos(ang), jnp.sin(ang)
    scale = (D_QK ** -0.5) * math.log2(math.e)
    cos_q, sin_q = cos * scale, sin * scale

    cqn = _rms(proj[:, C_CQ:C_CQ + Q_RANK], gq_ref[...]).astype(BF16)
    q2 = _dot_nt(wqt_ref[...], cqn)
    hw = N_HEADS * LANES
    for hh in range(N_HEADS):
        qa = q2[hh * LANES:(hh + 1) * LANES, :]
        qb = q2[hw + hh * D_ROPE:hw + (hh + 1) * D_ROPE, :]
        r0 = hh * LANES
        qt_ref[0, r0:r0 + D_NOPE, :] = (qa[:D_NOPE] * scale).astype(BF16)
        for part in range(2):
            lo = D_NOPE + part * half
            qt_ref[0, r0 + lo:r0 + lo + half, :] = (
                qa[lo:lo + half] * cos_q + qb[part * half:(part + 1) * half] * sin_q).astype(BF16)
        qt_ref[0, r0 + D_QK:r0 + LANES, :] = jnp.zeros((LANES - D_QK, tm), BF16)

    ckvn = _rms(proj[:, C_CKV:C_CKV + KV_RANK], gkv_ref[...]).astype(BF16)
    tabk = jnp.transpose(jnp.concatenate([cos, cos, sin, sin, jnp.zeros((LANES - 2 * D_ROPE, tm), F32)], axis=0))
    kpe = (proj[:, C_KPE:C_KPE + LANES] * tabk).astype(BF16)
    k_ref[...] = (_dot(ckvn, wk_ref[...]) + _dot(kpe, e2_ref[...])).astype(BF16)
    vt_ref[0] = _dot_nt(wvt_ref[...], ckvn).astype(BF16)

    a = proj[:, C_CONV:C_CONV + CONV_CH]
    g = proj[:, C_CONV + CONV_CH:C_CONV + 2 * CONV_CH]
    u_ref[...] = a * jax.nn.sigmoid(g)
    gates_ref[...] = jax.nn.sigmoid(proj[:, C_GATE:] + bgate_ref[...]).astype(BF16)


def _split_specs(block, n_first, n_second):
    first = lambda i: jnp.minimum(i, n_first - 1)
    second = lambda i: jnp.clip(i - n_first, 0, n_second - 1)
    rest = (0,) * (len(block) - 1)
    return [pl.BlockSpec(block, lambda i: (first(i),) + rest), pl.BlockSpec(block, lambda i: (second(i),) + rest)]


def _in_proj(xa, xb, inv_b, g_mix, w_in_ext, g_q, wq_t, g_kv, wk_ext, wv_t, e2, b_gate, *, tm, seq_starts):
    na, nb = xa.shape[0] // tm, xb.shape[0] // tm
    nt = na + nb
    t = nt * tm
    row = lambda w: pl.BlockSpec((tm, w), lambda i: (i, 0))
    col = lambda h: pl.BlockSpec((h, tm), lambda i: (0, i))
    return pl.pallas_call(
        functools.partial(_in_proj_kernel, tm=tm, n_first=na, seq_starts=seq_starts),
        grid=(nt,),
        in_specs=_split_specs((tm, D_MODEL), na, nb) + [
                  _const_spec((D_ROPE // 2, tm)), _const_spec((1, D_MODEL)),
                  _const_spec((D_MODEL, D_IN_EXT)), _const_spec((1, Q_RANK)),
                  _const_spec((N_HEADS * (LANES + D_ROPE), Q_RANK)), _const_spec((1, KV_RANK)),
                  _const_spec((KV_RANK, N_HEADS * LANES)), _const_spec((N_HEADS * D_V, KV_RANK)),
                  _const_spec((LANES, N_HEADS * LANES)), _const_spec((1, 2 * D_MODEL))],
        out_specs=(pl.BlockSpec((1, N_HEADS * LANES, tm), lambda i: (i, 0, 0)), row(N_HEADS * LANES),
                   pl.BlockSpec((1, N_HEADS * D_V, tm), lambda i: (i, 0, 0)), row(CONV_CH), row(2 * D_MODEL)),
        out_shape=(jax.ShapeDtypeStruct((nt, N_HEADS * LANES, tm), BF16),
                   jax.ShapeDtypeStruct((t, N_HEADS * LANES), BF16),
                   jax.ShapeDtypeStruct((nt, N_HEADS * D_V, tm), BF16),
                   jax.ShapeDtypeStruct((t, CONV_CH), F32),
                   jax.ShapeDtypeStruct((t, 2 * D_MODEL), BF16)),
        compiler_params=pltpu.CompilerParams(dimension_semantics=("arbitrary",), vmem_limit_bytes=VMEM_LIMIT),
        name="in_proj",
    )(xa, xb, inv_b, g_mix, w_in_ext, g_q, wq_t, g_kv, wk_ext, wv_t, e2, b_gate)


def _attn_kernel(qt_ref, k_ref, vt_ref, o_ref, s_ref, acc_ref, m_ref, *, nq, n_chunks, tk, unroll):
    half = n_chunks // 2
    ones = jnp.ones((16, tk), BF16)

    def scores(a, j, slot):
        ks = pl.multiple_of(j * tk, tk)
        s_ref[slot] = _dot(k_ref[pl.ds(ks, tk), :], qt_ref[a])

    def softmax_pv(j, slot):
        st = s_ref[slot]
        m = jnp.where(j == 0, NEG_BIG, m_ref[...])
        m_new = jnp.maximum(m, jnp.max(st, axis=0, keepdims=True))
        p = jnp.exp2(st - m_new).astype(BF16)
        v1 = jnp.concatenate([vt_ref[j], ones], axis=0)
        acc = jnp.exp2(m - m_new) * acc_ref[...] + _dot(v1, p)
        acc_ref[...] = acc
        m_ref[...] = m_new
        return acc

    m_ref[...] = jnp.full(m_ref.shape, NEG_BIG, F32)
    acc_ref[...] = jnp.zeros(acc_ref.shape, F32)
    scores(0, 0, 0)

    def pair(pp, may_end_tile):
        a = lax.div(pp, half)
        j = 2 * lax.rem(pp, half)
        scores(a, j + 1, 1)
        softmax_pv(j, 0)
        nxt = pp + 1
        scores(jnp.minimum(lax.div(nxt, half), nq - 1), 2 * lax.rem(nxt, half), 0)
        acc = softmax_pv(j + 1, 1)
        if may_end_tile:
            o_ref[a] = (acc[:D_V] / acc[D_V:D_V + 1]).astype(o_ref.dtype)

    period = math.gcd(half, unroll)

    def trip(blk, c):
        for u in range(unroll):
            pair(blk * unroll + u, (u + 1) % period == 0)
        return c

    lax.fori_loop(0, nq * half // unroll, trip, 0)


def _attention(qt3, k, vt3, *, seq_start, seq_len, n_seq, tq, tk):
    nq = seq_len // tq
    nkc = seq_len // tk
    assert nkc % 2 == 0, "key chunks are processed in pairs"
    s0 = seq_start // seq_len
    n_pairs = nq * (nkc // 2)
    unroll = next(u for u in (8, 4, 2, 1) if n_pairs % u == 0)
    return pl.pallas_call(
        functools.partial(_attn_kernel, nq=nq, n_chunks=nkc, tk=tk, unroll=unroll),
        grid=(n_seq, N_HEADS),
        in_specs=[pl.BlockSpec((nq, LANES, tq), lambda b, h: (s0 + b, h, 0)),
                  pl.BlockSpec((seq_len, LANES), lambda b, h: (s0 + b, h)),
                  pl.BlockSpec((nkc, D_V, tk), lambda b, h: (s0 + b, h, 0))],
        out_specs=pl.BlockSpec((nq, D_V, tq), lambda b, h: (b, h, 0)),
        out_shape=jax.ShapeDtypeStruct((n_seq * nq, N_HEADS * D_V, tq), BF16),
        scratch_shapes=[pltpu.VMEM((2, tk, tq), F32), pltpu.VMEM((D_V + 16, tq), F32), pltpu.VMEM((1, tq), F32)],
        compiler_params=pltpu.CompilerParams(dimension_semantics=("arbitrary", "arbitrary"),
                                             vmem_limit_bytes=VMEM_LIMIT),
        name="attention",
    )(qt3, k, vt3)


def _local_rows(tm):
    return TOP_K * tm + N_EXPERTS * RUN_PAD


def _sorted_rows(t, nt, bm):
    runs = t * TOP_K + nt * N_EXPERTS * (RUN_PAD - 1)
    return -(-runs // bm) * bm + N_EXPERTS * bm


def _rows_copy(src, s, dst, d, sem, rows=RUN_PAD):
    s = pl.multiple_of(s, RUN_PAD)
    d = pl.multiple_of(d, RUN_PAD)
    return pltpu.make_async_copy(src.at[pl.ds(s, rows), :], dst.at[pl.ds(d, rows), :], sem)


def _for_each_piece(n_ref, a_ref, b_ref, tile, fn):
    def per_expert(e, c):
        n = n_ref[tile * N_EXPERTS + e]
        a0 = a_ref[tile * N_EXPERTS + e]
        b0 = b_ref[tile * N_EXPERTS + e]
        n_big = lax.div(n, BIG_PIECE // RUN_PAD)
        done = n_big * BIG_PIECE

        def big(q, c2):
            fn(a0 + q * BIG_PIECE, b0 + q * BIG_PIECE, BIG_PIECE)
            return c2

        def small(q, c2):
            fn(a0 + done + q * RUN_PAD, b0 + done + q * RUN_PAD, RUN_PAD)
            return c2

        lax.fori_loop(0, n_big, big, 0)
        lax.fori_loop(0, n - n_big * (BIG_PIECE // RUN_PAD), small, 0)
        return c

    lax.fori_loop(0, N_EXPERTS, per_expert, 0)


def _tile_pieces(n_ref, tile):
    def per_expert(e, c):
        n = n_ref[tile * N_EXPERTS + e]
        n_big = lax.div(n, BIG_PIECE // RUN_PAD)
        return c[0] + n_big, c[1] + n - n_big * (BIG_PIECE // RUN_PAD)

    return lax.fori_loop(0, N_EXPERTS, per_expert, (0, 0))


def _wait_rows(src, dst, sem, count, rows):
    def one(q, c):
        _rows_copy(src, 0, dst, 0, sem, rows=rows).wait()
        return c

    lax.fori_loop(0, count, one, 0)


def _wait_tile(src, dst, sem, n_ref, tile):
    n_big, n_small = _tile_pieces(n_ref, tile)
    _wait_rows(src, dst, sem, n_big, BIG_PIECE)
    _wait_rows(src, dst, sem, n_small, RUN_PAD)


def _post_mix_kernel(xa_ref, xb_ref, ota_ref, otb_ref, u_ref, up_ref, un_ref, gates_ref, woa_ref, cw_ref, cb_ref,
                     lng_ref, lnb_ref, wpb_ref, bpb_ref, wout_ref, gffn_ref, wr_ref, br_ref, ltri_ref, utri_ref,
                     x1_ref, route_ref, meta_ref, ext_ref, cv_ref, ph_ref, carry_ref, *, tm, n_first, seq_bounds):
    i = pl.program_id(0)
    t0 = i * tm
    is_start = functools.reduce(jnp.logical_or, [t0 == s for s, _ in seq_bounds])
    is_end = functools.reduce(jnp.logical_or, [t0 + tm == e for _, e in seq_bounds])

    ext_ref[0:HALO_ROWS, :] = jnp.where(is_start, 0.0, up_ref[...])
    ext_ref[HALO_ROWS:HALO_ROWS + tm, :] = u_ref[...]
    ext_ref[HALO_ROWS + tm:2 * HALO_ROWS + tm, :] = jnp.where(is_end, 0.0, un_ref[...])
    rc = min(tm, CONV_ROWS)
    sub = 8
    base = HALO_ROWS - CONV_PAD
    for c in range(tm // rc):
        acc = None
        for b in range(sub):
            part = None
            for o in range(CONV_WIDTH):
                if (o + base) % sub != b:
                    continue
                a0 = c * rc + (o + base) - b
                term = cw_ref[o:o + 1, :] * ext_ref[a0:a0 + rc + sub, :]
                part = term if part is None else part + term
            if b == 0:
                shifted = part[:rc]
            else:
                ph_ref[b] = part
                shifted = ph_ref[b, b:b + rc, :]
            acc = shifted if acc is None else acc + shifted
        cv_ref[c * rc:(c + 1) * rc, :] = acc + cb_ref[...]

    cv = cv_ref[...]
    mu = jnp.mean(cv, axis=-1, keepdims=True)
    xc = cv - mu
    y = xc * lax.rsqrt(jnp.mean(xc * xc, axis=-1, keepdims=True) + LN_EPS) * lng_ref[...] + lnb_ref[...]
    y = y * jax.nn.sigmoid(y)
    branch_b = _dot(y.astype(BF16), wpb_ref[...]) + bpb_ref[...]
    branch_a = _dot_tn(_tile_select(i, n_first, ota_ref, otb_ref)[0], woa_ref[...])
    gates = gates_ref[...]
    mix = gates[:, :D_MODEL].astype(F32) * branch_a + gates[:, D_MODEL:].astype(F32) * branch_b
    x1 = _tile_select(i, n_first, xa_ref, xb_ref) + _dot(mix.astype(BF16), wout_ref[...])
    x1_ref[...] = x1

    h2 = _rms(x1, gffn_ref[...])
    h2_hi = h2.astype(BF16)
    h2_lo = (h2 - h2_hi.astype(F32)).astype(BF16)
    hi_both = _dot(h2_hi, wr_ref[...])
    logits = hi_both[:, :LANES] + (hi_both[:, LANES:] + _dot(h2_lo, wr_ref[:, :LANES])) + br_ref[...]
    lane = lax.broadcasted_iota(I32, logits.shape, 1)
    lane_f = lane.astype(F32)
    vals, idxs, sels = [], [], []
    cur = logits
    for _ in range(TOP_K):
        mval = jnp.max(cur, axis=-1, keepdims=True)
        ik = jnp.min(jnp.where(cur == mval, lane_f, float(LANES)), axis=-1, keepdims=True)
        sel = lane_f == ik
        cur = jnp.where(sel, -jnp.inf, cur)
        vals.append(mval)
        idxs.append(ik)
        sels.append(sel)
    exps = [jnp.exp(v - vals[0]) for v in vals]
    denom = exps[0] + exps[1] + exps[2] + exps[3]

    onehot = sels[0].astype(F32) + sels[1].astype(F32) + sels[2].astype(F32) + sels[3].astype(F32)

    @pl.when(i == 0)
    def _():
        carry_ref[...] = jnp.zeros_like(carry_ref)

    n_e = jnp.sum(onehot, axis=0, keepdims=True)
    n_pieces = jnp.floor((n_e + (RUN_PAD - 1)) * (1.0 / RUN_PAD))
    pieces_b = jnp.broadcast_to(n_pieces, (8, LANES)).astype(BF16)
    off = _dot(pieces_b, utri_ref[...])[0:1] * RUN_PAD
    total_pieces = jnp.sum(n_pieces, axis=-1, keepdims=True)
    carry = carry_ref[...]
    prefix = _dot(ltri_ref[...], onehot.astype(BF16))
    local = off + prefix
    route = jnp.zeros(logits.shape, F32)
    for kk in range(TOP_K):
        pos = jnp.sum(jnp.where(sels[kk], local, 0.0), axis=-1, keepdims=True)
        route = jnp.where(lane == kk, idxs[kk], route)
        route = jnp.where(lane == TOP_K + kk, exps[kk] / denom, route)
        route = jnp.where(lane == 2 * TOP_K + kk, pos, route)
    route_ref[...] = route
    carry_ref[...] = carry + n_pieces * RUN_PAD

    row8 = lax.broadcasted_iota(I32, (8, LANES), 0)
    metav = jnp.where(row8 == 0, jnp.broadcast_to(carry, (8, LANES)), 0.0)
    metav = jnp.where(row8 == 1, jnp.broadcast_to(n_pieces, (8, LANES)), metav)
    metav = jnp.where(row8 == 2, jnp.broadcast_to(off, (8, LANES)), metav)
    metav = jnp.where(row8 == 3, jnp.broadcast_to(total_pieces, (8, LANES)), metav)
    meta_ref[0] = metav.astype(I32)


def _post_mix(xa, xb, ota, otb, u, gates, w_oa, conv_w, conv_b, ln_g, ln_b, w_pb, b_pb, w_out, g_ffn, w_r, b_r, ltri,
              utri, *, tm, seq_bounds):
    na, nb = xa.shape[0] // tm, xb.shape[0] // tm
    nt = na + nb
    t = nt * tm
    hb = tm // HALO_ROWS
    n_halo = t // HALO_ROWS
    row = lambda w: pl.BlockSpec((tm, w), lambda i: (i, 0))
    return pl.pallas_call(
        functools.partial(_post_mix_kernel, tm=tm, n_first=na, seq_bounds=seq_bounds),
        grid=(nt,),
        in_specs=_split_specs((tm, D_MODEL), na, nb) + _split_specs((1, N_HEADS * D_V, tm), na, nb) + [
                  row(CONV_CH),
                  pl.BlockSpec((HALO_ROWS, CONV_CH), lambda i: (jnp.maximum(i * hb - 1, 0), 0)),
                  pl.BlockSpec((HALO_ROWS, CONV_CH), lambda i: (jnp.minimum((i + 1) * hb, n_halo - 1), 0)),
                  row(2 * D_MODEL), _const_spec((N_HEADS * D_V, D_MODEL)), _const_spec((CONV_WIDTH, CONV_CH)),
                  _const_spec((1, CONV_CH)), _const_spec((1, CONV_CH)), _const_spec((1, CONV_CH)),
                  _const_spec((CONV_CH, D_MODEL)), _const_spec((1, D_MODEL)), _const_spec((D_MODEL, D_MODEL)),
                  _const_spec((1, D_MODEL)), _const_spec((D_MODEL, 2 * LANES)), _const_spec((1, LANES)),
                  _const_spec((tm, tm)), _const_spec((LANES, LANES))],
        out_specs=(row(D_MODEL), row(LANES), pl.BlockSpec((1, 8, LANES), lambda i: (i, 0, 0))),
        out_shape=(jax.ShapeDtypeStruct((t, D_MODEL), F32), jax.ShapeDtypeStruct((t, LANES), F32),
                   jax.ShapeDtypeStruct((nt, 8, LANES), I32)),
        scratch_shapes=[pltpu.VMEM((tm + 2 * HALO_ROWS, CONV_CH), F32), pltpu.VMEM((tm, CONV_CH), F32),
                        pltpu.VMEM((8, min(tm, CONV_ROWS) + 8, CONV_CH), F32), pltpu.VMEM((1, LANES), F32)],
        compiler_params=pltpu.CompilerParams(dimension_semantics=("arbitrary",), vmem_limit_bytes=VMEM_LIMIT),
        name="post_mix",
    )(xa, xb, ota, otb, u, u, u, gates, w_oa, conv_w, conv_b, ln_g, ln_b, w_pb, b_pb, w_out, g_ffn, w_r, b_r, ltri, utri)


def _dispatch_kernel(npc_s, off_s, dst_s, start_s, x1_ref, route_ref, gffn_ref, xs_hbm, xloc_ref, zero_ref, row_sems,
                     *, tm, nt, bm, n_rows):
    i = pl.program_id(0)
    slot = lax.rem(i, N_DISPATCH_BUFS)
    n_loc = _local_rows(tm)

    h2 = _rms(x1_ref[...], gffn_ref[...]).astype(BF16)
    pos_t = jnp.transpose(route_ref[...])[2 * TOP_K:3 * TOP_K, :].astype(I32)
    prow = lax.broadcasted_iota(I32, (n_loc, tm), 0)
    perm = jnp.zeros((n_loc, tm), F32)
    for kk in range(TOP_K):
        perm = jnp.where(prow == pos_t[kk:kk + 1, :], 1.0, perm)
    xloc_ref[slot] = _pack_rows(_dot(perm.astype(BF16), h2))

    def wait_tile(tile):
        _wait_tile(xloc_ref.at[0], xs_hbm, row_sems.at[lax.rem(tile, N_DISPATCH_BUFS)], npc_s, tile)

    @pl.when(i >= N_DISPATCH_BUFS - 1)
    def _():
        wait_tile(i - (N_DISPATCH_BUFS - 1))

    _for_each_piece(npc_s, off_s, dst_s, i,
                    lambda s, d, rows: _rows_copy(xloc_ref.at[slot], s, xs_hbm, d, row_sems.at[slot], rows=rows).start())

    @pl.when(i == nt - 1)
    def _():
        for back in range(N_DISPATCH_BUFS - 2, -1, -1):
            @pl.when(i - back >= 0)
            def _():
                wait_tile(i - back)
        row_sem = row_sems.at[0]
        zero_ref[...] = jnp.zeros(zero_ref.shape, U32)

        def tail(e, c):
            end = dst_s[i * N_EXPERTS + e] + npc_s[i * N_EXPERTS + e] * RUN_PAD
            n = lax.shift_right_logical(start_s[e + 1] - end, int(math.log2(RUN_PAD)))

            def piece(q, c2):
                _rows_copy(zero_ref, 0, xs_hbm, end + q * RUN_PAD, row_sem).start()
                return c2

            lax.fori_loop(0, n, piece, 0)
            return c + n

        _wait_rows(zero_ref, xs_hbm, row_sem, lax.fori_loop(0, N_EXPERTS, tail, 0), RUN_PAD)
        used = start_s[N_EXPERTS]
        n_free = (n_rows - used) // bm

        def fill(b, c):
            _rows_copy(zero_ref, 0, xs_hbm, used + b * bm, row_sem, rows=bm).start()
            return c

        lax.fori_loop(0, n_free, fill, 0)
        _wait_rows(zero_ref, xs_hbm, row_sem, n_free, bm)


def _dispatch(n_pieces, off, dst, start, x1, route, g_ffn, *, tm, bm, n_rows):
    t = x1.shape[0]
    nt = t // tm
    return pl.pallas_call(
        functools.partial(_dispatch_kernel, tm=tm, nt=nt, bm=bm, n_rows=n_rows),
        grid_spec=pltpu.PrefetchScalarGridSpec(
            num_scalar_prefetch=4,
            grid=(nt,),
            in_specs=[pl.BlockSpec((tm, D_MODEL), lambda i, *_: (i, 0)), pl.BlockSpec((tm, LANES), lambda i, *_: (i, 0)),
                      pl.BlockSpec((1, D_MODEL), lambda i, *_: (0, 0))],
            out_specs=pl.BlockSpec(memory_space=pl.ANY),
            scratch_shapes=[pltpu.VMEM((N_DISPATCH_BUFS, _local_rows(tm), PACKED), U32),
                            pltpu.VMEM((bm, PACKED), U32), pltpu.SemaphoreType.DMA((N_DISPATCH_BUFS,))],
        ),
        out_shape=jax.ShapeDtypeStruct((n_rows, PACKED), U32),
        compiler_params=pltpu.CompilerParams(dimension_semantics=("arbitrary",), vmem_limit_bytes=VMEM_LIMIT,
                                             has_side_effects=True),
        name="dispatch",
    )(n_pieces, off, dst, start, x1, route, g_ffn)


def _moe_kernel(exp_ref, n_ref, xs_ref, wgu_ref, bgu_ref, wd_ref, bd_ref, y_ref, wgu_bf, wd_bf):
    w = pl.program_id(0)
    used = w < n_ref[0]

    @pl.when(jnp.logical_or(w == 0, exp_ref[w] != exp_ref[jnp.maximum(w - 1, 0)]))
    def _():
        wgu_bf[...] = wgu_ref[0].astype(BF16)
        wd_bf[...] = wd_ref[0].astype(BF16)

    @pl.when(used)
    def _():
        x = _unpack_rows(xs_ref[...])
        gu = _dot(x, wgu_bf[...]) + bgu_ref[0]
        g = jnp.minimum(gu[:, :D_FF], SWIGLU_LIMIT)
        u = jnp.clip(gu[:, D_FF:], -SWIGLU_LIMIT, SWIGLU_LIMIT)
        act = (u + 1.0) * (g * jax.nn.sigmoid(SWIGLU_ALPHA * g))
        y_ref[...] = _pack_rows(_dot(act.astype(BF16), wd_bf[...]) + bd_ref[0])

    @pl.when(jnp.logical_not(used))
    def _():
        y_ref[...] = jnp.zeros(y_ref.shape, U32)


def _moe(ex, n_used, xs, w_gu, b_gu, w_down, b_down, *, bm):
    return pl.pallas_call(
        _moe_kernel,
        grid_spec=pltpu.PrefetchScalarGridSpec(
            num_scalar_prefetch=2,
            grid=(xs.shape[0] // bm,),
            in_specs=[pl.BlockSpec((bm, PACKED), lambda w, ex, n: (w, 0)),
                      pl.BlockSpec((1, D_MODEL, 2 * D_FF), lambda w, ex, n: (ex[w], 0, 0)),
                      pl.BlockSpec((1, 1, 2 * D_FF), lambda w, ex, n: (ex[w], 0, 0)),
                      pl.BlockSpec((1, D_FF, D_MODEL), lambda w, ex, n: (ex[w], 0, 0)),
                      pl.BlockSpec((1, 1, D_MODEL), lambda w, ex, n: (ex[w], 0, 0))],
            out_specs=pl.BlockSpec((bm, PACKED), lambda w, ex, n: (w, 0)),
            scratch_shapes=[pltpu.VMEM((D_MODEL, 2 * D_FF), BF16), pltpu.VMEM((D_FF, D_MODEL), BF16)],
        ),
        out_shape=jax.ShapeDtypeStruct(xs.shape, U32),
        compiler_params=pltpu.CompilerParams(dimension_semantics=("arbitrary",), vmem_limit_bytes=VMEM_LIMIT),
        name="moe",
    )(ex, n_used, xs, w_gu, b_gu, w_down, b_down)


def _expert_layout(totals, *, bm, n_blocks):
    n_blk = (totals + bm - 1) // bm
    blk_end = jnp.cumsum(n_blk)
    start = jnp.concatenate([jnp.zeros((1,), I32), blk_end * bm]).astype(I32)
    w = jnp.arange(n_blocks, dtype=I32)
    ex = jnp.minimum(jnp.sum((blk_end[None, :] <= w[:, None]).astype(I32), axis=1), N_EXPERTS - 1)
    return start, ex.astype(I32), blk_end[-1].astype(I32)[None]


def _combine_kernel(npc_s, src_s, off_s, x1_ref, route_ref, gfin_ref, y_hbm, outa_ref, outb_ref, yloc_ref, sems,
                    *, tm, nt, n_first):
    i = pl.program_id(0)
    slot = lax.rem(i, 2)
    n_loc = _local_rows(tm)

    def gather(tile, sl):
        _for_each_piece(npc_s, src_s, off_s, tile,
                        lambda s, d, rows: _rows_copy(y_hbm, s, yloc_ref.at[sl], d, sems.at[sl], rows=rows).start())

    @pl.when(i == 0)
    def _():
        yloc_ref[...] = jnp.zeros(yloc_ref.shape, U32)
        gather(0, 0)

    @pl.when(i + 1 < nt)
    def _():
        gather(i + 1, 1 - slot)

    _wait_tile(y_hbm, yloc_ref.at[slot], sems.at[slot], npc_s, i)

    route = route_ref[...]
    lanes = lax.broadcasted_iota(I32, (tm, n_loc), 1)
    pw = jnp.zeros((tm, n_loc), F32)
    for kk in range(TOP_K):
        pos = route[:, 2 * TOP_K + kk:2 * TOP_K + kk + 1].astype(I32)
        pw = jnp.where(lanes == pos, route[:, TOP_K + kk:TOP_K + kk + 1], pw)
    moe = _dot(pw.astype(BF16), _unpack_rows(yloc_ref[slot]))
    res = _rms(x1_ref[...] + moe, gfin_ref[...])

    @pl.when(i < n_first)
    def _():
        outa_ref[...] = res

    @pl.when(i >= n_first)
    def _():
        outb_ref[...] = res


def _combine(n_pieces, src, off, x1, route, g_final, y, *, tm, n_first):
    t = x1.shape[0]
    nt = t // tm
    n_second = nt - n_first
    return pl.pallas_call(
        functools.partial(_combine_kernel, tm=tm, nt=nt, n_first=n_first),
        grid_spec=pltpu.PrefetchScalarGridSpec(
            num_scalar_prefetch=3,
            grid=(nt,),
            in_specs=[pl.BlockSpec((tm, D_MODEL), lambda i, *_: (i, 0)), pl.BlockSpec((tm, LANES), lambda i, *_: (i, 0)),
                      pl.BlockSpec((1, D_MODEL), lambda i, *_: (0, 0)), pl.BlockSpec(memory_space=pl.ANY)],
            out_specs=(pl.BlockSpec((tm, D_MODEL), lambda i, *_: (jnp.minimum(i, n_first - 1), 0)),
                       pl.BlockSpec((tm, D_MODEL), lambda i, *_: (jnp.clip(i - n_first, 0, n_second - 1), 0))),
            scratch_shapes=[pltpu.VMEM((2, _local_rows(tm), PACKED), U32), pltpu.SemaphoreType.DMA((2,))],
        ),
        out_shape=(jax.ShapeDtypeStruct((n_first * tm, D_MODEL), F32),
                   jax.ShapeDtypeStruct((n_second * tm, D_MODEL), F32)),
        compiler_params=pltpu.CompilerParams(dimension_semantics=("arbitrary",), vmem_limit_bytes=VMEM_LIMIT),
        name="combine",
    )(n_pieces, src, off, x1, route, g_final, y)


def _rot_cols(w):
    half = w.shape[-1] // 2
    return jnp.concatenate([-w[..., half:], w[..., :half]], axis=-1)


def _prep_weights(w_in, w_uq, w_ukv):
    d = w_in.shape[0]
    w_kpe = w_in[:, C_KPE:C_KPE + D_ROPE]
    w_in_ext = jnp.concatenate(
        [w_in[:, :C_KPE], w_kpe, _rot_cols(w_kpe), jnp.zeros((d, LANES - 2 * D_ROPE), F32),
         w_in[:, C_KPE + D_ROPE:]], axis=1).astype(BF16)

    wq = w_uq.reshape(Q_RANK, N_HEADS, D_QK)
    nope, pe = wq[..., :D_NOPE], wq[..., D_NOPE:]
    zpad = jnp.zeros((Q_RANK, N_HEADS, LANES - D_QK), F32)
    plain = jnp.concatenate([nope, pe, zpad], axis=-1).reshape(Q_RANK, N_HEADS * LANES)
    rot = _rot_cols(pe).reshape(Q_RANK, N_HEADS * D_ROPE)
    wq_t = jnp.concatenate([plain, rot], axis=1).T.astype(BF16)

    wkv = w_ukv.reshape(KV_RANK, N_HEADS, D_NOPE + D_V)
    wk_ext = jnp.concatenate([wkv[..., :D_NOPE], jnp.zeros((KV_RANK, N_HEADS, LANES - D_NOPE), F32)],
                             axis=-1).reshape(KV_RANK, N_HEADS * LANES).astype(BF16)
    wv_t = wkv[..., D_NOPE:].reshape(KV_RANK, N_HEADS * D_V).T.astype(BF16)
    return w_in_ext, wq_t, wk_ext, wv_t


def _kpe_placement():
    e2 = np.zeros((LANES, N_HEADS * LANES), np.float32)
    for j in range(D_ROPE):
        for h in range(N_HEADS):
            e2[j, h * LANES + D_NOPE + j] = 1.0
            e2[D_ROPE + j, h * LANES + D_NOPE + j] = 1.0
    return jnp.asarray(e2, BF16)


def _inv_freq(tm):
    inv = 1.0 / (ROPE_THETA ** (jnp.arange(0, D_ROPE, 2, dtype=F32) / D_ROPE))
    return jnp.broadcast_to(inv[:, None], (D_ROPE // 2, tm))


def _pick(n, pref):
    return pref if n % pref == 0 else n


def kernel(x_prompt, x_sample, g_mix, w_in, b_gate, g_q, w_uq, g_kv, w_ukv, w_oa, conv_w, conv_b, ln_g, ln_b, w_pb,
           b_pb, w_out, g_ffn, w_router, b_router, w_gu, b_gu, w_down, b_down, g_final):
    b1, s1, d = x_prompt.shape
    b2, s2, _ = x_sample.shape
    t1, t2 = b1 * s1, b2 * s2
    t = t1 + t2
    xa, xb = x_prompt.reshape(t1, d), x_sample.reshape(t2, d)
    seq_bounds = tuple((b * s1, (b + 1) * s1) for b in range(b1)) + tuple(
        (t1 + b * s2, t1 + (b + 1) * s2) for b in range(b2))

    tm = _pick(math.gcd(s1, s2), 512)
    nt = t // tm
    w_in_ext, wq_t, wk_ext, wv_t = _prep_weights(w_in[0], w_uq[0], w_ukv[0])

    qt, k, vt3, u, gates = _in_proj(xa, xb, _inv_freq(tm), g_mix, w_in_ext, g_q, wq_t, g_kv, wk_ext, wv_t,
                                    _kpe_placement(), b_gate, tm=tm, seq_starts=tuple(s for s, _ in seq_bounds))

    ota = _attention(qt, k, vt3, seq_start=0, seq_len=s1, n_seq=b1, tq=tm, tk=tm)
    otb = _attention(qt, k, vt3, seq_start=t1, seq_len=s2, n_seq=b2, tq=tm, tk=tm)

    w_r = jnp.concatenate([w_router[0], jnp.zeros((d, LANES - N_EXPERTS), F32)], axis=1)
    w_r_hi = w_r.astype(BF16)
    w_r = jnp.concatenate([w_r_hi, (w_r - w_r_hi.astype(F32)).astype(BF16)], axis=1)
    b_r = jnp.concatenate([b_router[0], jnp.full((LANES - N_EXPERTS,), NEG_BIG, F32)])[None, :]
    ltri = jnp.asarray(np.tril(np.ones((tm, tm), np.float32), -1), BF16)
    utri = jnp.asarray(np.triu(np.ones((LANES, LANES), np.float32), 1), BF16)
    x1, route, meta = _post_mix(xa, xb, ota, otb, u, gates, w_oa[0].astype(BF16), conv_w[0], conv_b, ln_g, ln_b,
                                w_pb[0].astype(BF16), b_pb, w_out[0].astype(BF16), g_ffn, w_r, b_r, ltri, utri,
                                tm=tm, seq_bounds=seq_bounds)

    bm = MOE_ROWS
    n_rows = _sorted_rows(t, nt, bm)
    carry, n_pieces, off = (meta[:, r, :N_EXPERTS] for r in range(3))
    start, ex, n_used = _expert_layout(carry[-1] + n_pieces[-1] * RUN_PAD, bm=bm, n_blocks=n_rows // bm)
    run_row = (carry + start[None, :N_EXPERTS]).reshape(-1)
    n_pieces, off = n_pieces.reshape(-1), off.reshape(-1)

    xs = _dispatch(n_pieces, off, run_row, start, x1, route, g_ffn, tm=tm, bm=bm, n_rows=n_rows)
    y = _moe(ex, n_used, xs, w_gu[0], b_gu[0][:, None, :], w_down[0], b_down[0][:, None, :], bm=bm)
    out_a, out_b = _combine(n_pieces, run_row, off, x1, route, g_final[None, :], y, tm=tm, n_first=t1 // tm)
    return out_a.reshape(b1, s1, d), out_b.reshape(b2, s2, d)
```

```python
import functools
import math

import numpy as np
import jax
import jax.numpy as jnp
from jax import lax
from jax.experimental import pallas as pl
from jax.experimental.pallas import tpu as pltpu

F32 = jnp.float32
BF16 = jnp.bfloat16
I32 = jnp.int32
U32 = jnp.uint32

D_MODEL = 1024
N_HEADS = 8
Q_RANK = 384
KV_RANK = 256
D_NOPE = 64
D_ROPE = 32
D_V = 64
D_QK = D_NOPE + D_ROPE
ROPE_THETA = 10000.0
CONV_CH = 512
CONV_WIDTH = 31
CONV_PAD = (CONV_WIDTH - 1) // 2
N_EXPERTS = 32
TOP_K = 4
D_FF = 1024
SWIGLU_ALPHA = 1.702
SWIGLU_LIMIT = 7.0
RMS_EPS = 1e-6
LN_EPS = 1e-5

LANES = 128
HALO_ROWS = 16
CONV_ROWS = 512
NEG_BIG = -1e30
RUN_PAD = 8
BIG_PIECE = 32
N_DISPATCH_BUFS = 3
MOE_ROWS = 512

C_CQ = 0
C_CKV = Q_RANK
C_KPE = Q_RANK + KV_RANK
C_CONV = C_KPE + LANES
C_GATE = C_CONV + 2 * CONV_CH
D_IN_EXT = C_GATE + 2 * D_MODEL

VMEM_LIMIT = 56 * 1024 * 1024


def _rms(x, g):
    return x * lax.rsqrt(jnp.mean(x * x, axis=-1, keepdims=True) + RMS_EPS) * g


def _dot(a, b):
    return jnp.dot(a, b, preferred_element_type=F32)


def _dot_nt(a, b):
    return lax.dot_general(a, b, (((1,), (1,)), ((), ())), preferred_element_type=F32)


def _dot_tn(a, b):
    return lax.dot_general(a, b, (((0,), (0,)), ((), ())), preferred_element_type=F32)


def _const_spec(shape):
    return pl.BlockSpec(shape, lambda *_: (0,) * len(shape))


PACKED = D_MODEL // 2
HIGH_HALF = 0xFFFF0000


def _pack_rows(v):
    bits = lambda x: pltpu.bitcast(x.astype(BF16).astype(F32), U32)
    return bits(v[:, :PACKED]) | lax.shift_right_logical(bits(v[:, PACKED:]), jnp.uint32(16))


def _unpack_rows(w):
    hi = pltpu.bitcast(w & jnp.uint32(HIGH_HALF), F32).astype(BF16)
    lo = pltpu.bitcast(lax.shift_left(w, jnp.uint32(16)), F32).astype(BF16)
    return jnp.concatenate([hi, lo], axis=1)


def _tile_select(i, n_first, a_ref, b_ref):
    return jnp.where(i < n_first, a_ref[...], b_ref[...])


def _in_proj_kernel(xa_ref, xb_ref, inv_ref, gmix_ref, win_ref, gq_ref, wqt_ref, gkv_ref, wk_ref, wvt_ref,
                    e2_ref, bgate_ref, qt_ref, k_ref, vt_ref, u_ref, gates_ref, *, tm, n_first, seq_starts):
    i = pl.program_id(0)
    h = _rms(_tile_select(i, n_first, xa_ref, xb_ref), gmix_ref[...]).astype(BF16)
    proj = _dot(h, win_ref[...])

    t0 = i * tm
    s0 = 0
    for s in seq_starts:
        s0 = jnp.where(t0 >= s, s, s0)
    half = D_ROPE // 2
    pos = (t0 - s0 + lax.broadcasted_iota(I32, (half, tm), 1)).astype(F32)
    ang = pos * inv_ref[...]
    cos, sin = jnp.cos(ang), jnp.sin(ang)
    scale = (D_QK ** -0.5) * math.log2(math.e)
    cos_q, sin_q = cos * scale, sin * scale

    cqn = _rms(proj[:, C_CQ:C_CQ + Q_RANK], gq_ref[...]).astype(BF16)
    q2 = _dot_nt(wqt_ref[...], cqn)
    hw = N_HEADS * LANES
    for hh in range(N_HEADS):
        qa = q2[hh * LANES:(hh + 1) * LANES, :]
        qb = q2[hw + hh * D_ROPE:hw + (hh + 1) * D_ROPE, :]
        r0 = hh * LANES
        qt_ref[0, r0:r0 + D_NOPE, :] = (qa[:D_NOPE] * scale).astype(BF16)
        for part in range(2):
            lo = D_NOPE + part * half
            qt_ref[0, r0 + lo:r0 + lo + half, :] = (
                qa[lo:lo + half] * cos_q + qb[part * half:(part + 1) * half] * sin_q).astype(BF16)
        qt_ref[0, r0 + D_QK:r0 + LANES, :] = jnp.zeros((LANES - D_QK, tm), BF16)

    ckvn = _rms(proj[:, C_CKV:C_CKV + KV_RANK], gkv_ref[...]).astype(BF16)
    tabk = jnp.transpose(jnp.concatenate([cos, cos, sin, sin, jnp.zeros((LANES - 2 * D_ROPE, tm), F32)], axis=0))
    kpe = (proj[:, C_KPE:C_KPE + LANES] * tabk).astype(BF16)
    k_ref[...] = (_dot(ckvn, wk_ref[...]) + _dot(kpe, e2_ref[...])).astype(BF16)
    vt_ref[0] = _dot_nt(wvt_ref[...], ckvn).astype(BF16)

    a = proj[:, C_CONV:C_CONV + CONV_CH]
    g = proj[:, C_CONV + CONV_CH:C_CONV + 2 * CONV_CH]
    u_ref[...] = a * jax.nn.sigmoid(g)
    gates_ref[...] = jax.nn.sigmoid(proj[:, C_GATE:] + bgate_ref[...]).astype(BF16)


def _split_specs(block, n_first, n_second):
    first = lambda i: jnp.minimum(i, n_first - 1)
    second = lambda i: jnp.clip(i - n_first, 0, n_second - 1)
    rest = (0,) * (len(block) - 1)
    return [pl.BlockSpec(block, lambda i: (first(i),) + rest), pl.BlockSpec(block, lambda i: (second(i),) + rest)]


def _in_proj(xa, xb, inv_b, g_mix, w_in_ext, g_q, wq_t, g_kv, wk_ext, wv_t, e2, b_gate, *, tm, seq_starts):
    na, nb = xa.shape[0] // tm, xb.shape[0] // tm
    nt = na + nb
    t = nt * tm
    row = lambda w: pl.BlockSpec((tm, w), lambda i: (i, 0))
    col = lambda h: pl.BlockSpec((h, tm), lambda i: (0, i))
    return pl.pallas_call(
        functools.partial(_in_proj_kernel, tm=tm, n_first=na, seq_starts=seq_starts),
        grid=(nt,),
        in_specs=_split_specs((tm, D_MODEL), na, nb) + [
                  _const_spec((D_ROPE // 2, tm)), _const_spec((1, D_MODEL)),
                  _const_spec((D_MODEL, D_IN_EXT)), _const_spec((1, Q_RANK)),
                  _const_spec((N_HEADS * (LANES + D_ROPE), Q_RANK)), _const_spec((1, KV_RANK)),
                  _const_spec((KV_RANK, N_HEADS * LANES)), _const_spec((N_HEADS * D_V, KV_RANK)),
                  _const_spec((LANES, N_HEADS * LANES)), _const_spec((1, 2 * D_MODEL))],
        out_specs=(pl.BlockSpec((1, N_HEADS * LANES, tm), lambda i: (i, 0, 0)), row(N_HEADS * LANES),
                   pl.BlockSpec((1, N_HEADS * D_V, tm), lambda i: (i, 0, 0)), row(CONV_CH), row(2 * D_MODEL)),
        out_shape=(jax.ShapeDtypeStruct((nt, N_HEADS * LANES, tm), BF16),
                   jax.ShapeDtypeStruct((t, N_HEADS * LANES), BF16),
                   jax.ShapeDtypeStruct((nt, N_HEADS * D_V, tm), BF16),
                   jax.ShapeDtypeStruct((t, CONV_CH), F32),
                   jax.ShapeDtypeStruct((t, 2 * D_MODEL), BF16)),
        compiler_params=pltpu.CompilerParams(dimension_semantics=("arbitrary",), vmem_limit_bytes=VMEM_LIMIT),
        name="in_proj",
    )(xa, xb, inv_b, g_mix, w_in_ext, g_q, wq_t, g_kv, wk_ext, wv_t, e2, b_gate)


def _attn_kernel(qt_ref, k_ref, vt_ref, o_ref, s_ref, acc_ref, m_ref, *, nq, n_chunks, tk, unroll):
    half = n_chunks // 2
    ones = jnp.ones((16, tk), BF16)

    def scores(a, j, slot):
        ks = pl.multiple_of(j * tk, tk)
        s_ref[slot] = _dot(k_ref[pl.ds(ks, tk), :], qt_ref[a])

    def softmax_pv(j, slot):
        st = s_ref[slot]
        m = jnp.where(j == 0, NEG_BIG, m_ref[...])
        m_new = jnp.maximum(m, jnp.max(st, axis=0, keepdims=True))
        p = jnp.exp2(st - m_new).astype(BF16)
        v1 = jnp.concatenate([vt_ref[j], ones], axis=0)
        acc = jnp.exp2(m - m_new) * acc_ref[...] + _dot(v1, p)
        acc_ref[...] = acc
        m_ref[...] = m_new
        return acc

    m_ref[...] = jnp.full(m_ref.shape, NEG_BIG, F32)
    acc_ref[...] = jnp.zeros(acc_ref.shape, F32)
    scores(0, 0, 0)

    def pair(pp, may_end_tile):
        a = lax.div(pp, half)
        j = 2 * lax.rem(pp, half)
        scores(a, j + 1, 1)
        softmax_pv(j, 0)
        nxt = pp + 1
        scores(jnp.minimum(lax.div(nxt, half), nq - 1), 2 * lax.rem(nxt, half), 0)
        acc = softmax_pv(j + 1, 1)
        if may_end_tile:
            o_ref[a] = (acc[:D_V] / acc[D_V:D_V + 1]).astype(o_ref.dtype)

    period = math.gcd(half, unroll)

    def trip(blk, c):
        for u in range(unroll):
            pair(blk * unroll + u, (u + 1) % period == 0)
        return c

    lax.fori_loop(0, nq * half // unroll, trip, 0)


def _attention(qt3, k, vt3, *, seq_start, seq_len, n_seq, tq, tk):
    nq = seq_len // tq
    nkc = seq_len // tk
    assert nkc % 2 == 0, "key chunks are processed in pairs"
    s0 = seq_start // seq_len
    n_pairs = nq * (nkc // 2)
    unroll = next(u for u in (8, 4, 2, 1) if n_pairs % u == 0)
    return pl.pallas_call(
        functools.partial(_attn_kernel, nq=nq, n_chunks=nkc, tk=tk, unroll=unroll),
        grid=(n_seq, N_HEADS),
        in_specs=[pl.BlockSpec((nq, LANES, tq), lambda b, h: (s0 + b, h, 0)),
                  pl.BlockSpec((seq_len, LANES), lambda b, h: (s0 + b, h)),
                  pl.BlockSpec((nkc, D_V, tk), lambda b, h: (s0 + b, h, 0))],
        out_specs=pl.BlockSpec((nq, D_V, tq), lambda b, h: (b, h, 0)),
        out_shape=jax.ShapeDtypeStruct((n_seq * nq, N_HEADS * D_V, tq), BF16),
        scratch_shapes=[pltpu.VMEM((2, tk, tq), F32), pltpu.VMEM((D_V + 16, tq), F32), pltpu.VMEM((1, tq), F32)],
        compiler_params=pltpu.CompilerParams(dimension_semantics=("arbitrary", "arbitrary"),
                                             vmem_limit_bytes=VMEM_LIMIT),
        name="attention",
    )(qt3, k, vt3)


def _local_rows(tm):
    return TOP_K * tm + N_EXPERTS * RUN_PAD


def _sorted_rows(t, nt, bm):
    runs = t * TOP_K + nt * N_EXPERTS * (RUN_PAD - 1)
    return -(-runs // bm) * bm + N_EXPERTS * bm


def _rows_copy(src, s, dst, d, sem, rows=RUN_PAD):
    s = pl.multiple_of(s, RUN_PAD)
    d = pl.multiple_of(d, RUN_PAD)
    return pltpu.make_async_copy(src.at[pl.ds(s, rows), :], dst.at[pl.ds(d, rows), :], sem)


def _piece_caps(n_loc):
    return n_loc // BIG_PIECE, N_EXPERTS * (BIG_PIECE // RUN_PAD - 1)


def _piece_tables(n_pieces, off, run_row, *, n_loc):
    per_big = BIG_PIECE // RUN_PAD
    qb, qs = _piece_caps(n_loc)
    n_big = n_pieces // per_big
    n_sm = n_pieces - n_big * per_big

    def table(counts, loc0, row0, step, cap):
        end = jnp.cumsum(counts, axis=1)
        q = jnp.arange(cap, dtype=I32)
        ex = jnp.minimum(jnp.sum((end[:, None, :] <= q[None, :, None]).astype(I32), axis=2), N_EXPERTS - 1)
        onehot = ex[..., None] == jnp.arange(N_EXPERTS, dtype=I32)
        pick = lambda v: jnp.sum(jnp.where(onehot, v[:, None, :], 0), axis=2)
        idx = q[None, :] - pick(end - counts)
        return (pick(loc0) + idx * step).reshape(-1), (pick(row0) + idx * step).reshape(-1), end[:, -1]

    loc_b, row_b, cnt_b = table(n_big, off, run_row, BIG_PIECE, qb)
    done = n_big * BIG_PIECE
    loc_s, row_s, cnt_s = table(n_sm, off + done, run_row + done, RUN_PAD, qs)
    return jnp.stack([cnt_b, cnt_s], axis=1).reshape(-1), loc_b, row_b, loc_s, row_s


def _for_each_piece(tables, tile, n_loc, fn):
    cnt, loc_b, row_b, loc_s, row_s = tables
    qb, qs = _piece_caps(n_loc)

    def big(q, c):
        fn(loc_b[tile * qb + q], row_b[tile * qb + q], BIG_PIECE)
        return c

    def small(q, c):
        fn(loc_s[tile * qs + q], row_s[tile * qs + q], RUN_PAD)
        return c

    lax.fori_loop(0, cnt[2 * tile], big, 0)
    lax.fori_loop(0, cnt[2 * tile + 1], small, 0)


def _wait_rows(src, dst, sem, count, rows):
    def one(q, c):
        _rows_copy(src, 0, dst, 0, sem, rows=rows).wait()
        return c

    lax.fori_loop(0, count, one, 0)


def _wait_tile(src, dst, sem, cnt, tile):
    _wait_rows(src, dst, sem, cnt[2 * tile], BIG_PIECE)
    _wait_rows(src, dst, sem, cnt[2 * tile + 1], RUN_PAD)


def _post_mix_kernel(xa_ref, xb_ref, ota_ref, otb_ref, u_ref, up_ref, un_ref, gates_ref, woa_ref, cw_ref, cb_ref,
                     lng_ref, lnb_ref, wpb_ref, bpb_ref, wout_ref, gffn_ref, wr_ref, br_ref, ltri_ref, utri_ref,
                     x1_ref, route_ref, meta_ref, ext_ref, cv_ref, ph_ref, carry_ref, *, tm, n_first, seq_bounds):
    i = pl.program_id(0)
    t0 = i * tm
    is_start = functools.reduce(jnp.logical_or, [t0 == s for s, _ in seq_bounds])
    is_end = functools.reduce(jnp.logical_or, [t0 + tm == e for _, e in seq_bounds])

    ext_ref[0:HALO_ROWS, :] = jnp.where(is_start, 0.0, up_ref[...])
    ext_ref[HALO_ROWS:HALO_ROWS + tm, :] = u_ref[...]
    ext_ref[HALO_ROWS + tm:2 * HALO_ROWS + tm, :] = jnp.where(is_end, 0.0, un_ref[...])
    rc = min(tm, CONV_ROWS)
    sub = 8
    base = HALO_ROWS - CONV_PAD
    for c in range(tm // rc):
        acc = None
        for b in range(sub):
            part = None
            for o in range(CONV_WIDTH):
                if (o + base) % sub != b:
                    continue
                a0 = c * rc + (o + base) - b
                term = cw_ref[o:o + 1, :] * ext_ref[a0:a0 + rc + sub, :]
                part = term if part is None else part + term
            if b == 0:
                shifted = part[:rc]
            else:
                ph_ref[b] = part
                shifted = ph_ref[b, b:b + rc, :]
            acc = shifted if acc is None else acc + shifted
        cv_ref[c * rc:(c + 1) * rc, :] = acc + cb_ref[...]

    cv = cv_ref[...]
    mu = jnp.mean(cv, axis=-1, keepdims=True)
    xc = cv - mu
    y = xc * lax.rsqrt(jnp.mean(xc * xc, axis=-1, keepdims=True) + LN_EPS) * lng_ref[...] + lnb_ref[...]
    y = y * jax.nn.sigmoid(y)
    branch_b = _dot(y.astype(BF16), wpb_ref[...]) + bpb_ref[...]
    branch_a = _dot_tn(_tile_select(i, n_first, ota_ref, otb_ref)[0], woa_ref[...])
    gates = gates_ref[...]
    mix = gates[:, :D_MODEL].astype(F32) * branch_a + gates[:, D_MODEL:].astype(F32) * branch_b
    x1 = _tile_select(i, n_first, xa_ref, xb_ref) + _dot(mix.astype(BF16), wout_ref[...])
    x1_ref[...] = x1

    h2 = _rms(x1, gffn_ref[...])
    h2_hi = h2.astype(BF16)
    h2_lo = (h2 - h2_hi.astype(F32)).astype(BF16)
    hi_both = _dot(h2_hi, wr_ref[...])
    logits = hi_both[:, :LANES] + (hi_both[:, LANES:] + _dot(h2_lo, wr_ref[:, :LANES])) + br_ref[...]
    lane = lax.broadcasted_iota(I32, logits.shape, 1)
    lane_f = lane.astype(F32)
    vals, idxs, sels = [], [], []
    cur = logits
    for _ in range(TOP_K):
        mval = jnp.max(cur, axis=-1, keepdims=True)
        ik = jnp.min(jnp.where(cur == mval, lane_f, float(LANES)), axis=-1, keepdims=True)
        sel = lane_f == ik
        cur = jnp.where(sel, -jnp.inf, cur)
        vals.append(mval)
        idxs.append(ik)
        sels.append(sel)
    exps = [jnp.exp(v - vals[0]) for v in vals]
    denom = exps[0] + exps[1] + exps[2] + exps[3]

    onehot = sels[0].astype(F32) + sels[1].astype(F32) + sels[2].astype(F32) + sels[3].astype(F32)

    @pl.when(i == 0)
    def _():
        carry_ref[...] = jnp.zeros_like(carry_ref)

    n_e = jnp.sum(onehot, axis=0, keepdims=True)
    n_pieces = jnp.floor((n_e + (RUN_PAD - 1)) * (1.0 / RUN_PAD))
    pieces_b = jnp.broadcast_to(n_pieces, (8, LANES)).astype(BF16)
    off = _dot(pieces_b, utri_ref[...])[0:1] * RUN_PAD
    total_pieces = jnp.sum(n_pieces, axis=-1, keepdims=True)
    carry = carry_ref[...]
    prefix = _dot(ltri_ref[...], onehot.astype(BF16))
    local = off + prefix
    route = jnp.zeros(logits.shape, F32)
    for kk in range(TOP_K):
        pos = jnp.sum(jnp.where(sels[kk], local, 0.0), axis=-1, keepdims=True)
        route = jnp.where(lane == kk, idxs[kk], route)
        route = jnp.where(lane == TOP_K + kk, exps[kk] / denom, route)
        route = jnp.where(lane == 2 * TOP_K + kk, pos, route)
    route_ref[...] = route
    carry_ref[...] = carry + n_pieces * RUN_PAD

    row8 = lax.broadcasted_iota(I32, (8, LANES), 0)
    metav = jnp.where(row8 == 0, jnp.broadcast_to(carry, (8, LANES)), 0.0)
    metav = jnp.where(row8 == 1, jnp.broadcast_to(n_pieces, (8, LANES)), metav)
    metav = jnp.where(row8 == 2, jnp.broadcast_to(off, (8, LANES)), metav)
    metav = jnp.where(row8 == 3, jnp.broadcast_to(total_pieces, (8, LANES)), metav)
    meta_ref[0] = metav.astype(I32)


def _post_mix(xa, xb, ota, otb, u, gates, w_oa, conv_w, conv_b, ln_g, ln_b, w_pb, b_pb, w_out, g_ffn, w_r, b_r, ltri,
              utri, *, tm, seq_bounds):
    na, nb = xa.shape[0] // tm, xb.shape[0] // tm
    nt = na + nb
    t = nt * tm
    hb = tm // HALO_ROWS
    n_halo = t // HALO_ROWS
    row = lambda w: pl.BlockSpec((tm, w), lambda i: (i, 0))
    return pl.pallas_call(
        functools.partial(_post_mix_kernel, tm=tm, n_first=na, seq_bounds=seq_bounds),
        grid=(nt,),
        in_specs=_split_specs((tm, D_MODEL), na, nb) + _split_specs((1, N_HEADS * D_V, tm), na, nb) + [
                  row(CONV_CH),
                  pl.BlockSpec((HALO_ROWS, CONV_CH), lambda i: (jnp.maximum(i * hb - 1, 0), 0)),
                  pl.BlockSpec((HALO_ROWS, CONV_CH), lambda i: (jnp.minimum((i + 1) * hb, n_halo - 1), 0)),
                  row(2 * D_MODEL), _const_spec((N_HEADS * D_V, D_MODEL)), _const_spec((CONV_WIDTH, CONV_CH)),
                  _const_spec((1, CONV_CH)), _const_spec((1, CONV_CH)), _const_spec((1, CONV_CH)),
                  _const_spec((CONV_CH, D_MODEL)), _const_spec((1, D_MODEL)), _const_spec((D_MODEL, D_MODEL)),
                  _const_spec((1, D_MODEL)), _const_spec((D_MODEL, 2 * LANES)), _const_spec((1, LANES)),
                  _const_spec((tm, tm)), _const_spec((LANES, LANES))],
        out_specs=(row(D_MODEL), row(LANES), pl.BlockSpec((1, 8, LANES), lambda i: (i, 0, 0))),
        out_shape=(jax.ShapeDtypeStruct((t, D_MODEL), F32), jax.ShapeDtypeStruct((t, LANES), F32),
                   jax.ShapeDtypeStruct((nt, 8, LANES), I32)),
        scratch_shapes=[pltpu.VMEM((tm + 2 * HALO_ROWS, CONV_CH), F32), pltpu.VMEM((tm, CONV_CH), F32),
                        pltpu.VMEM((8, min(tm, CONV_ROWS) + 8, CONV_CH), F32), pltpu.VMEM((1, LANES), F32)],
        compiler_params=pltpu.CompilerParams(dimension_semantics=("arbitrary",), vmem_limit_bytes=VMEM_LIMIT),
        name="post_mix",
    )(xa, xb, ota, otb, u, u, u, gates, w_oa, conv_w, conv_b, ln_g, ln_b, w_pb, b_pb, w_out, g_ffn, w_r, b_r, ltri, utri)


def _dispatch_kernel(cnt_s, locb_s, rowb_s, locs_s, rows_s, start_s, end_s, x1_ref, route_ref, gffn_ref, xs_hbm,
                     xloc_ref, zero_ref, row_sems, *, tm, nt, bm, n_rows):
    tables = (cnt_s, locb_s, rowb_s, locs_s, rows_s)
    i = pl.program_id(0)
    slot = lax.rem(i, N_DISPATCH_BUFS)
    n_loc = _local_rows(tm)

    h2 = _rms(x1_ref[...], gffn_ref[...]).astype(BF16)
    pos_t = jnp.transpose(route_ref[...])[2 * TOP_K:3 * TOP_K, :].astype(I32)
    prow = lax.broadcasted_iota(I32, (n_loc, tm), 0)
    perm = jnp.zeros((n_loc, tm), F32)
    for kk in range(TOP_K):
        perm = jnp.where(prow == pos_t[kk:kk + 1, :], 1.0, perm)
    xloc_ref[slot] = _pack_rows(_dot(perm.astype(BF16), h2))

    def wait_tile(tile):
        _wait_tile(xloc_ref.at[0], xs_hbm, row_sems.at[lax.rem(tile, N_DISPATCH_BUFS)], cnt_s, tile)

    @pl.when(i >= N_DISPATCH_BUFS - 1)
    def _():
        wait_tile(i - (N_DISPATCH_BUFS - 1))

    _for_each_piece(tables, i, n_loc,
                    lambda s, d, rows: _rows_copy(xloc_ref.at[slot], s, xs_hbm, d, row_sems.at[slot], rows=rows).start())

    @pl.when(i == nt - 1)
    def _():
        for back in range(N_DISPATCH_BUFS - 2, -1, -1):
            @pl.when(i - back >= 0)
            def _():
                wait_tile(i - back)
        row_sem = row_sems.at[0]
        zero_ref[...] = jnp.zeros(zero_ref.shape, U32)

        def tail(e, c):
            end = end_s[e]
            n = lax.shift_right_logical(start_s[e + 1] - end, int(math.log2(RUN_PAD)))

            def piece(q, c2):
                _rows_copy(zero_ref, 0, xs_hbm, end + q * RUN_PAD, row_sem).start()
                return c2

            lax.fori_loop(0, n, piece, 0)
            return c + n

        _wait_rows(zero_ref, xs_hbm, row_sem, lax.fori_loop(0, N_EXPERTS, tail, 0), RUN_PAD)
        used = start_s[N_EXPERTS]
        n_free = (n_rows - used) // bm

        def fill(b, c):
            _rows_copy(zero_ref, 0, xs_hbm, used + b * bm, row_sem, rows=bm).start()
            return c

        lax.fori_loop(0, n_free, fill, 0)
        _wait_rows(zero_ref, xs_hbm, row_sem, n_free, bm)


def _dispatch(tables, start, end, x1, route, g_ffn, *, tm, bm, n_rows):
    t = x1.shape[0]
    nt = t // tm
    return pl.pallas_call(
        functools.partial(_dispatch_kernel, tm=tm, nt=nt, bm=bm, n_rows=n_rows),
        grid_spec=pltpu.PrefetchScalarGridSpec(
            num_scalar_prefetch=7,
            grid=(nt,),
            in_specs=[pl.BlockSpec((tm, D_MODEL), lambda i, *_: (i, 0)), pl.BlockSpec((tm, LANES), lambda i, *_: (i, 0)),
                      pl.BlockSpec((1, D_MODEL), lambda i, *_: (0, 0))],
            out_specs=pl.BlockSpec(memory_space=pl.ANY),
            scratch_shapes=[pltpu.VMEM((N_DISPATCH_BUFS, _local_rows(tm), PACKED), U32),
                            pltpu.VMEM((bm, PACKED), U32), pltpu.SemaphoreType.DMA((N_DISPATCH_BUFS,))],
        ),
        out_shape=jax.ShapeDtypeStruct((n_rows, PACKED), U32),
        compiler_params=pltpu.CompilerParams(dimension_semantics=("arbitrary",), vmem_limit_bytes=VMEM_LIMIT,
                                             has_side_effects=True),
        name="dispatch",
    )(*tables, start, end, x1, route, g_ffn)


def _moe_kernel(exp_ref, n_ref, xs_ref, wgu_ref, bgu_ref, wd_ref, bd_ref, y_ref, wgu_bf, wd_bf):
    w = pl.program_id(0)
    used = w < n_ref[0]

    @pl.when(jnp.logical_or(w == 0, exp_ref[w] != exp_ref[jnp.maximum(w - 1, 0)]))
    def _():
        wgu_bf[...] = wgu_ref[0].astype(BF16)
        wd_bf[...] = wd_ref[0].astype(BF16)

    @pl.when(used)
    def _():
        x = _unpack_rows(xs_ref[...])
        gu = _dot(x, wgu_bf[...]) + bgu_ref[0]
        g = jnp.minimum(gu[:, :D_FF], SWIGLU_LIMIT)
        u = jnp.clip(gu[:, D_FF:], -SWIGLU_LIMIT, SWIGLU_LIMIT)
        act = (u + 1.0) * (g * jax.nn.sigmoid(SWIGLU_ALPHA * g))
        y_ref[...] = _pack_rows(_dot(act.astype(BF16), wd_bf[...]) + bd_ref[0])

    @pl.when(jnp.logical_not(used))
    def _():
        y_ref[...] = jnp.zeros(y_ref.shape, U32)


def _moe(ex, n_used, xs, w_gu, b_gu, w_down, b_down, *, bm):
    return pl.pallas_call(
        _moe_kernel,
        grid_spec=pltpu.PrefetchScalarGridSpec(
            num_scalar_prefetch=2,
            grid=(xs.shape[0] // bm,),
            in_specs=[pl.BlockSpec((bm, PACKED), lambda w, ex, n: (w, 0)),
                      pl.BlockSpec((1, D_MODEL, 2 * D_FF), lambda w, ex, n: (ex[w], 0, 0)),
                      pl.BlockSpec((1, 1, 2 * D_FF), lambda w, ex, n: (ex[w], 0, 0)),
                      pl.BlockSpec((1, D_FF, D_MODEL), lambda w, ex, n: (ex[w], 0, 0)),
                      pl.BlockSpec((1, 1, D_MODEL), lambda w, ex, n: (ex[w], 0, 0))],
            out_specs=pl.BlockSpec((bm, PACKED), lambda w, ex, n: (w, 0)),
            scratch_shapes=[pltpu.VMEM((D_MODEL, 2 * D_FF), BF16), pltpu.VMEM((D_FF, D_MODEL), BF16)],
        ),
        out_shape=jax.ShapeDtypeStruct(xs.shape, U32),
        compiler_params=pltpu.CompilerParams(dimension_semantics=("arbitrary",), vmem_limit_bytes=VMEM_LIMIT),
        name="moe",
    )(ex, n_used, xs, w_gu, b_gu, w_down, b_down)


def _expert_layout(totals, *, bm, n_blocks):
    n_blk = (totals + bm - 1) // bm
    blk_end = jnp.cumsum(n_blk)
    start = jnp.concatenate([jnp.zeros((1,), I32), blk_end * bm]).astype(I32)
    w = jnp.arange(n_blocks, dtype=I32)
    ex = jnp.minimum(jnp.sum((blk_end[None, :] <= w[:, None]).astype(I32), axis=1), N_EXPERTS - 1)
    return start, ex.astype(I32), blk_end[-1].astype(I32)[None]


def _combine_kernel(cnt_s, locb_s, rowb_s, locs_s, rows_s, x1_ref, route_ref, gfin_ref, y_hbm, outa_ref, outb_ref,
                    yloc_ref, sems, *, tm, nt, n_first):
    tables = (cnt_s, locb_s, rowb_s, locs_s, rows_s)
    i = pl.program_id(0)
    slot = lax.rem(i, 2)
    n_loc = _local_rows(tm)

    def gather(tile, sl):
        _for_each_piece(tables, tile, n_loc,
                        lambda loc, row, rows: _rows_copy(y_hbm, row, yloc_ref.at[sl], loc, sems.at[sl],
                                                          rows=rows).start())

    @pl.when(i == 0)
    def _():
        yloc_ref[...] = jnp.zeros(yloc_ref.shape, U32)
        gather(0, 0)

    @pl.when(i + 1 < nt)
    def _():
        gather(i + 1, 1 - slot)

    _wait_tile(y_hbm, yloc_ref.at[slot], sems.at[slot], cnt_s, i)

    route = route_ref[...]
    lanes = lax.broadcasted_iota(I32, (tm, n_loc), 1)
    pw = jnp.zeros((tm, n_loc), F32)
    for kk in range(TOP_K):
        pos = route[:, 2 * TOP_K + kk:2 * TOP_K + kk + 1].astype(I32)
        pw = jnp.where(lanes == pos, route[:, TOP_K + kk:TOP_K + kk + 1], pw)
    moe = _dot(pw.astype(BF16), _unpack_rows(yloc_ref[slot]))
    res = _rms(x1_ref[...] + moe, gfin_ref[...])

    @pl.when(i < n_first)
    def _():
        outa_ref[...] = res

    @pl.when(i >= n_first)
    def _():
        outb_ref[...] = res


def _combine(tables, x1, route, g_final, y, *, tm, n_first):
    t = x1.shape[0]
    nt = t // tm
    n_second = nt - n_first
    return pl.pallas_call(
        functools.partial(_combine_kernel, tm=tm, nt=nt, n_first=n_first),
        grid_spec=pltpu.PrefetchScalarGridSpec(
            num_scalar_prefetch=5,
            grid=(nt,),
            in_specs=[pl.BlockSpec((tm, D_MODEL), lambda i, *_: (i, 0)), pl.BlockSpec((tm, LANES), lambda i, *_: (i, 0)),
                      pl.BlockSpec((1, D_MODEL), lambda i, *_: (0, 0)), pl.BlockSpec(memory_space=pl.ANY)],
            out_specs=(pl.BlockSpec((tm, D_MODEL), lambda i, *_: (jnp.minimum(i, n_first - 1), 0)),
                       pl.BlockSpec((tm, D_MODEL), lambda i, *_: (jnp.clip(i - n_first, 0, n_second - 1), 0))),
            scratch_shapes=[pltpu.VMEM((2, _local_rows(tm), PACKED), U32), pltpu.SemaphoreType.DMA((2,))],
        ),
        out_shape=(jax.ShapeDtypeStruct((n_first * tm, D_MODEL), F32),
                   jax.ShapeDtypeStruct((n_second * tm, D_MODEL), F32)),
        compiler_params=pltpu.CompilerParams(dimension_semantics=("arbitrary",), vmem_limit_bytes=VMEM_LIMIT),
        name="combine",
    )(*tables, x1, route, g_final, y)


def _rot_cols(w):
    half = w.shape[-1] // 2
    return jnp.concatenate([-w[..., half:], w[..., :half]], axis=-1)


def _prep_weights(w_in, w_uq, w_ukv):
    d = w_in.shape[0]
    w_kpe = w_in[:, C_KPE:C_KPE + D_ROPE]
    w_in_ext = jnp.concatenate(
        [w_in[:, :C_KPE], w_kpe, _rot_cols(w_kpe), jnp.zeros((d, LANES - 2 * D_ROPE), F32),
         w_in[:, C_KPE + D_ROPE:]], axis=1).astype(BF16)

    wq = w_uq.reshape(Q_RANK, N_HEADS, D_QK)
    nope, pe = wq[..., :D_NOPE], wq[..., D_NOPE:]
    zpad = jnp.zeros((Q_RANK, N_HEADS, LANES - D_QK), F32)
    plain = jnp.concatenate([nope, pe, zpad], axis=-1).reshape(Q_RANK, N_HEADS * LANES)
    rot = _rot_cols(pe).reshape(Q_RANK, N_HEADS * D_ROPE)
    wq_t = jnp.concatenate([plain, rot], axis=1).T.astype(BF16)

    wkv = w_ukv.reshape(KV_RANK, N_HEADS, D_NOPE + D_V)
    wk_ext = jnp.concatenate([wkv[..., :D_NOPE], jnp.zeros((KV_RANK, N_HEADS, LANES - D_NOPE), F32)],
                             axis=-1).reshape(KV_RANK, N_HEADS * LANES).astype(BF16)
    wv_t = wkv[..., D_NOPE:].reshape(KV_RANK, N_HEADS * D_V).T.astype(BF16)
    return w_in_ext, wq_t, wk_ext, wv_t


def _kpe_placement():
    e2 = np.zeros((LANES, N_HEADS * LANES), np.float32)
    for j in range(D_ROPE):
        for h in range(N_HEADS):
            e2[j, h * LANES + D_NOPE + j] = 1.0
            e2[D_ROPE + j, h * LANES + D_NOPE + j] = 1.0
    return jnp.asarray(e2, BF16)


def _inv_freq(tm):
    inv = 1.0 / (ROPE_THETA ** (jnp.arange(0, D_ROPE, 2, dtype=F32) / D_ROPE))
    return jnp.broadcast_to(inv[:, None], (D_ROPE // 2, tm))


def _pick(n, pref):
    return pref if n % pref == 0 else n


def kernel(x_prompt, x_sample, g_mix, w_in, b_gate, g_q, w_uq, g_kv, w_ukv, w_oa, conv_w, conv_b, ln_g, ln_b, w_pb,
           b_pb, w_out, g_ffn, w_router, b_router, w_gu, b_gu, w_down, b_down, g_final):
    b1, s1, d = x_prompt.shape
    b2, s2, _ = x_sample.shape
    t1, t2 = b1 * s1, b2 * s2
    t = t1 + t2
    xa, xb = x_prompt.reshape(t1, d), x_sample.reshape(t2, d)
    seq_bounds = tuple((b * s1, (b + 1) * s1) for b in range(b1)) + tuple(
        (t1 + b * s2, t1 + (b + 1) * s2) for b in range(b2))

    tm = _pick(math.gcd(s1, s2), 512)
    nt = t // tm
    w_in_ext, wq_t, wk_ext, wv_t = _prep_weights(w_in[0], w_uq[0], w_ukv[0])

    qt, k, vt3, u, gates = _in_proj(xa, xb, _inv_freq(tm), g_mix, w_in_ext, g_q, wq_t, g_kv, wk_ext, wv_t,
                                    _kpe_placement(), b_gate, tm=tm, seq_starts=tuple(s for s, _ in seq_bounds))

    ota = _attention(qt, k, vt3, seq_start=0, seq_len=s1, n_seq=b1, tq=tm, tk=tm)
    otb = _attention(qt, k, vt3, seq_start=t1, seq_len=s2, n_seq=b2, tq=tm, tk=tm)

    w_r = jnp.concatenate([w_router[0], jnp.zeros((d, LANES - N_EXPERTS), F32)], axis=1)
    w_r_hi = w_r.astype(BF16)
    w_r = jnp.concatenate([w_r_hi, (w_r - w_r_hi.astype(F32)).astype(BF16)], axis=1)
    b_r = jnp.concatenate([b_router[0], jnp.full((LANES - N_EXPERTS,), NEG_BIG, F32)])[None, :]
    ltri = jnp.asarray(np.tril(np.ones((tm, tm), np.float32), -1), BF16)
    utri = jnp.asarray(np.triu(np.ones((LANES, LANES), np.float32), 1), BF16)
    x1, route, meta = _post_mix(xa, xb, ota, otb, u, gates, w_oa[0].astype(BF16), conv_w[0], conv_b, ln_g, ln_b,
                                w_pb[0].astype(BF16), b_pb, w_out[0].astype(BF16), g_ffn, w_r, b_r, ltri, utri,
                                tm=tm, seq_bounds=seq_bounds)

    bm = MOE_ROWS
    n_rows = _sorted_rows(t, nt, bm)
    carry, n_pieces, off = (meta[:, r, :N_EXPERTS] for r in range(3))
    start, ex, n_used = _expert_layout(carry[-1] + n_pieces[-1] * RUN_PAD, bm=bm, n_blocks=n_rows // bm)
    run_row = carry + start[None, :N_EXPERTS]
    tables = _piece_tables(n_pieces, off, run_row, n_loc=_local_rows(tm))
    run_end = run_row[-1] + n_pieces[-1] * RUN_PAD

    xs = _dispatch(tables, start, run_end, x1, route, g_ffn, tm=tm, bm=bm, n_rows=n_rows)
    y = _moe(ex, n_used, xs, w_gu[0], b_gu[0][:, None, :], w_down[0], b_down[0][:, None, :], bm=bm)
    out_a, out_b = _combine(tables, x1, route, g_final[None, :], y, tm=tm, n_first=t1 // tm)
    return out_a.reshape(b1, s1, d), out_b.reshape(b2, s2, d)
```

```python
import functools
import math

import numpy as np
import jax
import jax.numpy as jnp
from jax import lax
from jax.experimental import pallas as pl
from jax.experimental.pallas import tpu as pltpu

F32 = jnp.float32
BF16 = jnp.bfloat16
I32 = jnp.int32
U32 = jnp.uint32

D_MODEL = 1024
N_HEADS = 8
Q_RANK = 384
KV_RANK = 256
D_NOPE = 64
D_ROPE = 32
D_V = 64
D_QK = D_NOPE + D_ROPE
ROPE_THETA = 10000.0
CONV_CH = 512
CONV_WIDTH = 31
CONV_PAD = (CONV_WIDTH - 1) // 2
N_EXPERTS = 32
TOP_K = 4
D_FF = 1024
SWIGLU_ALPHA = 1.702
SWIGLU_LIMIT = 7.0
RMS_EPS = 1e-6
LN_EPS = 1e-5

LANES = 128
HALO_ROWS = 16
CONV_ROWS = 512
NEG_BIG = -1e30
RUN_PAD = 8
BIG_PIECE = 32
N_DISPATCH_BUFS = 3
MOE_ROWS = 512

C_CQ = 0
C_CKV = Q_RANK
C_KPE = Q_RANK + KV_RANK
C_CONV = C_KPE + LANES
C_GATE = C_CONV + 2 * CONV_CH
D_IN_EXT = C_GATE + 2 * D_MODEL

VMEM_LIMIT = 56 * 1024 * 1024


def _rms(x, g):
    return x * lax.rsqrt(jnp.mean(x * x, axis=-1, keepdims=True) + RMS_EPS) * g


def _dot(a, b):
    return jnp.dot(a, b, preferred_element_type=F32)


def _dot_nt(a, b):
    return lax.dot_general(a, b, (((1,), (1,)), ((), ())), preferred_element_type=F32)


def _dot_tn(a, b):
    return lax.dot_general(a, b, (((0,), (0,)), ((), ())), preferred_element_type=F32)


def _const_spec(shape):
    return pl.BlockSpec(shape, lambda *_: (0,) * len(shape))


PACKED = D_MODEL // 2
HIGH_HALF = 0xFFFF0000


def _pack_rows(v):
    bits = lambda x: pltpu.bitcast(x.astype(BF16).astype(F32), U32)
    return bits(v[:, :PACKED]) | lax.shift_right_logical(bits(v[:, PACKED:]), jnp.uint32(16))


def _unpack_rows(w):
    hi = pltpu.bitcast(w & jnp.uint32(HIGH_HALF), F32).astype(BF16)
    lo = pltpu.bitcast(lax.shift_left(w, jnp.uint32(16)), F32).astype(BF16)
    return jnp.concatenate([hi, lo], axis=1)


def _tile_select(i, n_first, a_ref, b_ref):
    return jnp.where(i < n_first, a_ref[...], b_ref[...])


def _in_proj_kernel(xa_ref, xb_ref, inv_ref, gmix_ref, win_ref, gq_ref, wqt_ref, gkv_ref, wk_ref, wvt_ref,
                    e2_ref, bgate_ref, qt_ref, k_ref, vt_ref, u_ref, gates_ref, *, tm, n_first, seq_starts):
    i = pl.program_id(0)
    h = _rms(_tile_select(i, n_first, xa_ref, xb_ref), gmix_ref[...]).astype(BF16)
    proj = _dot(h, win_ref[...])

    t0 = i * tm
    s0 = 0
    for s in seq_starts:
        s0 = jnp.where(t0 >= s, s, s0)
    half = D_ROPE // 2
    pos = (t0 - s0 + lax.broadcasted_iota(I32, (half, tm), 1)).astype(F32)
    ang = pos * inv_ref[...]
    cos, sin = jnp.cos(ang), jnp.sin(ang)
    scale = (D_QK ** -0.5) * math.log2(math.e)
    cos_q, sin_q = cos * scale, sin * scale

    cqn = _rms(proj[:, C_CQ:C_CQ + Q_RANK], gq_ref[...]).astype(BF16)
    q2 = _dot_nt(wqt_ref[...], cqn)
    hw = N_HEADS * LANES
    for hh in range(N_HEADS):
        qa = q2[hh * LANES:(hh + 1) * LANES, :]
        qb = q2[hw + hh * D_ROPE:hw + (hh + 1) * D_ROPE, :]
        r0 = hh * LANES
        qt_ref[0, r0:r0 + D_NOPE, :] = (qa[:D_NOPE] * scale).astype(BF16)
        for part in range(2):
            lo = D_NOPE + part * half
            qt_ref[0, r0 + lo:r0 + lo + half, :] = (
                qa[lo:lo + half] * cos_q + qb[part * half:(part + 1) * half] * sin_q).astype(BF16)
        qt_ref[0, r0 + D_QK:r0 + LANES, :] = jnp.zeros((LANES - D_QK, tm), BF16)

    ckvn = _rms(proj[:, C_CKV:C_CKV + KV_RANK], gkv_ref[...]).astype(BF16)
    tabk = jnp.transpose(jnp.concatenate([cos, cos, sin, sin, jnp.zeros((LANES - 2 * D_ROPE, tm), F32)], axis=0))
    kpe = (proj[:, C_KPE:C_KPE + LANES] * tabk).astype(BF16)
    k_ref[...] = (_dot(ckvn, wk_ref[...]) + _dot(kpe, e2_ref[...])).astype(BF16)
    vt_ref[0] = _dot_nt(wvt_ref[...], ckvn).astype(BF16)

    a = proj[:, C_CONV:C_CONV + CONV_CH]
    g = proj[:, C_CONV + CONV_CH:C_CONV + 2 * CONV_CH]
    u_ref[...] = a * jax.nn.sigmoid(g)
    gates_ref[...] = jax.nn.sigmoid(proj[:, C_GATE:] + bgate_ref[...]).astype(BF16)


def _split_specs(block, n_first, n_second):
    first = lambda i: jnp.minimum(i, n_first - 1)
    second = lambda i: jnp.clip(i - n_first, 0, n_second - 1)
    rest = (0,) * (len(block) - 1)
    return [pl.BlockSpec(block, lambda i: (first(i),) + rest), pl.BlockSpec(block, lambda i: (second(i),) + rest)]


def _in_proj(xa, xb, inv_b, g_mix, w_in_ext, g_q, wq_t, g_kv, wk_ext, wv_t, e2, b_gate, *, tm, seq_starts):
    na, nb = xa.shape[0] // tm, xb.shape[0] // tm
    nt = na + nb
    t = nt * tm
    row = lambda w: pl.BlockSpec((tm, w), lambda i: (i, 0))
    col = lambda h: pl.BlockSpec((h, tm), lambda i: (0, i))
    return pl.pallas_call(
        functools.partial(_in_proj_kernel, tm=tm, n_first=na, seq_starts=seq_starts),
        grid=(nt,),
        in_specs=_split_specs((tm, D_MODEL), na, nb) + [
                  _const_spec((D_ROPE // 2, tm)), _const_spec((1, D_MODEL)),
                  _const_spec((D_MODEL, D_IN_EXT)), _const_spec((1, Q_RANK)),
                  _const_spec((N_HEADS * (LANES + D_ROPE), Q_RANK)), _const_spec((1, KV_RANK)),
                  _const_spec((KV_RANK, N_HEADS * LANES)), _const_spec((N_HEADS * D_V, KV_RANK)),
                  _const_spec((LANES, N_HEADS * LANES)), _const_spec((1, 2 * D_MODEL))],
        out_specs=(pl.BlockSpec((1, N_HEADS * LANES, tm), lambda i: (i, 0, 0)), row(N_HEADS * LANES),
                   pl.BlockSpec((1, N_HEADS * D_V, tm), lambda i: (i, 0, 0)), row(CONV_CH), row(2 * D_MODEL)),
        out_shape=(jax.ShapeDtypeStruct((nt, N_HEADS * LANES, tm), BF16),
                   jax.ShapeDtypeStruct((t, N_HEADS * LANES), BF16),
                   jax.ShapeDtypeStruct((nt, N_HEADS * D_V, tm), BF16),
                   jax.ShapeDtypeStruct((t, CONV_CH), F32),
                   jax.ShapeDtypeStruct((t, 2 * D_MODEL), BF16)),
        compiler_params=pltpu.CompilerParams(dimension_semantics=("arbitrary",), vmem_limit_bytes=VMEM_LIMIT),
        name="in_proj",
    )(xa, xb, inv_b, g_mix, w_in_ext, g_q, wq_t, g_kv, wk_ext, wv_t, e2, b_gate)


def _attn_kernel(qt_ref, k_ref, vt_ref, o_ref, s_ref, acc_ref, m_ref, *, nq, n_chunks, tk, unroll):
    half = n_chunks // 2
    tq = qt_ref.shape[2]
    ones = jnp.ones((16, tk), BF16)

    def scores(a, j, slot):
        ks = pl.multiple_of(j * tk, tk)
        s_ref[slot, :, :tq] = _dot(k_ref[pl.ds(ks, tk), :], qt_ref[a])

    def softmax_pv(j, slot):
        st = s_ref[slot, :, :tq]
        m = jnp.where(j == 0, NEG_BIG, m_ref[...])
        m_new = jnp.maximum(m, jnp.max(st, axis=0, keepdims=True))
        p = jnp.exp2(st - m_new).astype(BF16)
        v1 = jnp.concatenate([vt_ref[j], ones], axis=0)
        acc = jnp.exp2(m - m_new) * acc_ref[...] + _dot(v1, p)
        acc_ref[...] = acc
        m_ref[...] = m_new
        return acc

    m_ref[...] = jnp.full(m_ref.shape, NEG_BIG, F32)
    acc_ref[...] = jnp.zeros(acc_ref.shape, F32)
    scores(0, 0, 0)

    def pair(pp, may_end_tile):
        a = lax.div(pp, half)
        j = 2 * lax.rem(pp, half)
        scores(a, j + 1, 1)
        softmax_pv(j, 0)
        nxt = pp + 1
        scores(jnp.minimum(lax.div(nxt, half), nq - 1), 2 * lax.rem(nxt, half), 0)
        acc = softmax_pv(j + 1, 1)
        if may_end_tile:
            o_ref[a] = (acc[:D_V] / acc[D_V:D_V + 1]).astype(o_ref.dtype)

    period = math.gcd(half, unroll)

    def trip(blk, c):
        for u in range(unroll):
            pair(blk * unroll + u, (u + 1) % period == 0)
        return c

    lax.fori_loop(0, nq * half // unroll, trip, 0)


def _attention(qt3, k, vt3, *, seq_start, seq_len, n_seq, tq, tk):
    nq = seq_len // tq
    nkc = seq_len // tk
    assert nkc % 2 == 0, "key chunks are processed in pairs"
    s0 = seq_start // seq_len
    n_pairs = nq * (nkc // 2)
    unroll = next(u for u in (8, 4, 2, 1) if n_pairs % u == 0)
    return pl.pallas_call(
        functools.partial(_attn_kernel, nq=nq, n_chunks=nkc, tk=tk, unroll=unroll),
        grid=(n_seq, N_HEADS),
        in_specs=[pl.BlockSpec((nq, LANES, tq), lambda b, h: (s0 + b, h, 0)),
                  pl.BlockSpec((seq_len, LANES), lambda b, h: (s0 + b, h)),
                  pl.BlockSpec((nkc, D_V, tk), lambda b, h: (s0 + b, h, 0))],
        out_specs=pl.BlockSpec((nq, D_V, tq), lambda b, h: (b, h, 0)),
        out_shape=jax.ShapeDtypeStruct((n_seq * nq, N_HEADS * D_V, tq), BF16),
        scratch_shapes=[pltpu.VMEM((2, tk, tq + LANES), F32), pltpu.VMEM((D_V + 16, tq), F32),
                        pltpu.VMEM((1, tq), F32)],
        compiler_params=pltpu.CompilerParams(dimension_semantics=("arbitrary", "arbitrary"),
                                             vmem_limit_bytes=VMEM_LIMIT),
        name="attention",
    )(qt3, k, vt3)


def _local_rows(tm):
    return TOP_K * tm + N_EXPERTS * RUN_PAD


def _sorted_rows(t, nt, bm):
    runs = t * TOP_K + nt * N_EXPERTS * (RUN_PAD - 1)
    return -(-runs // bm) * bm + N_EXPERTS * bm


def _rows_copy(src, s, dst, d, sem, rows=RUN_PAD):
    s = pl.multiple_of(s, RUN_PAD)
    d = pl.multiple_of(d, RUN_PAD)
    return pltpu.make_async_copy(src.at[pl.ds(s, rows), :], dst.at[pl.ds(d, rows), :], sem)


def _piece_caps(n_loc):
    return n_loc // BIG_PIECE, N_EXPERTS * (BIG_PIECE // RUN_PAD - 1)


def _piece_tables(n_pieces, off, run_row, *, n_loc):
    per_big = BIG_PIECE // RUN_PAD
    qb, qs = _piece_caps(n_loc)
    n_big = n_pieces // per_big
    n_sm = n_pieces - n_big * per_big

    def table(counts, loc0, row0, step, cap):
        end = jnp.cumsum(counts, axis=1)
        q = jnp.arange(cap, dtype=I32)
        ex = jnp.minimum(jnp.sum((end[:, None, :] <= q[None, :, None]).astype(I32), axis=2), N_EXPERTS - 1)
        onehot = ex[..., None] == jnp.arange(N_EXPERTS, dtype=I32)
        pick = lambda v: jnp.sum(jnp.where(onehot, v[:, None, :], 0), axis=2)
        idx = q[None, :] - pick(end - counts)
        return (pick(loc0) + idx * step).reshape(-1), (pick(row0) + idx * step).reshape(-1), end[:, -1]

    loc_b, row_b, cnt_b = table(n_big, off, run_row, BIG_PIECE, qb)
    done = n_big * BIG_PIECE
    loc_s, row_s, cnt_s = table(n_sm, off + done, run_row + done, RUN_PAD, qs)
    return jnp.stack([cnt_b, cnt_s], axis=1).reshape(-1), loc_b, row_b, loc_s, row_s


def _for_each_piece(tables, tile, n_loc, fn):
    cnt, loc_b, row_b, loc_s, row_s = tables
    qb, qs = _piece_caps(n_loc)

    def big(q, c):
        fn(loc_b[tile * qb + q], row_b[tile * qb + q], BIG_PIECE)
        return c

    def small(q, c):
        fn(loc_s[tile * qs + q], row_s[tile * qs + q], RUN_PAD)
        return c

    lax.fori_loop(0, cnt[2 * tile], big, 0)
    lax.fori_loop(0, cnt[2 * tile + 1], small, 0)


def _wait_rows(src, dst, sem, count, rows):
    def one(q, c):
        _rows_copy(src, 0, dst, 0, sem, rows=rows).wait()
        return c

    lax.fori_loop(0, count, one, 0)


def _wait_tile(src, dst, sem, cnt, tile):
    _wait_rows(src, dst, sem, cnt[2 * tile], BIG_PIECE)
    _wait_rows(src, dst, sem, cnt[2 * tile + 1], RUN_PAD)


def _post_mix_kernel(xa_ref, xb_ref, ota_ref, otb_ref, u_ref, up_ref, un_ref, gates_ref, woa_ref, cw_ref, cb_ref,
                     lng_ref, lnb_ref, wpb_ref, bpb_ref, wout_ref, gffn_ref, wr_ref, br_ref, ltri_ref, utri_ref,
                     x1_ref, route_ref, meta_ref, ext_ref, cv_ref, ph_ref, carry_ref, *, tm, n_first, seq_bounds):
    i = pl.program_id(0)
    t0 = i * tm
    is_start = functools.reduce(jnp.logical_or, [t0 == s for s, _ in seq_bounds])
    is_end = functools.reduce(jnp.logical_or, [t0 + tm == e for _, e in seq_bounds])

    ext_ref[0:HALO_ROWS, :] = jnp.where(is_start, 0.0, up_ref[...])
    ext_ref[HALO_ROWS:HALO_ROWS + tm, :] = u_ref[...]
    ext_ref[HALO_ROWS + tm:2 * HALO_ROWS + tm, :] = jnp.where(is_end, 0.0, un_ref[...])
    rc = min(tm, CONV_ROWS)
    sub = 8
    base = HALO_ROWS - CONV_PAD
    for c in range(tm // rc):
        acc = None
        for b in range(sub):
            part = None
            for o in range(CONV_WIDTH):
                if (o + base) % sub != b:
                    continue
                a0 = c * rc + (o + base) - b
                term = cw_ref[o:o + 1, :] * ext_ref[a0:a0 + rc + sub, :]
                part = term if part is None else part + term
            if b == 0:
                shifted = part[:rc]
            else:
                ph_ref[b] = part
                shifted = ph_ref[b, b:b + rc, :]
            acc = shifted if acc is None else acc + shifted
        cv_ref[c * rc:(c + 1) * rc, :] = acc + cb_ref[...]

    cv = cv_ref[...]
    mu = jnp.mean(cv, axis=-1, keepdims=True)
    xc = cv - mu
    y = xc * lax.rsqrt(jnp.mean(xc * xc, axis=-1, keepdims=True) + LN_EPS) * lng_ref[...] + lnb_ref[...]
    y = y * jax.nn.sigmoid(y)
    branch_b = _dot(y.astype(BF16), wpb_ref[...]) + bpb_ref[...]
    branch_a = _dot_tn(_tile_select(i, n_first, ota_ref, otb_ref)[0], woa_ref[...])
    gates = gates_ref[...]
    mix = gates[:, :D_MODEL].astype(F32) * branch_a + gates[:, D_MODEL:].astype(F32) * branch_b
    x1 = _tile_select(i, n_first, xa_ref, xb_ref) + _dot(mix.astype(BF16), wout_ref[...])
    x1_ref[...] = x1

    h2 = _rms(x1, gffn_ref[...])
    h2_hi = h2.astype(BF16)
    h2_lo = (h2 - h2_hi.astype(F32)).astype(BF16)
    hi_both = _dot(h2_hi, wr_ref[...])
    logits = hi_both[:, :LANES] + (hi_both[:, LANES:] + _dot(h2_lo, wr_ref[:, :LANES])) + br_ref[...]
    lane = lax.broadcasted_iota(I32, logits.shape, 1)
    lane_f = lane.astype(F32)
    vals, idxs, sels = [], [], []
    cur = logits
    for _ in range(TOP_K):
        mval = jnp.max(cur, axis=-1, keepdims=True)
        ik = jnp.min(jnp.where(cur == mval, lane_f, float(LANES)), axis=-1, keepdims=True)
        sel = lane_f == ik
        cur = jnp.where(sel, -jnp.inf, cur)
        vals.append(mval)
        idxs.append(ik)
        sels.append(sel)
    exps = [jnp.exp(v - vals[0]) for v in vals]
    denom = exps[0] + exps[1] + exps[2] + exps[3]

    onehot = sels[0].astype(F32) + sels[1].astype(F32) + sels[2].astype(F32) + sels[3].astype(F32)

    @pl.when(i == 0)
    def _():
        carry_ref[...] = jnp.zeros_like(carry_ref)

    n_e = jnp.sum(onehot, axis=0, keepdims=True)
    n_pieces = jnp.floor((n_e + (RUN_PAD - 1)) * (1.0 / RUN_PAD))
    pieces_b = jnp.broadcast_to(n_pieces, (8, LANES)).astype(BF16)
    off = _dot(pieces_b, utri_ref[...])[0:1] * RUN_PAD
    total_pieces = jnp.sum(n_pieces, axis=-1, keepdims=True)
    carry = carry_ref[...]
    prefix = _dot(ltri_ref[...], onehot.astype(BF16))
    local = off + prefix
    route = jnp.zeros(logits.shape, F32)
    for kk in range(TOP_K):
        pos = jnp.sum(jnp.where(sels[kk], local, 0.0), axis=-1, keepdims=True)
        route = jnp.where(lane == kk, idxs[kk], route)
        route = jnp.where(lane == TOP_K + kk, exps[kk] / denom, route)
        route = jnp.where(lane == 2 * TOP_K + kk, pos, route)
    route_ref[...] = route
    carry_ref[...] = carry + n_pieces * RUN_PAD

    row8 = lax.broadcasted_iota(I32, (8, LANES), 0)
    metav = jnp.where(row8 == 0, jnp.broadcast_to(carry, (8, LANES)), 0.0)
    metav = jnp.where(row8 == 1, jnp.broadcast_to(n_pieces, (8, LANES)), metav)
    metav = jnp.where(row8 == 2, jnp.broadcast_to(off, (8, LANES)), metav)
    metav = jnp.where(row8 == 3, jnp.broadcast_to(total_pieces, (8, LANES)), metav)
    meta_ref[0] = metav.astype(I32)


def _post_mix(xa, xb, ota, otb, u, gates, w_oa, conv_w, conv_b, ln_g, ln_b, w_pb, b_pb, w_out, g_ffn, w_r, b_r, ltri,
              utri, *, tm, seq_bounds):
    na, nb = xa.shape[0] // tm, xb.shape[0] // tm
    nt = na + nb
    t = nt * tm
    hb = tm // HALO_ROWS
    n_halo = t // HALO_ROWS
    row = lambda w: pl.BlockSpec((tm, w), lambda i: (i, 0))
    return pl.pallas_call(
        functools.partial(_post_mix_kernel, tm=tm, n_first=na, seq_bounds=seq_bounds),
        grid=(nt,),
        in_specs=_split_specs((tm, D_MODEL), na, nb) + _split_specs((1, N_HEADS * D_V, tm), na, nb) + [
                  row(CONV_CH),
                  pl.BlockSpec((HALO_ROWS, CONV_CH), lambda i: (jnp.maximum(i * hb - 1, 0), 0)),
                  pl.BlockSpec((HALO_ROWS, CONV_CH), lambda i: (jnp.minimum((i + 1) * hb, n_halo - 1), 0)),
                  row(2 * D_MODEL), _const_spec((N_HEADS * D_V, D_MODEL)), _const_spec((CONV_WIDTH, CONV_CH)),
                  _const_spec((1, CONV_CH)), _const_spec((1, CONV_CH)), _const_spec((1, CONV_CH)),
                  _const_spec((CONV_CH, D_MODEL)), _const_spec((1, D_MODEL)), _const_spec((D_MODEL, D_MODEL)),
                  _const_spec((1, D_MODEL)), _const_spec((D_MODEL, 2 * LANES)), _const_spec((1, LANES)),
                  _const_spec((tm, tm)), _const_spec((LANES, LANES))],
        out_specs=(row(D_MODEL), row(LANES), pl.BlockSpec((1, 8, LANES), lambda i: (i, 0, 0))),
        out_shape=(jax.ShapeDtypeStruct((t, D_MODEL), F32), jax.ShapeDtypeStruct((t, LANES), F32),
                   jax.ShapeDtypeStruct((nt, 8, LANES), I32)),
        scratch_shapes=[pltpu.VMEM((tm + 2 * HALO_ROWS, CONV_CH), F32), pltpu.VMEM((tm, CONV_CH), F32),
                        pltpu.VMEM((8, min(tm, CONV_ROWS) + 8, CONV_CH), F32), pltpu.VMEM((1, LANES), F32)],
        compiler_params=pltpu.CompilerParams(dimension_semantics=("arbitrary",), vmem_limit_bytes=VMEM_LIMIT),
        name="post_mix",
    )(xa, xb, ota, otb, u, u, u, gates, w_oa, conv_w, conv_b, ln_g, ln_b, w_pb, b_pb, w_out, g_ffn, w_r, b_r, ltri, utri)


def _dispatch_kernel(cnt_s, locb_s, rowb_s, locs_s, rows_s, start_s, end_s, x1_ref, route_ref, gffn_ref, xs_hbm,
                     xloc_ref, zero_ref, row_sems, *, tm, nt, bm, n_rows):
    tables = (cnt_s, locb_s, rowb_s, locs_s, rows_s)
    i = pl.program_id(0)
    slot = lax.rem(i, N_DISPATCH_BUFS)
    n_loc = _local_rows(tm)

    h2 = _rms(x1_ref[...], gffn_ref[...]).astype(BF16)
    pos_t = jnp.transpose(route_ref[...])[2 * TOP_K:3 * TOP_K, :].astype(I32)
    prow = lax.broadcasted_iota(I32, (n_loc, tm), 0)
    perm = jnp.zeros((n_loc, tm), F32)
    for kk in range(TOP_K):
        perm = jnp.where(prow == pos_t[kk:kk + 1, :], 1.0, perm)
    xloc_ref[slot] = _pack_rows(_dot(perm.astype(BF16), h2))

    def wait_tile(tile):
        _wait_tile(xloc_ref.at[0], xs_hbm, row_sems.at[lax.rem(tile, N_DISPATCH_BUFS)], cnt_s, tile)

    @pl.when(i >= N_DISPATCH_BUFS - 1)
    def _():
        wait_tile(i - (N_DISPATCH_BUFS - 1))

    _for_each_piece(tables, i, n_loc,
                    lambda s, d, rows: _rows_copy(xloc_ref.at[slot], s, xs_hbm, d, row_sems.at[slot], rows=rows).start())

    @pl.when(i == nt - 1)
    def _():
        for back in range(N_DISPATCH_BUFS - 2, -1, -1):
            @pl.when(i - back >= 0)
            def _():
                wait_tile(i - back)
        row_sem = row_sems.at[0]
        zero_ref[...] = jnp.zeros(zero_ref.shape, U32)

        def tail(e, c):
            end = end_s[e]
            n = lax.shift_right_logical(start_s[e + 1] - end, int(math.log2(RUN_PAD)))

            def piece(q, c2):
                _rows_copy(zero_ref, 0, xs_hbm, end + q * RUN_PAD, row_sem).start()
                return c2

            lax.fori_loop(0, n, piece, 0)
            return c + n

        _wait_rows(zero_ref, xs_hbm, row_sem, lax.fori_loop(0, N_EXPERTS, tail, 0), RUN_PAD)
        used = start_s[N_EXPERTS]
        n_free = (n_rows - used) // bm

        def fill(b, c):
            _rows_copy(zero_ref, 0, xs_hbm, used + b * bm, row_sem, rows=bm).start()
            return c

        lax.fori_loop(0, n_free, fill, 0)
        _wait_rows(zero_ref, xs_hbm, row_sem, n_free, bm)


def _dispatch(tables, start, end, x1, route, g_ffn, *, tm, bm, n_rows):
    t = x1.shape[0]
    nt = t // tm
    return pl.pallas_call(
        functools.partial(_dispatch_kernel, tm=tm, nt=nt, bm=bm, n_rows=n_rows),
        grid_spec=pltpu.PrefetchScalarGridSpec(
            num_scalar_prefetch=7,
            grid=(nt,),
            in_specs=[pl.BlockSpec((tm, D_MODEL), lambda i, *_: (i, 0)), pl.BlockSpec((tm, LANES), lambda i, *_: (i, 0)),
                      pl.BlockSpec((1, D_MODEL), lambda i, *_: (0, 0))],
            out_specs=pl.BlockSpec(memory_space=pl.ANY),
            scratch_shapes=[pltpu.VMEM((N_DISPATCH_BUFS, _local_rows(tm), PACKED), U32),
                            pltpu.VMEM((bm, PACKED), U32), pltpu.SemaphoreType.DMA((N_DISPATCH_BUFS,))],
        ),
        out_shape=jax.ShapeDtypeStruct((n_rows, PACKED), U32),
        compiler_params=pltpu.CompilerParams(dimension_semantics=("arbitrary",), vmem_limit_bytes=VMEM_LIMIT,
                                             has_side_effects=True),
        name="dispatch",
    )(*tables, start, end, x1, route, g_ffn)


def _moe_kernel(exp_ref, n_ref, xs_ref, wgu_ref, bgu_ref, wd_ref, bd_ref, y_ref, wgu_bf, wd_bf):
    w = pl.program_id(0)
    used = w < n_ref[0]

    @pl.when(jnp.logical_or(w == 0, exp_ref[w] != exp_ref[jnp.maximum(w - 1, 0)]))
    def _():
        wgu_bf[...] = wgu_ref[0].astype(BF16)
        wd_bf[...] = wd_ref[0].astype(BF16)

    @pl.when(used)
    def _():
        x = _unpack_rows(xs_ref[...])
        gu = _dot(x, wgu_bf[...]) + bgu_ref[0]
        g = jnp.minimum(gu[:, :D_FF], SWIGLU_LIMIT)
        u = jnp.clip(gu[:, D_FF:], -SWIGLU_LIMIT, SWIGLU_LIMIT)
        act = (u + 1.0) * (g * jax.nn.sigmoid(SWIGLU_ALPHA * g))
        y_ref[...] = _pack_rows(_dot(act.astype(BF16), wd_bf[...]) + bd_ref[0])

    @pl.when(jnp.logical_not(used))
    def _():
        y_ref[...] = jnp.zeros(y_ref.shape, U32)


def _moe(ex, n_used, xs, w_gu, b_gu, w_down, b_down, *, bm):
    return pl.pallas_call(
        _moe_kernel,
        grid_spec=pltpu.PrefetchScalarGridSpec(
            num_scalar_prefetch=2,
            grid=(xs.shape[0] // bm,),
            in_specs=[pl.BlockSpec((bm, PACKED), lambda w, ex, n: (w, 0)),
                      pl.BlockSpec((1, D_MODEL, 2 * D_FF), lambda w, ex, n: (ex[w], 0, 0)),
                      pl.BlockSpec((1, 1, 2 * D_FF), lambda w, ex, n: (ex[w], 0, 0)),
                      pl.BlockSpec((1, D_FF, D_MODEL), lambda w, ex, n: (ex[w], 0, 0)),
                      pl.BlockSpec((1, 1, D_MODEL), lambda w, ex, n: (ex[w], 0, 0))],
            out_specs=pl.BlockSpec((bm, PACKED), lambda w, ex, n: (w, 0)),
            scratch_shapes=[pltpu.VMEM((D_MODEL, 2 * D_FF), BF16), pltpu.VMEM((D_FF, D_MODEL), BF16)],
        ),
        out_shape=jax.ShapeDtypeStruct(xs.shape, U32),
        compiler_params=pltpu.CompilerParams(dimension_semantics=("arbitrary",), vmem_limit_bytes=VMEM_LIMIT),
        name="moe",
    )(ex, n_used, xs, w_gu, b_gu, w_down, b_down)


def _expert_layout(totals, *, bm, n_blocks):
    n_blk = (totals + bm - 1) // bm
    blk_end = jnp.cumsum(n_blk)
    start = jnp.concatenate([jnp.zeros((1,), I32), blk_end * bm]).astype(I32)
    w = jnp.arange(n_blocks, dtype=I32)
    ex = jnp.minimum(jnp.sum((blk_end[None, :] <= w[:, None]).astype(I32), axis=1), N_EXPERTS - 1)
    return start, ex.astype(I32), blk_end[-1].astype(I32)[None]


def _combine_kernel(cnt_s, locb_s, rowb_s, locs_s, rows_s, x1_ref, route_ref, gfin_ref, y_hbm, outa_ref, outb_ref,
                    yloc_ref, sems, *, tm, nt, n_first):
    tables = (cnt_s, locb_s, rowb_s, locs_s, rows_s)
    i = pl.program_id(0)
    slot = lax.rem(i, 2)
    n_loc = _local_rows(tm)

    def gather(tile, sl):
        _for_each_piece(tables, tile, n_loc,
                        lambda loc, row, rows: _rows_copy(y_hbm, row, yloc_ref.at[sl], loc, sems.at[sl],
                                                          rows=rows).start())

    @pl.when(i == 0)
    def _():
        yloc_ref[...] = jnp.zeros(yloc_ref.shape, U32)
        gather(0, 0)

    @pl.when(i + 1 < nt)
    def _():
        gather(i + 1, 1 - slot)

    _wait_tile(y_hbm, yloc_ref.at[slot], sems.at[slot], cnt_s, i)

    route = route_ref[...]
    lanes = lax.broadcasted_iota(I32, (tm, n_loc), 1)
    pw = jnp.zeros((tm, n_loc), F32)
    for kk in range(TOP_K):
        pos = route[:, 2 * TOP_K + kk:2 * TOP_K + kk + 1].astype(I32)
        pw = jnp.where(lanes == pos, route[:, TOP_K + kk:TOP_K + kk + 1], pw)
    moe = _dot(pw.astype(BF16), _unpack_rows(yloc_ref[slot]))
    res = _rms(x1_ref[...] + moe, gfin_ref[...])

    @pl.when(i < n_first)
    def _():
        outa_ref[...] = res

    @pl.when(i >= n_first)
    def _():
        outb_ref[...] = res


def _combine(tables, x1, route, g_final, y, *, tm, n_first):
    t = x1.shape[0]
    nt = t // tm
    n_second = nt - n_first
    return pl.pallas_call(
        functools.partial(_combine_kernel, tm=tm, nt=nt, n_first=n_first),
        grid_spec=pltpu.PrefetchScalarGridSpec(
            num_scalar_prefetch=5,
            grid=(nt,),
            in_specs=[pl.BlockSpec((tm, D_MODEL), lambda i, *_: (i, 0)), pl.BlockSpec((tm, LANES), lambda i, *_: (i, 0)),
                      pl.BlockSpec((1, D_MODEL), lambda i, *_: (0, 0)), pl.BlockSpec(memory_space=pl.ANY)],
            out_specs=(pl.BlockSpec((tm, D_MODEL), lambda i, *_: (jnp.minimum(i, n_first - 1), 0)),
                       pl.BlockSpec((tm, D_MODEL), lambda i, *_: (jnp.clip(i - n_first, 0, n_second - 1), 0))),
            scratch_shapes=[pltpu.VMEM((2, _local_rows(tm), PACKED), U32), pltpu.SemaphoreType.DMA((2,))],
        ),
        out_shape=(jax.ShapeDtypeStruct((n_first * tm, D_MODEL), F32),
                   jax.ShapeDtypeStruct((n_second * tm, D_MODEL), F32)),
        compiler_params=pltpu.CompilerParams(dimension_semantics=("arbitrary",), vmem_limit_bytes=VMEM_LIMIT),
        name="combine",
    )(*tables, x1, route, g_final, y)


def _rot_cols(w):
    half = w.shape[-1] // 2
    return jnp.concatenate([-w[..., half:], w[..., :half]], axis=-1)


def _prep_weights(w_in, w_uq, w_ukv):
    d = w_in.shape[0]
    w_kpe = w_in[:, C_KPE:C_KPE + D_ROPE]
    w_in_ext = jnp.concatenate(
        [w_in[:, :C_KPE], w_kpe, _rot_cols(w_kpe), jnp.zeros((d, LANES - 2 * D_ROPE), F32),
         w_in[:, C_KPE + D_ROPE:]], axis=1).astype(BF16)

    wq = w_uq.reshape(Q_RANK, N_HEADS, D_QK)
    nope, pe = wq[..., :D_NOPE], wq[..., D_NOPE:]
    zpad = jnp.zeros((Q_RANK, N_HEADS, LANES - D_QK), F32)
    plain = jnp.concatenate([nope, pe, zpad], axis=-1).reshape(Q_RANK, N_HEADS * LANES)
    rot = _rot_cols(pe).reshape(Q_RANK, N_HEADS * D_ROPE)
    wq_t = jnp.concatenate([plain, rot], axis=1).T.astype(BF16)

    wkv = w_ukv.reshape(KV_RANK, N_HEADS, D_NOPE + D_V)
    wk_ext = jnp.concatenate([wkv[..., :D_NOPE], jnp.zeros((KV_RANK, N_HEADS, LANES - D_NOPE), F32)],
                             axis=-1).reshape(KV_RANK, N_HEADS * LANES).astype(BF16)
    wv_t = wkv[..., D_NOPE:].reshape(KV_RANK, N_HEADS * D_V).T.astype(BF16)
    return w_in_ext, wq_t, wk_ext, wv_t


def _kpe_placement():
    e2 = np.zeros((LANES, N_HEADS * LANES), np.float32)
    for j in range(D_ROPE):
        for h in range(N_HEADS):
            e2[j, h * LANES + D_NOPE + j] = 1.0
            e2[D_ROPE + j, h * LANES + D_NOPE + j] = 1.0
    return jnp.asarray(e2, BF16)


def _inv_freq(tm):
    inv = 1.0 / (ROPE_THETA ** (jnp.arange(0, D_ROPE, 2, dtype=F32) / D_ROPE))
    return jnp.broadcast_to(inv[:, None], (D_ROPE // 2, tm))


def _pick(n, pref):
    return pref if n % pref == 0 else n


def kernel(x_prompt, x_sample, g_mix, w_in, b_gate, g_q, w_uq, g_kv, w_ukv, w_oa, conv_w, conv_b, ln_g, ln_b, w_pb,
           b_pb, w_out, g_ffn, w_router, b_router, w_gu, b_gu, w_down, b_down, g_final):
    b1, s1, d = x_prompt.shape
    b2, s2, _ = x_sample.shape
    t1, t2 = b1 * s1, b2 * s2
    t = t1 + t2
    xa, xb = x_prompt.reshape(t1, d), x_sample.reshape(t2, d)
    seq_bounds = tuple((b * s1, (b + 1) * s1) for b in range(b1)) + tuple(
        (t1 + b * s2, t1 + (b + 1) * s2) for b in range(b2))

    tm = _pick(math.gcd(s1, s2), 512)
    nt = t // tm
    w_in_ext, wq_t, wk_ext, wv_t = _prep_weights(w_in[0], w_uq[0], w_ukv[0])

    qt, k, vt3, u, gates = _in_proj(xa, xb, _inv_freq(tm), g_mix, w_in_ext, g_q, wq_t, g_kv, wk_ext, wv_t,
                                    _kpe_placement(), b_gate, tm=tm, seq_starts=tuple(s for s, _ in seq_bounds))

    ota = _attention(qt, k, vt3, seq_start=0, seq_len=s1, n_seq=b1, tq=tm, tk=tm)
    otb = _attention(qt, k, vt3, seq_start=t1, seq_len=s2, n_seq=b2, tq=tm, tk=tm)

    w_r = jnp.concatenate([w_router[0], jnp.zeros((d, LANES - N_EXPERTS), F32)], axis=1)
    w_r_hi = w_r.astype(BF16)
    w_r = jnp.concatenate([w_r_hi, (w_r - w_r_hi.astype(F32)).astype(BF16)], axis=1)
    b_r = jnp.concatenate([b_router[0], jnp.full((LANES - N_EXPERTS,), NEG_BIG, F32)])[None, :]
    ltri = jnp.asarray(np.tril(np.ones((tm, tm), np.float32), -1), BF16)
    utri = jnp.asarray(np.triu(np.ones((LANES, LANES), np.float32), 1), BF16)
    x1, route, meta = _post_mix(xa, xb, ota, otb, u, gates, w_oa[0].astype(BF16), conv_w[0], conv_b, ln_g, ln_b,
                                w_pb[0].astype(BF16), b_pb, w_out[0].astype(BF16), g_ffn, w_r, b_r, ltri, utri,
                                tm=tm, seq_bounds=seq_bounds)

    bm = MOE_ROWS
    n_rows = _sorted_rows(t, nt, bm)
    carry, n_pieces, off = (meta[:, r, :N_EXPERTS] for r in range(3))
    start, ex, n_used = _expert_layout(carry[-1] + n_pieces[-1] * RUN_PAD, bm=bm, n_blocks=n_rows // bm)
    run_row = carry + start[None, :N_EXPERTS]
    tables = _piece_tables(n_pieces, off, run_row, n_loc=_local_rows(tm))
    run_end = run_row[-1] + n_pieces[-1] * RUN_PAD

    xs = _dispatch(tables, start, run_end, x1, route, g_ffn, tm=tm, bm=bm, n_rows=n_rows)
    y = _moe(ex, n_used, xs, w_gu[0], b_gu[0][:, None, :], w_down[0], b_down[0][:, None, :], bm=bm)
    out_a, out_b = _combine(tables, x1, route, g_final[None, :], y, tm=tm, n_first=t1 // tm)
    return out_a.reshape(b1, s1, d), out_b.reshape(b2, s2, d)
```

```python
import functools
import math

import numpy as np
import jax
import jax.numpy as jnp
from jax import lax
from jax.experimental import pallas as pl
from jax.experimental.pallas import tpu as pltpu

F32 = jnp.float32
BF16 = jnp.bfloat16
I32 = jnp.int32
U32 = jnp.uint32

D_MODEL = 1024
N_HEADS = 8
Q_RANK = 384
KV_RANK = 256
D_NOPE = 64
D_ROPE = 32
D_V = 64
D_QK = D_NOPE + D_ROPE
ROPE_THETA = 10000.0
CONV_CH = 512
CONV_WIDTH = 31
CONV_PAD = (CONV_WIDTH - 1) // 2
N_EXPERTS = 32
TOP_K = 4
D_FF = 1024
SWIGLU_ALPHA = 1.702
SWIGLU_LIMIT = 7.0
RMS_EPS = 1e-6
LN_EPS = 1e-5

LANES = 128
SUBLANES = 8
BF16_ROWS = 16
TOKEN_TILE = 512
HALO_ROWS = 16
CONV_ROWS = 512
NEG_BIG = -1e30
RUN_PAD = SUBLANES
BIG_PIECE = 32
N_DISPATCH_BUFS = 3
MOE_ROWS = 512

C_CQ = 0
C_CKV = Q_RANK
C_KPE = Q_RANK + KV_RANK
C_CONV = C_KPE + LANES
C_GATE = C_CONV + 2 * CONV_CH
D_IN_EXT = C_GATE + 2 * D_MODEL

VMEM_LIMIT = 56 * 1024 * 1024


def _rms(x, g):
    return x * lax.rsqrt(jnp.mean(x * x, axis=-1, keepdims=True) + RMS_EPS) * g


def _dot(a, b):
    return jnp.dot(a, b, preferred_element_type=F32)


def _dot_nt(a, b):
    return lax.dot_general(a, b, (((1,), (1,)), ((), ())), preferred_element_type=F32)


def _dot_tn(a, b):
    return lax.dot_general(a, b, (((0,), (0,)), ((), ())), preferred_element_type=F32)


def _const_spec(shape):
    return pl.BlockSpec(shape, lambda *_: (0,) * len(shape))


PACKED = D_MODEL // 2
HIGH_HALF = 0xFFFF0000


def _pack_rows(v):
    bits = lambda x: pltpu.bitcast(x.astype(BF16).astype(F32), U32)
    return bits(v[:, :PACKED]) | lax.shift_right_logical(bits(v[:, PACKED:]), jnp.uint32(16))


def _unpack_rows(w):
    hi = pltpu.bitcast(w & jnp.uint32(HIGH_HALF), F32).astype(BF16)
    lo = pltpu.bitcast(lax.shift_left(w, jnp.uint32(16)), F32).astype(BF16)
    return jnp.concatenate([hi, lo], axis=1)


def _tile_select(i, n_first, a_ref, b_ref):
    return jnp.where(i < n_first, a_ref[...], b_ref[...])


def _in_proj_kernel(xa_ref, xb_ref, inv_ref, gmix_ref, win_ref, gq_ref, wqt_ref, gkv_ref, wk_ref, wvt_ref,
                    e2_ref, bgate_ref, qt_ref, k_ref, vt_ref, u_ref, gates_ref, *, tm, n_first, seq_starts):
    i = pl.program_id(0)
    h = _rms(_tile_select(i, n_first, xa_ref, xb_ref), gmix_ref[...]).astype(BF16)
    proj = _dot(h, win_ref[...])

    t0 = i * tm
    s0 = 0
    for s in seq_starts:
        s0 = jnp.where(t0 >= s, s, s0)
    half = D_ROPE // 2
    pos = (t0 - s0 + lax.broadcasted_iota(I32, (half, tm), 1)).astype(F32)
    ang = pos * inv_ref[...]
    cos, sin = jnp.cos(ang), jnp.sin(ang)
    scale = (D_QK ** -0.5) * math.log2(math.e)
    cos_q, sin_q = cos * scale, sin * scale

    cqn = _rms(proj[:, C_CQ:C_CQ + Q_RANK], gq_ref[...]).astype(BF16)
    q2 = _dot_nt(wqt_ref[...], cqn)
    hw = N_HEADS * LANES
    for hh in range(N_HEADS):
        qa = q2[hh * LANES:(hh + 1) * LANES, :]
        qb = q2[hw + hh * D_ROPE:hw + (hh + 1) * D_ROPE, :]
        r0 = hh * LANES
        qt_ref[0, r0:r0 + D_NOPE, :] = (qa[:D_NOPE] * scale).astype(BF16)
        for part in range(2):
            lo = D_NOPE + part * half
            qt_ref[0, r0 + lo:r0 + lo + half, :] = (
                qa[lo:lo + half] * cos_q + qb[part * half:(part + 1) * half] * sin_q).astype(BF16)
        qt_ref[0, r0 + D_QK:r0 + LANES, :] = jnp.zeros((LANES - D_QK, tm), BF16)

    ckvn = _rms(proj[:, C_CKV:C_CKV + KV_RANK], gkv_ref[...]).astype(BF16)
    tabk = jnp.transpose(jnp.concatenate([cos, cos, sin, sin, jnp.zeros((LANES - 2 * D_ROPE, tm), F32)], axis=0))
    kpe = (proj[:, C_KPE:C_KPE + LANES] * tabk).astype(BF16)
    k_ref[...] = (_dot(ckvn, wk_ref[...]) + _dot(kpe, e2_ref[...])).astype(BF16)
    vt_ref[0] = _dot_nt(wvt_ref[...], ckvn).astype(BF16)

    a = proj[:, C_CONV:C_CONV + CONV_CH]
    g = proj[:, C_CONV + CONV_CH:C_CONV + 2 * CONV_CH]
    u_ref[...] = a * jax.nn.sigmoid(g)
    gates_ref[...] = jax.nn.sigmoid(proj[:, C_GATE:] + bgate_ref[...]).astype(BF16)


def _split_specs(block, n_first, n_second):
    first = lambda i: jnp.minimum(i, n_first - 1)
    second = lambda i: jnp.clip(i - n_first, 0, n_second - 1)
    rest = (0,) * (len(block) - 1)
    return [pl.BlockSpec(block, lambda i: (first(i),) + rest), pl.BlockSpec(block, lambda i: (second(i),) + rest)]


def _in_proj(xa, xb, inv_b, g_mix, w_in_ext, g_q, wq_t, g_kv, wk_ext, wv_t, e2, b_gate, *, tm, seq_starts):
    na, nb = xa.shape[0] // tm, xb.shape[0] // tm
    nt = na + nb
    t = nt * tm
    row = lambda w: pl.BlockSpec((tm, w), lambda i: (i, 0))
    col = lambda h: pl.BlockSpec((h, tm), lambda i: (0, i))
    return pl.pallas_call(
        functools.partial(_in_proj_kernel, tm=tm, n_first=na, seq_starts=seq_starts),
        grid=(nt,),
        in_specs=_split_specs((tm, D_MODEL), na, nb) + [
                  _const_spec((D_ROPE // 2, tm)), _const_spec((1, D_MODEL)),
                  _const_spec((D_MODEL, D_IN_EXT)), _const_spec((1, Q_RANK)),
                  _const_spec((N_HEADS * (LANES + D_ROPE), Q_RANK)), _const_spec((1, KV_RANK)),
                  _const_spec((KV_RANK, N_HEADS * LANES)), _const_spec((N_HEADS * D_V, KV_RANK)),
                  _const_spec((LANES, N_HEADS * LANES)), _const_spec((1, 2 * D_MODEL))],
        out_specs=(pl.BlockSpec((1, N_HEADS * LANES, tm), lambda i: (i, 0, 0)), row(N_HEADS * LANES),
                   pl.BlockSpec((1, N_HEADS * D_V, tm), lambda i: (i, 0, 0)), row(CONV_CH), row(2 * D_MODEL)),
        out_shape=(jax.ShapeDtypeStruct((nt, N_HEADS * LANES, tm), BF16),
                   jax.ShapeDtypeStruct((t, N_HEADS * LANES), BF16),
                   jax.ShapeDtypeStruct((nt, N_HEADS * D_V, tm), BF16),
                   jax.ShapeDtypeStruct((t, CONV_CH), F32),
                   jax.ShapeDtypeStruct((t, 2 * D_MODEL), BF16)),
        compiler_params=pltpu.CompilerParams(dimension_semantics=("arbitrary",), vmem_limit_bytes=VMEM_LIMIT),
        name="in_proj",
    )(xa, xb, inv_b, g_mix, w_in_ext, g_q, wq_t, g_kv, wk_ext, wv_t, e2, b_gate)


def _attn_kernel(qt_ref, k_ref, vt_ref, o_ref, s_ref, acc_ref, m_ref, *, nq, n_chunks, tk, unroll):
    half = n_chunks // 2
    ones = jnp.ones((BF16_ROWS, tk), BF16)

    def scores(a, j, slot):
        ks = pl.multiple_of(j * tk, tk)
        s_ref[slot] = _dot(k_ref[pl.ds(ks, tk), :], qt_ref[a])

    def softmax_pv(j, slot):
        st = s_ref[slot]
        m = jnp.where(j == 0, NEG_BIG, m_ref[...])
        m_new = jnp.maximum(m, jnp.max(st, axis=0, keepdims=True))
        p = jnp.exp2(st - m_new).astype(BF16)
        v1 = jnp.concatenate([vt_ref[j], ones], axis=0)
        acc = jnp.exp2(m - m_new) * acc_ref[...] + _dot(v1, p)
        acc_ref[...] = acc
        m_ref[...] = m_new
        return acc

    m_ref[...] = jnp.full(m_ref.shape, NEG_BIG, F32)
    acc_ref[...] = jnp.zeros(acc_ref.shape, F32)
    scores(0, 0, 0)

    def pair(pp, may_end_tile):
        a = lax.div(pp, half)
        j = 2 * lax.rem(pp, half)
        scores(a, j + 1, 1)
        softmax_pv(j, 0)
        nxt = pp + 1
        scores(jnp.minimum(lax.div(nxt, half), nq - 1), 2 * lax.rem(nxt, half), 0)
        acc = softmax_pv(j + 1, 1)
        if may_end_tile:
            o_ref[a] = (acc[:D_V] / acc[D_V:D_V + 1]).astype(o_ref.dtype)

    period = math.gcd(half, unroll)

    def trip(blk, c):
        for u in range(unroll):
            pair(blk * unroll + u, (u + 1) % period == 0)
        return c

    lax.fori_loop(0, nq * half // unroll, trip, 0)


def _attention(qt3, k, vt3, *, seq_start, seq_len, n_seq, tq, tk):
    nq = seq_len // tq
    nkc = seq_len // tk
    assert nkc % 2 == 0, "key chunks are processed in pairs"
    s0 = seq_start // seq_len
    n_pairs = nq * (nkc // 2)
    unroll = next(u for u in (16, 8, 4, 2, 1) if n_pairs % u == 0)
    return pl.pallas_call(
        functools.partial(_attn_kernel, nq=nq, n_chunks=nkc, tk=tk, unroll=unroll),
        grid=(n_seq, N_HEADS),
        in_specs=[pl.BlockSpec((nq, LANES, tq), lambda b, h: (s0 + b, h, 0)),
                  pl.BlockSpec((seq_len, LANES), lambda b, h: (s0 + b, h)),
                  pl.BlockSpec((nkc, D_V, tk), lambda b, h: (s0 + b, h, 0))],
        out_specs=pl.BlockSpec((nq, D_V, tq), lambda b, h: (b, h, 0)),
        out_shape=jax.ShapeDtypeStruct((n_seq * nq, N_HEADS * D_V, tq), BF16),
        scratch_shapes=[pltpu.VMEM((2, tk, tq), F32), pltpu.VMEM((D_V + BF16_ROWS, tq), F32),
                        pltpu.VMEM((1, tq), F32)],
        compiler_params=pltpu.CompilerParams(dimension_semantics=("arbitrary", "arbitrary"),
                                             vmem_limit_bytes=VMEM_LIMIT),
        name="attention",
    )(qt3, k, vt3)


def _local_rows(tm):
    return TOP_K * tm + N_EXPERTS * RUN_PAD


def _sorted_rows(t, nt, bm):
    runs = t * TOP_K + nt * N_EXPERTS * (RUN_PAD - 1)
    return -(-runs // bm) * bm + N_EXPERTS * bm


def _rows_copy(src, s, dst, d, sem, rows=RUN_PAD):
    s = pl.multiple_of(s, RUN_PAD)
    d = pl.multiple_of(d, RUN_PAD)
    return pltpu.make_async_copy(src.at[pl.ds(s, rows), :], dst.at[pl.ds(d, rows), :], sem)


def _piece_caps(n_loc):
    return n_loc // BIG_PIECE, N_EXPERTS * (BIG_PIECE // RUN_PAD - 1)


def _piece_tables(n_pieces, off, run_row, *, n_loc):
    per_big = BIG_PIECE // RUN_PAD
    qb, qs = _piece_caps(n_loc)
    n_big = n_pieces // per_big
    n_sm = n_pieces - n_big * per_big

    def table(counts, loc0, row0, step, cap):
        end = jnp.cumsum(counts, axis=1)
        q = jnp.arange(cap, dtype=I32)
        ex = jnp.minimum(jnp.sum((end[:, None, :] <= q[None, :, None]).astype(I32), axis=2), N_EXPERTS - 1)
        onehot = ex[..., None] == jnp.arange(N_EXPERTS, dtype=I32)
        pick = lambda v: jnp.sum(jnp.where(onehot, v[:, None, :], 0), axis=2)
        idx = q[None, :] - pick(end - counts)
        return (pick(loc0) + idx * step).reshape(-1), (pick(row0) + idx * step).reshape(-1), end[:, -1]

    loc_b, row_b, cnt_b = table(n_big, off, run_row, BIG_PIECE, qb)
    done = n_big * BIG_PIECE
    loc_s, row_s, cnt_s = table(n_sm, off + done, run_row + done, RUN_PAD, qs)
    return jnp.stack([cnt_b, cnt_s], axis=1).reshape(-1), loc_b, row_b, loc_s, row_s


def _for_each_piece(tables, tile, n_loc, fn):
    cnt, loc_b, row_b, loc_s, row_s = tables
    qb, qs = _piece_caps(n_loc)

    def big(q, c):
        fn(loc_b[tile * qb + q], row_b[tile * qb + q], BIG_PIECE)
        return c

    def small(q, c):
        fn(loc_s[tile * qs + q], row_s[tile * qs + q], RUN_PAD)
        return c

    lax.fori_loop(0, cnt[2 * tile], big, 0)
    lax.fori_loop(0, cnt[2 * tile + 1], small, 0)


def _wait_rows(src, dst, sem, count, rows):
    def one(q, c):
        _rows_copy(src, 0, dst, 0, sem, rows=rows).wait()
        return c

    lax.fori_loop(0, count, one, 0)


def _wait_tile(src, dst, sem, cnt, tile):
    _wait_rows(src, dst, sem, cnt[2 * tile], BIG_PIECE)
    _wait_rows(src, dst, sem, cnt[2 * tile + 1], RUN_PAD)


def _post_mix_kernel(xa_ref, xb_ref, ota_ref, otb_ref, u_ref, up_ref, un_ref, gates_ref, woa_ref, cw_ref, cb_ref,
                     lng_ref, lnb_ref, wpb_ref, bpb_ref, wout_ref, gffn_ref, wr_ref, br_ref, ltri_ref, utri_ref,
                     x1_ref, route_ref, meta_ref, ext_ref, cv_ref, ph_ref, carry_ref, *, tm, n_first, seq_bounds):
    i = pl.program_id(0)
    t0 = i * tm
    is_start = functools.reduce(jnp.logical_or, [t0 == s for s, _ in seq_bounds])
    is_end = functools.reduce(jnp.logical_or, [t0 + tm == e for _, e in seq_bounds])

    ext_ref[0:HALO_ROWS, :] = jnp.where(is_start, 0.0, up_ref[...])
    ext_ref[HALO_ROWS:HALO_ROWS + tm, :] = u_ref[...]
    ext_ref[HALO_ROWS + tm:2 * HALO_ROWS + tm, :] = jnp.where(is_end, 0.0, un_ref[...])
    rc = min(tm, CONV_ROWS)
    sub = SUBLANES
    base = HALO_ROWS - CONV_PAD
    for c in range(tm // rc):
        acc = None
        for b in range(sub):
            part = None
            for o in range(CONV_WIDTH):
                if (o + base) % sub != b:
                    continue
                a0 = c * rc + (o + base) - b
                term = cw_ref[o:o + 1, :] * ext_ref[a0:a0 + rc + sub, :]
                part = term if part is None else part + term
            if b == 0:
                shifted = part[:rc]
            else:
                ph_ref[b] = part
                shifted = ph_ref[b, b:b + rc, :]
            acc = shifted if acc is None else acc + shifted
        cv_ref[c * rc:(c + 1) * rc, :] = acc + cb_ref[...]

    cv = cv_ref[...]
    mu = jnp.mean(cv, axis=-1, keepdims=True)
    xc = cv - mu
    y = xc * lax.rsqrt(jnp.mean(xc * xc, axis=-1, keepdims=True) + LN_EPS) * lng_ref[...] + lnb_ref[...]
    y = y * jax.nn.sigmoid(y)
    branch_b = _dot(y.astype(BF16), wpb_ref[...]) + bpb_ref[...]
    branch_a = _dot_tn(_tile_select(i, n_first, ota_ref, otb_ref)[0], woa_ref[...])
    gates = gates_ref[...]
    mix = gates[:, :D_MODEL].astype(F32) * branch_a + gates[:, D_MODEL:].astype(F32) * branch_b
    x1 = _tile_select(i, n_first, xa_ref, xb_ref) + _dot(mix.astype(BF16), wout_ref[...])
    x1_ref[...] = x1

    h2 = _rms(x1, gffn_ref[...])
    h2_hi = h2.astype(BF16)
    h2_lo = (h2 - h2_hi.astype(F32)).astype(BF16)
    hi_both = _dot(h2_hi, wr_ref[...])
    logits = hi_both[:, :LANES] + (hi_both[:, LANES:] + _dot(h2_lo, wr_ref[:, :LANES])) + br_ref[...]
    lane = lax.broadcasted_iota(I32, logits.shape, 1)
    lane_f = lane.astype(F32)
    vals, idxs, sels = [], [], []
    cur = logits
    for _ in range(TOP_K):
        mval = jnp.max(cur, axis=-1, keepdims=True)
        ik = jnp.min(jnp.where(cur == mval, lane_f, float(LANES)), axis=-1, keepdims=True)
        sel = lane_f == ik
        cur = jnp.where(sel, -jnp.inf, cur)
        vals.append(mval)
        idxs.append(ik)
        sels.append(sel)
    exps = [jnp.exp(v - vals[0]) for v in vals]
    denom = exps[0] + exps[1] + exps[2] + exps[3]

    onehot = sels[0].astype(F32) + sels[1].astype(F32) + sels[2].astype(F32) + sels[3].astype(F32)

    @pl.when(i == 0)
    def _():
        carry_ref[...] = jnp.zeros_like(carry_ref)

    n_e = jnp.sum(onehot, axis=0, keepdims=True)
    n_pieces = jnp.floor((n_e + (RUN_PAD - 1)) * (1.0 / RUN_PAD))
    pieces_b = jnp.broadcast_to(n_pieces, (SUBLANES, LANES)).astype(BF16)
    off = _dot(pieces_b, utri_ref[...])[0:1] * RUN_PAD
    total_pieces = jnp.sum(n_pieces, axis=-1, keepdims=True)
    carry = carry_ref[...]
    prefix = _dot(ltri_ref[...], onehot.astype(BF16))
    local = off + prefix
    route = jnp.zeros(logits.shape, F32)
    for kk in range(TOP_K):
        pos = jnp.sum(jnp.where(sels[kk], local, 0.0), axis=-1, keepdims=True)
        route = jnp.where(lane == kk, idxs[kk], route)
        route = jnp.where(lane == TOP_K + kk, exps[kk] / denom, route)
        route = jnp.where(lane == 2 * TOP_K + kk, pos, route)
    route_ref[...] = route
    carry_ref[...] = carry + n_pieces * RUN_PAD

    row8 = lax.broadcasted_iota(I32, (SUBLANES, LANES), 0)
    metav = jnp.where(row8 == 0, jnp.broadcast_to(carry, (SUBLANES, LANES)), 0.0)
    metav = jnp.where(row8 == 1, jnp.broadcast_to(n_pieces, (SUBLANES, LANES)), metav)
    metav = jnp.where(row8 == 2, jnp.broadcast_to(off, (SUBLANES, LANES)), metav)
    metav = jnp.where(row8 == 3, jnp.broadcast_to(total_pieces, (SUBLANES, LANES)), metav)
    meta_ref[0] = metav.astype(I32)


def _post_mix(xa, xb, ota, otb, u, gates, w_oa, conv_w, conv_b, ln_g, ln_b, w_pb, b_pb, w_out, g_ffn, w_r, b_r, ltri,
              utri, *, tm, seq_bounds):
    na, nb = xa.shape[0] // tm, xb.shape[0] // tm
    nt = na + nb
    t = nt * tm
    hb = tm // HALO_ROWS
    n_halo = t // HALO_ROWS
    row = lambda w: pl.BlockSpec((tm, w), lambda i: (i, 0))
    return pl.pallas_call(
        functools.partial(_post_mix_kernel, tm=tm, n_first=na, seq_bounds=seq_bounds),
        grid=(nt,),
        in_specs=_split_specs((tm, D_MODEL), na, nb) + _split_specs((1, N_HEADS * D_V, tm), na, nb) + [
                  row(CONV_CH),
                  pl.BlockSpec((HALO_ROWS, CONV_CH), lambda i: (jnp.maximum(i * hb - 1, 0), 0)),
                  pl.BlockSpec((HALO_ROWS, CONV_CH), lambda i: (jnp.minimum((i + 1) * hb, n_halo - 1), 0)),
                  row(2 * D_MODEL), _const_spec((N_HEADS * D_V, D_MODEL)), _const_spec((CONV_WIDTH, CONV_CH)),
                  _const_spec((1, CONV_CH)), _const_spec((1, CONV_CH)), _const_spec((1, CONV_CH)),
                  _const_spec((CONV_CH, D_MODEL)), _const_spec((1, D_MODEL)), _const_spec((D_MODEL, D_MODEL)),
                  _const_spec((1, D_MODEL)), _const_spec((D_MODEL, 2 * LANES)), _const_spec((1, LANES)),
                  _const_spec((tm, tm)), _const_spec((LANES, LANES))],
        out_specs=(row(D_MODEL), row(LANES), pl.BlockSpec((1, SUBLANES, LANES), lambda i: (i, 0, 0))),
        out_shape=(jax.ShapeDtypeStruct((t, D_MODEL), F32), jax.ShapeDtypeStruct((t, LANES), F32),
                   jax.ShapeDtypeStruct((nt, SUBLANES, LANES), I32)),
        scratch_shapes=[pltpu.VMEM((tm + 2 * HALO_ROWS, CONV_CH), F32), pltpu.VMEM((tm, CONV_CH), F32),
                        pltpu.VMEM((SUBLANES, min(tm, CONV_ROWS) + SUBLANES, CONV_CH), F32), pltpu.VMEM((1, LANES), F32)],
        compiler_params=pltpu.CompilerParams(dimension_semantics=("arbitrary",), vmem_limit_bytes=VMEM_LIMIT),
        name="post_mix",
    )(xa, xb, ota, otb, u, u, u, gates, w_oa, conv_w, conv_b, ln_g, ln_b, w_pb, b_pb, w_out, g_ffn, w_r, b_r, ltri, utri)


def _dispatch_kernel(cnt_s, locb_s, rowb_s, locs_s, rows_s, start_s, end_s, x1_ref, route_ref, gffn_ref, xs_hbm,
                     xloc_ref, zero_ref, row_sems, *, tm, nt, bm, n_rows):
    tables = (cnt_s, locb_s, rowb_s, locs_s, rows_s)
    i = pl.program_id(0)
    slot = lax.rem(i, N_DISPATCH_BUFS)
    n_loc = _local_rows(tm)

    h2 = _rms(x1_ref[...], gffn_ref[...]).astype(BF16)
    pos_t = jnp.transpose(route_ref[...])[2 * TOP_K:3 * TOP_K, :].astype(I32)
    prow = lax.broadcasted_iota(I32, (n_loc, tm), 0)
    perm = jnp.zeros((n_loc, tm), F32)
    for kk in range(TOP_K):
        perm = jnp.where(prow == pos_t[kk:kk + 1, :], 1.0, perm)
    xloc_ref[slot] = _pack_rows(_dot(perm.astype(BF16), h2))

    def wait_tile(tile):
        _wait_tile(xloc_ref.at[0], xs_hbm, row_sems.at[lax.rem(tile, N_DISPATCH_BUFS)], cnt_s, tile)

    @pl.when(i >= N_DISPATCH_BUFS - 1)
    def _():
        wait_tile(i - (N_DISPATCH_BUFS - 1))

    _for_each_piece(tables, i, n_loc,
                    lambda s, d, rows: _rows_copy(xloc_ref.at[slot], s, xs_hbm, d, row_sems.at[slot], rows=rows).start())

    @pl.when(i == nt - 1)
    def _():
        for back in range(N_DISPATCH_BUFS - 2, -1, -1):
            @pl.when(i - back >= 0)
            def _():
                wait_tile(i - back)
        row_sem = row_sems.at[0]
        zero_ref[...] = jnp.zeros(zero_ref.shape, U32)

        def tail(e, c):
            end = end_s[e]
            n = lax.shift_right_logical(start_s[e + 1] - end, int(math.log2(RUN_PAD)))

            def piece(q, c2):
                _rows_copy(zero_ref, 0, xs_hbm, end + q * RUN_PAD, row_sem).start()
                return c2

            lax.fori_loop(0, n, piece, 0)
            return c + n

        _wait_rows(zero_ref, xs_hbm, row_sem, lax.fori_loop(0, N_EXPERTS, tail, 0), RUN_PAD)
        used = start_s[N_EXPERTS]
        n_free = (n_rows - used) // bm

        def fill(b, c):
            _rows_copy(zero_ref, 0, xs_hbm, used + b * bm, row_sem, rows=bm).start()
            return c

        lax.fori_loop(0, n_free, fill, 0)
        _wait_rows(zero_ref, xs_hbm, row_sem, n_free, bm)


def _dispatch(tables, start, end, x1, route, g_ffn, *, tm, bm, n_rows):
    t = x1.shape[0]
    nt = t // tm
    return pl.pallas_call(
        functools.partial(_dispatch_kernel, tm=tm, nt=nt, bm=bm, n_rows=n_rows),
        grid_spec=pltpu.PrefetchScalarGridSpec(
            num_scalar_prefetch=7,
            grid=(nt,),
            in_specs=[pl.BlockSpec((tm, D_MODEL), lambda i, *_: (i, 0)), pl.BlockSpec((tm, LANES), lambda i, *_: (i, 0)),
                      pl.BlockSpec((1, D_MODEL), lambda i, *_: (0, 0))],
            out_specs=pl.BlockSpec(memory_space=pl.ANY),
            scratch_shapes=[pltpu.VMEM((N_DISPATCH_BUFS, _local_rows(tm), PACKED), U32),
                            pltpu.VMEM((bm, PACKED), U32), pltpu.SemaphoreType.DMA((N_DISPATCH_BUFS,))],
        ),
        out_shape=jax.ShapeDtypeStruct((n_rows, PACKED), U32),
        compiler_params=pltpu.CompilerParams(dimension_semantics=("arbitrary",), vmem_limit_bytes=VMEM_LIMIT,
                                             has_side_effects=True),
        name="dispatch",
    )(*tables, start, end, x1, route, g_ffn)


def _moe_kernel(exp_ref, n_ref, xs_ref, wgu_ref, bgu_ref, wd_ref, bd_ref, y_ref, wgu_bf, wd_bf):
    w = pl.program_id(0)
    used = w < n_ref[0]

    @pl.when(jnp.logical_or(w == 0, exp_ref[w] != exp_ref[jnp.maximum(w - 1, 0)]))
    def _():
        wgu_bf[...] = wgu_ref[0].astype(BF16)
        wd_bf[...] = wd_ref[0].astype(BF16)

    @pl.when(used)
    def _():
        x = _unpack_rows(xs_ref[...])
        gu = _dot(x, wgu_bf[...]) + bgu_ref[0]
        g = jnp.minimum(gu[:, :D_FF], SWIGLU_LIMIT)
        u = jnp.clip(gu[:, D_FF:], -SWIGLU_LIMIT, SWIGLU_LIMIT)
        act = (u + 1.0) * (g * jax.nn.sigmoid(SWIGLU_ALPHA * g))
        y_ref[...] = _pack_rows(_dot(act.astype(BF16), wd_bf[...]) + bd_ref[0])

    @pl.when(jnp.logical_not(used))
    def _():
        y_ref[...] = jnp.zeros(y_ref.shape, U32)


def _moe(ex, n_used, xs, w_gu, b_gu, w_down, b_down, *, bm):
    return pl.pallas_call(
        _moe_kernel,
        grid_spec=pltpu.PrefetchScalarGridSpec(
            num_scalar_prefetch=2,
            grid=(xs.shape[0] // bm,),
            in_specs=[pl.BlockSpec((bm, PACKED), lambda w, ex, n: (w, 0)),
                      pl.BlockSpec((1, D_MODEL, 2 * D_FF), lambda w, ex, n: (ex[w], 0, 0)),
                      pl.BlockSpec((1, 1, 2 * D_FF), lambda w, ex, n: (ex[w], 0, 0)),
                      pl.BlockSpec((1, D_FF, D_MODEL), lambda w, ex, n: (ex[w], 0, 0)),
                      pl.BlockSpec((1, 1, D_MODEL), lambda w, ex, n: (ex[w], 0, 0))],
            out_specs=pl.BlockSpec((bm, PACKED), lambda w, ex, n: (w, 0)),
            scratch_shapes=[pltpu.VMEM((D_MODEL, 2 * D_FF), BF16), pltpu.VMEM((D_FF, D_MODEL), BF16)],
        ),
        out_shape=jax.ShapeDtypeStruct(xs.shape, U32),
        compiler_params=pltpu.CompilerParams(dimension_semantics=("arbitrary",), vmem_limit_bytes=VMEM_LIMIT),
        name="moe",
    )(ex, n_used, xs, w_gu, b_gu, w_down, b_down)


def _expert_layout(totals, *, bm, n_blocks):
    n_blk = (totals + bm - 1) // bm
    blk_end = jnp.cumsum(n_blk)
    start = jnp.concatenate([jnp.zeros((1,), I32), blk_end * bm]).astype(I32)
    w = jnp.arange(n_blocks, dtype=I32)
    ex = jnp.minimum(jnp.sum((blk_end[None, :] <= w[:, None]).astype(I32), axis=1), N_EXPERTS - 1)
    return start, ex.astype(I32), blk_end[-1].astype(I32)[None]


def _combine_kernel(cnt_s, locb_s, rowb_s, locs_s, rows_s, x1_ref, route_ref, gfin_ref, y_hbm, outa_ref, outb_ref,
                    yloc_ref, sems, *, tm, nt, n_first):
    tables = (cnt_s, locb_s, rowb_s, locs_s, rows_s)
    i = pl.program_id(0)
    slot = lax.rem(i, 2)
    n_loc = _local_rows(tm)

    def gather(tile, sl):
        _for_each_piece(tables, tile, n_loc,
                        lambda loc, row, rows: _rows_copy(y_hbm, row, yloc_ref.at[sl], loc, sems.at[sl],
                                                          rows=rows).start())

    @pl.when(i == 0)
    def _():
        yloc_ref[...] = jnp.zeros(yloc_ref.shape, U32)
        gather(0, 0)

    @pl.when(i + 1 < nt)
    def _():
        gather(i + 1, 1 - slot)

    _wait_tile(y_hbm, yloc_ref.at[slot], sems.at[slot], cnt_s, i)

    route = route_ref[...]
    lanes = lax.broadcasted_iota(I32, (tm, n_loc), 1)
    pw = jnp.zeros((tm, n_loc), F32)
    for kk in range(TOP_K):
        pos = route[:, 2 * TOP_K + kk:2 * TOP_K + kk + 1].astype(I32)
        pw = jnp.where(lanes == pos, route[:, TOP_K + kk:TOP_K + kk + 1], pw)
    moe = _dot(pw.astype(BF16), _unpack_rows(yloc_ref[slot]))
    res = _rms(x1_ref[...] + moe, gfin_ref[...])

    @pl.when(i < n_first)
    def _():
        outa_ref[...] = res

    @pl.when(i >= n_first)
    def _():
        outb_ref[...] = res


def _combine(tables, x1, route, g_final, y, *, tm, n_first):
    t = x1.shape[0]
    nt = t // tm
    n_second = nt - n_first
    return pl.pallas_call(
        functools.partial(_combine_kernel, tm=tm, nt=nt, n_first=n_first),
        grid_spec=pltpu.PrefetchScalarGridSpec(
            num_scalar_prefetch=5,
            grid=(nt,),
            in_specs=[pl.BlockSpec((tm, D_MODEL), lambda i, *_: (i, 0)), pl.BlockSpec((tm, LANES), lambda i, *_: (i, 0)),
                      pl.BlockSpec((1, D_MODEL), lambda i, *_: (0, 0)), pl.BlockSpec(memory_space=pl.ANY)],
            out_specs=(pl.BlockSpec((tm, D_MODEL), lambda i, *_: (jnp.minimum(i, n_first - 1), 0)),
                       pl.BlockSpec((tm, D_MODEL), lambda i, *_: (jnp.clip(i - n_first, 0, n_second - 1), 0))),
            scratch_shapes=[pltpu.VMEM((2, _local_rows(tm), PACKED), U32), pltpu.SemaphoreType.DMA((2,))],
        ),
        out_shape=(jax.ShapeDtypeStruct((n_first * tm, D_MODEL), F32),
                   jax.ShapeDtypeStruct((n_second * tm, D_MODEL), F32)),
        compiler_params=pltpu.CompilerParams(dimension_semantics=("arbitrary",), vmem_limit_bytes=VMEM_LIMIT),
        name="combine",
    )(*tables, x1, route, g_final, y)


def _rot_cols(w):
    half = w.shape[-1] // 2
    return jnp.concatenate([-w[..., half:], w[..., :half]], axis=-1)


def _prep_weights(w_in, w_uq, w_ukv):
    d = w_in.shape[0]
    w_kpe = w_in[:, C_KPE:C_KPE + D_ROPE]
    w_in_ext = jnp.concatenate(
        [w_in[:, :C_KPE], w_kpe, _rot_cols(w_kpe), jnp.zeros((d, LANES - 2 * D_ROPE), F32),
         w_in[:, C_KPE + D_ROPE:]], axis=1).astype(BF16)

    wq = w_uq.reshape(Q_RANK, N_HEADS, D_QK)
    nope, pe = wq[..., :D_NOPE], wq[..., D_NOPE:]
    zpad = jnp.zeros((Q_RANK, N_HEADS, LANES - D_QK), F32)
    plain = jnp.concatenate([nope, pe, zpad], axis=-1).reshape(Q_RANK, N_HEADS * LANES)
    rot = _rot_cols(pe).reshape(Q_RANK, N_HEADS * D_ROPE)
    wq_t = jnp.concatenate([plain, rot], axis=1).T.astype(BF16)

    wkv = w_ukv.reshape(KV_RANK, N_HEADS, D_NOPE + D_V)
    wk_ext = jnp.concatenate([wkv[..., :D_NOPE], jnp.zeros((KV_RANK, N_HEADS, LANES - D_NOPE), F32)],
                             axis=-1).reshape(KV_RANK, N_HEADS * LANES).astype(BF16)
    wv_t = wkv[..., D_NOPE:].reshape(KV_RANK, N_HEADS * D_V).T.astype(BF16)
    return w_in_ext, wq_t, wk_ext, wv_t


def _kpe_placement():
    e2 = np.zeros((LANES, N_HEADS * LANES), np.float32)
    for j in range(D_ROPE):
        for h in range(N_HEADS):
            e2[j, h * LANES + D_NOPE + j] = 1.0
            e2[D_ROPE + j, h * LANES + D_NOPE + j] = 1.0
    return jnp.asarray(e2, BF16)


def _inv_freq(tm):
    inv = 1.0 / (ROPE_THETA ** (jnp.arange(0, D_ROPE, 2, dtype=F32) / D_ROPE))
    return jnp.broadcast_to(inv[:, None], (D_ROPE // 2, tm))


def _token_tile(s1, s2):
    g = math.gcd(s1, s2)
    return TOKEN_TILE if g % TOKEN_TILE == 0 else g


def kernel(x_prompt, x_sample, g_mix, w_in, b_gate, g_q, w_uq, g_kv, w_ukv, w_oa, conv_w, conv_b, ln_g, ln_b, w_pb,
           b_pb, w_out, g_ffn, w_router, b_router, w_gu, b_gu, w_down, b_down, g_final):
    b1, s1, d = x_prompt.shape
    b2, s2, _ = x_sample.shape
    t1, t2 = b1 * s1, b2 * s2
    t = t1 + t2
    xa, xb = x_prompt.reshape(t1, d), x_sample.reshape(t2, d)
    seq_bounds = tuple((b * s1, (b + 1) * s1) for b in range(b1)) + tuple(
        (t1 + b * s2, t1 + (b + 1) * s2) for b in range(b2))

    tm = _token_tile(s1, s2)
    nt = t // tm
    w_in_ext, wq_t, wk_ext, wv_t = _prep_weights(w_in[0], w_uq[0], w_ukv[0])

    qt, k, vt3, u, gates = _in_proj(xa, xb, _inv_freq(tm), g_mix, w_in_ext, g_q, wq_t, g_kv, wk_ext, wv_t,
                                    _kpe_placement(), b_gate, tm=tm, seq_starts=tuple(s for s, _ in seq_bounds))

    ota = _attention(qt, k, vt3, seq_start=0, seq_len=s1, n_seq=b1, tq=tm, tk=tm)
    otb = _attention(qt, k, vt3, seq_start=t1, seq_len=s2, n_seq=b2, tq=tm, tk=tm)

    w_r = jnp.concatenate([w_router[0], jnp.zeros((d, LANES - N_EXPERTS), F32)], axis=1)
    w_r_hi = w_r.astype(BF16)
    w_r = jnp.concatenate([w_r_hi, (w_r - w_r_hi.astype(F32)).astype(BF16)], axis=1)
    b_r = jnp.concatenate([b_router[0], jnp.full((LANES - N_EXPERTS,), NEG_BIG, F32)])[None, :]
    ltri = jnp.asarray(np.tril(np.ones((tm, tm), np.float32), -1), BF16)
    utri = jnp.asarray(np.triu(np.ones((LANES, LANES), np.float32), 1), BF16)
    x1, route, meta = _post_mix(xa, xb, ota, otb, u, gates, w_oa[0].astype(BF16), conv_w[0], conv_b, ln_g, ln_b,
                                w_pb[0].astype(BF16), b_pb, w_out[0].astype(BF16), g_ffn, w_r, b_r, ltri, utri,
                                tm=tm, seq_bounds=seq_bounds)

    bm = MOE_ROWS
    n_rows = _sorted_rows(t, nt, bm)
    carry, n_pieces, off = (meta[:, r, :N_EXPERTS] for r in range(3))
    start, ex, n_used = _expert_layout(carry[-1] + n_pieces[-1] * RUN_PAD, bm=bm, n_blocks=n_rows // bm)
    run_row = carry + start[None, :N_EXPERTS]
    tables = _piece_tables(n_pieces, off, run_row, n_loc=_local_rows(tm))
    run_end = run_row[-1] + n_pieces[-1] * RUN_PAD

    xs = _dispatch(tables, start, run_end, x1, route, g_ffn, tm=tm, bm=bm, n_rows=n_rows)
    y = _moe(ex, n_used, xs, w_gu[0], b_gu[0][:, None, :], w_down[0], b_down[0][:, None, :], bm=bm)
    out_a, out_b = _combine(tables, x1, route, g_final[None, :], y, tm=tm, n_first=t1 // tm)
    return out_a.reshape(b1, s1, d), out_b.reshape(b2, s2, d)
```

```python
import functools
import math

import numpy as np
import jax
import jax.numpy as jnp
from jax import lax
from jax.experimental import pallas as pl
from jax.experimental.pallas import tpu as pltpu

F32 = jnp.float32
BF16 = jnp.bfloat16
I32 = jnp.int32
U32 = jnp.uint32

D_MODEL = 1024
N_HEADS = 8
Q_RANK = 384
KV_RANK = 256
D_NOPE = 64
D_ROPE = 32
D_V = 64
D_QK = D_NOPE + D_ROPE
ROPE_THETA = 10000.0
CONV_CH = 512
CONV_WIDTH = 31
CONV_PAD = (CONV_WIDTH - 1) // 2
N_EXPERTS = 32
TOP_K = 4
D_FF = 1024
SWIGLU_ALPHA = 1.702
SWIGLU_LIMIT = 7.0
RMS_EPS = 1e-6
LN_EPS = 1e-5

LANES = 128
SUBLANES = 8
BF16_ROWS = 16
TOKEN_TILE = 512
HALO_ROWS = 16
CONV_ROWS = 512
NEG_BIG = -1e30
RUN_PAD = SUBLANES
BIG_PIECE = 32
N_DISPATCH_BUFS = 3
MOE_ROWS = 512

C_CQ = 0
C_CKV = Q_RANK
C_KPE = Q_RANK + KV_RANK
C_CONV = C_KPE + LANES
C_GATE = C_CONV + 2 * CONV_CH
D_IN_EXT = C_GATE + 2 * D_MODEL

VMEM_LIMIT = 56 * 1024 * 1024


def _rms(x, g):
    return x * lax.rsqrt(jnp.mean(x * x, axis=-1, keepdims=True) + RMS_EPS) * g


def _dot(a, b):
    return jnp.dot(a, b, preferred_element_type=F32)


def _dot_nt(a, b):
    return lax.dot_general(a, b, (((1,), (1,)), ((), ())), preferred_element_type=F32)


def _dot_tn(a, b):
    return lax.dot_general(a, b, (((0,), (0,)), ((), ())), preferred_element_type=F32)


def _const_spec(shape):
    return pl.BlockSpec(shape, lambda *_: (0,) * len(shape))


PACKED = D_MODEL // 2
HIGH_HALF = 0xFFFF0000


def _pack_rows(v):
    bits = lambda x: pltpu.bitcast(x.astype(BF16).astype(F32), U32)
    return bits(v[:, :PACKED]) | lax.shift_right_logical(bits(v[:, PACKED:]), jnp.uint32(16))


def _unpack_rows(w):
    hi = pltpu.bitcast(w & jnp.uint32(HIGH_HALF), F32).astype(BF16)
    lo = pltpu.bitcast(lax.shift_left(w, jnp.uint32(16)), F32).astype(BF16)
    return jnp.concatenate([hi, lo], axis=1)


def _tile_select(i, n_first, a_ref, b_ref):
    return jnp.where(i < n_first, a_ref[...], b_ref[...])


def _in_proj_kernel(xa_ref, xb_ref, inv_ref, gmix_ref, win_ref, gq_ref, wqt_ref, gkv_ref, wk_ref, wvt_ref,
                    bgate_ref, qt_ref, k_ref, vt_ref, u_ref, gates_ref, *, tm, n_first, seq_starts):
    i = pl.program_id(0)
    h = _rms(_tile_select(i, n_first, xa_ref, xb_ref), gmix_ref[...]).astype(BF16)
    proj = _dot(h, win_ref[...])

    t0 = i * tm
    s0 = 0
    for s in seq_starts:
        s0 = jnp.where(t0 >= s, s, s0)
    half = D_ROPE // 2
    pos = (t0 - s0 + lax.broadcasted_iota(I32, (half, tm), 1)).astype(F32)
    ang = pos * inv_ref[...]
    cos, sin = jnp.cos(ang), jnp.sin(ang)
    scale = (D_QK ** -0.5) * math.log2(math.e)
    cos_q, sin_q = cos * scale, sin * scale

    cqn = _rms(proj[:, C_CQ:C_CQ + Q_RANK], gq_ref[...]).astype(BF16)
    q2 = _dot_nt(wqt_ref[...], cqn)
    hw = N_HEADS * LANES
    for hh in range(N_HEADS):
        qa = q2[hh * LANES:(hh + 1) * LANES, :]
        qb = q2[hw + hh * D_ROPE:hw + (hh + 1) * D_ROPE, :]
        r0 = hh * LANES
        qt_ref[0, r0:r0 + D_NOPE, :] = (qa[:D_NOPE] * scale).astype(BF16)
        for part in range(2):
            lo = D_NOPE + part * half
            qt_ref[0, r0 + lo:r0 + lo + half, :] = (
                qa[lo:lo + half] * cos_q + qb[part * half:(part + 1) * half] * sin_q).astype(BF16)
        qt_ref[0, r0 + D_QK:r0 + LANES, :] = jnp.zeros((LANES - D_QK, tm), BF16)

    ckvn = _rms(proj[:, C_CKV:C_CKV + KV_RANK], gkv_ref[...]).astype(BF16)
    tabk = jnp.transpose(jnp.concatenate([cos, cos, sin, sin, jnp.zeros((LANES - 2 * D_ROPE, tm), F32)], axis=0))
    kpe = proj[:, C_KPE:C_KPE + LANES] * tabk
    lane = lax.broadcasted_iota(I32, kpe.shape, 1)
    roped = pltpu.roll(kpe, D_NOPE, axis=1) + pltpu.roll(kpe, D_NOPE - D_ROPE, axis=1)
    roped = jnp.where((lane >= D_NOPE) & (lane < D_QK), roped, 0.0)
    k_nope = _dot(ckvn, wk_ref[...])
    for hh in range(N_HEADS):
        k_ref[:, hh * LANES:(hh + 1) * LANES] = (k_nope[:, hh * LANES:(hh + 1) * LANES] + roped).astype(BF16)
    vt_ref[0] = _dot_nt(wvt_ref[...], ckvn).astype(BF16)

    a = proj[:, C_CONV:C_CONV + CONV_CH]
    g = proj[:, C_CONV + CONV_CH:C_CONV + 2 * CONV_CH]
    u_ref[...] = a * jax.nn.sigmoid(g)
    gates_ref[...] = jax.nn.sigmoid(proj[:, C_GATE:] + bgate_ref[...]).astype(BF16)


def _split_specs(block, n_first, n_second):
    first = lambda i: jnp.minimum(i, n_first - 1)
    second = lambda i: jnp.clip(i - n_first, 0, n_second - 1)
    rest = (0,) * (len(block) - 1)
    return [pl.BlockSpec(block, lambda i: (first(i),) + rest), pl.BlockSpec(block, lambda i: (second(i),) + rest)]


def _in_proj(xa, xb, inv_b, g_mix, w_in_ext, g_q, wq_t, g_kv, wk_ext, wv_t, b_gate, *, tm, seq_starts):
    na, nb = xa.shape[0] // tm, xb.shape[0] // tm
    nt = na + nb
    t = nt * tm
    row = lambda w: pl.BlockSpec((tm, w), lambda i: (i, 0))
    col = lambda h: pl.BlockSpec((h, tm), lambda i: (0, i))
    return pl.pallas_call(
        functools.partial(_in_proj_kernel, tm=tm, n_first=na, seq_starts=seq_starts),
        grid=(nt,),
        in_specs=_split_specs((tm, D_MODEL), na, nb) + [
                  _const_spec((D_ROPE // 2, tm)), _const_spec((1, D_MODEL)),
                  _const_spec((D_MODEL, D_IN_EXT)), _const_spec((1, Q_RANK)),
                  _const_spec((N_HEADS * (LANES + D_ROPE), Q_RANK)), _const_spec((1, KV_RANK)),
                  _const_spec((KV_RANK, N_HEADS * LANES)), _const_spec((N_HEADS * D_V, KV_RANK)),
                  _const_spec((1, 2 * D_MODEL))],
        out_specs=(pl.BlockSpec((1, N_HEADS * LANES, tm), lambda i: (i, 0, 0)), row(N_HEADS * LANES),
                   pl.BlockSpec((1, N_HEADS * D_V, tm), lambda i: (i, 0, 0)), row(CONV_CH), row(2 * D_MODEL)),
        out_shape=(jax.ShapeDtypeStruct((nt, N_HEADS * LANES, tm), BF16),
                   jax.ShapeDtypeStruct((t, N_HEADS * LANES), BF16),
                   jax.ShapeDtypeStruct((nt, N_HEADS * D_V, tm), BF16),
                   jax.ShapeDtypeStruct((t, CONV_CH), F32),
                   jax.ShapeDtypeStruct((t, 2 * D_MODEL), BF16)),
        compiler_params=pltpu.CompilerParams(dimension_semantics=("arbitrary",), vmem_limit_bytes=VMEM_LIMIT),
        name="in_proj",
    )(xa, xb, inv_b, g_mix, w_in_ext, g_q, wq_t, g_kv, wk_ext, wv_t, b_gate)


def _attn_kernel(qt_ref, k_ref, vt_ref, o_ref, s_ref, acc_ref, m_ref, *, nq, n_chunks, tk, unroll):
    half = n_chunks // 2
    ones = jnp.ones((BF16_ROWS, tk), BF16)

    def scores(a, j, slot):
        ks = pl.multiple_of(j * tk, tk)
        s_ref[slot] = _dot(k_ref[pl.ds(ks, tk), :], qt_ref[a])

    def softmax_pv(j, slot):
        st = s_ref[slot]
        m = jnp.where(j == 0, NEG_BIG, m_ref[...])
        m_new = jnp.maximum(m, jnp.max(st, axis=0, keepdims=True))
        p = jnp.exp2(st - m_new).astype(BF16)
        v1 = jnp.concatenate([vt_ref[j], ones], axis=0)
        acc = jnp.exp2(m - m_new) * acc_ref[...] + _dot(v1, p)
        acc_ref[...] = acc
        m_ref[...] = m_new
        return acc

    m_ref[...] = jnp.full(m_ref.shape, NEG_BIG, F32)
    acc_ref[...] = jnp.zeros(acc_ref.shape, F32)
    scores(0, 0, 0)

    def pair(pp, may_end_tile):
        a = lax.div(pp, half)
        j = 2 * lax.rem(pp, half)
        scores(a, j + 1, 1)
        softmax_pv(j, 0)
        nxt = pp + 1
        scores(jnp.minimum(lax.div(nxt, half), nq - 1), 2 * lax.rem(nxt, half), 0)
        acc = softmax_pv(j + 1, 1)
        if may_end_tile:
            o_ref[a] = (acc[:D_V] / acc[D_V:D_V + 1]).astype(o_ref.dtype)

    period = math.gcd(half, unroll)

    def trip(blk, c):
        for u in range(unroll):
            pair(blk * unroll + u, (u + 1) % period == 0)
        return c

    lax.fori_loop(0, nq * half // unroll, trip, 0)


def _attention(qt3, k, vt3, *, seq_start, seq_len, n_seq, tq, tk):
    nq = seq_len // tq
    nkc = seq_len // tk
    assert nkc % 2 == 0, "key chunks are processed in pairs"
    s0 = seq_start // seq_len
    n_pairs = nq * (nkc // 2)
    unroll = next(u for u in (16, 8, 4, 2, 1) if n_pairs % u == 0)
    return pl.pallas_call(
        functools.partial(_attn_kernel, nq=nq, n_chunks=nkc, tk=tk, unroll=unroll),
        grid=(n_seq, N_HEADS),
        in_specs=[pl.BlockSpec((nq, LANES, tq), lambda b, h: (s0 + b, h, 0)),
                  pl.BlockSpec((seq_len, LANES), lambda b, h: (s0 + b, h)),
                  pl.BlockSpec((nkc, D_V, tk), lambda b, h: (s0 + b, h, 0))],
        out_specs=pl.BlockSpec((nq, D_V, tq), lambda b, h: (b, h, 0)),
        out_shape=jax.ShapeDtypeStruct((n_seq * nq, N_HEADS * D_V, tq), BF16),
        scratch_shapes=[pltpu.VMEM((2, tk, tq), F32), pltpu.VMEM((D_V + BF16_ROWS, tq), F32),
                        pltpu.VMEM((1, tq), F32)],
        compiler_params=pltpu.CompilerParams(dimension_semantics=("arbitrary", "arbitrary"),
                                             vmem_limit_bytes=VMEM_LIMIT),
        name="attention",
    )(qt3, k, vt3)


def _local_rows(tm):
    return TOP_K * tm + N_EXPERTS * RUN_PAD


def _sorted_rows(t, nt, bm):
    runs = t * TOP_K + nt * N_EXPERTS * (RUN_PAD - 1)
    return -(-runs // bm) * bm + N_EXPERTS * bm


def _rows_copy(src, s, dst, d, sem, rows=RUN_PAD):
    s = pl.multiple_of(s, RUN_PAD)
    d = pl.multiple_of(d, RUN_PAD)
    return pltpu.make_async_copy(src.at[pl.ds(s, rows), :], dst.at[pl.ds(d, rows), :], sem)


def _piece_caps(n_loc):
    return n_loc // BIG_PIECE, N_EXPERTS * (BIG_PIECE // RUN_PAD - 1)


def _piece_tables(n_pieces, off, run_row, *, n_loc):
    per_big = BIG_PIECE // RUN_PAD
    qb, qs = _piece_caps(n_loc)
    n_big = n_pieces // per_big
    n_sm = n_pieces - n_big * per_big

    def table(counts, loc0, row0, step, cap):
        end = jnp.cumsum(counts, axis=1)
        q = jnp.arange(cap, dtype=I32)
        ex = jnp.minimum(jnp.sum((end[:, None, :] <= q[None, :, None]).astype(I32), axis=2), N_EXPERTS - 1)
        onehot = ex[..., None] == jnp.arange(N_EXPERTS, dtype=I32)
        pick = lambda v: jnp.sum(jnp.where(onehot, v[:, None, :], 0), axis=2)
        idx = q[None, :] - pick(end - counts)
        return (pick(loc0) + idx * step).reshape(-1), (pick(row0) + idx * step).reshape(-1), end[:, -1]

    loc_b, row_b, cnt_b = table(n_big, off, run_row, BIG_PIECE, qb)
    done = n_big * BIG_PIECE
    loc_s, row_s, cnt_s = table(n_sm, off + done, run_row + done, RUN_PAD, qs)
    return jnp.stack([cnt_b, cnt_s], axis=1).reshape(-1), loc_b, row_b, loc_s, row_s


def _for_each_piece(tables, tile, n_loc, fn):
    cnt, loc_b, row_b, loc_s, row_s = tables
    qb, qs = _piece_caps(n_loc)

    def big(q, c):
        fn(loc_b[tile * qb + q], row_b[tile * qb + q], BIG_PIECE)
        return c

    def small(q, c):
        fn(loc_s[tile * qs + q], row_s[tile * qs + q], RUN_PAD)
        return c

    lax.fori_loop(0, cnt[2 * tile], big, 0)
    lax.fori_loop(0, cnt[2 * tile + 1], small, 0)


def _wait_rows(src, dst, sem, count, rows):
    def one(q, c):
        _rows_copy(src, 0, dst, 0, sem, rows=rows).wait()
        return c

    lax.fori_loop(0, count, one, 0)


def _wait_tile(src, dst, sem, cnt, tile):
    _wait_rows(src, dst, sem, cnt[2 * tile], BIG_PIECE)
    _wait_rows(src, dst, sem, cnt[2 * tile + 1], RUN_PAD)


def _post_mix_kernel(xa_ref, xb_ref, ota_ref, otb_ref, u_ref, up_ref, un_ref, gates_ref, woa_ref, cw_ref, cb_ref,
                     lng_ref, lnb_ref, wpb_ref, bpb_ref, wout_ref, gffn_ref, wr_ref, br_ref, ltri_ref, utri_ref,
                     x1_ref, route_ref, meta_ref, ext_ref, cv_ref, ph_ref, carry_ref, *, tm, n_first, seq_bounds):
    i = pl.program_id(0)
    t0 = i * tm
    is_start = functools.reduce(jnp.logical_or, [t0 == s for s, _ in seq_bounds])
    is_end = functools.reduce(jnp.logical_or, [t0 + tm == e for _, e in seq_bounds])

    ext_ref[0:HALO_ROWS, :] = jnp.where(is_start, 0.0, up_ref[...])
    ext_ref[HALO_ROWS:HALO_ROWS + tm, :] = u_ref[...]
    ext_ref[HALO_ROWS + tm:2 * HALO_ROWS + tm, :] = jnp.where(is_end, 0.0, un_ref[...])
    rc = min(tm, CONV_ROWS)
    sub = SUBLANES
    base = HALO_ROWS - CONV_PAD
    for c in range(tm // rc):
        acc = None
        for b in range(sub):
            part = None
            for o in range(CONV_WIDTH):
                if (o + base) % sub != b:
                    continue
                a0 = c * rc + (o + base) - b
                term = cw_ref[o:o + 1, :] * ext_ref[a0:a0 + rc + sub, :]
                part = term if part is None else part + term
            if b == 0:
                shifted = part[:rc]
            else:
                ph_ref[b] = part
                shifted = ph_ref[b, b:b + rc, :]
            acc = shifted if acc is None else acc + shifted
        cv_ref[c * rc:(c + 1) * rc, :] = acc + cb_ref[...]

    cv = cv_ref[...]
    mu = jnp.mean(cv, axis=-1, keepdims=True)
    xc = cv - mu
    y = xc * lax.rsqrt(jnp.mean(xc * xc, axis=-1, keepdims=True) + LN_EPS) * lng_ref[...] + lnb_ref[...]
    y = y * jax.nn.sigmoid(y)
    branch_b = _dot(y.astype(BF16), wpb_ref[...]) + bpb_ref[...]
    branch_a = _dot_tn(_tile_select(i, n_first, ota_ref, otb_ref)[0], woa_ref[...])
    gates = gates_ref[...]
    mix = gates[:, :D_MODEL].astype(F32) * branch_a + gates[:, D_MODEL:].astype(F32) * branch_b
    x1 = _tile_select(i, n_first, xa_ref, xb_ref) + _dot(mix.astype(BF16), wout_ref[...])
    x1_ref[...] = x1

    h2 = _rms(x1, gffn_ref[...])
    h2_hi = h2.astype(BF16)
    h2_lo = (h2 - h2_hi.astype(F32)).astype(BF16)
    hi_both = _dot(h2_hi, wr_ref[...])
    logits = hi_both[:, :LANES] + (hi_both[:, LANES:] + _dot(h2_lo, wr_ref[:, :LANES])) + br_ref[...]
    lane = lax.broadcasted_iota(I32, logits.shape, 1)
    lane_f = lane.astype(F32)
    vals, idxs, sels = [], [], []
    cur = logits
    for _ in range(TOP_K):
        mval = jnp.max(cur, axis=-1, keepdims=True)
        ik = jnp.min(jnp.where(cur == mval, lane_f, float(LANES)), axis=-1, keepdims=True)
        sel = lane_f == ik
        cur = jnp.where(sel, -jnp.inf, cur)
        vals.append(mval)
        idxs.append(ik)
        sels.append(sel)
    exps = [jnp.exp(v - vals[0]) for v in vals]
    denom = exps[0] + exps[1] + exps[2] + exps[3]

    onehot = sels[0].astype(F32) + sels[1].astype(F32) + sels[2].astype(F32) + sels[3].astype(F32)

    @pl.when(i == 0)
    def _():
        carry_ref[...] = jnp.zeros_like(carry_ref)

    n_e = jnp.sum(onehot, axis=0, keepdims=True)
    n_pieces = jnp.floor((n_e + (RUN_PAD - 1)) * (1.0 / RUN_PAD))
    pieces_b = jnp.broadcast_to(n_pieces, (SUBLANES, LANES)).astype(BF16)
    off = _dot(pieces_b, utri_ref[...])[0:1] * RUN_PAD
    total_pieces = jnp.sum(n_pieces, axis=-1, keepdims=True)
    carry = carry_ref[...]
    prefix = _dot(ltri_ref[...], onehot.astype(BF16))
    local = off + prefix
    route = jnp.zeros(logits.shape, F32)
    for kk in range(TOP_K):
        pos = jnp.sum(jnp.where(sels[kk], local, 0.0), axis=-1, keepdims=True)
        route = jnp.where(lane == kk, idxs[kk], route)
        route = jnp.where(lane == TOP_K + kk, exps[kk] / denom, route)
        route = jnp.where(lane == 2 * TOP_K + kk, pos, route)
    route_ref[...] = route
    carry_ref[...] = carry + n_pieces * RUN_PAD

    row8 = lax.broadcasted_iota(I32, (SUBLANES, LANES), 0)
    metav = jnp.where(row8 == 0, jnp.broadcast_to(carry, (SUBLANES, LANES)), 0.0)
    metav = jnp.where(row8 == 1, jnp.broadcast_to(n_pieces, (SUBLANES, LANES)), metav)
    metav = jnp.where(row8 == 2, jnp.broadcast_to(off, (SUBLANES, LANES)), metav)
    metav = jnp.where(row8 == 3, jnp.broadcast_to(total_pieces, (SUBLANES, LANES)), metav)
    meta_ref[0] = metav.astype(I32)


def _post_mix(xa, xb, ota, otb, u, gates, w_oa, conv_w, conv_b, ln_g, ln_b, w_pb, b_pb, w_out, g_ffn, w_r, b_r, ltri,
              utri, *, tm, seq_bounds):
    na, nb = xa.shape[0] // tm, xb.shape[0] // tm
    nt = na + nb
    t = nt * tm
    hb = tm // HALO_ROWS
    n_halo = t // HALO_ROWS
    row = lambda w: pl.BlockSpec((tm, w), lambda i: (i, 0))
    return pl.pallas_call(
        functools.partial(_post_mix_kernel, tm=tm, n_first=na, seq_bounds=seq_bounds),
        grid=(nt,),
        in_specs=_split_specs((tm, D_MODEL), na, nb) + _split_specs((1, N_HEADS * D_V, tm), na, nb) + [
                  row(CONV_CH),
                  pl.BlockSpec((HALO_ROWS, CONV_CH), lambda i: (jnp.maximum(i * hb - 1, 0), 0)),
                  pl.BlockSpec((HALO_ROWS, CONV_CH), lambda i: (jnp.minimum((i + 1) * hb, n_halo - 1), 0)),
                  row(2 * D_MODEL), _const_spec((N_HEADS * D_V, D_MODEL)), _const_spec((CONV_WIDTH, CONV_CH)),
                  _const_spec((1, CONV_CH)), _const_spec((1, CONV_CH)), _const_spec((1, CONV_CH)),
                  _const_spec((CONV_CH, D_MODEL)), _const_spec((1, D_MODEL)), _const_spec((D_MODEL, D_MODEL)),
                  _const_spec((1, D_MODEL)), _const_spec((D_MODEL, 2 * LANES)), _const_spec((1, LANES)),
                  _const_spec((tm, tm)), _const_spec((LANES, LANES))],
        out_specs=(row(D_MODEL), row(LANES), pl.BlockSpec((1, SUBLANES, LANES), lambda i: (i, 0, 0))),
        out_shape=(jax.ShapeDtypeStruct((t, D_MODEL), F32), jax.ShapeDtypeStruct((t, LANES), F32),
                   jax.ShapeDtypeStruct((nt, SUBLANES, LANES), I32)),
        scratch_shapes=[pltpu.VMEM((tm + 2 * HALO_ROWS, CONV_CH), F32), pltpu.VMEM((tm, CONV_CH), F32),
                        pltpu.VMEM((SUBLANES, min(tm, CONV_ROWS) + SUBLANES, CONV_CH), F32), pltpu.VMEM((1, LANES), F32)],
        compiler_params=pltpu.CompilerParams(dimension_semantics=("arbitrary",), vmem_limit_bytes=VMEM_LIMIT),
        name="post_mix",
    )(xa, xb, ota, otb, u, u, u, gates, w_oa, conv_w, conv_b, ln_g, ln_b, w_pb, b_pb, w_out, g_ffn, w_r, b_r, ltri, utri)


def _dispatch_kernel(cnt_s, locb_s, rowb_s, locs_s, rows_s, start_s, end_s, x1_ref, route_ref, gffn_ref, xs_hbm,
                     xloc_ref, zero_ref, row_sems, *, tm, nt, bm, n_rows):
    tables = (cnt_s, locb_s, rowb_s, locs_s, rows_s)
    i = pl.program_id(0)
    slot = lax.rem(i, N_DISPATCH_BUFS)
    n_loc = _local_rows(tm)

    h2 = _rms(x1_ref[...], gffn_ref[...]).astype(BF16)
    pos_t = jnp.transpose(route_ref[...])[2 * TOP_K:3 * TOP_K, :].astype(I32)
    prow = lax.broadcasted_iota(I32, (n_loc, tm), 0)
    perm = jnp.zeros((n_loc, tm), F32)
    for kk in range(TOP_K):
        perm = jnp.where(prow == pos_t[kk:kk + 1, :], 1.0, perm)
    xloc_ref[slot] = _pack_rows(_dot(perm.astype(BF16), h2))

    def wait_tile(tile):
        _wait_tile(xloc_ref.at[0], xs_hbm, row_sems.at[lax.rem(tile, N_DISPATCH_BUFS)], cnt_s, tile)

    @pl.when(i >= N_DISPATCH_BUFS - 1)
    def _():
        wait_tile(i - (N_DISPATCH_BUFS - 1))

    _for_each_piece(tables, i, n_loc,
                    lambda s, d, rows: _rows_copy(xloc_ref.at[slot], s, xs_hbm, d, row_sems.at[slot], rows=rows).start())

    @pl.when(i == nt - 1)
    def _():
        for back in range(N_DISPATCH_BUFS - 2, -1, -1):
            @pl.when(i - back >= 0)
            def _():
                wait_tile(i - back)
        row_sem = row_sems.at[0]
        zero_ref[...] = jnp.zeros(zero_ref.shape, U32)

        def tail(e, c):
            end = end_s[e]
            n = lax.shift_right_logical(start_s[e + 1] - end, int(math.log2(RUN_PAD)))

            def piece(q, c2):
                _rows_copy(zero_ref, 0, xs_hbm, end + q * RUN_PAD, row_sem).start()
                return c2

            lax.fori_loop(0, n, piece, 0)
            return c + n

        _wait_rows(zero_ref, xs_hbm, row_sem, lax.fori_loop(0, N_EXPERTS, tail, 0), RUN_PAD)
        used = start_s[N_EXPERTS]
        n_free = (n_rows - used) // bm

        def fill(b, c):
            _rows_copy(zero_ref, 0, xs_hbm, used + b * bm, row_sem, rows=bm).start()
            return c

        lax.fori_loop(0, n_free, fill, 0)
        _wait_rows(zero_ref, xs_hbm, row_sem, n_free, bm)


def _dispatch(tables, start, end, x1, route, g_ffn, *, tm, bm, n_rows):
    t = x1.shape[0]
    nt = t // tm
    return pl.pallas_call(
        functools.partial(_dispatch_kernel, tm=tm, nt=nt, bm=bm, n_rows=n_rows),
        grid_spec=pltpu.PrefetchScalarGridSpec(
            num_scalar_prefetch=7,
            grid=(nt,),
            in_specs=[pl.BlockSpec((tm, D_MODEL), lambda i, *_: (i, 0)), pl.BlockSpec((tm, LANES), lambda i, *_: (i, 0)),
                      pl.BlockSpec((1, D_MODEL), lambda i, *_: (0, 0))],
            out_specs=pl.BlockSpec(memory_space=pl.ANY),
            scratch_shapes=[pltpu.VMEM((N_DISPATCH_BUFS, _local_rows(tm), PACKED), U32),
                            pltpu.VMEM((bm, PACKED), U32), pltpu.SemaphoreType.DMA((N_DISPATCH_BUFS,))],
        ),
        out_shape=jax.ShapeDtypeStruct((n_rows, PACKED), U32),
        compiler_params=pltpu.CompilerParams(dimension_semantics=("arbitrary",), vmem_limit_bytes=VMEM_LIMIT,
                                             has_side_effects=True),
        name="dispatch",
    )(*tables, start, end, x1, route, g_ffn)


def _moe_kernel(exp_ref, n_ref, xs_ref, wgu_ref, bgu_ref, wd_ref, bd_ref, y_ref, wgu_bf, wd_bf):
    w = pl.program_id(0)
    used = w < n_ref[0]

    @pl.when(jnp.logical_or(w == 0, exp_ref[w] != exp_ref[jnp.maximum(w - 1, 0)]))
    def _():
        wgu_bf[...] = wgu_ref[0].astype(BF16)
        wd_bf[...] = wd_ref[0].astype(BF16)

    @pl.when(used)
    def _():
        x = _unpack_rows(xs_ref[...])
        gu = _dot(x, wgu_bf[...]) + bgu_ref[0]
        g = jnp.minimum(gu[:, :D_FF], SWIGLU_LIMIT)
        u = jnp.clip(gu[:, D_FF:], -SWIGLU_LIMIT, SWIGLU_LIMIT)
        act = (u + 1.0) * (g * jax.nn.sigmoid(SWIGLU_ALPHA * g))
        y_ref[...] = _pack_rows(_dot(act.astype(BF16), wd_bf[...]) + bd_ref[0])

    @pl.when(jnp.logical_not(used))
    def _():
        y_ref[...] = jnp.zeros(y_ref.shape, U32)


def _moe(ex, n_used, xs, w_gu, b_gu, w_down, b_down, *, bm):
    return pl.pallas_call(
        _moe_kernel,
        grid_spec=pltpu.PrefetchScalarGridSpec(
            num_scalar_prefetch=2,
            grid=(xs.shape[0] // bm,),
            in_specs=[pl.BlockSpec((bm, PACKED), lambda w, ex, n: (w, 0)),
                      pl.BlockSpec((1, D_MODEL, 2 * D_FF), lambda w, ex, n: (ex[w], 0, 0)),
                      pl.BlockSpec((1, 1, 2 * D_FF), lambda w, ex, n: (ex[w], 0, 0)),
                      pl.BlockSpec((1, D_FF, D_MODEL), lambda w, ex, n: (ex[w], 0, 0)),
                      pl.BlockSpec((1, 1, D_MODEL), lambda w, ex, n: (ex[w], 0, 0))],
            out_specs=pl.BlockSpec((bm, PACKED), lambda w, ex, n: (w, 0)),
            scratch_shapes=[pltpu.VMEM((D_MODEL, 2 * D_FF), BF16), pltpu.VMEM((D_FF, D_MODEL), BF16)],
        ),
        out_shape=jax.ShapeDtypeStruct(xs.shape, U32),
        compiler_params=pltpu.CompilerParams(dimension_semantics=("arbitrary",), vmem_limit_bytes=VMEM_LIMIT),
        name="moe",
    )(ex, n_used, xs, w_gu, b_gu, w_down, b_down)


def _expert_layout(totals, *, bm, n_blocks):
    n_blk = (totals + bm - 1) // bm
    blk_end = jnp.cumsum(n_blk)
    start = jnp.concatenate([jnp.zeros((1,), I32), blk_end * bm]).astype(I32)
    w = jnp.arange(n_blocks, dtype=I32)
    ex = jnp.minimum(jnp.sum((blk_end[None, :] <= w[:, None]).astype(I32), axis=1), N_EXPERTS - 1)
    return start, ex.astype(I32), blk_end[-1].astype(I32)[None]


def _combine_kernel(cnt_s, locb_s, rowb_s, locs_s, rows_s, x1_ref, route_ref, gfin_ref, y_hbm, outa_ref, outb_ref,
                    yloc_ref, sems, *, tm, nt, n_first):
    tables = (cnt_s, locb_s, rowb_s, locs_s, rows_s)
    i = pl.program_id(0)
    slot = lax.rem(i, 2)
    n_loc = _local_rows(tm)

    def gather(tile, sl):
        _for_each_piece(tables, tile, n_loc,
                        lambda loc, row, rows: _rows_copy(y_hbm, row, yloc_ref.at[sl], loc, sems.at[sl],
                                                          rows=rows).start())

    @pl.when(i == 0)
    def _():
        yloc_ref[...] = jnp.zeros(yloc_ref.shape, U32)
        gather(0, 0)

    @pl.when(i + 1 < nt)
    def _():
        gather(i + 1, 1 - slot)

    _wait_tile(y_hbm, yloc_ref.at[slot], sems.at[slot], cnt_s, i)

    route = route_ref[...]
    lanes = lax.broadcasted_iota(I32, (tm, n_loc), 1)
    pw = jnp.zeros((tm, n_loc), F32)
    for kk in range(TOP_K):
        pos = route[:, 2 * TOP_K + kk:2 * TOP_K + kk + 1].astype(I32)
        pw = jnp.where(lanes == pos, route[:, TOP_K + kk:TOP_K + kk + 1], pw)
    moe = _dot(pw.astype(BF16), _unpack_rows(yloc_ref[slot]))
    res = _rms(x1_ref[...] + moe, gfin_ref[...])

    @pl.when(i < n_first)
    def _():
        outa_ref[...] = res

    @pl.when(i >= n_first)
    def _():
        outb_ref[...] = res


def _combine(tables, x1, route, g_final, y, *, tm, n_first):
    t = x1.shape[0]
    nt = t // tm
    n_second = nt - n_first
    return pl.pallas_call(
        functools.partial(_combine_kernel, tm=tm, nt=nt, n_first=n_first),
        grid_spec=pltpu.PrefetchScalarGridSpec(
            num_scalar_prefetch=5,
            grid=(nt,),
            in_specs=[pl.BlockSpec((tm, D_MODEL), lambda i, *_: (i, 0)), pl.BlockSpec((tm, LANES), lambda i, *_: (i, 0)),
                      pl.BlockSpec((1, D_MODEL), lambda i, *_: (0, 0)), pl.BlockSpec(memory_space=pl.ANY)],
            out_specs=(pl.BlockSpec((tm, D_MODEL), lambda i, *_: (jnp.minimum(i, n_first - 1), 0)),
                       pl.BlockSpec((tm, D_MODEL), lambda i, *_: (jnp.clip(i - n_first, 0, n_second - 1), 0))),
            scratch_shapes=[pltpu.VMEM((2, _local_rows(tm), PACKED), U32), pltpu.SemaphoreType.DMA((2,))],
        ),
        out_shape=(jax.ShapeDtypeStruct((n_first * tm, D_MODEL), F32),
                   jax.ShapeDtypeStruct((n_second * tm, D_MODEL), F32)),
        compiler_params=pltpu.CompilerParams(dimension_semantics=("arbitrary",), vmem_limit_bytes=VMEM_LIMIT),
        name="combine",
    )(*tables, x1, route, g_final, y)


def _rot_cols(w):
    half = w.shape[-1] // 2
    return jnp.concatenate([-w[..., half:], w[..., :half]], axis=-1)


def _prep_weights(w_in, w_uq, w_ukv):
    d = w_in.shape[0]
    w_kpe = w_in[:, C_KPE:C_KPE + D_ROPE]
    w_in_ext = jnp.concatenate(
        [w_in[:, :C_KPE], w_kpe, _rot_cols(w_kpe), jnp.zeros((d, LANES - 2 * D_ROPE), F32),
         w_in[:, C_KPE + D_ROPE:]], axis=1).astype(BF16)

    wq = w_uq.reshape(Q_RANK, N_HEADS, D_QK)
    nope, pe = wq[..., :D_NOPE], wq[..., D_NOPE:]
    zpad = jnp.zeros((Q_RANK, N_HEADS, LANES - D_QK), F32)
    plain = jnp.concatenate([nope, pe, zpad], axis=-1).reshape(Q_RANK, N_HEADS * LANES)
    rot = _rot_cols(pe).reshape(Q_RANK, N_HEADS * D_ROPE)
    wq_t = jnp.concatenate([plain, rot], axis=1).T.astype(BF16)

    wkv = w_ukv.reshape(KV_RANK, N_HEADS, D_NOPE + D_V)
    wk_ext = jnp.concatenate([wkv[..., :D_NOPE], jnp.zeros((KV_RANK, N_HEADS, LANES - D_NOPE), F32)],
                             axis=-1).reshape(KV_RANK, N_HEADS * LANES).astype(BF16)
    wv_t = wkv[..., D_NOPE:].reshape(KV_RANK, N_HEADS * D_V).T.astype(BF16)
    return w_in_ext, wq_t, wk_ext, wv_t


def _inv_freq(tm):
    inv = 1.0 / (ROPE_THETA ** (jnp.arange(0, D_ROPE, 2, dtype=F32) / D_ROPE))
    return jnp.broadcast_to(inv[:, None], (D_ROPE // 2, tm))


def _token_tile(s1, s2):
    g = math.gcd(s1, s2)
    return TOKEN_TILE if g % TOKEN_TILE == 0 else g


def kernel(x_prompt, x_sample, g_mix, w_in, b_gate, g_q, w_uq, g_kv, w_ukv, w_oa, conv_w, conv_b, ln_g, ln_b, w_pb,
           b_pb, w_out, g_ffn, w_router, b_router, w_gu, b_gu, w_down, b_down, g_final):
    b1, s1, d = x_prompt.shape
    b2, s2, _ = x_sample.shape
    t1, t2 = b1 * s1, b2 * s2
    t = t1 + t2
    xa, xb = x_prompt.reshape(t1, d), x_sample.reshape(t2, d)
    seq_bounds = tuple((b * s1, (b + 1) * s1) for b in range(b1)) + tuple(
        (t1 + b * s2, t1 + (b + 1) * s2) for b in range(b2))

    tm = _token_tile(s1, s2)
    nt = t // tm
    w_in_ext, wq_t, wk_ext, wv_t = _prep_weights(w_in[0], w_uq[0], w_ukv[0])

    qt, k, vt3, u, gates = _in_proj(xa, xb, _inv_freq(tm), g_mix, w_in_ext, g_q, wq_t, g_kv, wk_ext, wv_t,
                                    b_gate, tm=tm, seq_starts=tuple(s for s, _ in seq_bounds))

    ota = _attention(qt, k, vt3, seq_start=0, seq_len=s1, n_seq=b1, tq=tm, tk=tm)
    otb = _attention(qt, k, vt3, seq_start=t1, seq_len=s2, n_seq=b2, tq=tm, tk=tm)

    w_r = jnp.concatenate([w_router[0], jnp.zeros((d, LANES - N_EXPERTS), F32)], axis=1)
    w_r_hi = w_r.astype(BF16)
    w_r = jnp.concatenate([w_r_hi, (w_r - w_r_hi.astype(F32)).astype(BF16)], axis=1)
    b_r = jnp.concatenate([b_router[0], jnp.full((LANES - N_EXPERTS,), NEG_BIG, F32)])[None, :]
    ltri = jnp.asarray(np.tril(np.ones((tm, tm), np.float32), -1), BF16)
    utri = jnp.asarray(np.triu(np.ones((LANES, LANES), np.float32), 1), BF16)
    x1, route, meta = _post_mix(xa, xb, ota, otb, u, gates, w_oa[0].astype(BF16), conv_w[0], conv_b, ln_g, ln_b,
                                w_pb[0].astype(BF16), b_pb, w_out[0].astype(BF16), g_ffn, w_r, b_r, ltri, utri,
                                tm=tm, seq_bounds=seq_bounds)

    bm = MOE_ROWS
    n_rows = _sorted_rows(t, nt, bm)
    carry, n_pieces, off = (meta[:, r, :N_EXPERTS] for r in range(3))
    start, ex, n_used = _expert_layout(carry[-1] + n_pieces[-1] * RUN_PAD, bm=bm, n_blocks=n_rows // bm)
    run_row = carry + start[None, :N_EXPERTS]
    tables = _piece_tables(n_pieces, off, run_row, n_loc=_local_rows(tm))
    run_end = run_row[-1] + n_pieces[-1] * RUN_PAD

    xs = _dispatch(tables, start, run_end, x1, route, g_ffn, tm=tm, bm=bm, n_rows=n_rows)
    y = _moe(ex, n_used, xs, w_gu[0], b_gu[0][:, None, :], w_down[0], b_down[0][:, None, :], bm=bm)
    out_a, out_b = _combine(tables, x1, route, g_final[None, :], y, tm=tm, n_first=t1 // tm)
    return out_a.reshape(b1, s1, d), out_b.reshape(b2, s2, d)
```

```python
import functools
import math

import numpy as np
import jax
import jax.numpy as jnp
from jax import lax
from jax.experimental import pallas as pl
from jax.experimental.pallas import tpu as pltpu

F32 = jnp.float32
BF16 = jnp.bfloat16
I32 = jnp.int32
U32 = jnp.uint32

D_MODEL = 1024
N_HEADS = 8
Q_RANK = 384
KV_RANK = 256
D_NOPE = 64
D_ROPE = 32
D_V = 64
D_QK = D_NOPE + D_ROPE
ROPE_THETA = 10000.0
CONV_CH = 512
CONV_WIDTH = 31
CONV_PAD = (CONV_WIDTH - 1) // 2
N_EXPERTS = 32
TOP_K = 4
D_FF = 1024
SWIGLU_ALPHA = 1.702
SWIGLU_LIMIT = 7.0
RMS_EPS = 1e-6
LN_EPS = 1e-5

LANES = 128
SUBLANES = 8
BF16_ROWS = 16
TOKEN_TILE = 512
HALO_ROWS = 16
CONV_ROWS = 512
NEG_BIG = -1e30
RUN_PAD = SUBLANES
BIG_PIECE = 32
N_DISPATCH_BUFS = 3
MOE_ROWS = 512

C_CQ = 0
C_CKV = Q_RANK
C_KPE = Q_RANK + KV_RANK
C_CONV = C_KPE + LANES
C_GATE = C_CONV + 2 * CONV_CH
D_IN_EXT = C_GATE + 2 * D_MODEL

VMEM_LIMIT = 56 * 1024 * 1024


def _rms(x, g):
    return x * lax.rsqrt(jnp.mean(x * x, axis=-1, keepdims=True) + RMS_EPS) * g


def _dot(a, b):
    return jnp.dot(a, b, preferred_element_type=F32)


def _dot_nt(a, b):
    return lax.dot_general(a, b, (((1,), (1,)), ((), ())), preferred_element_type=F32)


def _dot_tn(a, b):
    return lax.dot_general(a, b, (((0,), (0,)), ((), ())), preferred_element_type=F32)


def _const_spec(shape):
    return pl.BlockSpec(shape, lambda *_: (0,) * len(shape))


PACKED = D_MODEL // 2
HIGH_HALF = 0xFFFF0000


def _pack_rows(v):
    bits = lambda x: pltpu.bitcast(x.astype(BF16).astype(F32), U32)
    return bits(v[:, :PACKED]) | lax.shift_right_logical(bits(v[:, PACKED:]), jnp.uint32(16))


def _unpack_rows(w):
    hi = pltpu.bitcast(w & jnp.uint32(HIGH_HALF), F32).astype(BF16)
    lo = pltpu.bitcast(lax.shift_left(w, jnp.uint32(16)), F32).astype(BF16)
    return jnp.concatenate([hi, lo], axis=1)


def _tile_select(i, n_first, a_ref, b_ref):
    return jnp.where(i < n_first, a_ref[...], b_ref[...])


def _in_proj_kernel(xa_ref, xb_ref, inv_ref, gmix_ref, win_ref, gq_ref, wqt_ref, gkv_ref, wk_ref, wvt_ref,
                    e2_ref, bgate_ref, qt_ref, k_ref, vt_ref, u_ref, gates_ref, *, tm, n_first, seq_starts):
    i = pl.program_id(0)
    h = _rms(_tile_select(i, n_first, xa_ref, xb_ref), gmix_ref[...]).astype(BF16)
    proj = _dot(h, win_ref[...])

    t0 = i * tm
    s0 = 0
    for s in seq_starts:
        s0 = jnp.where(t0 >= s, s, s0)
    half = D_ROPE // 2
    pos = (t0 - s0 + lax.broadcasted_iota(I32, (half, tm), 1)).astype(F32)
    ang = pos * inv_ref[...]
    cos, sin = jnp.cos(ang), jnp.sin(ang)
    scale = (D_QK ** -0.5) * math.log2(math.e)
    cos_q, sin_q = cos * scale, sin * scale

    cqn = _rms(proj[:, C_CQ:C_CQ + Q_RANK], gq_ref[...]).astype(BF16)
    q2 = _dot_nt(wqt_ref[...], cqn)
    hw = N_HEADS * LANES
    for hh in range(N_HEADS):
        qa = q2[hh * LANES:(hh + 1) * LANES, :]
        qb = q2[hw + hh * D_ROPE:hw + (hh + 1) * D_ROPE, :]
        r0 = hh * LANES
        qt_ref[0, r0:r0 + D_NOPE, :] = (qa[:D_NOPE] * scale).astype(BF16)
        for part in range(2):
            lo = D_NOPE + part * half
            qt_ref[0, r0 + lo:r0 + lo + half, :] = (
                qa[lo:lo + half] * cos_q + qb[part * half:(part + 1) * half] * sin_q).astype(BF16)
        qt_ref[0, r0 + D_QK:r0 + LANES, :] = jnp.zeros((LANES - D_QK, tm), BF16)

    ckvn = _rms(proj[:, C_CKV:C_CKV + KV_RANK], gkv_ref[...]).astype(BF16)
    tabk = jnp.transpose(jnp.concatenate([cos, cos, sin, sin, jnp.zeros((LANES - 2 * D_ROPE, tm), F32)], axis=0))
    kpe = (proj[:, C_KPE:C_KPE + LANES] * tabk).astype(BF16)
    k_ref[...] = (_dot(ckvn, wk_ref[...]) + _dot(kpe, e2_ref[...])).astype(BF16)
    vt_ref[0] = _dot_nt(wvt_ref[...], ckvn).astype(BF16)

    a = proj[:, C_CONV:C_CONV + CONV_CH]
    g = proj[:, C_CONV + CONV_CH:C_CONV + 2 * CONV_CH]
    u_ref[...] = a * jax.nn.sigmoid(g)
    gates_ref[...] = jax.nn.sigmoid(proj[:, C_GATE:] + bgate_ref[...]).astype(BF16)


def _split_specs(block, n_first, n_second):
    first = lambda i: jnp.minimum(i, n_first - 1)
    second = lambda i: jnp.clip(i - n_first, 0, n_second - 1)
    rest = (0,) * (len(block) - 1)
    return [pl.BlockSpec(block, lambda i: (first(i),) + rest), pl.BlockSpec(block, lambda i: (second(i),) + rest)]


def _in_proj(xa, xb, inv_b, g_mix, w_in_ext, g_q, wq_t, g_kv, wk_ext, wv_t, e2, b_gate, *, tm, seq_starts):
    na, nb = xa.shape[0] // tm, xb.shape[0] // tm
    nt = na + nb
    t = nt * tm
    row = lambda w: pl.BlockSpec((tm, w), lambda i: (i, 0))
    col = lambda h: pl.BlockSpec((h, tm), lambda i: (0, i))
    return pl.pallas_call(
        functools.partial(_in_proj_kernel, tm=tm, n_first=na, seq_starts=seq_starts),
        grid=(nt,),
        in_specs=_split_specs((tm, D_MODEL), na, nb) + [
                  _const_spec((D_ROPE // 2, tm)), _const_spec((1, D_MODEL)),
                  _const_spec((D_MODEL, D_IN_EXT)), _const_spec((1, Q_RANK)),
                  _const_spec((N_HEADS * (LANES + D_ROPE), Q_RANK)), _const_spec((1, KV_RANK)),
                  _const_spec((KV_RANK, N_HEADS * LANES)), _const_spec((N_HEADS * D_V, KV_RANK)),
                  _const_spec((LANES, N_HEADS * LANES)), _const_spec((1, 2 * D_MODEL))],
        out_specs=(pl.BlockSpec((1, N_HEADS * LANES, tm), lambda i: (i, 0, 0)), row(N_HEADS * LANES),
                   pl.BlockSpec((1, N_HEADS * D_V, tm), lambda i: (i, 0, 0)), row(CONV_CH), row(2 * D_MODEL)),
        out_shape=(jax.ShapeDtypeStruct((nt, N_HEADS * LANES, tm), BF16),
                   jax.ShapeDtypeStruct((t, N_HEADS * LANES), BF16),
                   jax.ShapeDtypeStruct((nt, N_HEADS * D_V, tm), BF16),
                   jax.ShapeDtypeStruct((t, CONV_CH), F32),
                   jax.ShapeDtypeStruct((t, 2 * D_MODEL), BF16)),
        compiler_params=pltpu.CompilerParams(dimension_semantics=("arbitrary",), vmem_limit_bytes=VMEM_LIMIT),
        name="in_proj",
    )(xa, xb, inv_b, g_mix, w_in_ext, g_q, wq_t, g_kv, wk_ext, wv_t, e2, b_gate)


def _attn_kernel(qt_ref, k_ref, vt_ref, o_ref, s_ref, acc_ref, m_ref, *, nq, n_chunks, tk, unroll):
    half = n_chunks // 2
    ones = jnp.ones((BF16_ROWS, tk), BF16)

    def scores(a, j, slot):
        ks = pl.multiple_of(j * tk, tk)
        s_ref[slot] = _dot(k_ref[pl.ds(ks, tk), :], qt_ref[a])

    def softmax_pv(j, slot):
        st = s_ref[slot]
        m = jnp.where(j == 0, NEG_BIG, m_ref[...])
        m_new = jnp.maximum(m, jnp.max(st, axis=0, keepdims=True))
        p = jnp.exp2(st - m_new).astype(BF16)
        v1 = jnp.concatenate([vt_ref[j], ones], axis=0)
        acc = jnp.exp2(m - m_new) * acc_ref[...] + _dot(v1, p)
        acc_ref[...] = acc
        m_ref[...] = m_new
        return acc

    m_ref[...] = jnp.full(m_ref.shape, NEG_BIG, F32)
    acc_ref[...] = jnp.zeros(acc_ref.shape, F32)
    scores(0, 0, 0)

    def pair(pp, may_end_tile):
        a = lax.div(pp, half)
        j = 2 * lax.rem(pp, half)
        scores(a, j + 1, 1)
        softmax_pv(j, 0)
        nxt = pp + 1
        scores(jnp.minimum(lax.div(nxt, half), nq - 1), 2 * lax.rem(nxt, half), 0)
        acc = softmax_pv(j + 1, 1)
        if may_end_tile:
            o_ref[a] = (acc[:D_V] / acc[D_V:D_V + 1]).astype(o_ref.dtype)

    period = math.gcd(half, unroll)

    def trip(blk, c):
        for u in range(unroll):
            pair(blk * unroll + u, (u + 1) % period == 0)
        return c

    lax.fori_loop(0, nq * half // unroll, trip, 0)


def _attention(qt3, k, vt3, *, seq_start, seq_len, n_seq, tq, tk):
    nq = seq_len // tq
    nkc = seq_len // tk
    assert nkc % 2 == 0, "key chunks are processed in pairs"
    s0 = seq_start // seq_len
    n_pairs = nq * (nkc // 2)
    unroll = next(u for u in (32, 16, 8, 4, 2, 1) if n_pairs % u == 0)
    return pl.pallas_call(
        functools.partial(_attn_kernel, nq=nq, n_chunks=nkc, tk=tk, unroll=unroll),
        grid=(n_seq, N_HEADS),
        in_specs=[pl.BlockSpec((nq, LANES, tq), lambda b, h: (s0 + b, h, 0)),
                  pl.BlockSpec((seq_len, LANES), lambda b, h: (s0 + b, h)),
                  pl.BlockSpec((nkc, D_V, tk), lambda b, h: (s0 + b, h, 0))],
        out_specs=pl.BlockSpec((nq, D_V, tq), lambda b, h: (b, h, 0)),
        out_shape=jax.ShapeDtypeStruct((n_seq * nq, N_HEADS * D_V, tq), BF16),
        scratch_shapes=[pltpu.VMEM((2, tk, tq), F32), pltpu.VMEM((D_V + BF16_ROWS, tq), F32),
                        pltpu.VMEM((1, tq), F32)],
        compiler_params=pltpu.CompilerParams(dimension_semantics=("arbitrary", "arbitrary"),
                                             vmem_limit_bytes=VMEM_LIMIT),
        name="attention",
    )(qt3, k, vt3)


def _local_rows(tm):
    return TOP_K * tm + N_EXPERTS * RUN_PAD


def _sorted_rows(t, nt, bm):
    runs = t * TOP_K + nt * N_EXPERTS * (RUN_PAD - 1)
    return -(-runs // bm) * bm + N_EXPERTS * bm


def _rows_copy(src, s, dst, d, sem, rows=RUN_PAD):
    s = pl.multiple_of(s, RUN_PAD)
    d = pl.multiple_of(d, RUN_PAD)
    return pltpu.make_async_copy(src.at[pl.ds(s, rows), :], dst.at[pl.ds(d, rows), :], sem)


def _piece_caps(n_loc):
    return n_loc // BIG_PIECE, N_EXPERTS * (BIG_PIECE // RUN_PAD - 1)


def _piece_tables(n_pieces, off, run_row, *, n_loc):
    per_big = BIG_PIECE // RUN_PAD
    qb, qs = _piece_caps(n_loc)
    n_big = n_pieces // per_big
    n_sm = n_pieces - n_big * per_big

    def table(counts, loc0, row0, step, cap):
        end = jnp.cumsum(counts, axis=1)
        q = jnp.arange(cap, dtype=I32)
        ex = jnp.minimum(jnp.sum((end[:, None, :] <= q[None, :, None]).astype(I32), axis=2), N_EXPERTS - 1)
        onehot = ex[..., None] == jnp.arange(N_EXPERTS, dtype=I32)
        pick = lambda v: jnp.sum(jnp.where(onehot, v[:, None, :], 0), axis=2)
        idx = q[None, :] - pick(end - counts)
        return (pick(loc0) + idx * step).reshape(-1), (pick(row0) + idx * step).reshape(-1), end[:, -1]

    loc_b, row_b, cnt_b = table(n_big, off, run_row, BIG_PIECE, qb)
    done = n_big * BIG_PIECE
    loc_s, row_s, cnt_s = table(n_sm, off + done, run_row + done, RUN_PAD, qs)
    return jnp.stack([cnt_b, cnt_s], axis=1).reshape(-1), loc_b, row_b, loc_s, row_s


def _for_each_piece(tables, tile, n_loc, fn):
    cnt, loc_b, row_b, loc_s, row_s = tables
    qb, qs = _piece_caps(n_loc)

    def big(q, c):
        fn(loc_b[tile * qb + q], row_b[tile * qb + q], BIG_PIECE)
        return c

    def small(q, c):
        fn(loc_s[tile * qs + q], row_s[tile * qs + q], RUN_PAD)
        return c

    lax.fori_loop(0, cnt[2 * tile], big, 0)
    lax.fori_loop(0, cnt[2 * tile + 1], small, 0)


def _wait_rows(src, dst, sem, count, rows):
    def one(q, c):
        _rows_copy(src, 0, dst, 0, sem, rows=rows).wait()
        return c

    lax.fori_loop(0, count, one, 0)


def _wait_tile(src, dst, sem, cnt, tile):
    _wait_rows(src, dst, sem, cnt[2 * tile], BIG_PIECE)
    _wait_rows(src, dst, sem, cnt[2 * tile + 1], RUN_PAD)


def _post_mix_kernel(xa_ref, xb_ref, ota_ref, otb_ref, u_ref, up_ref, un_ref, gates_ref, woa_ref, cw_ref, cb_ref,
                     lng_ref, lnb_ref, wpb_ref, bpb_ref, wout_ref, gffn_ref, wr_ref, br_ref, ltri_ref, utri_ref,
                     x1_ref, route_ref, meta_ref, ext_ref, cv_ref, ph_ref, carry_ref, *, tm, n_first, seq_bounds):
    i = pl.program_id(0)
    t0 = i * tm
    is_start = functools.reduce(jnp.logical_or, [t0 == s for s, _ in seq_bounds])
    is_end = functools.reduce(jnp.logical_or, [t0 + tm == e for _, e in seq_bounds])

    ext_ref[0:HALO_ROWS, :] = jnp.where(is_start, 0.0, up_ref[...])
    ext_ref[HALO_ROWS:HALO_ROWS + tm, :] = u_ref[...]
    ext_ref[HALO_ROWS + tm:2 * HALO_ROWS + tm, :] = jnp.where(is_end, 0.0, un_ref[...])
    rc = min(tm, CONV_ROWS)
    sub = SUBLANES
    base = HALO_ROWS - CONV_PAD
    for c in range(tm // rc):
        acc = None
        for b in range(sub):
            part = None
            for o in range(CONV_WIDTH):
                if (o + base) % sub != b:
                    continue
                a0 = c * rc + (o + base) - b
                term = cw_ref[o:o + 1, :] * ext_ref[a0:a0 + rc + sub, :]
                part = term if part is None else part + term
            if b == 0:
                shifted = part[:rc]
            else:
                ph_ref[b] = part
                shifted = ph_ref[b, b:b + rc, :]
            acc = shifted if acc is None else acc + shifted
        cv_ref[c * rc:(c + 1) * rc, :] = acc + cb_ref[...]

    cv = cv_ref[...]
    mu = jnp.mean(cv, axis=-1, keepdims=True)
    xc = cv - mu
    y = xc * lax.rsqrt(jnp.mean(xc * xc, axis=-1, keepdims=True) + LN_EPS) * lng_ref[...] + lnb_ref[...]
    y = y * jax.nn.sigmoid(y)
    branch_b = _dot(y.astype(BF16), wpb_ref[...]) + bpb_ref[...]
    branch_a = _dot_tn(_tile_select(i, n_first, ota_ref, otb_ref)[0], woa_ref[...])
    gates = gates_ref[...]
    mix = gates[:, :D_MODEL].astype(F32) * branch_a + gates[:, D_MODEL:].astype(F32) * branch_b
    x1 = _tile_select(i, n_first, xa_ref, xb_ref) + _dot(mix.astype(BF16), wout_ref[...])
    x1_ref[...] = x1

    h2 = _rms(x1, gffn_ref[...])
    h2_hi = h2.astype(BF16)
    h2_lo = (h2 - h2_hi.astype(F32)).astype(BF16)
    hi_both = _dot(h2_hi, wr_ref[...])
    logits = hi_both[:, :LANES] + (hi_both[:, LANES:] + _dot(h2_lo, wr_ref[:, :LANES])) + br_ref[...]
    lane = lax.broadcasted_iota(I32, logits.shape, 1)
    lane_f = lane.astype(F32)
    vals, idxs, sels = [], [], []
    cur = logits
    for _ in range(TOP_K):
        mval = jnp.max(cur, axis=-1, keepdims=True)
        ik = jnp.min(jnp.where(cur == mval, lane_f, float(LANES)), axis=-1, keepdims=True)
        sel = lane_f == ik
        cur = jnp.where(sel, -jnp.inf, cur)
        vals.append(mval)
        idxs.append(ik)
        sels.append(sel)
    exps = [jnp.exp(v - vals[0]) for v in vals]
    denom = exps[0] + exps[1] + exps[2] + exps[3]

    onehot = sels[0].astype(F32) + sels[1].astype(F32) + sels[2].astype(F32) + sels[3].astype(F32)

    @pl.when(i == 0)
    def _():
        carry_ref[...] = jnp.zeros_like(carry_ref)

    n_e = jnp.sum(onehot, axis=0, keepdims=True)
    n_pieces = jnp.floor((n_e + (RUN_PAD - 1)) * (1.0 / RUN_PAD))
    pieces_b = jnp.broadcast_to(n_pieces, (SUBLANES, LANES)).astype(BF16)
    off = _dot(pieces_b, utri_ref[...])[0:1] * RUN_PAD
    total_pieces = jnp.sum(n_pieces, axis=-1, keepdims=True)
    carry = carry_ref[...]
    prefix = _dot(ltri_ref[...], onehot.astype(BF16))
    local = off + prefix
    route = jnp.zeros(logits.shape, F32)
    for kk in range(TOP_K):
        pos = jnp.sum(jnp.where(sels[kk], local, 0.0), axis=-1, keepdims=True)
        route = jnp.where(lane == kk, idxs[kk], route)
        route = jnp.where(lane == TOP_K + kk, exps[kk] / denom, route)
        route = jnp.where(lane == 2 * TOP_K + kk, pos, route)
    route_ref[...] = route
    carry_ref[...] = carry + n_pieces * RUN_PAD

    row8 = lax.broadcasted_iota(I32, (SUBLANES, LANES), 0)
    metav = jnp.where(row8 == 0, jnp.broadcast_to(carry, (SUBLANES, LANES)), 0.0)
    metav = jnp.where(row8 == 1, jnp.broadcast_to(n_pieces, (SUBLANES, LANES)), metav)
    metav = jnp.where(row8 == 2, jnp.broadcast_to(off, (SUBLANES, LANES)), metav)
    metav = jnp.where(row8 == 3, jnp.broadcast_to(total_pieces, (SUBLANES, LANES)), metav)
    meta_ref[0] = metav.astype(I32)


def _post_mix(xa, xb, ota, otb, u, gates, w_oa, conv_w, conv_b, ln_g, ln_b, w_pb, b_pb, w_out, g_ffn, w_r, b_r, ltri,
              utri, *, tm, seq_bounds):
    na, nb = xa.shape[0] // tm, xb.shape[0] // tm
    nt = na + nb
    t = nt * tm
    hb = tm // HALO_ROWS
    n_halo = t // HALO_ROWS
    row = lambda w: pl.BlockSpec((tm, w), lambda i: (i, 0))
    return pl.pallas_call(
        functools.partial(_post_mix_kernel, tm=tm, n_first=na, seq_bounds=seq_bounds),
        grid=(nt,),
        in_specs=_split_specs((tm, D_MODEL), na, nb) + _split_specs((1, N_HEADS * D_V, tm), na, nb) + [
                  row(CONV_CH),
                  pl.BlockSpec((HALO_ROWS, CONV_CH), lambda i: (jnp.maximum(i * hb - 1, 0), 0)),
                  pl.BlockSpec((HALO_ROWS, CONV_CH), lambda i: (jnp.minimum((i + 1) * hb, n_halo - 1), 0)),
                  row(2 * D_MODEL), _const_spec((N_HEADS * D_V, D_MODEL)), _const_spec((CONV_WIDTH, CONV_CH)),
                  _const_spec((1, CONV_CH)), _const_spec((1, CONV_CH)), _const_spec((1, CONV_CH)),
                  _const_spec((CONV_CH, D_MODEL)), _const_spec((1, D_MODEL)), _const_spec((D_MODEL, D_MODEL)),
                  _const_spec((1, D_MODEL)), _const_spec((D_MODEL, 2 * LANES)), _const_spec((1, LANES)),
                  _const_spec((tm, tm)), _const_spec((LANES, LANES))],
        out_specs=(row(D_MODEL), row(LANES), pl.BlockSpec((1, SUBLANES, LANES), lambda i: (i, 0, 0))),
        out_shape=(jax.ShapeDtypeStruct((t, D_MODEL), F32), jax.ShapeDtypeStruct((t, LANES), F32),
                   jax.ShapeDtypeStruct((nt, SUBLANES, LANES), I32)),
        scratch_shapes=[pltpu.VMEM((tm + 2 * HALO_ROWS, CONV_CH), F32), pltpu.VMEM((tm, CONV_CH), F32),
                        pltpu.VMEM((SUBLANES, min(tm, CONV_ROWS) + SUBLANES, CONV_CH), F32), pltpu.VMEM((1, LANES), F32)],
        compiler_params=pltpu.CompilerParams(dimension_semantics=("arbitrary",), vmem_limit_bytes=VMEM_LIMIT),
        name="post_mix",
    )(xa, xb, ota, otb, u, u, u, gates, w_oa, conv_w, conv_b, ln_g, ln_b, w_pb, b_pb, w_out, g_ffn, w_r, b_r, ltri, utri)


def _dispatch_kernel(cnt_s, locb_s, rowb_s, locs_s, rows_s, start_s, end_s, x1_ref, route_ref, gffn_ref, xs_hbm,
                     xloc_ref, zero_ref, row_sems, *, tm, nt, bm, n_rows):
    tables = (cnt_s, locb_s, rowb_s, locs_s, rows_s)
    i = pl.program_id(0)
    slot = lax.rem(i, N_DISPATCH_BUFS)
    n_loc = _local_rows(tm)

    h2 = _rms(x1_ref[...], gffn_ref[...]).astype(BF16)
    pos_t = jnp.transpose(route_ref[...])[2 * TOP_K:3 * TOP_K, :].astype(I32)
    prow = lax.broadcasted_iota(I32, (n_loc, tm), 0)
    perm = jnp.zeros((n_loc, tm), F32)
    for kk in range(TOP_K):
        perm = jnp.where(prow == pos_t[kk:kk + 1, :], 1.0, perm)
    xloc_ref[slot] = _pack_rows(_dot(perm.astype(BF16), h2))

    def wait_tile(tile):
        _wait_tile(xloc_ref.at[0], xs_hbm, row_sems.at[lax.rem(tile, N_DISPATCH_BUFS)], cnt_s, tile)

    @pl.when(i >= N_DISPATCH_BUFS - 1)
    def _():
        wait_tile(i - (N_DISPATCH_BUFS - 1))

    _for_each_piece(tables, i, n_loc,
                    lambda s, d, rows: _rows_copy(xloc_ref.at[slot], s, xs_hbm, d, row_sems.at[slot], rows=rows).start())

    @pl.when(i == nt - 1)
    def _():
        for back in range(N_DISPATCH_BUFS - 2, -1, -1):
            @pl.when(i - back >= 0)
            def _():
                wait_tile(i - back)
        row_sem = row_sems.at[0]
        zero_ref[...] = jnp.zeros(zero_ref.shape, U32)

        def tail(e, c):
            end = end_s[e]
            n = lax.shift_right_logical(start_s[e + 1] - end, int(math.log2(RUN_PAD)))

            def piece(q, c2):
                _rows_copy(zero_ref, 0, xs_hbm, end + q * RUN_PAD, row_sem).start()
                return c2

            lax.fori_loop(0, n, piece, 0)
            return c + n

        _wait_rows(zero_ref, xs_hbm, row_sem, lax.fori_loop(0, N_EXPERTS, tail, 0), RUN_PAD)
        used = start_s[N_EXPERTS]
        n_free = (n_rows - used) // bm

        def fill(b, c):
            _rows_copy(zero_ref, 0, xs_hbm, used + b * bm, row_sem, rows=bm).start()
            return c

        lax.fori_loop(0, n_free, fill, 0)
        _wait_rows(zero_ref, xs_hbm, row_sem, n_free, bm)


def _dispatch(tables, start, end, x1, route, g_ffn, *, tm, bm, n_rows):
    t = x1.shape[0]
    nt = t // tm
    return pl.pallas_call(
        functools.partial(_dispatch_kernel, tm=tm, nt=nt, bm=bm, n_rows=n_rows),
        grid_spec=pltpu.PrefetchScalarGridSpec(
            num_scalar_prefetch=7,
            grid=(nt,),
            in_specs=[pl.BlockSpec((tm, D_MODEL), lambda i, *_: (i, 0)), pl.BlockSpec((tm, LANES), lambda i, *_: (i, 0)),
                      pl.BlockSpec((1, D_MODEL), lambda i, *_: (0, 0))],
            out_specs=pl.BlockSpec(memory_space=pl.ANY),
            scratch_shapes=[pltpu.VMEM((N_DISPATCH_BUFS, _local_rows(tm), PACKED), U32),
                            pltpu.VMEM((bm, PACKED), U32), pltpu.SemaphoreType.DMA((N_DISPATCH_BUFS,))],
        ),
        out_shape=jax.ShapeDtypeStruct((n_rows, PACKED), U32),
        compiler_params=pltpu.CompilerParams(dimension_semantics=("arbitrary",), vmem_limit_bytes=VMEM_LIMIT,
                                             has_side_effects=True),
        name="dispatch",
    )(*tables, start, end, x1, route, g_ffn)


def _moe_kernel(exp_ref, n_ref, xs_ref, wgu_ref, bgu_ref, wd_ref, bd_ref, y_ref, wgu_bf, wd_bf):
    w = pl.program_id(0)
    used = w < n_ref[0]

    @pl.when(jnp.logical_or(w == 0, exp_ref[w] != exp_ref[jnp.maximum(w - 1, 0)]))
    def _():
        wgu_bf[...] = wgu_ref[0].astype(BF16)
        wd_bf[...] = wd_ref[0].astype(BF16)

    @pl.when(used)
    def _():
        x = _unpack_rows(xs_ref[...])
        gu = _dot(x, wgu_bf[...]) + bgu_ref[0]
        g = jnp.minimum(gu[:, :D_FF], SWIGLU_LIMIT)
        u = jnp.clip(gu[:, D_FF:], -SWIGLU_LIMIT, SWIGLU_LIMIT)
        act = (u + 1.0) * (g * jax.nn.sigmoid(SWIGLU_ALPHA * g))
        y_ref[...] = _pack_rows(_dot(act.astype(BF16), wd_bf[...]) + bd_ref[0])

    @pl.when(jnp.logical_not(used))
    def _():
        y_ref[...] = jnp.zeros(y_ref.shape, U32)


def _moe(ex, n_used, xs, w_gu, b_gu, w_down, b_down, *, bm):
    return pl.pallas_call(
        _moe_kernel,
        grid_spec=pltpu.PrefetchScalarGridSpec(
            num_scalar_prefetch=2,
            grid=(xs.shape[0] // bm,),
            in_specs=[pl.BlockSpec((bm, PACKED), lambda w, ex, n: (w, 0)),
                      pl.BlockSpec((1, D_MODEL, 2 * D_FF), lambda w, ex, n: (ex[w], 0, 0)),
                      pl.BlockSpec((1, 1, 2 * D_FF), lambda w, ex, n: (ex[w], 0, 0)),
                      pl.BlockSpec((1, D_FF, D_MODEL), lambda w, ex, n: (ex[w], 0, 0)),
                      pl.BlockSpec((1, 1, D_MODEL), lambda w, ex, n: (ex[w], 0, 0))],
            out_specs=pl.BlockSpec((bm, PACKED), lambda w, ex, n: (w, 0)),
            scratch_shapes=[pltpu.VMEM((D_MODEL, 2 * D_FF), BF16), pltpu.VMEM((D_FF, D_MODEL), BF16)],
        ),
        out_shape=jax.ShapeDtypeStruct(xs.shape, U32),
        compiler_params=pltpu.CompilerParams(dimension_semantics=("arbitrary",), vmem_limit_bytes=VMEM_LIMIT),
        name="moe",
    )(ex, n_used, xs, w_gu, b_gu, w_down, b_down)


def _expert_layout(totals, *, bm, n_blocks):
    n_blk = (totals + bm - 1) // bm
    blk_end = jnp.cumsum(n_blk)
    start = jnp.concatenate([jnp.zeros((1,), I32), blk_end * bm]).astype(I32)
    w = jnp.arange(n_blocks, dtype=I32)
    ex = jnp.minimum(jnp.sum((blk_end[None, :] <= w[:, None]).astype(I32), axis=1), N_EXPERTS - 1)
    return start, ex.astype(I32), blk_end[-1].astype(I32)[None]


def _combine_kernel(cnt_s, locb_s, rowb_s, locs_s, rows_s, x1_ref, route_ref, gfin_ref, y_hbm, outa_ref, outb_ref,
                    yloc_ref, sems, *, tm, nt, n_first):
    tables = (cnt_s, locb_s, rowb_s, locs_s, rows_s)
    i = pl.program_id(0)
    slot = lax.rem(i, 2)
    n_loc = _local_rows(tm)

    def gather(tile, sl):
        _for_each_piece(tables, tile, n_loc,
                        lambda loc, row, rows: _rows_copy(y_hbm, row, yloc_ref.at[sl], loc, sems.at[sl],
                                                          rows=rows).start())

    @pl.when(i == 0)
    def _():
        yloc_ref[...] = jnp.zeros(yloc_ref.shape, U32)
        gather(0, 0)

    @pl.when(i + 1 < nt)
    def _():
        gather(i + 1, 1 - slot)

    _wait_tile(y_hbm, yloc_ref.at[slot], sems.at[slot], cnt_s, i)

    route = route_ref[...]
    lanes = lax.broadcasted_iota(I32, (tm, n_loc), 1)
    pw = jnp.zeros((tm, n_loc), F32)
    for kk in range(TOP_K):
        pos = route[:, 2 * TOP_K + kk:2 * TOP_K + kk + 1].astype(I32)
        pw = jnp.where(lanes == pos, route[:, TOP_K + kk:TOP_K + kk + 1], pw)
    moe = _dot(pw.astype(BF16), _unpack_rows(yloc_ref[slot]))
    res = _rms(x1_ref[...] + moe, gfin_ref[...])

    @pl.when(i < n_first)
    def _():
        outa_ref[...] = res

    @pl.when(i >= n_first)
    def _():
        outb_ref[...] = res


def _combine(tables, x1, route, g_final, y, *, tm, n_first):
    t = x1.shape[0]
    nt = t // tm
    n_second = nt - n_first
    return pl.pallas_call(
        functools.partial(_combine_kernel, tm=tm, nt=nt, n_first=n_first),
        grid_spec=pltpu.PrefetchScalarGridSpec(
            num_scalar_prefetch=5,
            grid=(nt,),
            in_specs=[pl.BlockSpec((tm, D_MODEL), lambda i, *_: (i, 0)), pl.BlockSpec((tm, LANES), lambda i, *_: (i, 0)),
                      pl.BlockSpec((1, D_MODEL), lambda i, *_: (0, 0)), pl.BlockSpec(memory_space=pl.ANY)],
            out_specs=(pl.BlockSpec((tm, D_MODEL), lambda i, *_: (jnp.minimum(i, n_first - 1), 0)),
                       pl.BlockSpec((tm, D_MODEL), lambda i, *_: (jnp.clip(i - n_first, 0, n_second - 1), 0))),
            scratch_shapes=[pltpu.VMEM((2, _local_rows(tm), PACKED), U32), pltpu.SemaphoreType.DMA((2,))],
        ),
        out_shape=(jax.ShapeDtypeStruct((n_first * tm, D_MODEL), F32),
                   jax.ShapeDtypeStruct((n_second * tm, D_MODEL), F32)),
        compiler_params=pltpu.CompilerParams(dimension_semantics=("arbitrary",), vmem_limit_bytes=VMEM_LIMIT),
        name="combine",
    )(*tables, x1, route, g_final, y)


def _rot_cols(w):
    half = w.shape[-1] // 2
    return jnp.concatenate([-w[..., half:], w[..., :half]], axis=-1)


def _prep_weights(w_in, w_uq, w_ukv):
    d = w_in.shape[0]
    w_kpe = w_in[:, C_KPE:C_KPE + D_ROPE]
    w_in_ext = jnp.concatenate(
        [w_in[:, :C_KPE], w_kpe, _rot_cols(w_kpe), jnp.zeros((d, LANES - 2 * D_ROPE), F32),
         w_in[:, C_KPE + D_ROPE:]], axis=1).astype(BF16)

    wq = w_uq.reshape(Q_RANK, N_HEADS, D_QK)
    nope, pe = wq[..., :D_NOPE], wq[..., D_NOPE:]
    zpad = jnp.zeros((Q_RANK, N_HEADS, LANES - D_QK), F32)
    plain = jnp.concatenate([nope, pe, zpad], axis=-1).reshape(Q_RANK, N_HEADS * LANES)
    rot = _rot_cols(pe).reshape(Q_RANK, N_HEADS * D_ROPE)
    wq_t = jnp.concatenate([plain, rot], axis=1).T.astype(BF16)

    wkv = w_ukv.reshape(KV_RANK, N_HEADS, D_NOPE + D_V)
    wk_ext = jnp.concatenate([wkv[..., :D_NOPE], jnp.zeros((KV_RANK, N_HEADS, LANES - D_NOPE), F32)],
                             axis=-1).reshape(KV_RANK, N_HEADS * LANES).astype(BF16)
    wv_t = wkv[..., D_NOPE:].reshape(KV_RANK, N_HEADS * D_V).T.astype(BF16)
    return w_in_ext, wq_t, wk_ext, wv_t


def _kpe_placement():
    e2 = np.zeros((LANES, N_HEADS * LANES), np.float32)
    for j in range(D_ROPE):
        for h in range(N_HEADS):
            e2[j, h * LANES + D_NOPE + j] = 1.0
            e2[D_ROPE + j, h * LANES + D_NOPE + j] = 1.0
    return jnp.asarray(e2, BF16)


def _inv_freq(tm):
    inv = 1.0 / (ROPE_THETA ** (jnp.arange(0, D_ROPE, 2, dtype=F32) / D_ROPE))
    return jnp.broadcast_to(inv[:, None], (D_ROPE // 2, tm))


def _token_tile(s1, s2):
    g = math.gcd(s1, s2)
    return TOKEN_TILE if g % TOKEN_TILE == 0 else g


def kernel(x_prompt, x_sample, g_mix, w_in, b_gate, g_q, w_uq, g_kv, w_ukv, w_oa, conv_w, conv_b, ln_g, ln_b, w_pb,
           b_pb, w_out, g_ffn, w_router, b_router, w_gu, b_gu, w_down, b_down, g_final):
    b1, s1, d = x_prompt.shape
    b2, s2, _ = x_sample.shape
    t1, t2 = b1 * s1, b2 * s2
    t = t1 + t2
    xa, xb = x_prompt.reshape(t1, d), x_sample.reshape(t2, d)
    seq_bounds = tuple((b * s1, (b + 1) * s1) for b in range(b1)) + tuple(
        (t1 + b * s2, t1 + (b + 1) * s2) for b in range(b2))

    tm = _token_tile(s1, s2)
    nt = t // tm
    w_in_ext, wq_t, wk_ext, wv_t = _prep_weights(w_in[0], w_uq[0], w_ukv[0])

    qt, k, vt3, u, gates = _in_proj(xa, xb, _inv_freq(tm), g_mix, w_in_ext, g_q, wq_t, g_kv, wk_ext, wv_t,
                                    _kpe_placement(), b_gate, tm=tm, seq_starts=tuple(s for s, _ in seq_bounds))

    ota = _attention(qt, k, vt3, seq_start=0, seq_len=s1, n_seq=b1, tq=tm, tk=tm)
    otb = _attention(qt, k, vt3, seq_start=t1, seq_len=s2, n_seq=b2, tq=tm, tk=tm)

    w_r = jnp.concatenate([w_router[0], jnp.zeros((d, LANES - N_EXPERTS), F32)], axis=1)
    w_r_hi = w_r.astype(BF16)
    w_r = jnp.concatenate([w_r_hi, (w_r - w_r_hi.astype(F32)).astype(BF16)], axis=1)
    b_r = jnp.concatenate([b_router[0], jnp.full((LANES - N_EXPERTS,), NEG_BIG, F32)])[None, :]
    ltri = jnp.asarray(np.tril(np.ones((tm, tm), np.float32), -1), BF16)
    utri = jnp.asarray(np.triu(np.ones((LANES, LANES), np.float32), 1), BF16)
    x1, route, meta = _post_mix(xa, xb, ota, otb, u, gates, w_oa[0].astype(BF16), conv_w[0], conv_b, ln_g, ln_b,
                                w_pb[0].astype(BF16), b_pb, w_out[0].astype(BF16), g_ffn, w_r, b_r, ltri, utri,
                                tm=tm, seq_bounds=seq_bounds)

    bm = MOE_ROWS
    n_rows = _sorted_rows(t, nt, bm)
    carry, n_pieces, off = (meta[:, r, :N_EXPERTS] for r in range(3))
    start, ex, n_used = _expert_layout(carry[-1] + n_pieces[-1] * RUN_PAD, bm=bm, n_blocks=n_rows // bm)
    run_row = carry + start[None, :N_EXPERTS]
    tables = _piece_tables(n_pieces, off, run_row, n_loc=_local_rows(tm))
    run_end = run_row[-1] + n_pieces[-1] * RUN_PAD

    xs = _dispatch(tables, start, run_end, x1, route, g_ffn, tm=tm, bm=bm, n_rows=n_rows)
    y = _moe(ex, n_used, xs, w_gu[0], b_gu[0][:, None, :], w_down[0], b_down[0][:, None, :], bm=bm)
    out_a, out_b = _combine(tables, x1, route, g_final[None, :], y, tm=tm, n_first=t1 // tm)
    return out_a.reshape(b1, s1, d), out_b.reshape(b2, s2, d)
```

```python
import functools
import math

import numpy as np
import jax
import jax.numpy as jnp
from jax import lax
from jax.experimental import pallas as pl
from jax.experimental.pallas import tpu as pltpu

F32 = jnp.float32
BF16 = jnp.bfloat16
I32 = jnp.int32
U32 = jnp.uint32

D_MODEL = 1024
N_HEADS = 8
Q_RANK = 384
KV_RANK = 256
D_NOPE = 64
D_ROPE = 32
D_V = 64
D_QK = D_NOPE + D_ROPE
ROPE_THETA = 10000.0
CONV_CH = 512
CONV_WIDTH = 31
CONV_PAD = (CONV_WIDTH - 1) // 2
N_EXPERTS = 32
TOP_K = 4
D_FF = 1024
SWIGLU_ALPHA = 1.702
SWIGLU_LIMIT = 7.0
RMS_EPS = 1e-6
LN_EPS = 1e-5

LANES = 128
SUBLANES = 8
BF16_ROWS = 16
TOKEN_TILE = 512
HALO_ROWS = 16
NEG_BIG = -1e30
RUN_PAD = SUBLANES
BIG_PIECE = 32
N_DISPATCH_BUFS = 3
MOE_ROWS = 512

C_CQ = 0
C_CKV = Q_RANK
C_KPE = Q_RANK + KV_RANK
C_CONV = C_KPE + LANES
C_GATE = C_CONV + 2 * CONV_CH
D_IN_EXT = C_GATE + 2 * D_MODEL

VMEM_LIMIT = 56 * 1024 * 1024


def _rms(x, g):
    return x * lax.rsqrt(jnp.mean(x * x, axis=-1, keepdims=True) + RMS_EPS) * g


def _dot(a, b):
    return jnp.dot(a, b, preferred_element_type=F32)


def _dot_nt(a, b):
    return lax.dot_general(a, b, (((1,), (1,)), ((), ())), preferred_element_type=F32)


def _dot_tn(a, b):
    return lax.dot_general(a, b, (((0,), (0,)), ((), ())), preferred_element_type=F32)


def _const_spec(shape):
    return pl.BlockSpec(shape, lambda *_: (0,) * len(shape))


PACKED = D_MODEL // 2
HIGH_HALF = 0xFFFF0000


def _pack_rows(v):
    bits = lambda x: pltpu.bitcast(x.astype(BF16).astype(F32), U32)
    return bits(v[:, :PACKED]) | lax.shift_right_logical(bits(v[:, PACKED:]), jnp.uint32(16))


def _unpack_rows(w):
    hi = pltpu.bitcast(w & jnp.uint32(HIGH_HALF), F32).astype(BF16)
    lo = pltpu.bitcast(lax.shift_left(w, jnp.uint32(16)), F32).astype(BF16)
    return jnp.concatenate([hi, lo], axis=1)


def _tile_select(i, n_first, a_ref, b_ref):
    return jnp.where(i < n_first, a_ref[...], b_ref[...])


def _in_proj_kernel(xa_ref, xb_ref, inv_ref, gmix_ref, win_ref, gq_ref, wqt_ref, gkv_ref, wk_ref, wvt_ref,
                    e2_ref, bgate_ref, qt_ref, k_ref, vt_ref, u_ref, gates_ref, *, tm, n_first, seq_starts):
    i = pl.program_id(0)
    h = _rms(_tile_select(i, n_first, xa_ref, xb_ref), gmix_ref[...]).astype(BF16)
    proj = _dot(h, win_ref[...])

    t0 = i * tm
    s0 = 0
    for s in seq_starts:
        s0 = jnp.where(t0 >= s, s, s0)
    half = D_ROPE // 2
    pos = (t0 - s0 + lax.broadcasted_iota(I32, (half, tm), 1)).astype(F32)
    ang = pos * inv_ref[...]
    cos, sin = jnp.cos(ang), jnp.sin(ang)
    scale = (D_QK ** -0.5) * math.log2(math.e)
    cos_q, sin_q = cos * scale, sin * scale

    cqn = _rms(proj[:, C_CQ:C_CQ + Q_RANK], gq_ref[...]).astype(BF16)
    q2 = _dot_nt(wqt_ref[...], cqn)
    hw = N_HEADS * LANES
    for hh in range(N_HEADS):
        qa = q2[hh * LANES:(hh + 1) * LANES, :]
        qb = q2[hw + hh * D_ROPE:hw + (hh + 1) * D_ROPE, :]
        r0 = hh * LANES
        qt_ref[0, r0:r0 + D_NOPE, :] = (qa[:D_NOPE] * scale).astype(BF16)
        for part in range(2):
            lo = D_NOPE + part * half
            qt_ref[0, r0 + lo:r0 + lo + half, :] = (
                qa[lo:lo + half] * cos_q + qb[part * half:(part + 1) * half] * sin_q).astype(BF16)
        qt_ref[0, r0 + D_QK:r0 + LANES, :] = jnp.zeros((LANES - D_QK, tm), BF16)

    ckvn = _rms(proj[:, C_CKV:C_CKV + KV_RANK], gkv_ref[...]).astype(BF16)
    tabk = jnp.transpose(jnp.concatenate([cos, cos, sin, sin, jnp.zeros((LANES - 2 * D_ROPE, tm), F32)], axis=0))
    kpe = (proj[:, C_KPE:C_KPE + LANES] * tabk).astype(BF16)
    k_ref[...] = (_dot(ckvn, wk_ref[...]) + _dot(kpe, e2_ref[...])).astype(BF16)
    vt_ref[0] = _dot_nt(wvt_ref[...], ckvn).astype(BF16)

    a = proj[:, C_CONV:C_CONV + CONV_CH]
    g = proj[:, C_CONV + CONV_CH:C_CONV + 2 * CONV_CH]
    u_ref[...] = a * jax.nn.sigmoid(g)
    gates_ref[...] = jax.nn.sigmoid(proj[:, C_GATE:] + bgate_ref[...]).astype(BF16)


def _split_specs(block, n_first, n_second):
    first = lambda i: jnp.minimum(i, n_first - 1)
    second = lambda i: jnp.clip(i - n_first, 0, n_second - 1)
    rest = (0,) * (len(block) - 1)
    return [pl.BlockSpec(block, lambda i: (first(i),) + rest), pl.BlockSpec(block, lambda i: (second(i),) + rest)]


def _in_proj(xa, xb, inv_b, g_mix, w_in_ext, g_q, wq_t, g_kv, wk_ext, wv_t, e2, b_gate, *, tm, seq_starts):
    na, nb = xa.shape[0] // tm, xb.shape[0] // tm
    nt = na + nb
    t = nt * tm
    row = lambda w: pl.BlockSpec((tm, w), lambda i: (i, 0))
    col = lambda h: pl.BlockSpec((h, tm), lambda i: (0, i))
    return pl.pallas_call(
        functools.partial(_in_proj_kernel, tm=tm, n_first=na, seq_starts=seq_starts),
        grid=(nt,),
        in_specs=_split_specs((tm, D_MODEL), na, nb) + [
                  _const_spec((D_ROPE // 2, tm)), _const_spec((1, D_MODEL)),
                  _const_spec((D_MODEL, D_IN_EXT)), _const_spec((1, Q_RANK)),
                  _const_spec((N_HEADS * (LANES + D_ROPE), Q_RANK)), _const_spec((1, KV_RANK)),
                  _const_spec((KV_RANK, N_HEADS * LANES)), _const_spec((N_HEADS * D_V, KV_RANK)),
                  _const_spec((LANES, N_HEADS * LANES)), _const_spec((1, 2 * D_MODEL))],
        out_specs=(pl.BlockSpec((1, N_HEADS * LANES, tm), lambda i: (i, 0, 0)), row(N_HEADS * LANES),
                   pl.BlockSpec((1, N_HEADS * D_V, tm), lambda i: (i, 0, 0)), row(CONV_CH), row(2 * D_MODEL)),
        out_shape=(jax.ShapeDtypeStruct((nt, N_HEADS * LANES, tm), BF16),
                   jax.ShapeDtypeStruct((t, N_HEADS * LANES), BF16),
                   jax.ShapeDtypeStruct((nt, N_HEADS * D_V, tm), BF16),
                   jax.ShapeDtypeStruct((t, CONV_CH), F32),
                   jax.ShapeDtypeStruct((t, 2 * D_MODEL), BF16)),
        compiler_params=pltpu.CompilerParams(dimension_semantics=("arbitrary",), vmem_limit_bytes=VMEM_LIMIT),
        name="in_proj",
    )(xa, xb, inv_b, g_mix, w_in_ext, g_q, wq_t, g_kv, wk_ext, wv_t, e2, b_gate)


def _attn_kernel(qt_ref, k_ref, vt_ref, o_ref, s_ref, acc_ref, m_ref, *, nq, n_chunks, tk, unroll):
    half = n_chunks // 2
    ones = jnp.ones((BF16_ROWS, tk), BF16)

    def scores(a, j, slot):
        ks = pl.multiple_of(j * tk, tk)
        s_ref[slot] = _dot(k_ref[pl.ds(ks, tk), :], qt_ref[a])

    def softmax_pv(j, slot):
        st = s_ref[slot]
        m = jnp.where(j == 0, NEG_BIG, m_ref[...])
        m_new = jnp.maximum(m, jnp.max(st, axis=0, keepdims=True))
        p = jnp.exp2(st - m_new).astype(BF16)
        v1 = jnp.concatenate([vt_ref[j], ones], axis=0)
        acc = jnp.exp2(m - m_new) * acc_ref[...] + _dot(v1, p)
        acc_ref[...] = acc
        m_ref[...] = m_new
        return acc

    m_ref[...] = jnp.full(m_ref.shape, NEG_BIG, F32)
    acc_ref[...] = jnp.zeros(acc_ref.shape, F32)
    scores(0, 0, 0)

    def pair(pp, may_end_tile):
        a = lax.div(pp, half)
        j = 2 * lax.rem(pp, half)
        scores(a, j + 1, 1)
        softmax_pv(j, 0)
        nxt = pp + 1
        scores(jnp.minimum(lax.div(nxt, half), nq - 1), 2 * lax.rem(nxt, half), 0)
        acc = softmax_pv(j + 1, 1)
        if may_end_tile:
            o_ref[a] = (acc[:D_V] / acc[D_V:D_V + 1]).astype(o_ref.dtype)

    period = math.gcd(half, unroll)

    def trip(blk, c):
        for u in range(unroll):
            pair(blk * unroll + u, (u + 1) % period == 0)
        return c

    lax.fori_loop(0, nq * half // unroll, trip, 0)


def _attention(qt3, k, vt3, *, seq_start, seq_len, n_seq, tq, tk):
    nq = seq_len // tq
    nkc = seq_len // tk
    assert nkc % 2 == 0, "key chunks are processed in pairs"
    s0 = seq_start // seq_len
    n_pairs = nq * (nkc // 2)
    unroll = next(u for u in (16, 8, 4, 2, 1) if n_pairs % u == 0)
    return pl.pallas_call(
        functools.partial(_attn_kernel, nq=nq, n_chunks=nkc, tk=tk, unroll=unroll),
        grid=(n_seq, N_HEADS),
        in_specs=[pl.BlockSpec((nq, LANES, tq), lambda b, h: (s0 + b, h, 0)),
                  pl.BlockSpec((seq_len, LANES), lambda b, h: (s0 + b, h)),
                  pl.BlockSpec((nkc, D_V, tk), lambda b, h: (s0 + b, h, 0))],
        out_specs=pl.BlockSpec((nq, D_V, tq), lambda b, h: (b, h, 0)),
        out_shape=jax.ShapeDtypeStruct((n_seq * nq, N_HEADS * D_V, tq), BF16),
        scratch_shapes=[pltpu.VMEM((2, tk, tq), F32), pltpu.VMEM((D_V + BF16_ROWS, tq), F32),
                        pltpu.VMEM((1, tq), F32)],
        compiler_params=pltpu.CompilerParams(dimension_semantics=("arbitrary", "arbitrary"),
                                             vmem_limit_bytes=VMEM_LIMIT),
        name="attention",
    )(qt3, k, vt3)


def _local_rows(tm):
    return TOP_K * tm + N_EXPERTS * RUN_PAD


def _sorted_rows(t, nt, bm):
    runs = t * TOP_K + nt * N_EXPERTS * (RUN_PAD - 1)
    return -(-runs // bm) * bm + N_EXPERTS * bm


def _rows_copy(src, s, dst, d, sem, rows=RUN_PAD):
    s = pl.multiple_of(s, RUN_PAD)
    d = pl.multiple_of(d, RUN_PAD)
    return pltpu.make_async_copy(src.at[pl.ds(s, rows), :], dst.at[pl.ds(d, rows), :], sem)


def _piece_caps(n_loc):
    return n_loc // BIG_PIECE, N_EXPERTS * (BIG_PIECE // RUN_PAD - 1)


def _piece_tables(n_pieces, off, run_row, *, n_loc):
    per_big = BIG_PIECE // RUN_PAD
    qb, qs = _piece_caps(n_loc)
    n_big = n_pieces // per_big
    n_sm = n_pieces - n_big * per_big

    def table(counts, loc0, row0, step, cap):
        end = jnp.cumsum(counts, axis=1)
        q = jnp.arange(cap, dtype=I32)
        ex = jnp.minimum(jnp.sum((end[:, None, :] <= q[None, :, None]).astype(I32), axis=2), N_EXPERTS - 1)
        onehot = ex[..., None] == jnp.arange(N_EXPERTS, dtype=I32)
        pick = lambda v: jnp.sum(jnp.where(onehot, v[:, None, :], 0), axis=2)
        idx = q[None, :] - pick(end - counts)
        return (pick(loc0) + idx * step).reshape(-1), (pick(row0) + idx * step).reshape(-1), end[:, -1]

    loc_b, row_b, cnt_b = table(n_big, off, run_row, BIG_PIECE, qb)
    done = n_big * BIG_PIECE
    loc_s, row_s, cnt_s = table(n_sm, off + done, run_row + done, RUN_PAD, qs)
    return jnp.stack([cnt_b, cnt_s], axis=1).reshape(-1), loc_b, row_b, loc_s, row_s


def _for_each_piece(tables, tile, n_loc, fn):
    cnt, loc_b, row_b, loc_s, row_s = tables
    qb, qs = _piece_caps(n_loc)

    def big(q, c):
        fn(loc_b[tile * qb + q], row_b[tile * qb + q], BIG_PIECE)
        return c

    def small(q, c):
        fn(loc_s[tile * qs + q], row_s[tile * qs + q], RUN_PAD)
        return c

    lax.fori_loop(0, cnt[2 * tile], big, 0)
    lax.fori_loop(0, cnt[2 * tile + 1], small, 0)


def _wait_rows(src, dst, sem, count, rows):
    def one(q, c):
        _rows_copy(src, 0, dst, 0, sem, rows=rows).wait()
        return c

    lax.fori_loop(0, count, one, 0)


def _wait_tile(src, dst, sem, cnt, tile):
    _wait_rows(src, dst, sem, cnt[2 * tile], BIG_PIECE)
    _wait_rows(src, dst, sem, cnt[2 * tile + 1], RUN_PAD)


def _post_mix_kernel(xa_ref, xb_ref, ota_ref, otb_ref, u_ref, up_ref, un_ref, gates_ref, woa_ref, cw_ref, cb_ref,
                     lng_ref, lnb_ref, wpb_ref, bpb_ref, wout_ref, gffn_ref, wr_ref, br_ref, ltri_ref, utri_ref,
                     x1_ref, route_ref, meta_ref, ext_ref, ph_ref, carry_ref, *, tm, n_first, seq_bounds):
    i = pl.program_id(0)
    t0 = i * tm
    is_start = functools.reduce(jnp.logical_or, [t0 == s for s, _ in seq_bounds])
    is_end = functools.reduce(jnp.logical_or, [t0 + tm == e for _, e in seq_bounds])

    ext_ref[0:HALO_ROWS, :] = jnp.where(is_start, 0.0, up_ref[...])
    ext_ref[HALO_ROWS:HALO_ROWS + tm, :] = u_ref[...]
    ext_ref[HALO_ROWS + tm:2 * HALO_ROWS + tm, :] = jnp.where(is_end, 0.0, un_ref[...])
    sub = SUBLANES
    base = HALO_ROWS - CONV_PAD
    cv = None
    for b in range(sub):
        part = None
        for o in range(CONV_WIDTH):
            if (o + base) % sub != b:
                continue
            a0 = (o + base) - b
            term = cw_ref[o:o + 1, :] * ext_ref[a0:a0 + tm + sub, :]
            part = term if part is None else part + term
        if b == 0:
            shifted = part[:tm]
        else:
            ph_ref[b] = part
            shifted = ph_ref[b, b:b + tm, :]
        cv = shifted if cv is None else cv + shifted
    cv = cv + cb_ref[...]
    mu = jnp.mean(cv, axis=-1, keepdims=True)
    xc = cv - mu
    y = xc * lax.rsqrt(jnp.mean(xc * xc, axis=-1, keepdims=True) + LN_EPS) * lng_ref[...] + lnb_ref[...]
    y = y * jax.nn.sigmoid(y)
    branch_b = _dot(y.astype(BF16), wpb_ref[...]) + bpb_ref[...]
    branch_a = _dot_tn(_tile_select(i, n_first, ota_ref, otb_ref)[0], woa_ref[...])
    gates = gates_ref[...]
    mix = gates[:, :D_MODEL].astype(F32) * branch_a + gates[:, D_MODEL:].astype(F32) * branch_b
    x1 = _tile_select(i, n_first, xa_ref, xb_ref) + _dot(mix.astype(BF16), wout_ref[...])
    x1_ref[...] = x1

    h2 = _rms(x1, gffn_ref[...])
    h2_hi = h2.astype(BF16)
    h2_lo = (h2 - h2_hi.astype(F32)).astype(BF16)
    hi_both = _dot(h2_hi, wr_ref[...])
    logits = hi_both[:, :LANES] + (hi_both[:, LANES:] + _dot(h2_lo, wr_ref[:, :LANES])) + br_ref[...]
    lane = lax.broadcasted_iota(I32, logits.shape, 1)
    lane_f = lane.astype(F32)
    vals, idxs, sels = [], [], []
    cur = logits
    for _ in range(TOP_K):
        mval = jnp.max(cur, axis=-1, keepdims=True)
        ik = jnp.min(jnp.where(cur == mval, lane_f, float(LANES)), axis=-1, keepdims=True)
        sel = lane_f == ik
        cur = jnp.where(sel, -jnp.inf, cur)
        vals.append(mval)
        idxs.append(ik)
        sels.append(sel)
    exps = [jnp.exp(v - vals[0]) for v in vals]
    denom = exps[0] + exps[1] + exps[2] + exps[3]

    onehot = sels[0].astype(F32) + sels[1].astype(F32) + sels[2].astype(F32) + sels[3].astype(F32)

    @pl.when(i == 0)
    def _():
        carry_ref[...] = jnp.zeros_like(carry_ref)

    n_e = jnp.sum(onehot, axis=0, keepdims=True)
    n_pieces = jnp.floor((n_e + (RUN_PAD - 1)) * (1.0 / RUN_PAD))
    pieces_b = jnp.broadcast_to(n_pieces, (SUBLANES, LANES)).astype(BF16)
    off = _dot(pieces_b, utri_ref[...])[0:1] * RUN_PAD
    total_pieces = jnp.sum(n_pieces, axis=-1, keepdims=True)
    carry = carry_ref[...]
    prefix = _dot(ltri_ref[...], onehot.astype(BF16))
    local = off + prefix
    route = jnp.zeros(logits.shape, F32)
    for kk in range(TOP_K):
        pos = jnp.sum(jnp.where(sels[kk], local, 0.0), axis=-1, keepdims=True)
        route = jnp.where(lane == kk, idxs[kk], route)
        route = jnp.where(lane == TOP_K + kk, exps[kk] / denom, route)
        route = jnp.where(lane == 2 * TOP_K + kk, pos, route)
    route_ref[...] = route
    carry_ref[...] = carry + n_pieces * RUN_PAD

    row8 = lax.broadcasted_iota(I32, (SUBLANES, LANES), 0)
    metav = jnp.where(row8 == 0, jnp.broadcast_to(carry, (SUBLANES, LANES)), 0.0)
    metav = jnp.where(row8 == 1, jnp.broadcast_to(n_pieces, (SUBLANES, LANES)), metav)
    metav = jnp.where(row8 == 2, jnp.broadcast_to(off, (SUBLANES, LANES)), metav)
    metav = jnp.where(row8 == 3, jnp.broadcast_to(total_pieces, (SUBLANES, LANES)), metav)
    meta_ref[0] = metav.astype(I32)


def _post_mix(xa, xb, ota, otb, u, gates, w_oa, conv_w, conv_b, ln_g, ln_b, w_pb, b_pb, w_out, g_ffn, w_r, b_r, ltri,
              utri, *, tm, seq_bounds):
    na, nb = xa.shape[0] // tm, xb.shape[0] // tm
    nt = na + nb
    t = nt * tm
    hb = tm // HALO_ROWS
    n_halo = t // HALO_ROWS
    row = lambda w: pl.BlockSpec((tm, w), lambda i: (i, 0))
    return pl.pallas_call(
        functools.partial(_post_mix_kernel, tm=tm, n_first=na, seq_bounds=seq_bounds),
        grid=(nt,),
        in_specs=_split_specs((tm, D_MODEL), na, nb) + _split_specs((1, N_HEADS * D_V, tm), na, nb) + [
                  row(CONV_CH),
                  pl.BlockSpec((HALO_ROWS, CONV_CH), lambda i: (jnp.maximum(i * hb - 1, 0), 0)),
                  pl.BlockSpec((HALO_ROWS, CONV_CH), lambda i: (jnp.minimum((i + 1) * hb, n_halo - 1), 0)),
                  row(2 * D_MODEL), _const_spec((N_HEADS * D_V, D_MODEL)), _const_spec((CONV_WIDTH, CONV_CH)),
                  _const_spec((1, CONV_CH)), _const_spec((1, CONV_CH)), _const_spec((1, CONV_CH)),
                  _const_spec((CONV_CH, D_MODEL)), _const_spec((1, D_MODEL)), _const_spec((D_MODEL, D_MODEL)),
                  _const_spec((1, D_MODEL)), _const_spec((D_MODEL, 2 * LANES)), _const_spec((1, LANES)),
                  _const_spec((tm, tm)), _const_spec((LANES, LANES))],
        out_specs=(row(D_MODEL), row(LANES), pl.BlockSpec((1, SUBLANES, LANES), lambda i: (i, 0, 0))),
        out_shape=(jax.ShapeDtypeStruct((t, D_MODEL), F32), jax.ShapeDtypeStruct((t, LANES), F32),
                   jax.ShapeDtypeStruct((nt, SUBLANES, LANES), I32)),
        scratch_shapes=[pltpu.VMEM((tm + 2 * HALO_ROWS, CONV_CH), F32),
                        pltpu.VMEM((SUBLANES, tm + SUBLANES, CONV_CH), F32), pltpu.VMEM((1, LANES), F32)],
        compiler_params=pltpu.CompilerParams(dimension_semantics=("arbitrary",), vmem_limit_bytes=VMEM_LIMIT),
        name="post_mix",
    )(xa, xb, ota, otb, u, u, u, gates, w_oa, conv_w, conv_b, ln_g, ln_b, w_pb, b_pb, w_out, g_ffn, w_r, b_r, ltri, utri)


def _dispatch_kernel(cnt_s, locb_s, rowb_s, locs_s, rows_s, start_s, end_s, x1_ref, route_ref, gffn_ref, xs_hbm,
                     xloc_ref, zero_ref, row_sems, *, tm, nt, bm, n_rows):
    tables = (cnt_s, locb_s, rowb_s, locs_s, rows_s)
    i = pl.program_id(0)
    slot = lax.rem(i, N_DISPATCH_BUFS)
    n_loc = _local_rows(tm)

    h2 = _rms(x1_ref[...], gffn_ref[...]).astype(BF16)
    pos_t = jnp.transpose(route_ref[...])[2 * TOP_K:3 * TOP_K, :].astype(I32)
    prow = lax.broadcasted_iota(I32, (n_loc, tm), 0)
    perm = jnp.zeros((n_loc, tm), F32)
    for kk in range(TOP_K):
        perm = jnp.where(prow == pos_t[kk:kk + 1, :], 1.0, perm)
    xloc_ref[slot] = _pack_rows(_dot(perm.astype(BF16), h2))

    def wait_tile(tile):
        _wait_tile(xloc_ref.at[0], xs_hbm, row_sems.at[lax.rem(tile, N_DISPATCH_BUFS)], cnt_s, tile)

    @pl.when(i >= N_DISPATCH_BUFS - 1)
    def _():
        wait_tile(i - (N_DISPATCH_BUFS - 1))

    _for_each_piece(tables, i, n_loc,
                    lambda s, d, rows: _rows_copy(xloc_ref.at[slot], s, xs_hbm, d, row_sems.at[slot], rows=rows).start())

    @pl.when(i == nt - 1)
    def _():
        for back in range(N_DISPATCH_BUFS - 2, -1, -1):
            @pl.when(i - back >= 0)
            def _():
                wait_tile(i - back)
        row_sem = row_sems.at[0]
        zero_ref[...] = jnp.zeros(zero_ref.shape, U32)

        def tail(e, c):
            end = end_s[e]
            n = lax.shift_right_logical(start_s[e + 1] - end, int(math.log2(RUN_PAD)))

            def piece(q, c2):
                _rows_copy(zero_ref, 0, xs_hbm, end + q * RUN_PAD, row_sem).start()
                return c2

            lax.fori_loop(0, n, piece, 0)
            return c + n

        _wait_rows(zero_ref, xs_hbm, row_sem, lax.fori_loop(0, N_EXPERTS, tail, 0), RUN_PAD)
        used = start_s[N_EXPERTS]
        n_free = (n_rows - used) // bm

        def fill(b, c):
            _rows_copy(zero_ref, 0, xs_hbm, used + b * bm, row_sem, rows=bm).start()
            return c

        lax.fori_loop(0, n_free, fill, 0)
        _wait_rows(zero_ref, xs_hbm, row_sem, n_free, bm)


def _dispatch(tables, start, end, x1, route, g_ffn, *, tm, bm, n_rows):
    t = x1.shape[0]
    nt = t // tm
    return pl.pallas_call(
        functools.partial(_dispatch_kernel, tm=tm, nt=nt, bm=bm, n_rows=n_rows),
        grid_spec=pltpu.PrefetchScalarGridSpec(
            num_scalar_prefetch=7,
            grid=(nt,),
            in_specs=[pl.BlockSpec((tm, D_MODEL), lambda i, *_: (i, 0)), pl.BlockSpec((tm, LANES), lambda i, *_: (i, 0)),
                      pl.BlockSpec((1, D_MODEL), lambda i, *_: (0, 0))],
            out_specs=pl.BlockSpec(memory_space=pl.ANY),
            scratch_shapes=[pltpu.VMEM((N_DISPATCH_BUFS, _local_rows(tm), PACKED), U32),
                            pltpu.VMEM((bm, PACKED), U32), pltpu.SemaphoreType.DMA((N_DISPATCH_BUFS,))],
        ),
        out_shape=jax.ShapeDtypeStruct((n_rows, PACKED), U32),
        compiler_params=pltpu.CompilerParams(dimension_semantics=("arbitrary",), vmem_limit_bytes=VMEM_LIMIT,
                                             has_side_effects=True),
        name="dispatch",
    )(*tables, start, end, x1, route, g_ffn)


def _moe_kernel(exp_ref, n_ref, xs_ref, wgu_ref, bgu_ref, wd_ref, bd_ref, y_ref, wgu_bf, wd_bf):
    w = pl.program_id(0)
    used = w < n_ref[0]

    @pl.when(jnp.logical_or(w == 0, exp_ref[w] != exp_ref[jnp.maximum(w - 1, 0)]))
    def _():
        wgu_bf[...] = wgu_ref[0].astype(BF16)
        wd_bf[...] = wd_ref[0].astype(BF16)

    @pl.when(used)
    def _():
        x = _unpack_rows(xs_ref[...])
        gu = _dot(x, wgu_bf[...]) + bgu_ref[0]
        g = jnp.minimum(gu[:, :D_FF], SWIGLU_LIMIT)
        u = jnp.clip(gu[:, D_FF:], -SWIGLU_LIMIT, SWIGLU_LIMIT)
        act = (u + 1.0) * (g * jax.nn.sigmoid(SWIGLU_ALPHA * g))
        y_ref[...] = _pack_rows(_dot(act.astype(BF16), wd_bf[...]) + bd_ref[0])

    @pl.when(jnp.logical_not(used))
    def _():
        y_ref[...] = jnp.zeros(y_ref.shape, U32)


def _moe(ex, n_used, xs, w_gu, b_gu, w_down, b_down, *, bm):
    return pl.pallas_call(
        _moe_kernel,
        grid_spec=pltpu.PrefetchScalarGridSpec(
            num_scalar_prefetch=2,
            grid=(xs.shape[0] // bm,),
            in_specs=[pl.BlockSpec((bm, PACKED), lambda w, ex, n: (w, 0)),
                      pl.BlockSpec((1, D_MODEL, 2 * D_FF), lambda w, ex, n: (ex[w], 0, 0)),
                      pl.BlockSpec((1, 1, 2 * D_FF), lambda w, ex, n: (ex[w], 0, 0)),
                      pl.BlockSpec((1, D_FF, D_MODEL), lambda w, ex, n: (ex[w], 0, 0)),
                      pl.BlockSpec((1, 1, D_MODEL), lambda w, ex, n: (ex[w], 0, 0))],
            out_specs=pl.BlockSpec((bm, PACKED), lambda w, ex, n: (w, 0)),
            scratch_shapes=[pltpu.VMEM((D_MODEL, 2 * D_FF), BF16), pltpu.VMEM((D_FF, D_MODEL), BF16)],
        ),
        out_shape=jax.ShapeDtypeStruct(xs.shape, U32),
        compiler_params=pltpu.CompilerParams(dimension_semantics=("arbitrary",), vmem_limit_bytes=VMEM_LIMIT),
        name="moe",
    )(ex, n_used, xs, w_gu, b_gu, w_down, b_down)


def _expert_layout(totals, *, bm, n_blocks):
    n_blk = (totals + bm - 1) // bm
    blk_end = jnp.cumsum(n_blk)
    start = jnp.concatenate([jnp.zeros((1,), I32), blk_end * bm]).astype(I32)
    w = jnp.arange(n_blocks, dtype=I32)
    ex = jnp.minimum(jnp.sum((blk_end[None, :] <= w[:, None]).astype(I32), axis=1), N_EXPERTS - 1)
    return start, ex.astype(I32), blk_end[-1].astype(I32)[None]


def _combine_kernel(cnt_s, locb_s, rowb_s, locs_s, rows_s, x1_ref, route_ref, gfin_ref, y_hbm, outa_ref, outb_ref,
                    yloc_ref, sems, *, tm, nt, n_first):
    tables = (cnt_s, locb_s, rowb_s, locs_s, rows_s)
    i = pl.program_id(0)
    slot = lax.rem(i, 2)
    n_loc = _local_rows(tm)

    def gather(tile, sl):
        _for_each_piece(tables, tile, n_loc,
                        lambda loc, row, rows: _rows_copy(y_hbm, row, yloc_ref.at[sl], loc, sems.at[sl],
                                                          rows=rows).start())

    @pl.when(i == 0)
    def _():
        yloc_ref[...] = jnp.zeros(yloc_ref.shape, U32)
        gather(0, 0)

    @pl.when(i + 1 < nt)
    def _():
        gather(i + 1, 1 - slot)

    _wait_tile(y_hbm, yloc_ref.at[slot], sems.at[slot], cnt_s, i)

    route = route_ref[...]
    lanes = lax.broadcasted_iota(I32, (tm, n_loc), 1)
    pw = jnp.zeros((tm, n_loc), F32)
    for kk in range(TOP_K):
        pos = route[:, 2 * TOP_K + kk:2 * TOP_K + kk + 1].astype(I32)
        pw = jnp.where(lanes == pos, route[:, TOP_K + kk:TOP_K + kk + 1], pw)
    moe = _dot(pw.astype(BF16), _unpack_rows(yloc_ref[slot]))
    res = _rms(x1_ref[...] + moe, gfin_ref[...])

    @pl.when(i < n_first)
    def _():
        outa_ref[...] = res

    @pl.when(i >= n_first)
    def _():
        outb_ref[...] = res


def _combine(tables, x1, route, g_final, y, *, tm, n_first):
    t = x1.shape[0]
    nt = t // tm
    n_second = nt - n_first
    return pl.pallas_call(
        functools.partial(_combine_kernel, tm=tm, nt=nt, n_first=n_first),
        grid_spec=pltpu.PrefetchScalarGridSpec(
            num_scalar_prefetch=5,
            grid=(nt,),
            in_specs=[pl.BlockSpec((tm, D_MODEL), lambda i, *_: (i, 0)), pl.BlockSpec((tm, LANES), lambda i, *_: (i, 0)),
                      pl.BlockSpec((1, D_MODEL), lambda i, *_: (0, 0)), pl.BlockSpec(memory_space=pl.ANY)],
            out_specs=(pl.BlockSpec((tm, D_MODEL), lambda i, *_: (jnp.minimum(i, n_first - 1), 0)),
                       pl.BlockSpec((tm, D_MODEL), lambda i, *_: (jnp.clip(i - n_first, 0, n_second - 1), 0))),
            scratch_shapes=[pltpu.VMEM((2, _local_rows(tm), PACKED), U32), pltpu.SemaphoreType.DMA((2,))],
        ),
        out_shape=(jax.ShapeDtypeStruct((n_first * tm, D_MODEL), F32),
                   jax.ShapeDtypeStruct((n_second * tm, D_MODEL), F32)),
        compiler_params=pltpu.CompilerParams(dimension_semantics=("arbitrary",), vmem_limit_bytes=VMEM_LIMIT),
        name="combine",
    )(*tables, x1, route, g_final, y)


def _rot_cols(w):
    half = w.shape[-1] // 2
    return jnp.concatenate([-w[..., half:], w[..., :half]], axis=-1)


def _prep_weights(w_in, w_uq, w_ukv):
    d = w_in.shape[0]
    w_kpe = w_in[:, C_KPE:C_KPE + D_ROPE]
    w_in_ext = jnp.concatenate(
        [w_in[:, :C_KPE], w_kpe, _rot_cols(w_kpe), jnp.zeros((d, LANES - 2 * D_ROPE), F32),
         w_in[:, C_KPE + D_ROPE:]], axis=1).astype(BF16)

    wq = w_uq.reshape(Q_RANK, N_HEADS, D_QK)
    nope, pe = wq[..., :D_NOPE], wq[..., D_NOPE:]
    zpad = jnp.zeros((Q_RANK, N_HEADS, LANES - D_QK), F32)
    plain = jnp.concatenate([nope, pe, zpad], axis=-1).reshape(Q_RANK, N_HEADS * LANES)
    rot = _rot_cols(pe).reshape(Q_RANK, N_HEADS * D_ROPE)
    wq_t = jnp.concatenate([plain, rot], axis=1).T.astype(BF16)

    wkv = w_ukv.reshape(KV_RANK, N_HEADS, D_NOPE + D_V)
    wk_ext = jnp.concatenate([wkv[..., :D_NOPE], jnp.zeros((KV_RANK, N_HEADS, LANES - D_NOPE), F32)],
                             axis=-1).reshape(KV_RANK, N_HEADS * LANES).astype(BF16)
    wv_t = wkv[..., D_NOPE:].reshape(KV_RANK, N_HEADS * D_V).T.astype(BF16)
    return w_in_ext, wq_t, wk_ext, wv_t


def _kpe_placement():
    e2 = np.zeros((LANES, N_HEADS * LANES), np.float32)
    for j in range(D_ROPE):
        for h in range(N_HEADS):
            e2[j, h * LANES + D_NOPE + j] = 1.0
            e2[D_ROPE + j, h * LANES + D_NOPE + j] = 1.0
    return jnp.asarray(e2, BF16)


def _inv_freq(tm):
    inv = 1.0 / (ROPE_THETA ** (jnp.arange(0, D_ROPE, 2, dtype=F32) / D_ROPE))
    return jnp.broadcast_to(inv[:, None], (D_ROPE // 2, tm))


def _token_tile(s1, s2):
    g = math.gcd(s1, s2)
    return TOKEN_TILE if g % TOKEN_TILE == 0 else g


def kernel(x_prompt, x_sample, g_mix, w_in, b_gate, g_q, w_uq, g_kv, w_ukv, w_oa, conv_w, conv_b, ln_g, ln_b, w_pb,
           b_pb, w_out, g_ffn, w_router, b_router, w_gu, b_gu, w_down, b_down, g_final):
    b1, s1, d = x_prompt.shape
    b2, s2, _ = x_sample.shape
    t1, t2 = b1 * s1, b2 * s2
    t = t1 + t2
    xa, xb = x_prompt.reshape(t1, d), x_sample.reshape(t2, d)
    seq_bounds = tuple((b * s1, (b + 1) * s1) for b in range(b1)) + tuple(
        (t1 + b * s2, t1 + (b + 1) * s2) for b in range(b2))

    tm = _token_tile(s1, s2)
    nt = t // tm
    w_in_ext, wq_t, wk_ext, wv_t = _prep_weights(w_in[0], w_uq[0], w_ukv[0])

    qt, k, vt3, u, gates = _in_proj(xa, xb, _inv_freq(tm), g_mix, w_in_ext, g_q, wq_t, g_kv, wk_ext, wv_t,
                                    _kpe_placement(), b_gate, tm=tm, seq_starts=tuple(s for s, _ in seq_bounds))

    ota = _attention(qt, k, vt3, seq_start=0, seq_len=s1, n_seq=b1, tq=tm, tk=tm)
    otb = _attention(qt, k, vt3, seq_start=t1, seq_len=s2, n_seq=b2, tq=tm, tk=tm)

    w_r = jnp.concatenate([w_router[0], jnp.zeros((d, LANES - N_EXPERTS), F32)], axis=1)
    w_r_hi = w_r.astype(BF16)
    w_r = jnp.concatenate([w_r_hi, (w_r - w_r_hi.astype(F32)).astype(BF16)], axis=1)
    b_r = jnp.concatenate([b_router[0], jnp.full((LANES - N_EXPERTS,), NEG_BIG, F32)])[None, :]
    ltri = jnp.asarray(np.tril(np.ones((tm, tm), np.float32), -1), BF16)
    utri = jnp.asarray(np.triu(np.ones((LANES, LANES), np.float32), 1), BF16)
    x1, route, meta = _post_mix(xa, xb, ota, otb, u, gates, w_oa[0].astype(BF16), conv_w[0], conv_b, ln_g, ln_b,
                                w_pb[0].astype(BF16), b_pb, w_out[0].astype(BF16), g_ffn, w_r, b_r, ltri, utri,
                                tm=tm, seq_bounds=seq_bounds)

    bm = MOE_ROWS
    n_rows = _sorted_rows(t, nt, bm)
    carry, n_pieces, off = (meta[:, r, :N_EXPERTS] for r in range(3))
    start, ex, n_used = _expert_layout(carry[-1] + n_pieces[-1] * RUN_PAD, bm=bm, n_blocks=n_rows // bm)
    run_row = carry + start[None, :N_EXPERTS]
    tables = _piece_tables(n_pieces, off, run_row, n_loc=_local_rows(tm))
    run_end = run_row[-1] + n_pieces[-1] * RUN_PAD

    xs = _dispatch(tables, start, run_end, x1, route, g_ffn, tm=tm, bm=bm, n_rows=n_rows)
    y = _moe(ex, n_used, xs, w_gu[0], b_gu[0][:, None, :], w_down[0], b_down[0][:, None, :], bm=bm)
    out_a, out_b = _combine(tables, x1, route, g_final[None, :], y, tm=tm, n_first=t1 // tm)
    return out_a.reshape(b1, s1, d), out_b.reshape(b2, s2, d)
```

```python
import functools
import math

import numpy as np
import jax
import jax.numpy as jnp
from jax import lax
from jax.experimental import pallas as pl
from jax.experimental.pallas import tpu as pltpu

F32 = jnp.float32
BF16 = jnp.bfloat16
I32 = jnp.int32
U32 = jnp.uint32

D_MODEL = 1024
N_HEADS = 8
Q_RANK = 384
KV_RANK = 256
D_NOPE = 64
D_ROPE = 32
D_V = 64
D_QK = D_NOPE + D_ROPE
ROPE_THETA = 10000.0
CONV_CH = 512
CONV_WIDTH = 31
CONV_PAD = (CONV_WIDTH - 1) // 2
N_EXPERTS = 32
TOP_K = 4
D_FF = 1024
SWIGLU_ALPHA = 1.702
SWIGLU_LIMIT = 7.0
RMS_EPS = 1e-6
LN_EPS = 1e-5

LANES = 128
SUBLANES = 8
BF16_ROWS = 16
HEAD_GROUP = 2
TOKEN_TILE = 512
HALO_ROWS = 16
CONV_ROWS = 512
NEG_BIG = -1e30
RUN_PAD = SUBLANES
BIG_PIECE = 32
N_DISPATCH_BUFS = 3
MOE_ROWS = 512

C_CQ = 0
C_CKV = Q_RANK
C_KPE = Q_RANK + KV_RANK
C_CONV = C_KPE + LANES
C_GATE = C_CONV + 2 * CONV_CH
D_IN_EXT = C_GATE + 2 * D_MODEL

VMEM_LIMIT = 56 * 1024 * 1024


def _rms(x, g):
    return x * lax.rsqrt(jnp.mean(x * x, axis=-1, keepdims=True) + RMS_EPS) * g


def _dot(a, b):
    return jnp.dot(a, b, preferred_element_type=F32)


def _dot_nt(a, b):
    return lax.dot_general(a, b, (((1,), (1,)), ((), ())), preferred_element_type=F32)


def _dot_tn(a, b):
    return lax.dot_general(a, b, (((0,), (0,)), ((), ())), preferred_element_type=F32)


def _const_spec(shape):
    return pl.BlockSpec(shape, lambda *_: (0,) * len(shape))


PACKED = D_MODEL // 2
HIGH_HALF = 0xFFFF0000


def _pack_rows(v):
    bits = lambda x: pltpu.bitcast(x.astype(BF16).astype(F32), U32)
    return bits(v[:, :PACKED]) | lax.shift_right_logical(bits(v[:, PACKED:]), jnp.uint32(16))


def _unpack_rows(w):
    hi = pltpu.bitcast(w & jnp.uint32(HIGH_HALF), F32).astype(BF16)
    lo = pltpu.bitcast(lax.shift_left(w, jnp.uint32(16)), F32).astype(BF16)
    return jnp.concatenate([hi, lo], axis=1)


def _tile_select(i, n_first, a_ref, b_ref):
    return jnp.where(i < n_first, a_ref[...], b_ref[...])


def _in_proj_kernel(xa_ref, xb_ref, inv_ref, gmix_ref, win_ref, gq_ref, wqt_ref, gkv_ref, wk_ref, wvt_ref,
                    e2_ref, bgate_ref, qt_ref, k_ref, vt_ref, u_ref, gates_ref, *, tm, n_first, seq_starts):
    i = pl.program_id(0)
    h = _rms(_tile_select(i, n_first, xa_ref, xb_ref), gmix_ref[...]).astype(BF16)
    proj = _dot(h, win_ref[...])

    t0 = i * tm
    s0 = 0
    for s in seq_starts:
        s0 = jnp.where(t0 >= s, s, s0)
    half = D_ROPE // 2
    pos = (t0 - s0 + lax.broadcasted_iota(I32, (half, tm), 1)).astype(F32)
    ang = pos * inv_ref[...]
    cos, sin = jnp.cos(ang), jnp.sin(ang)
    scale = (D_QK ** -0.5) * math.log2(math.e)
    cos_q, sin_q = cos * scale, sin * scale

    cqn = _rms(proj[:, C_CQ:C_CQ + Q_RANK], gq_ref[...]).astype(BF16)
    q2 = _dot_nt(wqt_ref[...], cqn)
    hw = N_HEADS * LANES
    for hh in range(N_HEADS):
        qa = q2[hh * LANES:(hh + 1) * LANES, :]
        qb = q2[hw + hh * D_ROPE:hw + (hh + 1) * D_ROPE, :]
        r0 = hh * LANES
        qt_ref[0, r0:r0 + D_NOPE, :] = (qa[:D_NOPE] * scale).astype(BF16)
        for part in range(2):
            lo = D_NOPE + part * half
            qt_ref[0, r0 + lo:r0 + lo + half, :] = (
                qa[lo:lo + half] * cos_q + qb[part * half:(part + 1) * half] * sin_q).astype(BF16)
        qt_ref[0, r0 + D_QK:r0 + LANES, :] = jnp.zeros((LANES - D_QK, tm), BF16)

    ckvn = _rms(proj[:, C_CKV:C_CKV + KV_RANK], gkv_ref[...]).astype(BF16)
    tabk = jnp.transpose(jnp.concatenate([cos, cos, sin, sin, jnp.zeros((LANES - 2 * D_ROPE, tm), F32)], axis=0))
    kpe = (proj[:, C_KPE:C_KPE + LANES] * tabk).astype(BF16)
    k_ref[...] = (_dot(ckvn, wk_ref[...]) + _dot(kpe, e2_ref[...])).astype(BF16)
    vt_ref[0] = _dot_nt(wvt_ref[...], ckvn).astype(BF16)

    a = proj[:, C_CONV:C_CONV + CONV_CH]
    g = proj[:, C_CONV + CONV_CH:C_CONV + 2 * CONV_CH]
    u_ref[...] = a * jax.nn.sigmoid(g)
    gates_ref[...] = jax.nn.sigmoid(proj[:, C_GATE:] + bgate_ref[...]).astype(BF16)


def _split_specs(block, n_first, n_second):
    first = lambda i: jnp.minimum(i, n_first - 1)
    second = lambda i: jnp.clip(i - n_first, 0, n_second - 1)
    rest = (0,) * (len(block) - 1)
    return [pl.BlockSpec(block, lambda i: (first(i),) + rest), pl.BlockSpec(block, lambda i: (second(i),) + rest)]


def _in_proj(xa, xb, inv_b, g_mix, w_in_ext, g_q, wq_t, g_kv, wk_ext, wv_t, e2, b_gate, *, tm, seq_starts):
    na, nb = xa.shape[0] // tm, xb.shape[0] // tm
    nt = na + nb
    t = nt * tm
    row = lambda w: pl.BlockSpec((tm, w), lambda i: (i, 0))
    col = lambda h: pl.BlockSpec((h, tm), lambda i: (0, i))
    return pl.pallas_call(
        functools.partial(_in_proj_kernel, tm=tm, n_first=na, seq_starts=seq_starts),
        grid=(nt,),
        in_specs=_split_specs((tm, D_MODEL), na, nb) + [
                  _const_spec((D_ROPE // 2, tm)), _const_spec((1, D_MODEL)),
                  _const_spec((D_MODEL, D_IN_EXT)), _const_spec((1, Q_RANK)),
                  _const_spec((N_HEADS * (LANES + D_ROPE), Q_RANK)), _const_spec((1, KV_RANK)),
                  _const_spec((KV_RANK, N_HEADS * LANES)), _const_spec((N_HEADS * D_V, KV_RANK)),
                  _const_spec((LANES, N_HEADS * LANES)), _const_spec((1, 2 * D_MODEL))],
        out_specs=(pl.BlockSpec((1, N_HEADS * LANES, tm), lambda i: (i, 0, 0)), row(N_HEADS * LANES),
                   pl.BlockSpec((1, N_HEADS * D_V, tm), lambda i: (i, 0, 0)), row(CONV_CH), row(2 * D_MODEL)),
        out_shape=(jax.ShapeDtypeStruct((nt, N_HEADS * LANES, tm), BF16),
                   jax.ShapeDtypeStruct((t, N_HEADS * LANES), BF16),
                   jax.ShapeDtypeStruct((nt, N_HEADS * D_V, tm), BF16),
                   jax.ShapeDtypeStruct((t, CONV_CH), F32),
                   jax.ShapeDtypeStruct((t, 2 * D_MODEL), BF16)),
        compiler_params=pltpu.CompilerParams(dimension_semantics=("arbitrary",), vmem_limit_bytes=VMEM_LIMIT),
        name="in_proj",
    )(xa, xb, inv_b, g_mix, w_in_ext, g_q, wq_t, g_kv, wk_ext, wv_t, e2, b_gate)


def _attn_kernel(qt_ref, k_ref, vt_ref, o_ref, s_ref, acc_ref, m_ref, *, nq, n_chunks, tk, unroll):
    half = n_chunks // 2
    ones = jnp.ones((BF16_ROWS, tk), BF16)
    heads = range(HEAD_GROUP)

    def scores(g, a, j, slot):
        ks = pl.multiple_of(j * tk, tk)
        kc = k_ref[pl.ds(ks, tk), g * LANES:(g + 1) * LANES]
        s_ref[2 * g + slot] = _dot(kc, qt_ref[a, g * LANES:(g + 1) * LANES, :])

    def softmax_pv(g, j, slot):
        st = s_ref[2 * g + slot]
        m = jnp.where(j == 0, NEG_BIG, m_ref[g])
        m_new = jnp.maximum(m, jnp.max(st, axis=0, keepdims=True))
        p = jnp.exp2(st - m_new).astype(BF16)
        v1 = jnp.concatenate([vt_ref[j, g * D_V:(g + 1) * D_V, :], ones], axis=0)
        acc = jnp.exp2(m - m_new) * acc_ref[g] + _dot(v1, p)
        acc_ref[g] = acc
        m_ref[g] = m_new
        return acc

    m_ref[...] = jnp.full(m_ref.shape, NEG_BIG, F32)
    acc_ref[...] = jnp.zeros(acc_ref.shape, F32)
    for g in heads:
        scores(g, 0, 0, 0)

    def pair(pp, may_end_tile):
        a = lax.div(pp, half)
        j = 2 * lax.rem(pp, half)
        nxt = pp + 1
        a_nxt = jnp.minimum(lax.div(nxt, half), nq - 1)
        j_nxt = 2 * lax.rem(nxt, half)
        for g in heads:
            scores(g, a, j + 1, 1)
        for g in heads:
            softmax_pv(g, j, 0)
        for g in heads:
            scores(g, a_nxt, j_nxt, 0)
        for g in heads:
            acc = softmax_pv(g, j + 1, 1)
            if may_end_tile:
                o_ref[a, g * D_V:(g + 1) * D_V, :] = (acc[:D_V] / acc[D_V:D_V + 1]).astype(o_ref.dtype)

    period = math.gcd(half, unroll)

    def trip(blk, c):
        for u in range(unroll):
            pair(blk * unroll + u, (u + 1) % period == 0)
        return c

    lax.fori_loop(0, nq * half // unroll, trip, 0)


def _attention(qt3, k, vt3, *, seq_start, seq_len, n_seq, tq, tk):
    nq = seq_len // tq
    nkc = seq_len // tk
    assert nkc % 2 == 0, "key chunks are processed in pairs"
    s0 = seq_start // seq_len
    n_pairs = nq * (nkc // 2)
    unroll = next(u for u in (16 // HEAD_GROUP, 4, 2, 1) if n_pairs % u == 0)
    hg = HEAD_GROUP
    return pl.pallas_call(
        functools.partial(_attn_kernel, nq=nq, n_chunks=nkc, tk=tk, unroll=unroll),
        grid=(n_seq, N_HEADS // hg),
        in_specs=[pl.BlockSpec((nq, hg * LANES, tq), lambda b, h: (s0 + b, h, 0)),
                  pl.BlockSpec((seq_len, hg * LANES), lambda b, h: (s0 + b, h)),
                  pl.BlockSpec((nkc, hg * D_V, tk), lambda b, h: (s0 + b, h, 0))],
        out_specs=pl.BlockSpec((nq, hg * D_V, tq), lambda b, h: (b, h, 0)),
        out_shape=jax.ShapeDtypeStruct((n_seq * nq, N_HEADS * D_V, tq), BF16),
        scratch_shapes=[pltpu.VMEM((2 * hg, tk, tq), F32), pltpu.VMEM((hg, D_V + BF16_ROWS, tq), F32),
                        pltpu.VMEM((hg, 1, tq), F32)],
        compiler_params=pltpu.CompilerParams(dimension_semantics=("arbitrary", "arbitrary"),
                                             vmem_limit_bytes=VMEM_LIMIT),
        name="attention",
    )(qt3, k, vt3)


def _local_rows(tm):
    return TOP_K * tm + N_EXPERTS * RUN_PAD


def _sorted_rows(t, nt, bm):
    runs = t * TOP_K + nt * N_EXPERTS * (RUN_PAD - 1)
    return -(-runs // bm) * bm + N_EXPERTS * bm


def _rows_copy(src, s, dst, d, sem, rows=RUN_PAD):
    s = pl.multiple_of(s, RUN_PAD)
    d = pl.multiple_of(d, RUN_PAD)
    return pltpu.make_async_copy(src.at[pl.ds(s, rows), :], dst.at[pl.ds(d, rows), :], sem)


def _piece_caps(n_loc):
    return n_loc // BIG_PIECE, N_EXPERTS * (BIG_PIECE // RUN_PAD - 1)


def _piece_tables(n_pieces, off, run_row, *, n_loc):
    per_big = BIG_PIECE // RUN_PAD
    qb, qs = _piece_caps(n_loc)
    n_big = n_pieces // per_big
    n_sm = n_pieces - n_big * per_big

    def table(counts, loc0, row0, step, cap):
        end = jnp.cumsum(counts, axis=1)
        q = jnp.arange(cap, dtype=I32)
        ex = jnp.minimum(jnp.sum((end[:, None, :] <= q[None, :, None]).astype(I32), axis=2), N_EXPERTS - 1)
        onehot = ex[..., None] == jnp.arange(N_EXPERTS, dtype=I32)
        pick = lambda v: jnp.sum(jnp.where(onehot, v[:, None, :], 0), axis=2)
        idx = q[None, :] - pick(end - counts)
        return (pick(loc0) + idx * step).reshape(-1), (pick(row0) + idx * step).reshape(-1), end[:, -1]

    loc_b, row_b, cnt_b = table(n_big, off, run_row, BIG_PIECE, qb)
    done = n_big * BIG_PIECE
    loc_s, row_s, cnt_s = table(n_sm, off + done, run_row + done, RUN_PAD, qs)
    return jnp.stack([cnt_b, cnt_s], axis=1).reshape(-1), loc_b, row_b, loc_s, row_s


def _for_each_piece(tables, tile, n_loc, fn):
    cnt, loc_b, row_b, loc_s, row_s = tables
    qb, qs = _piece_caps(n_loc)

    def big(q, c):
        fn(loc_b[tile * qb + q], row_b[tile * qb + q], BIG_PIECE)
        return c

    def small(q, c):
        fn(loc_s[tile * qs + q], row_s[tile * qs + q], RUN_PAD)
        return c

    lax.fori_loop(0, cnt[2 * tile], big, 0)
    lax.fori_loop(0, cnt[2 * tile + 1], small, 0)


def _wait_rows(src, dst, sem, count, rows):
    def one(q, c):
        _rows_copy(src, 0, dst, 0, sem, rows=rows).wait()
        return c

    lax.fori_loop(0, count, one, 0)


def _wait_tile(src, dst, sem, cnt, tile):
    _wait_rows(src, dst, sem, cnt[2 * tile], BIG_PIECE)
    _wait_rows(src, dst, sem, cnt[2 * tile + 1], RUN_PAD)


def _post_mix_kernel(xa_ref, xb_ref, ota_ref, otb_ref, u_ref, up_ref, un_ref, gates_ref, woa_ref, cw_ref, cb_ref,
                     lng_ref, lnb_ref, wpb_ref, bpb_ref, wout_ref, gffn_ref, wr_ref, br_ref, ltri_ref, utri_ref,
                     x1_ref, route_ref, meta_ref, ext_ref, cv_ref, ph_ref, carry_ref, *, tm, n_first, seq_bounds):
    i = pl.program_id(0)
    t0 = i * tm
    is_start = functools.reduce(jnp.logical_or, [t0 == s for s, _ in seq_bounds])
    is_end = functools.reduce(jnp.logical_or, [t0 + tm == e for _, e in seq_bounds])

    ext_ref[0:HALO_ROWS, :] = jnp.where(is_start, 0.0, up_ref[...])
    ext_ref[HALO_ROWS:HALO_ROWS + tm, :] = u_ref[...]
    ext_ref[HALO_ROWS + tm:2 * HALO_ROWS + tm, :] = jnp.where(is_end, 0.0, un_ref[...])
    rc = min(tm, CONV_ROWS)
    sub = SUBLANES
    base = HALO_ROWS - CONV_PAD
    for c in range(tm // rc):
        acc = None
        for b in range(sub):
            part = None
            for o in range(CONV_WIDTH):
                if (o + base) % sub != b:
                    continue
                a0 = c * rc + (o + base) - b
                term = cw_ref[o:o + 1, :] * ext_ref[a0:a0 + rc + sub, :]
                part = term if part is None else part + term
            if b == 0:
                shifted = part[:rc]
            else:
                ph_ref[b] = part
                shifted = ph_ref[b, b:b + rc, :]
            acc = shifted if acc is None else acc + shifted
        cv_ref[c * rc:(c + 1) * rc, :] = acc + cb_ref[...]

    cv = cv_ref[...]
    mu = jnp.mean(cv, axis=-1, keepdims=True)
    xc = cv - mu
    y = xc * lax.rsqrt(jnp.mean(xc * xc, axis=-1, keepdims=True) + LN_EPS) * lng_ref[...] + lnb_ref[...]
    y = y * jax.nn.sigmoid(y)
    branch_b = _dot(y.astype(BF16), wpb_ref[...]) + bpb_ref[...]
    branch_a = _dot_tn(_tile_select(i, n_first, ota_ref, otb_ref)[0], woa_ref[...])
    gates = gates_ref[...]
    mix = gates[:, :D_MODEL].astype(F32) * branch_a + gates[:, D_MODEL:].astype(F32) * branch_b
    x1 = _tile_select(i, n_first, xa_ref, xb_ref) + _dot(mix.astype(BF16), wout_ref[...])
    x1_ref[...] = x1

    h2 = _rms(x1, gffn_ref[...])
    h2_hi = h2.astype(BF16)
    h2_lo = (h2 - h2_hi.astype(F32)).astype(BF16)
    hi_both = _dot(h2_hi, wr_ref[...])
    logits = hi_both[:, :LANES] + (hi_both[:, LANES:] + _dot(h2_lo, wr_ref[:, :LANES])) + br_ref[...]
    lane = lax.broadcasted_iota(I32, logits.shape, 1)
    lane_f = lane.astype(F32)
    vals, idxs, sels = [], [], []
    cur = logits
    for _ in range(TOP_K):
        mval = jnp.max(cur, axis=-1, keepdims=True)
        ik = jnp.min(jnp.where(cur == mval, lane_f, float(LANES)), axis=-1, keepdims=True)
        sel = lane_f == ik
        cur = jnp.where(sel, -jnp.inf, cur)
        vals.append(mval)
        idxs.append(ik)
        sels.append(sel)
    exps = [jnp.exp(v - vals[0]) for v in vals]
    denom = exps[0] + exps[1] + exps[2] + exps[3]

    onehot = sels[0].astype(F32) + sels[1].astype(F32) + sels[2].astype(F32) + sels[3].astype(F32)

    @pl.when(i == 0)
    def _():
        carry_ref[...] = jnp.zeros_like(carry_ref)

    n_e = jnp.sum(onehot, axis=0, keepdims=True)
    n_pieces = jnp.floor((n_e + (RUN_PAD - 1)) * (1.0 / RUN_PAD))
    pieces_b = jnp.broadcast_to(n_pieces, (SUBLANES, LANES)).astype(BF16)
    off = _dot(pieces_b, utri_ref[...])[0:1] * RUN_PAD
    total_pieces = jnp.sum(n_pieces, axis=-1, keepdims=True)
    carry = carry_ref[...]
    prefix = _dot(ltri_ref[...], onehot.astype(BF16))
    local = off + prefix
    route = jnp.zeros(logits.shape, F32)
    for kk in range(TOP_K):
        pos = jnp.sum(jnp.where(sels[kk], local, 0.0), axis=-1, keepdims=True)
        route = jnp.where(lane == kk, idxs[kk], route)
        route = jnp.where(lane == TOP_K + kk, exps[kk] / denom, route)
        route = jnp.where(lane == 2 * TOP_K + kk, pos, route)
    route_ref[...] = route
    carry_ref[...] = carry + n_pieces * RUN_PAD

    row8 = lax.broadcasted_iota(I32, (SUBLANES, LANES), 0)
    metav = jnp.where(row8 == 0, jnp.broadcast_to(carry, (SUBLANES, LANES)), 0.0)
    metav = jnp.where(row8 == 1, jnp.broadcast_to(n_pieces, (SUBLANES, LANES)), metav)
    metav = jnp.where(row8 == 2, jnp.broadcast_to(off, (SUBLANES, LANES)), metav)
    metav = jnp.where(row8 == 3, jnp.broadcast_to(total_pieces, (SUBLANES, LANES)), metav)
    meta_ref[0] = metav.astype(I32)


def _post_mix(xa, xb, ota, otb, u, gates, w_oa, conv_w, conv_b, ln_g, ln_b, w_pb, b_pb, w_out, g_ffn, w_r, b_r, ltri,
              utri, *, tm, seq_bounds):
    na, nb = xa.shape[0] // tm, xb.shape[0] // tm
    nt = na + nb
    t = nt * tm
    hb = tm // HALO_ROWS
    n_halo = t // HALO_ROWS
    row = lambda w: pl.BlockSpec((tm, w), lambda i: (i, 0))
    return pl.pallas_call(
        functools.partial(_post_mix_kernel, tm=tm, n_first=na, seq_bounds=seq_bounds),
        grid=(nt,),
        in_specs=_split_specs((tm, D_MODEL), na, nb) + _split_specs((1, N_HEADS * D_V, tm), na, nb) + [
                  row(CONV_CH),
                  pl.BlockSpec((HALO_ROWS, CONV_CH), lambda i: (jnp.maximum(i * hb - 1, 0), 0)),
                  pl.BlockSpec((HALO_ROWS, CONV_CH), lambda i: (jnp.minimum((i + 1) * hb, n_halo - 1), 0)),
                  row(2 * D_MODEL), _const_spec((N_HEADS * D_V, D_MODEL)), _const_spec((CONV_WIDTH, CONV_CH)),
                  _const_spec((1, CONV_CH)), _const_spec((1, CONV_CH)), _const_spec((1, CONV_CH)),
                  _const_spec((CONV_CH, D_MODEL)), _const_spec((1, D_MODEL)), _const_spec((D_MODEL, D_MODEL)),
                  _const_spec((1, D_MODEL)), _const_spec((D_MODEL, 2 * LANES)), _const_spec((1, LANES)),
                  _const_spec((tm, tm)), _const_spec((LANES, LANES))],
        out_specs=(row(D_MODEL), row(LANES), pl.BlockSpec((1, SUBLANES, LANES), lambda i: (i, 0, 0))),
        out_shape=(jax.ShapeDtypeStruct((t, D_MODEL), F32), jax.ShapeDtypeStruct((t, LANES), F32),
                   jax.ShapeDtypeStruct((nt, SUBLANES, LANES), I32)),
        scratch_shapes=[pltpu.VMEM((tm + 2 * HALO_ROWS, CONV_CH), F32), pltpu.VMEM((tm, CONV_CH), F32),
                        pltpu.VMEM((SUBLANES, min(tm, CONV_ROWS) + SUBLANES, CONV_CH), F32), pltpu.VMEM((1, LANES), F32)],
        compiler_params=pltpu.CompilerParams(dimension_semantics=("arbitrary",), vmem_limit_bytes=VMEM_LIMIT),
        name="post_mix",
    )(xa, xb, ota, otb, u, u, u, gates, w_oa, conv_w, conv_b, ln_g, ln_b, w_pb, b_pb, w_out, g_ffn, w_r, b_r, ltri, utri)


def _dispatch_kernel(cnt_s, locb_s, rowb_s, locs_s, rows_s, start_s, end_s, x1_ref, route_ref, gffn_ref, xs_hbm,
                     xloc_ref, zero_ref, row_sems, *, tm, nt, bm, n_rows):
    tables = (cnt_s, locb_s, rowb_s, locs_s, rows_s)
    i = pl.program_id(0)
    slot = lax.rem(i, N_DISPATCH_BUFS)
    n_loc = _local_rows(tm)

    h2 = _rms(x1_ref[...], gffn_ref[...]).astype(BF16)
    pos_t = jnp.transpose(route_ref[...])[2 * TOP_K:3 * TOP_K, :].astype(I32)
    prow = lax.broadcasted_iota(I32, (n_loc, tm), 0)
    perm = jnp.zeros((n_loc, tm), F32)
    for kk in range(TOP_K):
        perm = jnp.where(prow == pos_t[kk:kk + 1, :], 1.0, perm)
    xloc_ref[slot] = _pack_rows(_dot(perm.astype(BF16), h2))

    def wait_tile(tile):
        _wait_tile(xloc_ref.at[0], xs_hbm, row_sems.at[lax.rem(tile, N_DISPATCH_BUFS)], cnt_s, tile)

    @pl.when(i >= N_DISPATCH_BUFS - 1)
    def _():
        wait_tile(i - (N_DISPATCH_BUFS - 1))

    _for_each_piece(tables, i, n_loc,
                    lambda s, d, rows: _rows_copy(xloc_ref.at[slot], s, xs_hbm, d, row_sems.at[slot], rows=rows).start())

    @pl.when(i == nt - 1)
    def _():
        for back in range(N_DISPATCH_BUFS - 2, -1, -1):
            @pl.when(i - back >= 0)
            def _():
                wait_tile(i - back)
        row_sem = row_sems.at[0]
        zero_ref[...] = jnp.zeros(zero_ref.shape, U32)

        def tail(e, c):
            end = end_s[e]
            n = lax.shift_right_logical(start_s[e + 1] - end, int(math.log2(RUN_PAD)))

            def piece(q, c2):
                _rows_copy(zero_ref, 0, xs_hbm, end + q * RUN_PAD, row_sem).start()
                return c2

            lax.fori_loop(0, n, piece, 0)
            return c + n

        _wait_rows(zero_ref, xs_hbm, row_sem, lax.fori_loop(0, N_EXPERTS, tail, 0), RUN_PAD)
        used = start_s[N_EXPERTS]
        n_free = (n_rows - used) // bm

        def fill(b, c):
            _rows_copy(zero_ref, 0, xs_hbm, used + b * bm, row_sem, rows=bm).start()
            return c

        lax.fori_loop(0, n_free, fill, 0)
        _wait_rows(zero_ref, xs_hbm, row_sem, n_free, bm)


def _dispatch(tables, start, end, x1, route, g_ffn, *, tm, bm, n_rows):
    t = x1.shape[0]
    nt = t // tm
    return pl.pallas_call(
        functools.partial(_dispatch_kernel, tm=tm, nt=nt, bm=bm, n_rows=n_rows),
        grid_spec=pltpu.PrefetchScalarGridSpec(
            num_scalar_prefetch=7,
            grid=(nt,),
            in_specs=[pl.BlockSpec((tm, D_MODEL), lambda i, *_: (i, 0)), pl.BlockSpec((tm, LANES), lambda i, *_: (i, 0)),
                      pl.BlockSpec((1, D_MODEL), lambda i, *_: (0, 0))],
            out_specs=pl.BlockSpec(memory_space=pl.ANY),
            scratch_shapes=[pltpu.VMEM((N_DISPATCH_BUFS, _local_rows(tm), PACKED), U32),
                            pltpu.VMEM((bm, PACKED), U32), pltpu.SemaphoreType.DMA((N_DISPATCH_BUFS,))],
        ),
        out_shape=jax.ShapeDtypeStruct((n_rows, PACKED), U32),
        compiler_params=pltpu.CompilerParams(dimension_semantics=("arbitrary",), vmem_limit_bytes=VMEM_LIMIT,
                                             has_side_effects=True),
        name="dispatch",
    )(*tables, start, end, x1, route, g_ffn)


def _moe_kernel(exp_ref, n_ref, xs_ref, wgu_ref, bgu_ref, wd_ref, bd_ref, y_ref, wgu_bf, wd_bf):
    w = pl.program_id(0)
    used = w < n_ref[0]

    @pl.when(jnp.logical_or(w == 0, exp_ref[w] != exp_ref[jnp.maximum(w - 1, 0)]))
    def _():
        wgu_bf[...] = wgu_ref[0].astype(BF16)
        wd_bf[...] = wd_ref[0].astype(BF16)

    @pl.when(used)
    def _():
        x = _unpack_rows(xs_ref[...])
        gu = _dot(x, wgu_bf[...]) + bgu_ref[0]
        g = jnp.minimum(gu[:, :D_FF], SWIGLU_LIMIT)
        u = jnp.clip(gu[:, D_FF:], -SWIGLU_LIMIT, SWIGLU_LIMIT)
        act = (u + 1.0) * (g * jax.nn.sigmoid(SWIGLU_ALPHA * g))
        y_ref[...] = _pack_rows(_dot(act.astype(BF16), wd_bf[...]) + bd_ref[0])

    @pl.when(jnp.logical_not(used))
    def _():
        y_ref[...] = jnp.zeros(y_ref.shape, U32)


def _moe(ex, n_used, xs, w_gu, b_gu, w_down, b_down, *, bm):
    return pl.pallas_call(
        _moe_kernel,
        grid_spec=pltpu.PrefetchScalarGridSpec(
            num_scalar_prefetch=2,
            grid=(xs.shape[0] // bm,),
            in_specs=[pl.BlockSpec((bm, PACKED), lambda w, ex, n: (w, 0)),
                      pl.BlockSpec((1, D_MODEL, 2 * D_FF), lambda w, ex, n: (ex[w], 0, 0)),
                      pl.BlockSpec((1, 1, 2 * D_FF), lambda w, ex, n: (ex[w], 0, 0)),
                      pl.BlockSpec((1, D_FF, D_MODEL), lambda w, ex, n: (ex[w], 0, 0)),
                      pl.BlockSpec((1, 1, D_MODEL), lambda w, ex, n: (ex[w], 0, 0))],
            out_specs=pl.BlockSpec((bm, PACKED), lambda w, ex, n: (w, 0)),
            scratch_shapes=[pltpu.VMEM((D_MODEL, 2 * D_FF), BF16), pltpu.VMEM((D_FF, D_MODEL), BF16)],
        ),
        out_shape=jax.ShapeDtypeStruct(xs.shape, U32),
        compiler_params=pltpu.CompilerParams(dimension_semantics=("arbitrary",), vmem_limit_bytes=VMEM_LIMIT),
        name="moe",
    )(ex, n_used, xs, w_gu, b_gu, w_down, b_down)


def _expert_layout(totals, *, bm, n_blocks):
    n_blk = (totals + bm - 1) // bm
    blk_end = jnp.cumsum(n_blk)
    start = jnp.concatenate([jnp.zeros((1,), I32), blk_end * bm]).astype(I32)
    w = jnp.arange(n_blocks, dtype=I32)
    ex = jnp.minimum(jnp.sum((blk_end[None, :] <= w[:, None]).astype(I32), axis=1), N_EXPERTS - 1)
    return start, ex.astype(I32), blk_end[-1].astype(I32)[None]


def _combine_kernel(cnt_s, locb_s, rowb_s, locs_s, rows_s, x1_ref, route_ref, gfin_ref, y_hbm, outa_ref, outb_ref,
                    yloc_ref, sems, *, tm, nt, n_first):
    tables = (cnt_s, locb_s, rowb_s, locs_s, rows_s)
    i = pl.program_id(0)
    slot = lax.rem(i, 2)
    n_loc = _local_rows(tm)

    def gather(tile, sl):
        _for_each_piece(tables, tile, n_loc,
                        lambda loc, row, rows: _rows_copy(y_hbm, row, yloc_ref.at[sl], loc, sems.at[sl],
                                                          rows=rows).start())

    @pl.when(i == 0)
    def _():
        yloc_ref[...] = jnp.zeros(yloc_ref.shape, U32)
        gather(0, 0)

    @pl.when(i + 1 < nt)
    def _():
        gather(i + 1, 1 - slot)

    _wait_tile(y_hbm, yloc_ref.at[slot], sems.at[slot], cnt_s, i)

    route = route_ref[...]
    lanes = lax.broadcasted_iota(I32, (tm, n_loc), 1)
    pw = jnp.zeros((tm, n_loc), F32)
    for kk in range(TOP_K):
        pos = route[:, 2 * TOP_K + kk:2 * TOP_K + kk + 1].astype(I32)
        pw = jnp.where(lanes == pos, route[:, TOP_K + kk:TOP_K + kk + 1], pw)
    moe = _dot(pw.astype(BF16), _unpack_rows(yloc_ref[slot]))
    res = _rms(x1_ref[...] + moe, gfin_ref[...])

    @pl.when(i < n_first)
    def _():
        outa_ref[...] = res

    @pl.when(i >= n_first)
    def _():
        outb_ref[...] = res


def _combine(tables, x1, route, g_final, y, *, tm, n_first):
    t = x1.shape[0]
    nt = t // tm
    n_second = nt - n_first
    return pl.pallas_call(
        functools.partial(_combine_kernel, tm=tm, nt=nt, n_first=n_first),
        grid_spec=pltpu.PrefetchScalarGridSpec(
            num_scalar_prefetch=5,
            grid=(nt,),
            in_specs=[pl.BlockSpec((tm, D_MODEL), lambda i, *_: (i, 0)), pl.BlockSpec((tm, LANES), lambda i, *_: (i, 0)),
                      pl.BlockSpec((1, D_MODEL), lambda i, *_: (0, 0)), pl.BlockSpec(memory_space=pl.ANY)],
            out_specs=(pl.BlockSpec((tm, D_MODEL), lambda i, *_: (jnp.minimum(i, n_first - 1), 0)),
                       pl.BlockSpec((tm, D_MODEL), lambda i, *_: (jnp.clip(i - n_first, 0, n_second - 1), 0))),
            scratch_shapes=[pltpu.VMEM((2, _local_rows(tm), PACKED), U32), pltpu.SemaphoreType.DMA((2,))],
        ),
        out_shape=(jax.ShapeDtypeStruct((n_first * tm, D_MODEL), F32),
                   jax.ShapeDtypeStruct((n_second * tm, D_MODEL), F32)),
        compiler_params=pltpu.CompilerParams(dimension_semantics=("arbitrary",), vmem_limit_bytes=VMEM_LIMIT),
        name="combine",
    )(*tables, x1, route, g_final, y)


def _rot_cols(w):
    half = w.shape[-1] // 2
    return jnp.concatenate([-w[..., half:], w[..., :half]], axis=-1)


def _prep_weights(w_in, w_uq, w_ukv):
    d = w_in.shape[0]
    w_kpe = w_in[:, C_KPE:C_KPE + D_ROPE]
    w_in_ext = jnp.concatenate(
        [w_in[:, :C_KPE], w_kpe, _rot_cols(w_kpe), jnp.zeros((d, LANES - 2 * D_ROPE), F32),
         w_in[:, C_KPE + D_ROPE:]], axis=1).astype(BF16)

    wq = w_uq.reshape(Q_RANK, N_HEADS, D_QK)
    nope, pe = wq[..., :D_NOPE], wq[..., D_NOPE:]
    zpad = jnp.zeros((Q_RANK, N_HEADS, LANES - D_QK), F32)
    plain = jnp.concatenate([nope, pe, zpad], axis=-1).reshape(Q_RANK, N_HEADS * LANES)
    rot = _rot_cols(pe).reshape(Q_RANK, N_HEADS * D_ROPE)
    wq_t = jnp.concatenate([plain, rot], axis=1).T.astype(BF16)

    wkv = w_ukv.reshape(KV_RANK, N_HEADS, D_NOPE + D_V)
    wk_ext = jnp.concatenate([wkv[..., :D_NOPE], jnp.zeros((KV_RANK, N_HEADS, LANES - D_NOPE), F32)],
                             axis=-1).reshape(KV_RANK, N_HEADS * LANES).astype(BF16)
    wv_t = wkv[..., D_NOPE:].reshape(KV_RANK, N_HEADS * D_V).T.astype(BF16)
    return w_in_ext, wq_t, wk_ext, wv_t


def _kpe_placement():
    e2 = np.zeros((LANES, N_HEADS * LANES), np.float32)
    for j in range(D_ROPE):
        for h in range(N_HEADS):
            e2[j, h * LANES + D_NOPE + j] = 1.0
            e2[D_ROPE + j, h * LANES + D_NOPE + j] = 1.0
    return jnp.asarray(e2, BF16)


def _inv_freq(tm):
    inv = 1.0 / (ROPE_THETA ** (jnp.arange(0, D_ROPE, 2, dtype=F32) / D_ROPE))
    return jnp.broadcast_to(inv[:, None], (D_ROPE // 2, tm))


def _token_tile(s1, s2):
    g = math.gcd(s1, s2)
    return TOKEN_TILE if g % TOKEN_TILE == 0 else g


def kernel(x_prompt, x_sample, g_mix, w_in, b_gate, g_q, w_uq, g_kv, w_ukv, w_oa, conv_w, conv_b, ln_g, ln_b, w_pb,
           b_pb, w_out, g_ffn, w_router, b_router, w_gu, b_gu, w_down, b_down, g_final):
    b1, s1, d = x_prompt.shape
    b2, s2, _ = x_sample.shape
    t1, t2 = b1 * s1, b2 * s2
    t = t1 + t2
    xa, xb = x_prompt.reshape(t1, d), x_sample.reshape(t2, d)
    seq_bounds = tuple((b * s1, (b + 1) * s1) for b in range(b1)) + tuple(
        (t1 + b * s2, t1 + (b + 1) * s2) for b in range(b2))

    tm = _token_tile(s1, s2)
    nt = t // tm
    w_in_ext, wq_t, wk_ext, wv_t = _prep_weights(w_in[0], w_uq[0], w_ukv[0])

    qt, k, vt3, u, gates = _in_proj(xa, xb, _inv_freq(tm), g_mix, w_in_ext, g_q, wq_t, g_kv, wk_ext, wv_t,
                                    _kpe_placement(), b_gate, tm=tm, seq_starts=tuple(s for s, _ in seq_bounds))

    ota = _attention(qt, k, vt3, seq_start=0, seq_len=s1, n_seq=b1, tq=tm, tk=tm)
    otb = _attention(qt, k, vt3, seq_start=t1, seq_len=s2, n_seq=b2, tq=tm, tk=tm)

    w_r = jnp.concatenate([w_router[0], jnp.zeros((d, LANES - N_EXPERTS), F32)], axis=1)
    w_r_hi = w_r.astype(BF16)
    w_r = jnp.concatenate([w_r_hi, (w_r - w_r_hi.astype(F32)).astype(BF16)], axis=1)
    b_r = jnp.concatenate([b_router[0], jnp.full((LANES - N_EXPERTS,), NEG_BIG, F32)])[None, :]
    ltri = jnp.asarray(np.tril(np.ones((tm, tm), np.float32), -1), BF16)
    utri = jnp.asarray(np.triu(np.ones((LANES, LANES), np.float32), 1), BF16)
    x1, route, meta = _post_mix(xa, xb, ota, otb, u, gates, w_oa[0].astype(BF16), conv_w[0], conv_b, ln_g, ln_b,
                                w_pb[0].astype(BF16), b_pb, w_out[0].astype(BF16), g_ffn, w_r, b_r, ltri, utri,
                                tm=tm, seq_bounds=seq_bounds)

    bm = MOE_ROWS
    n_rows = _sorted_rows(t, nt, bm)
    carry, n_pieces, off = (meta[:, r, :N_EXPERTS] for r in range(3))
    start, ex, n_used = _expert_layout(carry[-1] + n_pieces[-1] * RUN_PAD, bm=bm, n_blocks=n_rows // bm)
    run_row = carry + start[None, :N_EXPERTS]
    tables = _piece_tables(n_pieces, off, run_row, n_loc=_local_rows(tm))
    run_end = run_row[-1] + n_pieces[-1] * RUN_PAD

    xs = _dispatch(tables, start, run_end, x1, route, g_ffn, tm=tm, bm=bm, n_rows=n_rows)
    y = _moe(ex, n_used, xs, w_gu[0], b_gu[0][:, None, :], w_down[0], b_down[0][:, None, :], bm=bm)
    out_a, out_b = _combine(tables, x1, route, g_final[None, :], y, tm=tm, n_first=t1 // tm)
    return out_a.reshape(b1, s1, d), out_b.reshape(b2, s2, d)
```

```python
import functools
import math

import numpy as np
import jax
import jax.numpy as jnp
from jax import lax
from jax.experimental import pallas as pl
from jax.experimental.pallas import tpu as pltpu

F32 = jnp.float32
BF16 = jnp.bfloat16
I32 = jnp.int32
U32 = jnp.uint32

D_MODEL = 1024
N_HEADS = 8
Q_RANK = 384
KV_RANK = 256
D_NOPE = 64
D_ROPE = 32
D_V = 64
D_QK = D_NOPE + D_ROPE
ROPE_THETA = 10000.0
CONV_CH = 512
CONV_WIDTH = 31
CONV_PAD = (CONV_WIDTH - 1) // 2
N_EXPERTS = 32
TOP_K = 4
D_FF = 1024
SWIGLU_ALPHA = 1.702
SWIGLU_LIMIT = 7.0
RMS_EPS = 1e-6
LN_EPS = 1e-5

LANES = 128
SUBLANES = 8
BF16_ROWS = 16
TOKEN_TILE = 512
HALO_ROWS = 16
CONV_ROWS = 512
NEG_BIG = -1e30
RUN_PAD = SUBLANES
BIG_PIECE = 32
N_DISPATCH_BUFS = 3
MOE_ROWS = 512

C_CQ = 0
C_CKV = Q_RANK
C_KPE = Q_RANK + KV_RANK
C_CONV = C_KPE + LANES
C_GATE = C_CONV + 2 * CONV_CH
D_IN_EXT = C_GATE + 2 * D_MODEL

VMEM_LIMIT = 56 * 1024 * 1024


def _rms(x, g):
    return x * lax.rsqrt(jnp.mean(x * x, axis=-1, keepdims=True) + RMS_EPS) * g


def _dot(a, b):
    return jnp.dot(a, b, preferred_element_type=F32)


def _dot_nt(a, b):
    return lax.dot_general(a, b, (((1,), (1,)), ((), ())), preferred_element_type=F32)


def _dot_tn(a, b):
    return lax.dot_general(a, b, (((0,), (0,)), ((), ())), preferred_element_type=F32)


def _const_spec(shape):
    return pl.BlockSpec(shape, lambda *_: (0,) * len(shape))


PACKED = D_MODEL // 2
HIGH_HALF = 0xFFFF0000


def _pack_rows(v):
    bits = lambda x: pltpu.bitcast(x.astype(BF16).astype(F32), U32)
    return bits(v[:, :PACKED]) | lax.shift_right_logical(bits(v[:, PACKED:]), jnp.uint32(16))


def _unpack_rows(w):
    hi = pltpu.bitcast(w & jnp.uint32(HIGH_HALF), F32).astype(BF16)
    lo = pltpu.bitcast(lax.shift_left(w, jnp.uint32(16)), F32).astype(BF16)
    return jnp.concatenate([hi, lo], axis=1)


def _tile_select(i, n_first, a_ref, b_ref):
    return jnp.where(i < n_first, a_ref[...], b_ref[...])


def _in_proj_kernel(xa_ref, xb_ref, inv_ref, gmix_ref, win_ref, gq_ref, wqt_ref, gkv_ref, wk_ref, wvt_ref,
                    e2_ref, bgate_ref, qt_ref, k_ref, vt_ref, u_ref, gates_ref, *, tm, n_first, seq_starts):
    i = pl.program_id(0)
    h = _rms(_tile_select(i, n_first, xa_ref, xb_ref), gmix_ref[...]).astype(BF16)
    proj = _dot(h, win_ref[...])

    t0 = i * tm
    s0 = 0
    for s in seq_starts:
        s0 = jnp.where(t0 >= s, s, s0)
    half = D_ROPE // 2
    pos = (t0 - s0 + lax.broadcasted_iota(I32, (half, tm), 1)).astype(F32)
    ang = pos * inv_ref[...]
    cos, sin = jnp.cos(ang), jnp.sin(ang)
    scale = (D_QK ** -0.5) * math.log2(math.e)
    cos_q, sin_q = cos * scale, sin * scale

    cqn = _rms(proj[:, C_CQ:C_CQ + Q_RANK], gq_ref[...]).astype(BF16)
    q2 = _dot_nt(wqt_ref[...], cqn)
    hw = N_HEADS * LANES
    for hh in range(N_HEADS):
        qa = q2[hh * LANES:(hh + 1) * LANES, :]
        qb = q2[hw + hh * D_ROPE:hw + (hh + 1) * D_ROPE, :]
        r0 = hh * LANES
        qt_ref[0, r0:r0 + D_NOPE, :] = (qa[:D_NOPE] * scale).astype(BF16)
        for part in range(2):
            lo = D_NOPE + part * half
            qt_ref[0, r0 + lo:r0 + lo + half, :] = (
                qa[lo:lo + half] * cos_q + qb[part * half:(part + 1) * half] * sin_q).astype(BF16)
        qt_ref[0, r0 + D_QK:r0 + LANES, :] = jnp.zeros((LANES - D_QK, tm), BF16)

    ckvn = _rms(proj[:, C_CKV:C_CKV + KV_RANK], gkv_ref[...]).astype(BF16)
    tabk = jnp.transpose(jnp.concatenate([cos, cos, sin, sin, jnp.zeros((LANES - 2 * D_ROPE, tm), F32)], axis=0))
    kpe = (proj[:, C_KPE:C_KPE + LANES] * tabk).astype(BF16)
    k_ref[...] = (_dot(ckvn, wk_ref[...]) + _dot(kpe, e2_ref[...])).astype(BF16)
    vt_ref[0] = _dot_nt(wvt_ref[...], ckvn).astype(BF16)

    a = proj[:, C_CONV:C_CONV + CONV_CH]
    g = proj[:, C_CONV + CONV_CH:C_CONV + 2 * CONV_CH]
    u_ref[...] = a * jax.nn.sigmoid(g)
    gates_ref[...] = jax.nn.sigmoid(proj[:, C_GATE:] + bgate_ref[...]).astype(BF16)


def _split_specs(block, n_first, n_second):
    first = lambda i: jnp.minimum(i, n_first - 1)
    second = lambda i: jnp.clip(i - n_first, 0, n_second - 1)
    rest = (0,) * (len(block) - 1)
    return [pl.BlockSpec(block, lambda i: (first(i),) + rest), pl.BlockSpec(block, lambda i: (second(i),) + rest)]


def _in_proj(xa, xb, inv_b, g_mix, w_in_ext, g_q, wq_t, g_kv, wk_ext, wv_t, e2, b_gate, *, tm, seq_starts):
    na, nb = xa.shape[0] // tm, xb.shape[0] // tm
    nt = na + nb
    t = nt * tm
    row = lambda w: pl.BlockSpec((tm, w), lambda i: (i, 0))
    col = lambda h: pl.BlockSpec((h, tm), lambda i: (0, i))
    return pl.pallas_call(
        functools.partial(_in_proj_kernel, tm=tm, n_first=na, seq_starts=seq_starts),
        grid=(nt,),
        in_specs=_split_specs((tm, D_MODEL), na, nb) + [
                  _const_spec((D_ROPE // 2, tm)), _const_spec((1, D_MODEL)),
                  _const_spec((D_MODEL, D_IN_EXT)), _const_spec((1, Q_RANK)),
                  _const_spec((N_HEADS * (LANES + D_ROPE), Q_RANK)), _const_spec((1, KV_RANK)),
                  _const_spec((KV_RANK, N_HEADS * LANES)), _const_spec((N_HEADS * D_V, KV_RANK)),
                  _const_spec((LANES, N_HEADS * LANES)), _const_spec((1, 2 * D_MODEL))],
        out_specs=(pl.BlockSpec((1, N_HEADS * LANES, tm), lambda i: (i, 0, 0)), row(N_HEADS * LANES),
                   pl.BlockSpec((1, N_HEADS * D_V, tm), lambda i: (i, 0, 0)), row(CONV_CH), row(2 * D_MODEL)),
        out_shape=(jax.ShapeDtypeStruct((nt, N_HEADS * LANES, tm), BF16),
                   jax.ShapeDtypeStruct((t, N_HEADS * LANES), BF16),
                   jax.ShapeDtypeStruct((nt, N_HEADS * D_V, tm), BF16),
                   jax.ShapeDtypeStruct((t, CONV_CH), F32),
                   jax.ShapeDtypeStruct((t, 2 * D_MODEL), BF16)),
        compiler_params=pltpu.CompilerParams(dimension_semantics=("arbitrary",), vmem_limit_bytes=VMEM_LIMIT),
        name="in_proj",
    )(xa, xb, inv_b, g_mix, w_in_ext, g_q, wq_t, g_kv, wk_ext, wv_t, e2, b_gate)


def _attn_kernel(qt_ref, k_ref, vt_ref, o_ref, s_ref, acc_ref, m_ref, *, nq, n_chunks, tk, unroll):
    half = n_chunks // 2
    ones = jnp.ones((BF16_ROWS, tk), BF16)

    def scores(a, j, slot):
        ks = pl.multiple_of(j * tk, tk)
        s_ref[slot] = _dot(k_ref[pl.ds(ks, tk), :], qt_ref[a])

    def softmax_pv(j, slot):
        st = s_ref[slot]
        m = jnp.where(j == 0, NEG_BIG, m_ref[...])
        m_new = jnp.maximum(m, jnp.max(st, axis=0, keepdims=True))
        p = jnp.exp2(st - m_new).astype(BF16)
        v1 = jnp.concatenate([vt_ref[j], ones], axis=0)
        acc = jnp.exp2(m - m_new) * acc_ref[...] + _dot(v1, p)
        acc_ref[...] = acc
        m_ref[...] = m_new
        return acc

    m_ref[...] = jnp.full(m_ref.shape, NEG_BIG, F32)
    acc_ref[...] = jnp.zeros(acc_ref.shape, F32)
    scores(0, 0, 0)

    def pair(pp, may_end_tile):
        a = lax.div(pp, half)
        j = 2 * lax.rem(pp, half)
        scores(a, j + 1, 1)
        softmax_pv(j, 0)
        nxt = pp + 1
        scores(jnp.minimum(lax.div(nxt, half), nq - 1), 2 * lax.rem(nxt, half), 0)
        acc = softmax_pv(j + 1, 1)
        if may_end_tile:
            o_ref[a] = (acc[:D_V] / acc[D_V:D_V + 1]).astype(o_ref.dtype)

    period = math.gcd(half, unroll)

    def trip(blk, c):
        for u in range(unroll):
            pair(blk * unroll + u, (u + 1) % period == 0)
        return c

    lax.fori_loop(0, nq * half // unroll, trip, 0)


def _attention(qt3, k, vt3, *, seq_start, seq_len, n_seq, tq, tk):
    nq = seq_len // tq
    nkc = seq_len // tk
    assert nkc % 2 == 0, "key chunks are processed in pairs"
    s0 = seq_start // seq_len
    n_pairs = nq * (nkc // 2)
    unroll = next(u for u in (16, 8, 4, 2, 1) if n_pairs % u == 0)
    return pl.pallas_call(
        functools.partial(_attn_kernel, nq=nq, n_chunks=nkc, tk=tk, unroll=unroll),
        grid=(n_seq, N_HEADS),
        in_specs=[pl.BlockSpec((nq, LANES, tq), lambda b, h: (s0 + b, h, 0)),
                  pl.BlockSpec((seq_len, LANES), lambda b, h: (s0 + b, h)),
                  pl.BlockSpec((nkc, D_V, tk), lambda b, h: (s0 + b, h, 0))],
        out_specs=pl.BlockSpec((nq, D_V, tq), lambda b, h: (b, h, 0)),
        out_shape=jax.ShapeDtypeStruct((n_seq * nq, N_HEADS * D_V, tq), BF16),
        scratch_shapes=[pltpu.VMEM((2, tk, tq), F32), pltpu.VMEM((D_V + BF16_ROWS, tq), F32),
                        pltpu.VMEM((1, tq), F32)],
        compiler_params=pltpu.CompilerParams(dimension_semantics=("arbitrary", "arbitrary"),
                                             vmem_limit_bytes=VMEM_LIMIT),
        name="attention",
    )(qt3, k, vt3)


def _local_rows(tm):
    return TOP_K * tm + N_EXPERTS * RUN_PAD


def _sorted_rows(t, nt, bm):
    runs = t * TOP_K + nt * N_EXPERTS * (RUN_PAD - 1)
    return -(-runs // bm) * bm + N_EXPERTS * bm


def _rows_copy(src, s, dst, d, sem, rows=RUN_PAD):
    s = pl.multiple_of(s, RUN_PAD)
    d = pl.multiple_of(d, RUN_PAD)
    return pltpu.make_async_copy(src.at[pl.ds(s, rows), :], dst.at[pl.ds(d, rows), :], sem)


def _piece_priority(rows):
    return 0 if rows == BIG_PIECE else 1


def _piece_caps(n_loc):
    return n_loc // BIG_PIECE, N_EXPERTS * (BIG_PIECE // RUN_PAD - 1)


def _piece_tables(n_pieces, off, run_row, *, n_loc):
    per_big = BIG_PIECE // RUN_PAD
    qb, qs = _piece_caps(n_loc)
    n_big = n_pieces // per_big
    n_sm = n_pieces - n_big * per_big

    def table(counts, loc0, row0, step, cap):
        end = jnp.cumsum(counts, axis=1)
        q = jnp.arange(cap, dtype=I32)
        ex = jnp.minimum(jnp.sum((end[:, None, :] <= q[None, :, None]).astype(I32), axis=2), N_EXPERTS - 1)
        onehot = ex[..., None] == jnp.arange(N_EXPERTS, dtype=I32)
        pick = lambda v: jnp.sum(jnp.where(onehot, v[:, None, :], 0), axis=2)
        idx = q[None, :] - pick(end - counts)
        return (pick(loc0) + idx * step).reshape(-1), (pick(row0) + idx * step).reshape(-1), end[:, -1]

    loc_b, row_b, cnt_b = table(n_big, off, run_row, BIG_PIECE, qb)
    done = n_big * BIG_PIECE
    loc_s, row_s, cnt_s = table(n_sm, off + done, run_row + done, RUN_PAD, qs)
    return jnp.stack([cnt_b, cnt_s], axis=1).reshape(-1), loc_b, row_b, loc_s, row_s


def _for_each_piece(tables, tile, n_loc, fn):
    cnt, loc_b, row_b, loc_s, row_s = tables
    qb, qs = _piece_caps(n_loc)

    def big(q, c):
        fn(loc_b[tile * qb + q], row_b[tile * qb + q], BIG_PIECE)
        return c

    def small(q, c):
        fn(loc_s[tile * qs + q], row_s[tile * qs + q], RUN_PAD)
        return c

    lax.fori_loop(0, cnt[2 * tile], big, 0)
    lax.fori_loop(0, cnt[2 * tile + 1], small, 0)


def _wait_rows(src, dst, sem, count, rows):
    def one(q, c):
        _rows_copy(src, 0, dst, 0, sem, rows=rows).wait()
        return c

    lax.fori_loop(0, count, one, 0)


def _wait_tile(src, dst, sem, cnt, tile):
    _wait_rows(src, dst, sem, cnt[2 * tile], BIG_PIECE)
    _wait_rows(src, dst, sem, cnt[2 * tile + 1], RUN_PAD)


def _post_mix_kernel(xa_ref, xb_ref, ota_ref, otb_ref, u_ref, up_ref, un_ref, gates_ref, woa_ref, cw_ref, cb_ref,
                     lng_ref, lnb_ref, wpb_ref, bpb_ref, wout_ref, gffn_ref, wr_ref, br_ref, ltri_ref, utri_ref,
                     x1_ref, route_ref, meta_ref, ext_ref, cv_ref, ph_ref, carry_ref, *, tm, n_first, seq_bounds):
    i = pl.program_id(0)
    t0 = i * tm
    is_start = functools.reduce(jnp.logical_or, [t0 == s for s, _ in seq_bounds])
    is_end = functools.reduce(jnp.logical_or, [t0 + tm == e for _, e in seq_bounds])

    ext_ref[0:HALO_ROWS, :] = jnp.where(is_start, 0.0, up_ref[...])
    ext_ref[HALO_ROWS:HALO_ROWS + tm, :] = u_ref[...]
    ext_ref[HALO_ROWS + tm:2 * HALO_ROWS + tm, :] = jnp.where(is_end, 0.0, un_ref[...])
    rc = min(tm, CONV_ROWS)
    sub = SUBLANES
    base = HALO_ROWS - CONV_PAD
    for c in range(tm // rc):
        acc = None
        for b in range(sub):
            part = None
            for o in range(CONV_WIDTH):
                if (o + base) % sub != b:
                    continue
                a0 = c * rc + (o + base) - b
                term = cw_ref[o:o + 1, :] * ext_ref[a0:a0 + rc + sub, :]
                part = term if part is None else part + term
            if b == 0:
                shifted = part[:rc]
            else:
                ph_ref[b] = part
                shifted = ph_ref[b, b:b + rc, :]
            acc = shifted if acc is None else acc + shifted
        cv_ref[c * rc:(c + 1) * rc, :] = acc + cb_ref[...]

    cv = cv_ref[...]
    mu = jnp.mean(cv, axis=-1, keepdims=True)
    xc = cv - mu
    y = xc * lax.rsqrt(jnp.mean(xc * xc, axis=-1, keepdims=True) + LN_EPS) * lng_ref[...] + lnb_ref[...]
    y = y * jax.nn.sigmoid(y)
    branch_b = _dot(y.astype(BF16), wpb_ref[...]) + bpb_ref[...]
    branch_a = _dot_tn(_tile_select(i, n_first, ota_ref, otb_ref)[0], woa_ref[...])
    gates = gates_ref[...]
    mix = gates[:, :D_MODEL].astype(F32) * branch_a + gates[:, D_MODEL:].astype(F32) * branch_b
    x1 = _tile_select(i, n_first, xa_ref, xb_ref) + _dot(mix.astype(BF16), wout_ref[...])
    x1_ref[...] = x1

    h2 = _rms(x1, gffn_ref[...])
    h2_hi = h2.astype(BF16)
    h2_lo = (h2 - h2_hi.astype(F32)).astype(BF16)
    hi_both = _dot(h2_hi, wr_ref[...])
    logits = hi_both[:, :LANES] + (hi_both[:, LANES:] + _dot(h2_lo, wr_ref[:, :LANES])) + br_ref[...]
    lane = lax.broadcasted_iota(I32, logits.shape, 1)
    lane_f = lane.astype(F32)
    vals, idxs, sels = [], [], []
    cur = logits
    for _ in range(TOP_K):
        mval = jnp.max(cur, axis=-1, keepdims=True)
        ik = jnp.min(jnp.where(cur == mval, lane_f, float(LANES)), axis=-1, keepdims=True)
        sel = lane_f == ik
        cur = jnp.where(sel, -jnp.inf, cur)
        vals.append(mval)
        idxs.append(ik)
        sels.append(sel)
    exps = [jnp.exp(v - vals[0]) for v in vals]
    denom = exps[0] + exps[1] + exps[2] + exps[3]

    onehot = sels[0].astype(F32) + sels[1].astype(F32) + sels[2].astype(F32) + sels[3].astype(F32)

    @pl.when(i == 0)
    def _():
        carry_ref[...] = jnp.zeros_like(carry_ref)

    n_e = jnp.sum(onehot, axis=0, keepdims=True)
    n_pieces = jnp.floor((n_e + (RUN_PAD - 1)) * (1.0 / RUN_PAD))
    pieces_b = jnp.broadcast_to(n_pieces, (SUBLANES, LANES)).astype(BF16)
    off = _dot(pieces_b, utri_ref[...])[0:1] * RUN_PAD
    total_pieces = jnp.sum(n_pieces, axis=-1, keepdims=True)
    carry = carry_ref[...]
    prefix = _dot(ltri_ref[...], onehot.astype(BF16))
    local = off + prefix
    route = jnp.zeros(logits.shape, F32)
    for kk in range(TOP_K):
        pos = jnp.sum(jnp.where(sels[kk], local, 0.0), axis=-1, keepdims=True)
        route = jnp.where(lane == kk, idxs[kk], route)
        route = jnp.where(lane == TOP_K + kk, exps[kk] / denom, route)
        route = jnp.where(lane == 2 * TOP_K + kk, pos, route)
    route_ref[...] = route
    carry_ref[...] = carry + n_pieces * RUN_PAD

    row8 = lax.broadcasted_iota(I32, (SUBLANES, LANES), 0)
    metav = jnp.where(row8 == 0, jnp.broadcast_to(carry, (SUBLANES, LANES)), 0.0)
    metav = jnp.where(row8 == 1, jnp.broadcast_to(n_pieces, (SUBLANES, LANES)), metav)
    metav = jnp.where(row8 == 2, jnp.broadcast_to(off, (SUBLANES, LANES)), metav)
    metav = jnp.where(row8 == 3, jnp.broadcast_to(total_pieces, (SUBLANES, LANES)), metav)
    meta_ref[0] = metav.astype(I32)


def _post_mix(xa, xb, ota, otb, u, gates, w_oa, conv_w, conv_b, ln_g, ln_b, w_pb, b_pb, w_out, g_ffn, w_r, b_r, ltri,
              utri, *, tm, seq_bounds):
    na, nb = xa.shape[0] // tm, xb.shape[0] // tm
    nt = na + nb
    t = nt * tm
    hb = tm // HALO_ROWS
    n_halo = t // HALO_ROWS
    row = lambda w: pl.BlockSpec((tm, w), lambda i: (i, 0))
    return pl.pallas_call(
        functools.partial(_post_mix_kernel, tm=tm, n_first=na, seq_bounds=seq_bounds),
        grid=(nt,),
        in_specs=_split_specs((tm, D_MODEL), na, nb) + _split_specs((1, N_HEADS * D_V, tm), na, nb) + [
                  row(CONV_CH),
                  pl.BlockSpec((HALO_ROWS, CONV_CH), lambda i: (jnp.maximum(i * hb - 1, 0), 0)),
                  pl.BlockSpec((HALO_ROWS, CONV_CH), lambda i: (jnp.minimum((i + 1) * hb, n_halo - 1), 0)),
                  row(2 * D_MODEL), _const_spec((N_HEADS * D_V, D_MODEL)), _const_spec((CONV_WIDTH, CONV_CH)),
                  _const_spec((1, CONV_CH)), _const_spec((1, CONV_CH)), _const_spec((1, CONV_CH)),
                  _const_spec((CONV_CH, D_MODEL)), _const_spec((1, D_MODEL)), _const_spec((D_MODEL, D_MODEL)),
                  _const_spec((1, D_MODEL)), _const_spec((D_MODEL, 2 * LANES)), _const_spec((1, LANES)),
                  _const_spec((tm, tm)), _const_spec((LANES, LANES))],
        out_specs=(row(D_MODEL), row(LANES), pl.BlockSpec((1, SUBLANES, LANES), lambda i: (i, 0, 0))),
        out_shape=(jax.ShapeDtypeStruct((t, D_MODEL), F32), jax.ShapeDtypeStruct((t, LANES), F32),
                   jax.ShapeDtypeStruct((nt, SUBLANES, LANES), I32)),
        scratch_shapes=[pltpu.VMEM((tm + 2 * HALO_ROWS, CONV_CH), F32), pltpu.VMEM((tm, CONV_CH), F32),
                        pltpu.VMEM((SUBLANES, min(tm, CONV_ROWS) + SUBLANES, CONV_CH), F32), pltpu.VMEM((1, LANES), F32)],
        compiler_params=pltpu.CompilerParams(dimension_semantics=("arbitrary",), vmem_limit_bytes=VMEM_LIMIT),
        name="post_mix",
    )(xa, xb, ota, otb, u, u, u, gates, w_oa, conv_w, conv_b, ln_g, ln_b, w_pb, b_pb, w_out, g_ffn, w_r, b_r, ltri, utri)


def _dispatch_kernel(cnt_s, locb_s, rowb_s, locs_s, rows_s, start_s, end_s, x1_ref, route_ref, gffn_ref, xs_hbm,
                     xloc_ref, zero_ref, row_sems, *, tm, nt, bm, n_rows):
    tables = (cnt_s, locb_s, rowb_s, locs_s, rows_s)
    i = pl.program_id(0)
    slot = lax.rem(i, N_DISPATCH_BUFS)
    n_loc = _local_rows(tm)

    h2 = _rms(x1_ref[...], gffn_ref[...]).astype(BF16)
    pos_t = jnp.transpose(route_ref[...])[2 * TOP_K:3 * TOP_K, :].astype(I32)
    prow = lax.broadcasted_iota(I32, (n_loc, tm), 0)
    perm = jnp.zeros((n_loc, tm), F32)
    for kk in range(TOP_K):
        perm = jnp.where(prow == pos_t[kk:kk + 1, :], 1.0, perm)
    xloc_ref[slot] = _pack_rows(_dot(perm.astype(BF16), h2))

    def wait_tile(tile):
        _wait_tile(xloc_ref.at[0], xs_hbm, row_sems.at[lax.rem(tile, N_DISPATCH_BUFS)], cnt_s, tile)

    @pl.when(i >= N_DISPATCH_BUFS - 1)
    def _():
        wait_tile(i - (N_DISPATCH_BUFS - 1))

    _for_each_piece(tables, i, n_loc,
                    lambda s, d, rows: _rows_copy(xloc_ref.at[slot], s, xs_hbm, d, row_sems.at[slot], rows=rows).start(
                        priority=_piece_priority(rows)))

    @pl.when(i == nt - 1)
    def _():
        for back in range(N_DISPATCH_BUFS - 2, -1, -1):
            @pl.when(i - back >= 0)
            def _():
                wait_tile(i - back)
        row_sem = row_sems.at[0]
        zero_ref[...] = jnp.zeros(zero_ref.shape, U32)

        def tail(e, c):
            end = end_s[e]
            n = lax.shift_right_logical(start_s[e + 1] - end, int(math.log2(RUN_PAD)))

            def piece(q, c2):
                _rows_copy(zero_ref, 0, xs_hbm, end + q * RUN_PAD, row_sem).start()
                return c2

            lax.fori_loop(0, n, piece, 0)
            return c + n

        _wait_rows(zero_ref, xs_hbm, row_sem, lax.fori_loop(0, N_EXPERTS, tail, 0), RUN_PAD)
        used = start_s[N_EXPERTS]
        n_free = (n_rows - used) // bm

        def fill(b, c):
            _rows_copy(zero_ref, 0, xs_hbm, used + b * bm, row_sem, rows=bm).start()
            return c

        lax.fori_loop(0, n_free, fill, 0)
        _wait_rows(zero_ref, xs_hbm, row_sem, n_free, bm)


def _dispatch(tables, start, end, x1, route, g_ffn, *, tm, bm, n_rows):
    t = x1.shape[0]
    nt = t // tm
    return pl.pallas_call(
        functools.partial(_dispatch_kernel, tm=tm, nt=nt, bm=bm, n_rows=n_rows),
        grid_spec=pltpu.PrefetchScalarGridSpec(
            num_scalar_prefetch=7,
            grid=(nt,),
            in_specs=[pl.BlockSpec((tm, D_MODEL), lambda i, *_: (i, 0)), pl.BlockSpec((tm, LANES), lambda i, *_: (i, 0)),
                      pl.BlockSpec((1, D_MODEL), lambda i, *_: (0, 0))],
            out_specs=pl.BlockSpec(memory_space=pl.ANY),
            scratch_shapes=[pltpu.VMEM((N_DISPATCH_BUFS, _local_rows(tm), PACKED), U32),
                            pltpu.VMEM((bm, PACKED), U32), pltpu.SemaphoreType.DMA((N_DISPATCH_BUFS,))],
        ),
        out_shape=jax.ShapeDtypeStruct((n_rows, PACKED), U32),
        compiler_params=pltpu.CompilerParams(dimension_semantics=("arbitrary",), vmem_limit_bytes=VMEM_LIMIT,
                                             has_side_effects=True),
        name="dispatch",
    )(*tables, start, end, x1, route, g_ffn)


def _moe_kernel(exp_ref, n_ref, xs_ref, wgu_ref, bgu_ref, wd_ref, bd_ref, y_ref, wgu_bf, wd_bf):
    w = pl.program_id(0)
    used = w < n_ref[0]

    @pl.when(jnp.logical_or(w == 0, exp_ref[w] != exp_ref[jnp.maximum(w - 1, 0)]))
    def _():
        wgu_bf[...] = wgu_ref[0].astype(BF16)
        wd_bf[...] = wd_ref[0].astype(BF16)

    @pl.when(used)
    def _():
        x = _unpack_rows(xs_ref[...])
        gu = _dot(x, wgu_bf[...]) + bgu_ref[0]
        g = jnp.minimum(gu[:, :D_FF], SWIGLU_LIMIT)
        u = jnp.clip(gu[:, D_FF:], -SWIGLU_LIMIT, SWIGLU_LIMIT)
        act = (u + 1.0) * (g * jax.nn.sigmoid(SWIGLU_ALPHA * g))
        y_ref[...] = _pack_rows(_dot(act.astype(BF16), wd_bf[...]) + bd_ref[0])

    @pl.when(jnp.logical_not(used))
    def _():
        y_ref[...] = jnp.zeros(y_ref.shape, U32)


def _moe(ex, n_used, xs, w_gu, b_gu, w_down, b_down, *, bm):
    return pl.pallas_call(
        _moe_kernel,
        grid_spec=pltpu.PrefetchScalarGridSpec(
            num_scalar_prefetch=2,
            grid=(xs.shape[0] // bm,),
            in_specs=[pl.BlockSpec((bm, PACKED), lambda w, ex, n: (w, 0)),
                      pl.BlockSpec((1, D_MODEL, 2 * D_FF), lambda w, ex, n: (ex[w], 0, 0)),
                      pl.BlockSpec((1, 1, 2 * D_FF), lambda w, ex, n: (ex[w], 0, 0)),
                      pl.BlockSpec((1, D_FF, D_MODEL), lambda w, ex, n: (ex[w], 0, 0)),
                      pl.BlockSpec((1, 1, D_MODEL), lambda w, ex, n: (ex[w], 0, 0))],
            out_specs=pl.BlockSpec((bm, PACKED), lambda w, ex, n: (w, 0)),
            scratch_shapes=[pltpu.VMEM((D_MODEL, 2 * D_FF), BF16), pltpu.VMEM((D_FF, D_MODEL), BF16)],
        ),
        out_shape=jax.ShapeDtypeStruct(xs.shape, U32),
        compiler_params=pltpu.CompilerParams(dimension_semantics=("arbitrary",), vmem_limit_bytes=VMEM_LIMIT),
        name="moe",
    )(ex, n_used, xs, w_gu, b_gu, w_down, b_down)


def _expert_layout(totals, *, bm, n_blocks):
    n_blk = (totals + bm - 1) // bm
    blk_end = jnp.cumsum(n_blk)
    start = jnp.concatenate([jnp.zeros((1,), I32), blk_end * bm]).astype(I32)
    w = jnp.arange(n_blocks, dtype=I32)
    ex = jnp.minimum(jnp.sum((blk_end[None, :] <= w[:, None]).astype(I32), axis=1), N_EXPERTS - 1)
    return start, ex.astype(I32), blk_end[-1].astype(I32)[None]


def _combine_kernel(cnt_s, locb_s, rowb_s, locs_s, rows_s, x1_ref, route_ref, gfin_ref, y_hbm, outa_ref, outb_ref,
                    yloc_ref, sems, *, tm, nt, n_first):
    tables = (cnt_s, locb_s, rowb_s, locs_s, rows_s)
    i = pl.program_id(0)
    slot = lax.rem(i, 2)
    n_loc = _local_rows(tm)

    def gather(tile, sl):
        _for_each_piece(tables, tile, n_loc,
                        lambda loc, row, rows: _rows_copy(y_hbm, row, yloc_ref.at[sl], loc, sems.at[sl],
                                                          rows=rows).start(priority=_piece_priority(rows)))

    @pl.when(i == 0)
    def _():
        yloc_ref[...] = jnp.zeros(yloc_ref.shape, U32)
        gather(0, 0)

    @pl.when(i + 1 < nt)
    def _():
        gather(i + 1, 1 - slot)

    _wait_tile(y_hbm, yloc_ref.at[slot], sems.at[slot], cnt_s, i)

    route = route_ref[...]
    lanes = lax.broadcasted_iota(I32, (tm, n_loc), 1)
    pw = jnp.zeros((tm, n_loc), F32)
    for kk in range(TOP_K):
        pos = route[:, 2 * TOP_K + kk:2 * TOP_K + kk + 1].astype(I32)
        pw = jnp.where(lanes == pos, route[:, TOP_K + kk:TOP_K + kk + 1], pw)
    moe = _dot(pw.astype(BF16), _unpack_rows(yloc_ref[slot]))
    res = _rms(x1_ref[...] + moe, gfin_ref[...])

    @pl.when(i < n_first)
    def _():
        outa_ref[...] = res

    @pl.when(i >= n_first)
    def _():
        outb_ref[...] = res


def _combine(tables, x1, route, g_final, y, *, tm, n_first):
    t = x1.shape[0]
    nt = t // tm
    n_second = nt - n_first
    return pl.pallas_call(
        functools.partial(_combine_kernel, tm=tm, nt=nt, n_first=n_first),
        grid_spec=pltpu.PrefetchScalarGridSpec(
            num_scalar_prefetch=5,
            grid=(nt,),
            in_specs=[pl.BlockSpec((tm, D_MODEL), lambda i, *_: (i, 0)), pl.BlockSpec((tm, LANES), lambda i, *_: (i, 0)),
                      pl.BlockSpec((1, D_MODEL), lambda i, *_: (0, 0)), pl.BlockSpec(memory_space=pl.ANY)],
            out_specs=(pl.BlockSpec((tm, D_MODEL), lambda i, *_: (jnp.minimum(i, n_first - 1), 0)),
                       pl.BlockSpec((tm, D_MODEL), lambda i, *_: (jnp.clip(i - n_first, 0, n_second - 1), 0))),
            scratch_shapes=[pltpu.VMEM((2, _local_rows(tm), PACKED), U32), pltpu.SemaphoreType.DMA((2,))],
        ),
        out_shape=(jax.ShapeDtypeStruct((n_first * tm, D_MODEL), F32),
                   jax.ShapeDtypeStruct((n_second * tm, D_MODEL), F32)),
        compiler_params=pltpu.CompilerParams(dimension_semantics=("arbitrary",), vmem_limit_bytes=VMEM_LIMIT),
        name="combine",
    )(*tables, x1, route, g_final, y)


def _rot_cols(w):
    half = w.shape[-1] // 2
    return jnp.concatenate([-w[..., half:], w[..., :half]], axis=-1)


def _prep_weights(w_in, w_uq, w_ukv):
    d = w_in.shape[0]
    w_kpe = w_in[:, C_KPE:C_KPE + D_ROPE]
    w_in_ext = jnp.concatenate(
        [w_in[:, :C_KPE], w_kpe, _rot_cols(w_kpe), jnp.zeros((d, LANES - 2 * D_ROPE), F32),
         w_in[:, C_KPE + D_ROPE:]], axis=1).astype(BF16)

    wq = w_uq.reshape(Q_RANK, N_HEADS, D_QK)
    nope, pe = wq[..., :D_NOPE], wq[..., D_NOPE:]
    zpad = jnp.zeros((Q_RANK, N_HEADS, LANES - D_QK), F32)
    plain = jnp.concatenate([nope, pe, zpad], axis=-1).reshape(Q_RANK, N_HEADS * LANES)
    rot = _rot_cols(pe).reshape(Q_RANK, N_HEADS * D_ROPE)
    wq_t = jnp.concatenate([plain, rot], axis=1).T.astype(BF16)

    wkv = w_ukv.reshape(KV_RANK, N_HEADS, D_NOPE + D_V)
    wk_ext = jnp.concatenate([wkv[..., :D_NOPE], jnp.zeros((KV_RANK, N_HEADS, LANES - D_NOPE), F32)],
                             axis=-1).reshape(KV_RANK, N_HEADS * LANES).astype(BF16)
    wv_t = wkv[..., D_NOPE:].reshape(KV_RANK, N_HEADS * D_V).T.astype(BF16)
    return w_in_ext, wq_t, wk_ext, wv_t


def _kpe_placement():
    e2 = np.zeros((LANES, N_HEADS * LANES), np.float32)
    for j in range(D_ROPE):
        for h in range(N_HEADS):
            e2[j, h * LANES + D_NOPE + j] = 1.0
            e2[D_ROPE + j, h * LANES + D_NOPE + j] = 1.0
    return jnp.asarray(e2, BF16)


def _inv_freq(tm):
    inv = 1.0 / (ROPE_THETA ** (jnp.arange(0, D_ROPE, 2, dtype=F32) / D_ROPE))
    return jnp.broadcast_to(inv[:, None], (D_ROPE // 2, tm))


def _token_tile(s1, s2):
    g = math.gcd(s1, s2)
    return TOKEN_TILE if g % TOKEN_TILE == 0 else g


def kernel(x_prompt, x_sample, g_mix, w_in, b_gate, g_q, w_uq, g_kv, w_ukv, w_oa, conv_w, conv_b, ln_g, ln_b, w_pb,
           b_pb, w_out, g_ffn, w_router, b_router, w_gu, b_gu, w_down, b_down, g_final):
    b1, s1, d = x_prompt.shape
    b2, s2, _ = x_sample.shape
    t1, t2 = b1 * s1, b2 * s2
    t = t1 + t2
    xa, xb = x_prompt.reshape(t1, d), x_sample.reshape(t2, d)
    seq_bounds = tuple((b * s1, (b + 1) * s1) for b in range(b1)) + tuple(
        (t1 + b * s2, t1 + (b + 1) * s2) for b in range(b2))

    tm = _token_tile(s1, s2)
    nt = t // tm
    w_in_ext, wq_t, wk_ext, wv_t = _prep_weights(w_in[0], w_uq[0], w_ukv[0])

    qt, k, vt3, u, gates = _in_proj(xa, xb, _inv_freq(tm), g_mix, w_in_ext, g_q, wq_t, g_kv, wk_ext, wv_t,
                                    _kpe_placement(), b_gate, tm=tm, seq_starts=tuple(s for s, _ in seq_bounds))

    ota = _attention(qt, k, vt3, seq_start=0, seq_len=s1, n_seq=b1, tq=tm, tk=tm)
    otb = _attention(qt, k, vt3, seq_start=t1, seq_len=s2, n_seq=b2, tq=tm, tk=tm)

    w_r = jnp.concatenate([w_router[0], jnp.zeros((d, LANES - N_EXPERTS), F32)], axis=1)
    w_r_hi = w_r.astype(BF16)
    w_r = jnp.concatenate([w_r_hi, (w_r - w_r_hi.astype(F32)).astype(BF16)], axis=1)
    b_r = jnp.concatenate([b_router[0], jnp.full((LANES - N_EXPERTS,), NEG_BIG, F32)])[None, :]
    ltri = jnp.asarray(np.tril(np.ones((tm, tm), np.float32), -1), BF16)
    utri = jnp.asarray(np.triu(np.ones((LANES, LANES), np.float32), 1), BF16)
    x1, route, meta = _post_mix(xa, xb, ota, otb, u, gates, w_oa[0].astype(BF16), conv_w[0], conv_b, ln_g, ln_b,
                                w_pb[0].astype(BF16), b_pb, w_out[0].astype(BF16), g_ffn, w_r, b_r, ltri, utri,
                                tm=tm, seq_bounds=seq_bounds)

    bm = MOE_ROWS
    n_rows = _sorted_rows(t, nt, bm)
    carry, n_pieces, off = (meta[:, r, :N_EXPERTS] for r in range(3))
    start, ex, n_used = _expert_layout(carry[-1] + n_pieces[-1] * RUN_PAD, bm=bm, n_blocks=n_rows // bm)
    run_row = carry + start[None, :N_EXPERTS]
    tables = _piece_tables(n_pieces, off, run_row, n_loc=_local_rows(tm))
    run_end = run_row[-1] + n_pieces[-1] * RUN_PAD

    xs = _dispatch(tables, start, run_end, x1, route, g_ffn, tm=tm, bm=bm, n_rows=n_rows)
    y = _moe(ex, n_used, xs, w_gu[0], b_gu[0][:, None, :], w_down[0], b_down[0][:, None, :], bm=bm)
    out_a, out_b = _combine(tables, x1, route, g_final[None, :], y, tm=tm, n_first=t1 // tm)
    return out_a.reshape(b1, s1, d), out_b.reshape(b2, s2, d)
```

```python
import functools
import math

import numpy as np
import jax
import jax.numpy as jnp
from jax import lax
from jax.experimental import pallas as pl
from jax.experimental.pallas import tpu as pltpu

F32 = jnp.float32
BF16 = jnp.bfloat16
I32 = jnp.int32
U32 = jnp.uint32

D_MODEL = 1024
N_HEADS = 8
Q_RANK = 384
KV_RANK = 256
D_NOPE = 64
D_ROPE = 32
D_V = 64
D_QK = D_NOPE + D_ROPE
ROPE_THETA = 10000.0
CONV_CH = 512
CONV_WIDTH = 31
CONV_PAD = (CONV_WIDTH - 1) // 2
N_EXPERTS = 32
TOP_K = 4
D_FF = 1024
SWIGLU_ALPHA = 1.702
SWIGLU_LIMIT = 7.0
RMS_EPS = 1e-6
LN_EPS = 1e-5

LANES = 128
SUBLANES = 8
BF16_ROWS = 16
TOKEN_TILE = 512
HALO_ROWS = 16
CONV_ROWS = 512
NEG_BIG = -1e30
RUN_PAD = SUBLANES
BIG_PIECE = 32
N_DISPATCH_BUFS = 2
MOE_ROWS = 512

C_CQ = 0
C_CKV = Q_RANK
C_KPE = Q_RANK + KV_RANK
C_CONV = C_KPE + LANES
C_GATE = C_CONV + 2 * CONV_CH
D_IN_EXT = C_GATE + 2 * D_MODEL

VMEM_LIMIT = 56 * 1024 * 1024


def _rms(x, g):
    return x * lax.rsqrt(jnp.mean(x * x, axis=-1, keepdims=True) + RMS_EPS) * g


def _dot(a, b):
    return jnp.dot(a, b, preferred_element_type=F32)


def _dot_nt(a, b):
    return lax.dot_general(a, b, (((1,), (1,)), ((), ())), preferred_element_type=F32)


def _dot_tn(a, b):
    return lax.dot_general(a, b, (((0,), (0,)), ((), ())), preferred_element_type=F32)


def _const_spec(shape):
    return pl.BlockSpec(shape, lambda *_: (0,) * len(shape))


PACKED = D_MODEL // 2
HIGH_HALF = 0xFFFF0000


def _pack_rows(v):
    bits = lambda x: pltpu.bitcast(x.astype(BF16).astype(F32), U32)
    return bits(v[:, :PACKED]) | lax.shift_right_logical(bits(v[:, PACKED:]), jnp.uint32(16))


def _unpack_rows(w):
    hi = pltpu.bitcast(w & jnp.uint32(HIGH_HALF), F32).astype(BF16)
    lo = pltpu.bitcast(lax.shift_left(w, jnp.uint32(16)), F32).astype(BF16)
    return jnp.concatenate([hi, lo], axis=1)


def _tile_select(i, n_first, a_ref, b_ref):
    return jnp.where(i < n_first, a_ref[...], b_ref[...])


def _in_proj_kernel(xa_ref, xb_ref, inv_ref, gmix_ref, win_ref, gq_ref, wqt_ref, gkv_ref, wk_ref, wvt_ref,
                    e2_ref, bgate_ref, qt_ref, k_ref, vt_ref, u_ref, gates_ref, *, tm, n_first, seq_starts):
    i = pl.program_id(0)
    h = _rms(_tile_select(i, n_first, xa_ref, xb_ref), gmix_ref[...]).astype(BF16)
    proj = _dot(h, win_ref[...])

    t0 = i * tm
    s0 = 0
    for s in seq_starts:
        s0 = jnp.where(t0 >= s, s, s0)
    half = D_ROPE // 2
    pos = (t0 - s0 + lax.broadcasted_iota(I32, (half, tm), 1)).astype(F32)
    ang = pos * inv_ref[...]
    cos, sin = jnp.cos(ang), jnp.sin(ang)
    scale = (D_QK ** -0.5) * math.log2(math.e)
    cos_q, sin_q = cos * scale, sin * scale

    cqn = _rms(proj[:, C_CQ:C_CQ + Q_RANK], gq_ref[...]).astype(BF16)
    q2 = _dot_nt(wqt_ref[...], cqn)
    hw = N_HEADS * LANES
    for hh in range(N_HEADS):
        qa = q2[hh * LANES:(hh + 1) * LANES, :]
        qb = q2[hw + hh * D_ROPE:hw + (hh + 1) * D_ROPE, :]
        r0 = hh * LANES
        qt_ref[0, r0:r0 + D_NOPE, :] = (qa[:D_NOPE] * scale).astype(BF16)
        for part in range(2):
            lo = D_NOPE + part * half
            qt_ref[0, r0 + lo:r0 + lo + half, :] = (
                qa[lo:lo + half] * cos_q + qb[part * half:(part + 1) * half] * sin_q).astype(BF16)
        qt_ref[0, r0 + D_QK:r0 + LANES, :] = jnp.zeros((LANES - D_QK, tm), BF16)

    ckvn = _rms(proj[:, C_CKV:C_CKV + KV_RANK], gkv_ref[...]).astype(BF16)
    tabk = jnp.transpose(jnp.concatenate([cos, cos, sin, sin, jnp.zeros((LANES - 2 * D_ROPE, tm), F32)], axis=0))
    kpe = (proj[:, C_KPE:C_KPE + LANES] * tabk).astype(BF16)
    k_ref[...] = (_dot(ckvn, wk_ref[...]) + _dot(kpe, e2_ref[...])).astype(BF16)
    vt_ref[0] = _dot_nt(wvt_ref[...], ckvn).astype(BF16)

    a = proj[:, C_CONV:C_CONV + CONV_CH]
    g = proj[:, C_CONV + CONV_CH:C_CONV + 2 * CONV_CH]
    u_ref[...] = a * jax.nn.sigmoid(g)
    gates_ref[...] = jax.nn.sigmoid(proj[:, C_GATE:] + bgate_ref[...]).astype(BF16)


def _split_specs(block, n_first, n_second):
    first = lambda i: jnp.minimum(i, n_first - 1)
    second = lambda i: jnp.clip(i - n_first, 0, n_second - 1)
    rest = (0,) * (len(block) - 1)
    return [pl.BlockSpec(block, lambda i: (first(i),) + rest), pl.BlockSpec(block, lambda i: (second(i),) + rest)]


def _in_proj(xa, xb, inv_b, g_mix, w_in_ext, g_q, wq_t, g_kv, wk_ext, wv_t, e2, b_gate, *, tm, seq_starts):
    na, nb = xa.shape[0] // tm, xb.shape[0] // tm
    nt = na + nb
    t = nt * tm
    row = lambda w: pl.BlockSpec((tm, w), lambda i: (i, 0))
    col = lambda h: pl.BlockSpec((h, tm), lambda i: (0, i))
    return pl.pallas_call(
        functools.partial(_in_proj_kernel, tm=tm, n_first=na, seq_starts=seq_starts),
        grid=(nt,),
        in_specs=_split_specs((tm, D_MODEL), na, nb) + [
                  _const_spec((D_ROPE // 2, tm)), _const_spec((1, D_MODEL)),
                  _const_spec((D_MODEL, D_IN_EXT)), _const_spec((1, Q_RANK)),
                  _const_spec((N_HEADS * (LANES + D_ROPE), Q_RANK)), _const_spec((1, KV_RANK)),
                  _const_spec((KV_RANK, N_HEADS * LANES)), _const_spec((N_HEADS * D_V, KV_RANK)),
                  _const_spec((LANES, N_HEADS * LANES)), _const_spec((1, 2 * D_MODEL))],
        out_specs=(pl.BlockSpec((1, N_HEADS * LANES, tm), lambda i: (i, 0, 0)), row(N_HEADS * LANES),
                   pl.BlockSpec((1, N_HEADS * D_V, tm), lambda i: (i, 0, 0)), row(CONV_CH), row(2 * D_MODEL)),
        out_shape=(jax.ShapeDtypeStruct((nt, N_HEADS * LANES, tm), BF16),
                   jax.ShapeDtypeStruct((t, N_HEADS * LANES), BF16),
                   jax.ShapeDtypeStruct((nt, N_HEADS * D_V, tm), BF16),
                   jax.ShapeDtypeStruct((t, CONV_CH), F32),
                   jax.ShapeDtypeStruct((t, 2 * D_MODEL), BF16)),
        compiler_params=pltpu.CompilerParams(dimension_semantics=("arbitrary",), vmem_limit_bytes=VMEM_LIMIT),
        name="in_proj",
    )(xa, xb, inv_b, g_mix, w_in_ext, g_q, wq_t, g_kv, wk_ext, wv_t, e2, b_gate)


def _attn_kernel(qt_ref, k_ref, vt_ref, o_ref, s_ref, acc_ref, m_ref, *, nq, n_chunks, tk, unroll):
    half = n_chunks // 2
    ones = jnp.ones((BF16_ROWS, tk), BF16)

    def scores(a, j, slot):
        ks = pl.multiple_of(j * tk, tk)
        s_ref[slot] = _dot(k_ref[pl.ds(ks, tk), :], qt_ref[a])

    def softmax_pv(j, slot):
        st = s_ref[slot]
        m = jnp.where(j == 0, NEG_BIG, m_ref[...])
        m_new = jnp.maximum(m, jnp.max(st, axis=0, keepdims=True))
        p = jnp.exp2(st - m_new).astype(BF16)
        v1 = jnp.concatenate([vt_ref[j], ones], axis=0)
        acc = jnp.exp2(m - m_new) * acc_ref[...] + _dot(v1, p)
        acc_ref[...] = acc
        m_ref[...] = m_new
        return acc

    m_ref[...] = jnp.full(m_ref.shape, NEG_BIG, F32)
    acc_ref[...] = jnp.zeros(acc_ref.shape, F32)
    scores(0, 0, 0)

    def pair(pp, may_end_tile):
        a = lax.div(pp, half)
        j = 2 * lax.rem(pp, half)
        scores(a, j + 1, 1)
        softmax_pv(j, 0)
        nxt = pp + 1
        scores(jnp.minimum(lax.div(nxt, half), nq - 1), 2 * lax.rem(nxt, half), 0)
        acc = softmax_pv(j + 1, 1)
        if may_end_tile:
            o_ref[a] = (acc[:D_V] / acc[D_V:D_V + 1]).astype(o_ref.dtype)

    period = math.gcd(half, unroll)

    def trip(blk, c):
        for u in range(unroll):
            pair(blk * unroll + u, (u + 1) % period == 0)
        return c

    lax.fori_loop(0, nq * half // unroll, trip, 0)


def _attention(qt3, k, vt3, *, seq_start, seq_len, n_seq, tq, tk):
    nq = seq_len // tq
    nkc = seq_len // tk
    assert nkc % 2 == 0, "key chunks are processed in pairs"
    s0 = seq_start // seq_len
    n_pairs = nq * (nkc // 2)
    unroll = next(u for u in (16, 8, 4, 2, 1) if n_pairs % u == 0)
    return pl.pallas_call(
        functools.partial(_attn_kernel, nq=nq, n_chunks=nkc, tk=tk, unroll=unroll),
        grid=(n_seq, N_HEADS),
        in_specs=[pl.BlockSpec((nq, LANES, tq), lambda b, h: (s0 + b, h, 0)),
                  pl.BlockSpec((seq_len, LANES), lambda b, h: (s0 + b, h)),
                  pl.BlockSpec((nkc, D_V, tk), lambda b, h: (s0 + b, h, 0))],
        out_specs=pl.BlockSpec((nq, D_V, tq), lambda b, h: (b, h, 0)),
        out_shape=jax.ShapeDtypeStruct((n_seq * nq, N_HEADS * D_V, tq), BF16),
        scratch_shapes=[pltpu.VMEM((2, tk, tq), F32), pltpu.VMEM((D_V + BF16_ROWS, tq), F32),
                        pltpu.VMEM((1, tq), F32)],
        compiler_params=pltpu.CompilerParams(dimension_semantics=("arbitrary", "arbitrary"),
                                             vmem_limit_bytes=VMEM_LIMIT),
        name="attention",
    )(qt3, k, vt3)


def _local_rows(tm):
    return TOP_K * tm + N_EXPERTS * RUN_PAD


def _sorted_rows(t, nt, bm):
    runs = t * TOP_K + nt * N_EXPERTS * (RUN_PAD - 1)
    return -(-runs // bm) * bm + N_EXPERTS * bm


def _rows_copy(src, s, dst, d, sem, rows=RUN_PAD):
    s = pl.multiple_of(s, RUN_PAD)
    d = pl.multiple_of(d, RUN_PAD)
    return pltpu.make_async_copy(src.at[pl.ds(s, rows), :], dst.at[pl.ds(d, rows), :], sem)


def _piece_caps(n_loc):
    return n_loc // BIG_PIECE, N_EXPERTS * (BIG_PIECE // RUN_PAD - 1)


def _piece_tables(n_pieces, off, run_row, *, n_loc):
    per_big = BIG_PIECE // RUN_PAD
    qb, qs = _piece_caps(n_loc)
    n_big = n_pieces // per_big
    n_sm = n_pieces - n_big * per_big

    def table(counts, loc0, row0, step, cap):
        end = jnp.cumsum(counts, axis=1)
        q = jnp.arange(cap, dtype=I32)
        ex = jnp.minimum(jnp.sum((end[:, None, :] <= q[None, :, None]).astype(I32), axis=2), N_EXPERTS - 1)
        onehot = ex[..., None] == jnp.arange(N_EXPERTS, dtype=I32)
        pick = lambda v: jnp.sum(jnp.where(onehot, v[:, None, :], 0), axis=2)
        idx = q[None, :] - pick(end - counts)
        return (pick(loc0) + idx * step).reshape(-1), (pick(row0) + idx * step).reshape(-1), end[:, -1]

    loc_b, row_b, cnt_b = table(n_big, off, run_row, BIG_PIECE, qb)
    done = n_big * BIG_PIECE
    loc_s, row_s, cnt_s = table(n_sm, off + done, run_row + done, RUN_PAD, qs)
    return jnp.stack([cnt_b, cnt_s], axis=1).reshape(-1), loc_b, row_b, loc_s, row_s


def _for_each_piece(tables, tile, n_loc, fn):
    cnt, loc_b, row_b, loc_s, row_s = tables
    qb, qs = _piece_caps(n_loc)

    def big(q, c):
        fn(loc_b[tile * qb + q], row_b[tile * qb + q], BIG_PIECE)
        return c

    def small(q, c):
        fn(loc_s[tile * qs + q], row_s[tile * qs + q], RUN_PAD)
        return c

    lax.fori_loop(0, cnt[2 * tile], big, 0)
    lax.fori_loop(0, cnt[2 * tile + 1], small, 0)


def _wait_rows(src, dst, sem, count, rows):
    def one(q, c):
        _rows_copy(src, 0, dst, 0, sem, rows=rows).wait()
        return c

    lax.fori_loop(0, count, one, 0)


def _wait_tile(src, dst, sem, cnt, tile):
    _wait_rows(src, dst, sem, cnt[2 * tile], BIG_PIECE)
    _wait_rows(src, dst, sem, cnt[2 * tile + 1], RUN_PAD)


def _post_mix_kernel(xa_ref, xb_ref, ota_ref, otb_ref, u_ref, up_ref, un_ref, gates_ref, woa_ref, cw_ref, cb_ref,
                     lng_ref, lnb_ref, wpb_ref, bpb_ref, wout_ref, gffn_ref, wr_ref, br_ref, ltri_ref, utri_ref,
                     x1_ref, route_ref, meta_ref, ext_ref, cv_ref, ph_ref, carry_ref, *, tm, n_first, seq_bounds):
    i = pl.program_id(0)
    t0 = i * tm
    is_start = functools.reduce(jnp.logical_or, [t0 == s for s, _ in seq_bounds])
    is_end = functools.reduce(jnp.logical_or, [t0 + tm == e for _, e in seq_bounds])

    ext_ref[0:HALO_ROWS, :] = jnp.where(is_start, 0.0, up_ref[...])
    ext_ref[HALO_ROWS:HALO_ROWS + tm, :] = u_ref[...]
    ext_ref[HALO_ROWS + tm:2 * HALO_ROWS + tm, :] = jnp.where(is_end, 0.0, un_ref[...])
    rc = min(tm, CONV_ROWS)
    sub = SUBLANES
    base = HALO_ROWS - CONV_PAD
    for c in range(tm // rc):
        acc = None
        for b in range(sub):
            part = None
            for o in range(CONV_WIDTH):
                if (o + base) % sub != b:
                    continue
                a0 = c * rc + (o + base) - b
                term = cw_ref[o:o + 1, :] * ext_ref[a0:a0 + rc + sub, :]
                part = term if part is None else part + term
            if b == 0:
                shifted = part[:rc]
            else:
                ph_ref[b] = part
                shifted = ph_ref[b, b:b + rc, :]
            acc = shifted if acc is None else acc + shifted
        cv_ref[c * rc:(c + 1) * rc, :] = acc + cb_ref[...]

    cv = cv_ref[...]
    mu = jnp.mean(cv, axis=-1, keepdims=True)
    xc = cv - mu
    y = xc * lax.rsqrt(jnp.mean(xc * xc, axis=-1, keepdims=True) + LN_EPS) * lng_ref[...] + lnb_ref[...]
    y = y * jax.nn.sigmoid(y)
    branch_b = _dot(y.astype(BF16), wpb_ref[...]) + bpb_ref[...]
    branch_a = _dot_tn(_tile_select(i, n_first, ota_ref, otb_ref)[0], woa_ref[...])
    gates = gates_ref[...]
    mix = gates[:, :D_MODEL].astype(F32) * branch_a + gates[:, D_MODEL:].astype(F32) * branch_b
    x1 = _tile_select(i, n_first, xa_ref, xb_ref) + _dot(mix.astype(BF16), wout_ref[...])
    x1_ref[...] = x1

    h2 = _rms(x1, gffn_ref[...])
    h2_hi = h2.astype(BF16)
    h2_lo = (h2 - h2_hi.astype(F32)).astype(BF16)
    hi_both = _dot(h2_hi, wr_ref[...])
    logits = hi_both[:, :LANES] + (hi_both[:, LANES:] + _dot(h2_lo, wr_ref[:, :LANES])) + br_ref[...]
    lane = lax.broadcasted_iota(I32, logits.shape, 1)
    lane_f = lane.astype(F32)
    vals, idxs, sels = [], [], []
    cur = logits
    for _ in range(TOP_K):
        mval = jnp.max(cur, axis=-1, keepdims=True)
        ik = jnp.min(jnp.where(cur == mval, lane_f, float(LANES)), axis=-1, keepdims=True)
        sel = lane_f == ik
        cur = jnp.where(sel, -jnp.inf, cur)
        vals.append(mval)
        idxs.append(ik)
        sels.append(sel)
    exps = [jnp.exp(v - vals[0]) for v in vals]
    denom = exps[0] + exps[1] + exps[2] + exps[3]

    onehot = sels[0].astype(F32) + sels[1].astype(F32) + sels[2].astype(F32) + sels[3].astype(F32)

    @pl.when(i == 0)
    def _():
        carry_ref[...] = jnp.zeros_like(carry_ref)

    n_e = jnp.sum(onehot, axis=0, keepdims=True)
    n_pieces = jnp.floor((n_e + (RUN_PAD - 1)) * (1.0 / RUN_PAD))
    pieces_b = jnp.broadcast_to(n_pieces, (SUBLANES, LANES)).astype(BF16)
    off = _dot(pieces_b, utri_ref[...])[0:1] * RUN_PAD
    total_pieces = jnp.sum(n_pieces, axis=-1, keepdims=True)
    carry = carry_ref[...]
    prefix = _dot(ltri_ref[...], onehot.astype(BF16))
    local = off + prefix
    route = jnp.zeros(logits.shape, F32)
    for kk in range(TOP_K):
        pos = jnp.sum(jnp.where(sels[kk], local, 0.0), axis=-1, keepdims=True)
        route = jnp.where(lane == kk, idxs[kk], route)
        route = jnp.where(lane == TOP_K + kk, exps[kk] / denom, route)
        route = jnp.where(lane == 2 * TOP_K + kk, pos, route)
    route_ref[...] = route
    carry_ref[...] = carry + n_pieces * RUN_PAD

    row8 = lax.broadcasted_iota(I32, (SUBLANES, LANES), 0)
    metav = jnp.where(row8 == 0, jnp.broadcast_to(carry, (SUBLANES, LANES)), 0.0)
    metav = jnp.where(row8 == 1, jnp.broadcast_to(n_pieces, (SUBLANES, LANES)), metav)
    metav = jnp.where(row8 == 2, jnp.broadcast_to(off, (SUBLANES, LANES)), metav)
    metav = jnp.where(row8 == 3, jnp.broadcast_to(total_pieces, (SUBLANES, LANES)), metav)
    meta_ref[0] = metav.astype(I32)


def _post_mix(xa, xb, ota, otb, u, gates, w_oa, conv_w, conv_b, ln_g, ln_b, w_pb, b_pb, w_out, g_ffn, w_r, b_r, ltri,
              utri, *, tm, seq_bounds):
    na, nb = xa.shape[0] // tm, xb.shape[0] // tm
    nt = na + nb
    t = nt * tm
    hb = tm // HALO_ROWS
    n_halo = t // HALO_ROWS
    row = lambda w: pl.BlockSpec((tm, w), lambda i: (i, 0))
    return pl.pallas_call(
        functools.partial(_post_mix_kernel, tm=tm, n_first=na, seq_bounds=seq_bounds),
        grid=(nt,),
        in_specs=_split_specs((tm, D_MODEL), na, nb) + _split_specs((1, N_HEADS * D_V, tm), na, nb) + [
                  row(CONV_CH),
                  pl.BlockSpec((HALO_ROWS, CONV_CH), lambda i: (jnp.maximum(i * hb - 1, 0), 0)),
                  pl.BlockSpec((HALO_ROWS, CONV_CH), lambda i: (jnp.minimum((i + 1) * hb, n_halo - 1), 0)),
                  row(2 * D_MODEL), _const_spec((N_HEADS * D_V, D_MODEL)), _const_spec((CONV_WIDTH, CONV_CH)),
                  _const_spec((1, CONV_CH)), _const_spec((1, CONV_CH)), _const_spec((1, CONV_CH)),
                  _const_spec((CONV_CH, D_MODEL)), _const_spec((1, D_MODEL)), _const_spec((D_MODEL, D_MODEL)),
                  _const_spec((1, D_MODEL)), _const_spec((D_MODEL, 2 * LANES)), _const_spec((1, LANES)),
                  _const_spec((tm, tm)), _const_spec((LANES, LANES))],
        out_specs=(row(D_MODEL), row(LANES), pl.BlockSpec((1, SUBLANES, LANES), lambda i: (i, 0, 0))),
        out_shape=(jax.ShapeDtypeStruct((t, D_MODEL), F32), jax.ShapeDtypeStruct((t, LANES), F32),
                   jax.ShapeDtypeStruct((nt, SUBLANES, LANES), I32)),
        scratch_shapes=[pltpu.VMEM((tm + 2 * HALO_ROWS, CONV_CH), F32), pltpu.VMEM((tm, CONV_CH), F32),
                        pltpu.VMEM((SUBLANES, min(tm, CONV_ROWS) + SUBLANES, CONV_CH), F32), pltpu.VMEM((1, LANES), F32)],
        compiler_params=pltpu.CompilerParams(dimension_semantics=("arbitrary",), vmem_limit_bytes=VMEM_LIMIT),
        name="post_mix",
    )(xa, xb, ota, otb, u, u, u, gates, w_oa, conv_w, conv_b, ln_g, ln_b, w_pb, b_pb, w_out, g_ffn, w_r, b_r, ltri, utri)


def _dispatch_kernel(cnt_s, locb_s, rowb_s, locs_s, rows_s, start_s, end_s, x1_ref, route_ref, gffn_ref, xs_hbm,
                     xloc_ref, zero_ref, row_sems, *, tm, nt, bm, n_rows):
    tables = (cnt_s, locb_s, rowb_s, locs_s, rows_s)
    i = pl.program_id(0)
    slot = lax.rem(i, N_DISPATCH_BUFS)
    n_loc = _local_rows(tm)

    h2 = _rms(x1_ref[...], gffn_ref[...]).astype(BF16)
    pos_t = jnp.transpose(route_ref[...])[2 * TOP_K:3 * TOP_K, :].astype(I32)
    prow = lax.broadcasted_iota(I32, (n_loc, tm), 0)
    perm = jnp.zeros((n_loc, tm), F32)
    for kk in range(TOP_K):
        perm = jnp.where(prow == pos_t[kk:kk + 1, :], 1.0, perm)
    xloc_ref[slot] = _pack_rows(_dot(perm.astype(BF16), h2))

    def wait_tile(tile):
        _wait_tile(xloc_ref.at[0], xs_hbm, row_sems.at[lax.rem(tile, N_DISPATCH_BUFS)], cnt_s, tile)

    @pl.when(i >= N_DISPATCH_BUFS - 1)
    def _():
        wait_tile(i - (N_DISPATCH_BUFS - 1))

    _for_each_piece(tables, i, n_loc,
                    lambda s, d, rows: _rows_copy(xloc_ref.at[slot], s, xs_hbm, d, row_sems.at[slot], rows=rows).start())

    @pl.when(i == nt - 1)
    def _():
        for back in range(N_DISPATCH_BUFS - 2, -1, -1):
            @pl.when(i - back >= 0)
            def _():
                wait_tile(i - back)
        row_sem = row_sems.at[0]
        zero_ref[...] = jnp.zeros(zero_ref.shape, U32)

        def tail(e, c):
            end = end_s[e]
            n = lax.shift_right_logical(start_s[e + 1] - end, int(math.log2(RUN_PAD)))

            def piece(q, c2):
                _rows_copy(zero_ref, 0, xs_hbm, end + q * RUN_PAD, row_sem).start()
                return c2

            lax.fori_loop(0, n, piece, 0)
            return c + n

        _wait_rows(zero_ref, xs_hbm, row_sem, lax.fori_loop(0, N_EXPERTS, tail, 0), RUN_PAD)
        used = start_s[N_EXPERTS]
        n_free = (n_rows - used) // bm

        def fill(b, c):
            _rows_copy(zero_ref, 0, xs_hbm, used + b * bm, row_sem, rows=bm).start()
            return c

        lax.fori_loop(0, n_free, fill, 0)
        _wait_rows(zero_ref, xs_hbm, row_sem, n_free, bm)


def _dispatch(tables, start, end, x1, route, g_ffn, *, tm, bm, n_rows):
    t = x1.shape[0]
    nt = t // tm
    return pl.pallas_call(
        functools.partial(_dispatch_kernel, tm=tm, nt=nt, bm=bm, n_rows=n_rows),
        grid_spec=pltpu.PrefetchScalarGridSpec(
            num_scalar_prefetch=7,
            grid=(nt,),
            in_specs=[pl.BlockSpec((tm, D_MODEL), lambda i, *_: (i, 0)), pl.BlockSpec((tm, LANES), lambda i, *_: (i, 0)),
                      pl.BlockSpec((1, D_MODEL), lambda i, *_: (0, 0))],
            out_specs=pl.BlockSpec(memory_space=pl.ANY),
            scratch_shapes=[pltpu.VMEM((N_DISPATCH_BUFS, _local_rows(tm), PACKED), U32),
                            pltpu.VMEM((bm, PACKED), U32), pltpu.SemaphoreType.DMA((N_DISPATCH_BUFS,))],
        ),
        out_shape=jax.ShapeDtypeStruct((n_rows, PACKED), U32),
        compiler_params=pltpu.CompilerParams(dimension_semantics=("arbitrary",), vmem_limit_bytes=VMEM_LIMIT,
                                             has_side_effects=True),
        name="dispatch",
    )(*tables, start, end, x1, route, g_ffn)


def _moe_kernel(exp_ref, n_ref, xs_ref, wgu_ref, bgu_ref, wd_ref, bd_ref, y_ref, wgu_bf, wd_bf):
    w = pl.program_id(0)
    used = w < n_ref[0]

    @pl.when(jnp.logical_or(w == 0, exp_ref[w] != exp_ref[jnp.maximum(w - 1, 0)]))
    def _():
        wgu_bf[...] = wgu_ref[0].astype(BF16)
        wd_bf[...] = wd_ref[0].astype(BF16)

    @pl.when(used)
    def _():
        x = _unpack_rows(xs_ref[...])
        gu = _dot(x, wgu_bf[...]) + bgu_ref[0]
        g = jnp.minimum(gu[:, :D_FF], SWIGLU_LIMIT)
        u = jnp.clip(gu[:, D_FF:], -SWIGLU_LIMIT, SWIGLU_LIMIT)
        act = (u + 1.0) * (g * jax.nn.sigmoid(SWIGLU_ALPHA * g))
        y_ref[...] = _pack_rows(_dot(act.astype(BF16), wd_bf[...]) + bd_ref[0])

    @pl.when(jnp.logical_not(used))
    def _():
        y_ref[...] = jnp.zeros(y_ref.shape, U32)


def _moe(ex, n_used, xs, w_gu, b_gu, w_down, b_down, *, bm):
    return pl.pallas_call(
        _moe_kernel,
        grid_spec=pltpu.PrefetchScalarGridSpec(
            num_scalar_prefetch=2,
            grid=(xs.shape[0] // bm,),
            in_specs=[pl.BlockSpec((bm, PACKED), lambda w, ex, n: (w, 0)),
                      pl.BlockSpec((1, D_MODEL, 2 * D_FF), lambda w, ex, n: (ex[w], 0, 0)),
                      pl.BlockSpec((1, 1, 2 * D_FF), lambda w, ex, n: (ex[w], 0, 0)),
                      pl.BlockSpec((1, D_FF, D_MODEL), lambda w, ex, n: (ex[w], 0, 0)),
                      pl.BlockSpec((1, 1, D_MODEL), lambda w, ex, n: (ex[w], 0, 0))],
            out_specs=pl.BlockSpec((bm, PACKED), lambda w, ex, n: (w, 0)),
            scratch_shapes=[pltpu.VMEM((D_MODEL, 2 * D_FF), BF16), pltpu.VMEM((D_FF, D_MODEL), BF16)],
        ),
        out_shape=jax.ShapeDtypeStruct(xs.shape, U32),
        compiler_params=pltpu.CompilerParams(dimension_semantics=("arbitrary",), vmem_limit_bytes=VMEM_LIMIT),
        name="moe",
    )(ex, n_used, xs, w_gu, b_gu, w_down, b_down)


def _expert_layout(totals, *, bm, n_blocks):
    n_blk = (totals + bm - 1) // bm
    blk_end = jnp.cumsum(n_blk)
    start = jnp.concatenate([jnp.zeros((1,), I32), blk_end * bm]).astype(I32)
    w = jnp.arange(n_blocks, dtype=I32)
    ex = jnp.minimum(jnp.sum((blk_end[None, :] <= w[:, None]).astype(I32), axis=1), N_EXPERTS - 1)
    return start, ex.astype(I32), blk_end[-1].astype(I32)[None]


def _combine_kernel(cnt_s, locb_s, rowb_s, locs_s, rows_s, x1_ref, route_ref, gfin_ref, y_hbm, outa_ref, outb_ref,
                    yloc_ref, sems, *, tm, nt, n_first):
    tables = (cnt_s, locb_s, rowb_s, locs_s, rows_s)
    i = pl.program_id(0)
    slot = lax.rem(i, 2)
    n_loc = _local_rows(tm)

    def gather(tile, sl):
        _for_each_piece(tables, tile, n_loc,
                        lambda loc, row, rows: _rows_copy(y_hbm, row, yloc_ref.at[sl], loc, sems.at[sl],
                                                          rows=rows).start())

    @pl.when(i == 0)
    def _():
        yloc_ref[...] = jnp.zeros(yloc_ref.shape, U32)
        gather(0, 0)

    @pl.when(i + 1 < nt)
    def _():
        gather(i + 1, 1 - slot)

    _wait_tile(y_hbm, yloc_ref.at[slot], sems.at[slot], cnt_s, i)

    route = route_ref[...]
    lanes = lax.broadcasted_iota(I32, (tm, n_loc), 1)
    pw = jnp.zeros((tm, n_loc), F32)
    for kk in range(TOP_K):
        pos = route[:, 2 * TOP_K + kk:2 * TOP_K + kk + 1].astype(I32)
        pw = jnp.where(lanes == pos, route[:, TOP_K + kk:TOP_K + kk + 1], pw)
    moe = _dot(pw.astype(BF16), _unpack_rows(yloc_ref[slot]))
    res = _rms(x1_ref[...] + moe, gfin_ref[...])

    @pl.when(i < n_first)
    def _():
        outa_ref[...] = res

    @pl.when(i >= n_first)
    def _():
        outb_ref[...] = res


def _combine(tables, x1, route, g_final, y, *, tm, n_first):
    t = x1.shape[0]
    nt = t // tm
    n_second = nt - n_first
    return pl.pallas_call(
        functools.partial(_combine_kernel, tm=tm, nt=nt, n_first=n_first),
        grid_spec=pltpu.PrefetchScalarGridSpec(
            num_scalar_prefetch=5,
            grid=(nt,),
            in_specs=[pl.BlockSpec((tm, D_MODEL), lambda i, *_: (i, 0)), pl.BlockSpec((tm, LANES), lambda i, *_: (i, 0)),
                      pl.BlockSpec((1, D_MODEL), lambda i, *_: (0, 0)), pl.BlockSpec(memory_space=pl.ANY)],
            out_specs=(pl.BlockSpec((tm, D_MODEL), lambda i, *_: (jnp.minimum(i, n_first - 1), 0)),
                       pl.BlockSpec((tm, D_MODEL), lambda i, *_: (jnp.clip(i - n_first, 0, n_second - 1), 0))),
            scratch_shapes=[pltpu.VMEM((2, _local_rows(tm), PACKED), U32), pltpu.SemaphoreType.DMA((2,))],
        ),
        out_shape=(jax.ShapeDtypeStruct((n_first * tm, D_MODEL), F32),
                   jax.ShapeDtypeStruct((n_second * tm, D_MODEL), F32)),
        compiler_params=pltpu.CompilerParams(dimension_semantics=("arbitrary",), vmem_limit_bytes=VMEM_LIMIT),
        name="combine",
    )(*tables, x1, route, g_final, y)


def _rot_cols(w):
    half = w.shape[-1] // 2
    return jnp.concatenate([-w[..., half:], w[..., :half]], axis=-1)


def _prep_weights(w_in, w_uq, w_ukv):
    d = w_in.shape[0]
    w_kpe = w_in[:, C_KPE:C_KPE + D_ROPE]
    w_in_ext = jnp.concatenate(
        [w_in[:, :C_KPE], w_kpe, _rot_cols(w_kpe), jnp.zeros((d, LANES - 2 * D_ROPE), F32),
         w_in[:, C_KPE + D_ROPE:]], axis=1).astype(BF16)

    wq = w_uq.reshape(Q_RANK, N_HEADS, D_QK)
    nope, pe = wq[..., :D_NOPE], wq[..., D_NOPE:]
    zpad = jnp.zeros((Q_RANK, N_HEADS, LANES - D_QK), F32)
    plain = jnp.concatenate([nope, pe, zpad], axis=-1).reshape(Q_RANK, N_HEADS * LANES)
    rot = _rot_cols(pe).reshape(Q_RANK, N_HEADS * D_ROPE)
    wq_t = jnp.concatenate([plain, rot], axis=1).T.astype(BF16)

    wkv = w_ukv.reshape(KV_RANK, N_HEADS, D_NOPE + D_V)
    wk_ext = jnp.concatenate([wkv[..., :D_NOPE], jnp.zeros((KV_RANK, N_HEADS, LANES - D_NOPE), F32)],
                             axis=-1).reshape(KV_RANK, N_HEADS * LANES).astype(BF16)
    wv_t = wkv[..., D_NOPE:].reshape(KV_RANK, N_HEADS * D_V).T.astype(BF16)
    return w_in_ext, wq_t, wk_ext, wv_t


def _kpe_placement():
    e2 = np.zeros((LANES, N_HEADS * LANES), np.float32)
    for j in range(D_ROPE):
        for h in range(N_HEADS):
            e2[j, h * LANES + D_NOPE + j] = 1.0
            e2[D_ROPE + j, h * LANES + D_NOPE + j] = 1.0
    return jnp.asarray(e2, BF16)


def _inv_freq(tm):
    inv = 1.0 / (ROPE_THETA ** (jnp.arange(0, D_ROPE, 2, dtype=F32) / D_ROPE))
    return jnp.broadcast_to(inv[:, None], (D_ROPE // 2, tm))


def _token_tile(s1, s2):
    g = math.gcd(s1, s2)
    return TOKEN_TILE if g % TOKEN_TILE == 0 else g


def kernel(x_prompt, x_sample, g_mix, w_in, b_gate, g_q, w_uq, g_kv, w_ukv, w_oa, conv_w, conv_b, ln_g, ln_b, w_pb,
           b_pb, w_out, g_ffn, w_router, b_router, w_gu, b_gu, w_down, b_down, g_final):
    b1, s1, d = x_prompt.shape
    b2, s2, _ = x_sample.shape
    t1, t2 = b1 * s1, b2 * s2
    t = t1 + t2
    xa, xb = x_prompt.reshape(t1, d), x_sample.reshape(t2, d)
    seq_bounds = tuple((b * s1, (b + 1) * s1) for b in range(b1)) + tuple(
        (t1 + b * s2, t1 + (b + 1) * s2) for b in range(b2))

    tm = _token_tile(s1, s2)
    nt = t // tm
    w_in_ext, wq_t, wk_ext, wv_t = _prep_weights(w_in[0], w_uq[0], w_ukv[0])

    qt, k, vt3, u, gates = _in_proj(xa, xb, _inv_freq(tm), g_mix, w_in_ext, g_q, wq_t, g_kv, wk_ext, wv_t,
                                    _kpe_placement(), b_gate, tm=tm, seq_starts=tuple(s for s, _ in seq_bounds))

    ota = _attention(qt, k, vt3, seq_start=0, seq_len=s1, n_seq=b1, tq=tm, tk=tm)
    otb = _attention(qt, k, vt3, seq_start=t1, seq_len=s2, n_seq=b2, tq=tm, tk=tm)

    w_r = jnp.concatenate([w_router[0], jnp.zeros((d, LANES - N_EXPERTS), F32)], axis=1)
    w_r_hi = w_r.astype(BF16)
    w_r = jnp.concatenate([w_r_hi, (w_r - w_r_hi.astype(F32)).astype(BF16)], axis=1)
    b_r = jnp.concatenate([b_router[0], jnp.full((LANES - N_EXPERTS,), NEG_BIG, F32)])[None, :]
    ltri = jnp.asarray(np.tril(np.ones((tm, tm), np.float32), -1), BF16)
    utri = jnp.asarray(np.triu(np.ones((LANES, LANES), np.float32), 1), BF16)
    x1, route, meta = _post_mix(xa, xb, ota, otb, u, gates, w_oa[0].astype(BF16), conv_w[0], conv_b, ln_g, ln_b,
                                w_pb[0].astype(BF16), b_pb, w_out[0].astype(BF16), g_ffn, w_r, b_r, ltri, utri,
                                tm=tm, seq_bounds=seq_bounds)

    bm = MOE_ROWS
    n_rows = _sorted_rows(t, nt, bm)
    carry, n_pieces, off = (meta[:, r, :N_EXPERTS] for r in range(3))
    start, ex, n_used = _expert_layout(carry[-1] + n_pieces[-1] * RUN_PAD, bm=bm, n_blocks=n_rows // bm)
    run_row = carry + start[None, :N_EXPERTS]
    tables = _piece_tables(n_pieces, off, run_row, n_loc=_local_rows(tm))
    run_end = run_row[-1] + n_pieces[-1] * RUN_PAD

    xs = _dispatch(tables, start, run_end, x1, route, g_ffn, tm=tm, bm=bm, n_rows=n_rows)
    y = _moe(ex, n_used, xs, w_gu[0], b_gu[0][:, None, :], w_down[0], b_down[0][:, None, :], bm=bm)
    out_a, out_b = _combine(tables, x1, route, g_final[None, :], y, tm=tm, n_first=t1 // tm)
    return out_a.reshape(b1, s1, d), out_b.reshape(b2, s2, d)
```
